```python
import math
import jax, jax.numpy as jnp
from jax import lax
import numpy as np

D_MODEL = 1024
BATCH = 32
SEQ = 256
DEPTH = 2
DEC_BATCH = 8
DEC_SEQ = 2048
PAST_LEN = 512

GRID_W = 64
HEAD_DIM = 64
ATTN_SCALE = HEAD_DIM ** -0.5
NA_HEADS = (3 * D_MODEL // 8) // HEAD_DIM
NA_KH = 8
NA_KW = 16
WA_HEADS = (3 * D_MODEL // 8) // HEAD_DIM
WA_KV_HEADS = 2
WA_GROUP = WA_HEADS // WA_KV_HEADS
WA_WINDOW = 128
WA_BLOCK = 128
SSM_CH = D_MODEL // 4
SSM_GROUP_CH = 16
SSM_GROUPS = SSM_CH // SSM_GROUP_CH
SSM_STATE = 64
NA_WIDTH = NA_HEADS * HEAD_DIM
WA_WIDTH = WA_HEADS * HEAD_DIM
WA_KV_WIDTH = WA_KV_HEADS * HEAD_DIM
MIX_WIDTH = NA_WIDTH + WA_WIDTH + SSM_CH
IN_SIZES = (NA_WIDTH, NA_WIDTH, NA_WIDTH, WA_WIDTH, WA_KV_WIDTH, WA_KV_WIDTH, SSM_CH)
IN_COLS = sum(IN_SIZES)
SPLIT_POINTS = tuple(sum(IN_SIZES[:i + 1]) for i in range(len(IN_SIZES) - 1))
N_EXPERTS = 16
EC_CAPACITY = 2
EXPERT_FF = 1024
ROPE_BASE = 10000.0
RMS_EPS = 1e-6
NEG_INF = -1e30

kernel_name = 'hybrid_diffusion_na_wgqa_s5_ec_step'


def _rmsnorm(x, g):
    xf = x.astype(jnp.float32)
    y = xf * lax.rsqrt(jnp.mean(xf * xf, axis=-1, keepdims=True) + RMS_EPS)
    return (y * g.astype(jnp.float32)).astype(x.dtype)


def _modulation(cond, w, b):
    m = jax.nn.silu(cond) @ w + b
    return jnp.split(m[:, None, :], 6, axis=-1)


def _rope_1d(x, pos):
    d = x.shape[-1]
    half = d // 2
    inv = ROPE_BASE ** (-jnp.arange(half, dtype=jnp.float32) / half)
    ang = pos[:, None] * inv[None, :]
    cos = jnp.cos(ang)[:, None, :]
    sin = jnp.sin(ang)[:, None, :]
    xf = x.astype(jnp.float32)
    x1, x2 = xf[..., :half], xf[..., half:]
    return jnp.concatenate([x1 * cos - x2 * sin, x2 * cos + x1 * sin], axis=-1)


def _axial_rope(x):
    L, d = x.shape[1], x.shape[-1]
    t = jnp.arange(L)
    row = (t // GRID_W).astype(jnp.float32)
    col = (t % GRID_W).astype(jnp.float32)
    out = jnp.concatenate([_rope_1d(x[..., :d // 2], row), _rope_1d(x[..., d // 2:], col)], axis=-1)
    return out.astype(x.dtype)


def _ctx_attention(q, k, v, sink=None):
    B, L = q.shape[:2]
    s = jnp.einsum('bqkgd,bjkd->bkgqj', q, k).astype(jnp.float32) * ATTN_SCALE
    if sink is not None:
        sk = jnp.broadcast_to(sink.astype(jnp.float32)[None, :, :, None, None], s.shape[:-1] + (1,))
        s = jnp.concatenate([s, sk], axis=-1)
    p = jax.nn.softmax(s, axis=-1)
    if sink is not None:
        p = p[..., :-1]
    o = jnp.einsum('bkgqj,bjkd->bqkgd', p.astype(v.dtype), v)
    return o.reshape(B, L, -1)


def _na_latent(q, k, v, kc, vc, rpb):
    B, L, H, D = q.shape
    rows = L // GRID_W
    kh = min(NA_KH, rows)
    r = jnp.arange(rows)
    row_start = jnp.clip(r - kh // 2, 0, rows - kh)
    row_idx = row_start[:, None] + jnp.arange(kh)[None, :]
    dr_i = row_idx - r[:, None] + (NA_KH - 1)
    cq = jnp.arange(GRID_W)
    col_start = jnp.clip(cq - NA_KW // 2, 0, GRID_W - NA_KW)
    col_mask = (cq[None, :] >= col_start[:, None]) & (cq[None, :] < col_start[:, None] + NA_KW)
    dc_i = jnp.clip(cq[None, :] - cq[:, None], -(NA_KW - 1), NA_KW - 1) + (NA_KW - 1)
    bias = rpb.astype(jnp.float32)[:, dr_i[:, None, :, None], dc_i[None, :, None, :]]
    bias = jnp.where(col_mask[None, None, :, None, :], bias, NEG_INF)
    qg = q.reshape(B, rows, GRID_W, H, D)
    kr = k.reshape(B, rows, GRID_W, H, D)[:, row_idx]
    vr = v.reshape(B, rows, GRID_W, H, D)[:, row_idx]
    s_loc = jnp.einsum('brchd,brkwhd->bhrckw', qg, kr).astype(jnp.float32) * ATTN_SCALE + bias[None]
    s_ctx = jnp.einsum('brchd,bjhd->bhrcj', qg, kc).astype(jnp.float32) * ATTN_SCALE
    nl = kh * GRID_W
    s = jnp.concatenate([s_loc.reshape(B, H, rows, GRID_W, nl), s_ctx], axis=-1)
    p = jax.nn.softmax(s, axis=-1).astype(v.dtype)
    p_loc = p[..., :nl].reshape(B, H, rows, GRID_W, kh, GRID_W)
    p_ctx = p[..., nl:]
    o = jnp.einsum('bhrckw,brkwhd->brchd', p_loc, vr) + jnp.einsum('bhrcj,bjhd->brchd', p_ctx, vc)
    return o.reshape(B, L, H * D)


def _wa_latent(q, k, v, kc, vc, sink):
    B, L, HQ, D = q.shape
    nb = L // WA_BLOCK
    qb = q.reshape(B, nb, WA_BLOCK, WA_KV_HEADS, WA_GROUP, D)
    pad = ((0, 0), (WA_BLOCK, WA_BLOCK), (0, 0), (0, 0))
    kp = jnp.pad(k, pad)
    vp = jnp.pad(v, pad)

    def bands(x):
        return jnp.concatenate(
            [x[:, j * WA_BLOCK: j * WA_BLOCK + L].reshape(B, nb, WA_BLOCK, WA_KV_HEADS, D) for j in range(3)],
            axis=2)

    kb, vb = bands(kp), bands(vp)
    qi = jnp.arange(WA_BLOCK)
    kj = jnp.arange(3 * WA_BLOCK)
    in_win = jnp.abs(kj[None, :] - WA_BLOCK - qi[:, None]) <= WA_WINDOW
    kpos = jnp.arange(nb)[:, None] * WA_BLOCK + kj[None, :] - WA_BLOCK
    in_seq = (kpos >= 0) & (kpos < L)
    mask = in_win[None] & in_seq[:, None, :]
    s_loc = jnp.einsum('bnqkgd,bnjkd->bkgnqj', qb, kb).astype(jnp.float32) * ATTN_SCALE
    s_loc = jnp.where(mask, s_loc, NEG_INF)
    s_ctx = jnp.einsum('bnqkgd,bjkd->bkgnqj', qb, kc).astype(jnp.float32) * ATTN_SCALE
    s_sink = jnp.broadcast_to(
        sink.astype(jnp.float32).reshape(WA_KV_HEADS, WA_GROUP)[None, :, :, None, None, None],
        s_ctx.shape[:-1] + (1,))
    s = jnp.concatenate([s_loc, s_ctx, s_sink], axis=-1)
    p = jax.nn.softmax(s, axis=-1).astype(v.dtype)
    nl = 3 * WA_BLOCK
    p_loc, p_ctx = p[..., :nl], p[..., nl:-1]
    o = jnp.einsum('bkgnqj,bnjkd->bnqkgd', p_loc, vb) + jnp.einsum('bkgnqj,bjkd->bnqkgd', p_ctx, vc)
    return o.reshape(B, L, HQ * D)


def _cmul(ar, ai, br, bi):
    return ar * br - ai * bi, ar * bi + ai * br


def _scan_combine(e1, e2):
    a1r, a1i, b1r, b1i = e1
    a2r, a2i, b2r, b2i = e2
    ar, ai = _cmul(a2r, a2i, a1r, a1i)
    br, bi = _cmul(a2r, a2i, b1r, b1i)
    return ar, ai, br + b2r, bi + b2i


def _s5_direction(u, h0, lam_re, lam_im, log_dt, b_re, b_im, c_re, c_im, reverse):
    f32 = jnp.float32
    lr, li = lam_re.astype(f32), lam_im.astype(f32)
    dt = jnp.exp(log_dt.astype(f32))[:, None]
    mag = jnp.exp(lr * dt)
    abar_r, abar_i = mag * jnp.cos(li * dt), mag * jnp.sin(li * dt)
    den = lr * lr + li * li
    nr = abar_r - 1.0
    fr = (nr * lr + abar_i * li) / den
    fi = (abar_i * lr - nr * li) / den
    bbr, bbi = _cmul(fr[..., None], fi[..., None], b_re.astype(f32), b_im.astype(f32))
    bu_r = jnp.einsum('blgh,gph->blgp', u, bbr)
    bu_i = jnp.einsum('blgh,gph->blgp', u, bbi)
    a_r = jnp.broadcast_to(abar_r, bu_r.shape)
    a_i = jnp.broadcast_to(abar_i, bu_r.shape)
    A_r, A_i, H_r, H_i = lax.associative_scan(_scan_combine, (a_r, a_i, bu_r, bu_i), reverse=reverse, axis=1)
    if h0 is not None:
        d_r, d_i = _cmul(A_r, A_i, h0[0][:, None], h0[1][:, None])
        H_r = H_r + d_r
        H_i = H_i + d_i
    y = (jnp.einsum('blgp,ghp->blgh', H_r, c_re.astype(f32))
         - jnp.einsum('blgp,ghp->blgh', H_i, c_im.astype(f32)))
    return y, H_r, H_i


def _s5_mixer(u, h0, lam_re, lam_im, log_dt, b_re, b_im, c_re, c_im, d_skip, w_glu):
    B, L, _ = u.shape
    uf = u.astype(jnp.float32)
    ug = uf.reshape(B, L, SSM_GROUPS, SSM_GROUP_CH)
    y = d_skip.astype(jnp.float32) * uf
    finals = []
    for dr in range(2):
        init = None if h0 is None else (h0[:, dr, 0].astype(jnp.float32), h0[:, dr, 1].astype(jnp.float32))
        yd, H_r, H_i = _s5_direction(ug, init, lam_re[dr], lam_im[dr], log_dt[dr], b_re[dr], b_im[dr],
                                     c_re[dr], c_im[dr], reverse=(dr == 1))
        y = y + yd.reshape(B, L, SSM_CH)
        if h0 is None:
            last = -1 if dr == 0 else 0
            finals.append(jnp.stack([H_r[:, last], H_i[:, last]], axis=1))
    z = y.astype(u.dtype) @ w_glu
    a, g = jnp.split(z, 2, axis=-1)
    out = a * jax.nn.sigmoid(g)
    state = jnp.stack(finals, axis=1) if h0 is None else None
    return out, state


def _ec_moe(h, w_router, w_gate, w_up, w_down):
    B, L, _ = h.shape
    cap = EC_CAPACITY * L // N_EXPERTS
    aff = jax.nn.softmax(jnp.einsum('bld,de->ble', h, w_router).astype(jnp.float32), axis=-1)
    gate, idx = lax.top_k(jnp.swapaxes(aff, 1, 2), cap)
    bidx = jnp.arange(B)[:, None, None]
    xs = h[bidx, idx]
    a = jnp.einsum('becd,edf->becf', xs, w_gate)
    u = jnp.einsum('becd,edf->becf', xs, w_up)
    y = jnp.einsum('becf,efd->becd', jax.nn.silu(a) * u, w_down)
    y = y * gate[..., None].astype(y.dtype)
    return jnp.zeros_like(h).at[bidx, idx].add(y.astype(h.dtype))


def _layer(x, cond, l, p, ctx):
    B, L, _ = x.shape
    sh1, sc1, g1, sh2, sc2, g2 = _modulation(cond, p['w_ada'][l], p['b_ada'][l])
    ng = p['norm_g'][l]
    h = _rmsnorm(x, ng[0]) * (1 + sc1) + sh1
    z = h @ p['w_in'][l]
    na_q, na_k, na_v, wa_q, wa_k, wa_v, ss_u = jnp.split(z, SPLIT_POINTS, axis=-1)
    na_q = na_q.reshape(B, L, NA_HEADS, HEAD_DIM)
    na_k = na_k.reshape(B, L, NA_HEADS, HEAD_DIM)
    na_v = na_v.reshape(B, L, NA_HEADS, HEAD_DIM)
    wa_q = wa_q.reshape(B, L, WA_HEADS, HEAD_DIM)
    wa_k = wa_k.reshape(B, L, WA_KV_HEADS, HEAD_DIM)
    wa_v = wa_v.reshape(B, L, WA_KV_HEADS, HEAD_DIM)
    ssm_p = (p['ssm_lambda_re'][l], p['ssm_lambda_im'][l], p['ssm_log_dt'][l], p['ssm_b_re'][l],
             p['ssm_b_im'][l], p['ssm_c_re'][l], p['ssm_c_im'][l], p['ssm_d'][l], p['w_glu'][l])
    if ctx is None:
        o_na = _ctx_attention(na_q[:, :, :, None], na_k, na_v)
        o_wa = _ctx_attention(wa_q.reshape(B, L, WA_KV_HEADS, WA_GROUP, HEAD_DIM), wa_k, wa_v,
                              p['wa_sink'][l].reshape(WA_KV_HEADS, WA_GROUP))
        o_ss, ss_state = _s5_mixer(ss_u, None, *ssm_p)
        new = (na_k, na_v, wa_k, wa_v, ss_state)
    else:
        c_na_k, c_na_v, c_wa_k, c_wa_v, h0 = ctx
        o_na = _na_latent(na_q, na_k, na_v, c_na_k, c_na_v, p['na_rpb'][l])
        o_wa = _wa_latent(_axial_rope(wa_q), _axial_rope(wa_k), wa_v, c_wa_k, c_wa_v, p['wa_sink'][l])
        o_ss, _ = _s5_mixer(ss_u, h0, *ssm_p)
        new = None
    mix = jnp.concatenate([o_na, o_wa, o_ss.astype(o_na.dtype)], axis=-1) @ p['w_out'][l]
    x = x + g1 * _rmsnorm(mix, ng[1])
    h2 = _rmsnorm(x, ng[2]) * (1 + sc2) + sh2
    f = _ec_moe(h2, p['w_router'][l], p['w_exp_gate'][l], p['w_exp_up'][l], p['w_exp_down'][l])
    x = x + g2 * _rmsnorm(f, ng[3])
    return x, new


def setup_inputs(seed: int = 0) -> dict:
    key = jax.random.key(seed)
    ks = jax.random.split(key, 32)
    f32 = jnp.float32

    def nrm(k, shape, s=1.0):
        return jax.random.normal(k, shape, f32) * s

    G, P, HG = SSM_GROUPS, SSM_STATE, SSM_GROUP_CH
    lam_re = -0.5 + nrm(ks[20], (DEPTH, 2, G, P), 0.01)
    lam_im = jnp.broadcast_to(math.pi * jnp.arange(P, dtype=f32), (DEPTH, 2, G, P)) + nrm(ks[21], (DEPTH, 2, G, P), 0.01)
    log_dt = jax.random.uniform(ks[22], (DEPTH, 2, G), f32, math.log(1e-3), math.log(1e-1))
    return {
        'x_prompt': nrm(ks[0], (BATCH, SEQ, D_MODEL)),
        'x_sample': nrm(ks[1], (DEC_BATCH, DEC_SEQ, D_MODEL)),
        'c': nrm(ks[2], (DEC_BATCH, D_MODEL)),
        'cache_na_k': nrm(ks[3], (DEC_BATCH, DEPTH, PAST_LEN, NA_HEADS, HEAD_DIM)),
        'cache_na_v': nrm(ks[4], (DEC_BATCH, DEPTH, PAST_LEN, NA_HEADS, HEAD_DIM)),
        'cache_wa_k': nrm(ks[5], (DEC_BATCH, DEPTH, PAST_LEN, WA_KV_HEADS, HEAD_DIM)),
        'cache_wa_v': nrm(ks[6], (DEC_BATCH, DEPTH, PAST_LEN, WA_KV_HEADS, HEAD_DIM)),
        'state_ssm': nrm(ks[7], (DEC_BATCH, DEPTH, 2, 2, G, P), 0.5),
        'c_ctx': nrm(ks[8], (D_MODEL,)),
        'w_ada': nrm(ks[9], (DEPTH, D_MODEL, 6 * D_MODEL), 0.5 * D_MODEL ** -0.5),
        'b_ada': nrm(ks[10], (DEPTH, 6 * D_MODEL), 0.02),
        'norm_g': 1.0 + nrm(ks[11], (DEPTH, 4, D_MODEL), 0.1),
        'w_in': nrm(ks[12], (DEPTH, D_MODEL, IN_COLS), D_MODEL ** -0.5),
        'w_out': nrm(ks[13], (DEPTH, MIX_WIDTH, D_MODEL), MIX_WIDTH ** -0.5),
        'na_rpb': nrm(ks[14], (DEPTH, NA_HEADS, 2 * NA_KH - 1, 2 * NA_KW - 1), 0.5),
        'wa_sink': nrm(ks[15], (DEPTH, WA_HEADS)),
        'ssm_lambda_re': lam_re,
        'ssm_lambda_im': lam_im,
        'ssm_log_dt': log_dt,
        'ssm_b_re': nrm(ks[16], (DEPTH, 2, G, P, HG), (2 * HG) ** -0.5),
        'ssm_b_im': nrm(ks[17], (DEPTH, 2, G, P, HG), (2 * HG) ** -0.5),
        'ssm_c_re': nrm(ks[18], (DEPTH, 2, G, HG, P), P ** -0.5),
        'ssm_c_im': nrm(ks[19], (DEPTH, 2, G, HG, P), P ** -0.5),
        'ssm_d': nrm(ks[23], (DEPTH, SSM_CH)),
        'w_glu': nrm(ks[24], (DEPTH, SSM_CH, 2 * SSM_CH), SSM_CH ** -0.5),
        'w_router': nrm(ks[25], (DEPTH, D_MODEL, N_EXPERTS), D_MODEL ** -0.5),
        'w_exp_gate': nrm(ks[26], (DEPTH, N_EXPERTS, D_MODEL, EXPERT_FF), D_MODEL ** -0.5),
        'w_exp_up': nrm(ks[27], (DEPTH, N_EXPERTS, D_MODEL, EXPERT_FF), D_MODEL ** -0.5),
        'w_exp_down': nrm(ks[28], (DEPTH, N_EXPERTS, EXPERT_FF, D_MODEL), EXPERT_FF ** -0.5),
    }


def reference(x_prompt, x_sample, c, cache_na_k, cache_na_v, cache_wa_k, cache_wa_v, state_ssm,
              c_ctx, w_ada, b_ada, norm_g, w_in, w_out, na_rpb, wa_sink,
              ssm_lambda_re, ssm_lambda_im, ssm_log_dt, ssm_b_re, ssm_b_im, ssm_c_re, ssm_c_im,
              ssm_d, w_glu, w_router, w_exp_gate, w_exp_up, w_exp_down):
    p = dict(w_ada=w_ada, b_ada=b_ada, norm_g=norm_g, w_in=w_in, w_out=w_out, na_rpb=na_rpb,
             wa_sink=wa_sink, ssm_lambda_re=ssm_lambda_re, ssm_lambda_im=ssm_lambda_im,
             ssm_log_dt=ssm_log_dt, ssm_b_re=ssm_b_re, ssm_b_im=ssm_b_im, ssm_c_re=ssm_c_re,
             ssm_c_im=ssm_c_im, ssm_d=ssm_d, w_glu=w_glu, w_router=w_router,
             w_exp_gate=w_exp_gate, w_exp_up=w_exp_up, w_exp_down=w_exp_down)
    xp = x_prompt
    ctx_states = []
    for l in range(DEPTH):
        xp, st = _layer(xp, c_ctx[None, :], l, p, None)
        ctx_states.append(st)
    xs = x_sample
    for l in range(DEPTH):
        xs, _ = _layer(xs, c, l, p, (cache_na_k[:, l], cache_na_v[:, l], cache_wa_k[:, l],
                                    cache_wa_v[:, l], state_ssm[:, l]))
    new_na_k = jnp.stack([s[0] for s in ctx_states], axis=1)
    new_na_v = jnp.stack([s[1] for s in ctx_states], axis=1)
    new_wa_k = jnp.stack([s[2] for s in ctx_states], axis=1)
    new_wa_v = jnp.stack([s[3] for s in ctx_states], axis=1)
    new_ssm = jnp.stack([s[4] for s in ctx_states], axis=1)
    return (xp, xs, new_na_k, new_na_v, new_wa_k, new_wa_v, new_ssm)
```

```python
import functools
import math

import numpy as np
import jax
import jax.numpy as jnp
from jax import lax
from jax.experimental import pallas as pl
from jax.experimental.pallas import tpu as pltpu

F32 = jnp.float32
BF16 = jnp.bfloat16

D_MODEL = 1024
DEPTH = 2
GRID_W = 64
HEAD_DIM = 64
NA_HEADS = 6
NA_KH = 8
NA_KW = 16
WA_HEADS = 6
WA_KV_HEADS = 2
WA_WINDOW = 128
WA_BLOCK = 128
SSM_CH = 256
SSM_GROUP_CH = 16
SSM_GROUPS = 16
SSM_STATE = 64
SSM_N = SSM_GROUPS * SSM_STATE
NA_WIDTH = NA_HEADS * HEAD_DIM
WA_WIDTH = WA_HEADS * HEAD_DIM
WA_KV_WIDTH = WA_KV_HEADS * HEAD_DIM
IN_COLS = 2048
N_EXPERTS = 16
EC_CAPACITY = 2
EXPERT_FF = 1024
ROPE_BASE = 10000.0
RMS_EPS = 1e-6
NEG_INF = -1e30
ATTN_SCALE = HEAD_DIM ** -0.5

LANES = 128
SUBLANES = 8
VMEM_LIMIT = 48 * 1024 * 1024

C_NAQ, C_NAK, C_NAV, C_WAQ, C_WAK, C_WAV, C_SSU = 0, 384, 768, 1152, 1536, 1664, 1792
WA_HEAD_ORDER = (0, 3, 1, 4, 2, 5)

NA_QROWS = 4
NA_KROWS = NA_QROWS + NA_KH
TOKEN_TILE = 256
SCAN_TILE = 128
SCAN_BATCH = SUBLANES
FFN_ROWS = 512
SEL_CHUNK = 256


def _cparams(sem):
    return pltpu.CompilerParams(dimension_semantics=sem, vmem_limit_bytes=VMEM_LIMIT)


def _sigmoid(x):
    return 1.0 / (1.0 + jnp.exp(-x))


def _rms(x, g):
    ms = jnp.mean(x * x, axis=-1, keepdims=True)
    return x * lax.rsqrt(ms + RMS_EPS) * g


def _dot(a, b):
    return jnp.dot(a, b, preferred_element_type=F32)


def _dot_nt(a, b):
    return lax.dot_general(a, b, (((1,), (1,)), ((), ())), preferred_element_type=F32)


def _mod_body(c_ref, w_ref, b_ref, o_ref):
    c = c_ref[...]
    s = c * _sigmoid(c)
    o_ref[0] = _dot(s.astype(BF16), w_ref[0].astype(BF16)) + b_ref[0]


def _modulation(cond, w_ada, b_ada):
    n = cond.shape[0]
    tn = 1024
    return pl.pallas_call(
        _mod_body,
        grid=(DEPTH, 6 * D_MODEL // tn),
        in_specs=[
            pl.BlockSpec((n, D_MODEL), lambda l, j: (0, 0)),
            pl.BlockSpec((1, D_MODEL, tn), lambda l, j: (l, 0, j)),
            pl.BlockSpec((1, 1, tn), lambda l, j: (l, 0, j)),
        ],
        out_specs=pl.BlockSpec((1, n, tn), lambda l, j: (l, 0, j)),
        out_shape=jax.ShapeDtypeStruct((DEPTH, n, 6 * D_MODEL), F32),
        compiler_params=_cparams(("arbitrary", "arbitrary")),
        name="modulation",
    )(cond, w_ada, b_ada.reshape(DEPTH, 1, 6 * D_MODEL))


def _rope_chunk(x, cos, sin_signed):
    lane = lax.broadcasted_iota(jnp.int32, x.shape, 1)
    first = (lane % 32) < 16
    rot = jnp.where(first, pltpu.roll(x, LANES - 16, 1), pltpu.roll(x, 16, 1))
    return x * cos + rot * sin_signed


def _inproj_body(rope, x_ref, mod_ref, g_ref, w_ref, *rest):
    if rope:
        cos_ref, sin_ref = rest[:2]
        rest = rest[2:]
    qna_ref, kna_ref, vna_ref, qwa_ref, kwa_ref, vwa_ref, u_ref = rest
    x = x_ref[...]
    m = mod_ref[0]
    sh1 = m[:, 0:D_MODEL]
    sc1 = m[:, D_MODEL:2 * D_MODEL]
    h = _rms(x, g_ref[0:1, :]) * (1.0 + sc1) + sh1
    z = _dot(h.astype(BF16), w_ref[...])
    qna_ref[...] = (z[:, C_NAQ:C_NAK] * ATTN_SCALE).astype(BF16)
    kna_ref[...] = z[:, C_NAK:C_NAV]
    vna_ref[...] = z[:, C_NAV:C_WAQ]
    vwa_ref[...] = z[:, C_WAV:C_SSU]
    u_ref[...] = z[:, C_SSU:IN_COLS]
    if rope:
        cos = cos_ref[...]
        sin = sin_ref[...]
        for j in range(WA_WIDTH // LANES):
            qc = _rope_chunk(z[:, C_WAQ + j * LANES:C_WAQ + (j + 1) * LANES], cos, sin)
            qwa_ref[:, j * LANES:(j + 1) * LANES] = (qc * ATTN_SCALE).astype(BF16)
        kwa_ref[...] = _rope_chunk(z[:, C_WAK:C_WAV], cos, sin)
    else:
        qwa_ref[...] = (z[:, C_WAQ:C_WAK] * ATTN_SCALE).astype(BF16)
        kwa_ref[...] = z[:, C_WAK:C_WAV]


def _inproj(x, mod_l, mod_row_fn, g, w_bf, L, rope_tabs):
    T = x.shape[0]
    tm = TOKEN_TILE
    tiles_per_seq = L // tm
    in_specs = [
        pl.BlockSpec((tm, D_MODEL), lambda i: (i, 0)),
        pl.BlockSpec((1, 1, 6 * D_MODEL), lambda i: (mod_row_fn(i // tiles_per_seq), 0, 0)),
        pl.BlockSpec((4, D_MODEL), lambda i: (0, 0)),
        pl.BlockSpec((D_MODEL, IN_COLS), lambda i: (0, 0)),
    ]
    args = [x, mod_l, g, w_bf]
    if rope_tabs is not None:
        in_specs += [pl.BlockSpec((tm, LANES), lambda i: (i % tiles_per_seq, 0))] * 2
        args += list(rope_tabs)
    widths = (NA_WIDTH, NA_WIDTH, NA_WIDTH, WA_WIDTH, WA_KV_WIDTH, WA_KV_WIDTH, SSM_CH)
    dtypes = (BF16, F32, F32, BF16, F32, F32, F32)
    return pl.pallas_call(
        functools.partial(_inproj_body, rope_tabs is not None),
        grid=(T // tm,),
        in_specs=in_specs,
        out_specs=[pl.BlockSpec((tm, w), lambda i: (i, 0)) for w in widths],
        out_shape=[jax.ShapeDtypeStruct((T, w), dt) for w, dt in zip(widths, dtypes)],
        compiler_params=_cparams(("arbitrary",)),
        name="inproj",
    )(*args)


def _lane_lo(shape):
    return lax.broadcasted_iota(jnp.int32, shape, len(shape) - 1) < HEAD_DIM


def _keep_half(q, lo, half):
    keep = jnp.where(lo, 1.0 - half, 0.0 + half).astype(q.dtype)
    return q * keep


def _attn_ctx_body(sink_ref, qna, kna, vna, qwa, kwa, vwa, ona, owa):
    L = qna.shape[1]
    lo = _lane_lo((L, LANES))
    for j in range(NA_WIDTH // LANES):
        sl = slice(j * LANES, (j + 1) * LANES)
        q2 = qna[0, :, sl]
        k2 = kna[0, :, sl].astype(BF16)
        v2 = vna[0, :, sl].astype(BF16)
        halves = []
        for half in range(2):
            qm = _keep_half(q2, lo, half)
            s = _dot_nt(qm, k2)
            m = jnp.max(s, axis=-1, keepdims=True)
            p = jnp.exp(s - m)
            l = jnp.sum(p, axis=-1, keepdims=True)
            halves.append(_dot(p.astype(BF16), v2) / l)
        ona[0, :, sl] = jnp.where(lo, halves[0], halves[1]).astype(BF16)
    kw = kwa[0].astype(BF16)
    vw = vwa[0].astype(BF16)
    for j in range(WA_WIDTH // LANES):
        sl = slice(j * LANES, (j + 1) * LANES)
        q2 = qwa[0, :, sl]
        halves = []
        for half in range(2):
            head = WA_HEAD_ORDER[2 * j + half]
            qm = _keep_half(q2, lo, half)
            s = _dot_nt(qm, kw)
            sk = sink_ref[head]
            m = jnp.maximum(jnp.max(s, axis=-1, keepdims=True), sk)
            p = jnp.exp(s - m)
            l = jnp.sum(p, axis=-1, keepdims=True) + jnp.exp(sk - m)
            halves.append(_dot(p.astype(BF16), vw) / l)
        owa[0, :, sl] = jnp.where(lo, halves[0], halves[1]).astype(BF16)


def _attn_ctx(sink, qna, kna, vna, qwa, kwa, vwa, B, L):
    def spec(w):
        return pl.BlockSpec((1, L, w), lambda b: (b, 0, 0))

    r3 = lambda a: a.reshape(B, L, a.shape[-1])
    ona, owa = pl.pallas_call(
        _attn_ctx_body,
        grid=(B,),
        in_specs=[pl.BlockSpec(memory_space=pltpu.SMEM),
                  spec(NA_WIDTH), spec(NA_WIDTH), spec(NA_WIDTH),
                  spec(WA_WIDTH), spec(WA_KV_WIDTH), spec(WA_KV_WIDTH)],
        out_specs=[spec(NA_WIDTH), spec(WA_WIDTH)],
        out_shape=[jax.ShapeDtypeStruct((B, L, NA_WIDTH), BF16),
                   jax.ShapeDtypeStruct((B, L, WA_WIDTH), BF16)],
        compiler_params=_cparams(("arbitrary",)),
        name="attn_ctx",
    )(sink, r3(qna), r3(kna), r3(vna), r3(qwa), r3(kwa), r3(vwa))
    return ona.reshape(B * L, NA_WIDTH), owa.reshape(B * L, WA_WIDTH)


def _na_key_start(rb, rows):
    return jnp.clip(rb * NA_QROWS - NA_KH // 2, 0, rows - NA_KROWS)


def _attn_na_body(rows, q_ref, k_ref, v_ref, kc_ref, vc_ref, bias_ref, o_ref):
    rb = pl.program_id(1)
    nq = NA_QROWS * GRID_W
    nk = NA_KROWS * GRID_W
    start = pl.multiple_of(_na_key_start(rb, rows) * GRID_W, GRID_W)
    lo = _lane_lo((nq, LANES))
    for j in range(NA_WIDTH // LANES):
        sl = slice(j * LANES, (j + 1) * LANES)
        q2 = q_ref[0, :, sl]
        kl = k_ref[0, pl.ds(start, nk), sl].astype(BF16)
        vl = v_ref[0, pl.ds(start, nk), sl].astype(BF16)
        kc = kc_ref[0, 0, :, sl].astype(BF16)
        vc = vc_ref[0, 0, :, sl].astype(BF16)
        halves = []
        for half in range(2):
            qm = _keep_half(q2, lo, half)
            s_loc = _dot_nt(qm, kl) + bias_ref[0, 2 * j + half]
            s_ctx = _dot_nt(qm, kc)
            m = jnp.maximum(jnp.max(s_loc, axis=-1, keepdims=True),
                            jnp.max(s_ctx, axis=-1, keepdims=True))
            p_loc = jnp.exp(s_loc - m)
            p_ctx = jnp.exp(s_ctx - m)
            l = jnp.sum(p_loc, axis=-1, keepdims=True) + jnp.sum(p_ctx, axis=-1, keepdims=True)
            o2 = _dot(p_loc.astype(BF16), vl) + _dot(p_ctx.astype(BF16), vc)
            halves.append(o2 / l)
        o_ref[0, :, sl] = jnp.where(lo, halves[0], halves[1]).astype(BF16)


def _na_bias(rpb_l, rows):
    nrb = rows // NA_QROWS
    pats = (0, 1, nrb - 1)
    dr = np.zeros((3, NA_QROWS * GRID_W, NA_KROWS * GRID_W), np.int32)
    dc = np.zeros_like(dr)
    ok = np.zeros(dr.shape, bool)
    cq = np.arange(GRID_W)
    col_start = np.clip(cq - NA_KW // 2, 0, GRID_W - NA_KW)
    col_ok = (cq[None, :] >= col_start[:, None]) & (cq[None, :] < col_start[:, None] + NA_KW)
    dcol = np.clip(cq[None, :] - cq[:, None], -(NA_KW - 1), NA_KW - 1) + (NA_KW - 1)
    for pi, rb in enumerate(pats):
        ks = int(np.clip(rb * NA_QROWS - NA_KH // 2, 0, rows - NA_KROWS))
        for i in range(NA_QROWS):
            qr = rb * NA_QROWS + i
            rs = int(np.clip(qr - NA_KH // 2, 0, rows - NA_KH))
            for jj in range(NA_KROWS):
                kr = ks + jj
                valid = rs <= kr < rs + NA_KH
                blk = (slice(i * GRID_W, (i + 1) * GRID_W), slice(jj * GRID_W, (jj + 1) * GRID_W))
                dr[pi][blk] = np.clip(kr - qr + NA_KH - 1, 0, 2 * NA_KH - 2)
                dc[pi][blk] = dcol
                ok[pi][blk] = col_ok & valid
    b = rpb_l.astype(F32)[:, dr, dc]
    b = jnp.where(ok[None], b, NEG_INF)
    return jnp.transpose(b, (1, 0, 2, 3))


def _attn_na(q, k, v, cache_k, cache_v, bias, l, B, L):
    rows = L // GRID_W
    nrb = rows // NA_QROWS
    nq = NA_QROWS * GRID_W
    P = cache_k.shape[2]
    pat = lambda rb: jnp.where(rb == 0, 0, jnp.where(rb == nrb - 1, 2, 1))
    r3 = lambda a: a.reshape(B, L, a.shape[-1])
    ck = cache_k.reshape(B, DEPTH, P, NA_WIDTH)
    cv = cache_v.reshape(B, DEPTH, P, NA_WIDTH)
    o = pl.pallas_call(
        functools.partial(_attn_na_body, rows),
        grid=(B, nrb),
        in_specs=[
            pl.BlockSpec((1, nq, NA_WIDTH), lambda b, r: (b, r, 0)),
            pl.BlockSpec((1, L, NA_WIDTH), lambda b, r: (b, 0, 0)),
            pl.BlockSpec((1, L, NA_WIDTH), lambda b, r: (b, 0, 0)),
            pl.BlockSpec((1, 1, P, NA_WIDTH), lambda b, r: (b, l, 0, 0)),
            pl.BlockSpec((1, 1, P, NA_WIDTH), lambda b, r: (b, l, 0, 0)),
            pl.BlockSpec((1, NA_HEADS, nq, NA_KROWS * GRID_W), lambda b, r: (pat(r), 0, 0, 0)),
        ],
        out_specs=pl.BlockSpec((1, nq, NA_WIDTH), lambda b, r: (b, r, 0)),
        out_shape=jax.ShapeDtypeStruct((B, L, NA_WIDTH), BF16),
        compiler_params=_cparams(("arbitrary", "arbitrary")),
        name="attn_na",
    )(r3(q), r3(k), r3(v), ck, cv, bias)
    return o.reshape(B * L, NA_WIDTH)


def _attn_wa_body(L, sink_ref, q_ref, k_ref, v_ref, kc_ref, vc_ref, o_ref):
    n = pl.program_id(1)
    nk = 3 * WA_BLOCK
    npair = WA_WIDTH // LANES
    start = pl.multiple_of(jnp.clip((n - 1) * WA_BLOCK, 0, L - nk), WA_BLOCK)
    kl = k_ref[0, pl.ds(start, nk), :].astype(BF16)
    vl = v_ref[0, pl.ds(start, nk), :].astype(BF16)
    kc = kc_ref[0, 0].astype(BF16)
    vc = vc_ref[0, 0].astype(BF16)
    rows = npair * WA_BLOCK
    qpos = n * WA_BLOCK + lax.broadcasted_iota(jnp.int32, (rows, nk), 0) % WA_BLOCK
    kpos = start + lax.broadcasted_iota(jnp.int32, (rows, nk), 1)
    in_win = jnp.abs(kpos - qpos) <= WA_WINDOW
    lo = _lane_lo((rows, LANES))
    q_all = jnp.concatenate([q_ref[0, :, j * LANES:(j + 1) * LANES] for j in range(npair)], axis=0)
    blk = lax.broadcasted_iota(jnp.int32, (rows, 1), 0) // WA_BLOCK
    outs = []
    for g in range(WA_KV_HEADS):
        qm = _keep_half(q_all, lo, g)
        sk = jnp.zeros((rows, 1), F32)
        for j in range(npair):
            sk = jnp.where(blk == j, sink_ref[WA_HEAD_ORDER[2 * j + g]], sk)
        s_loc = jnp.where(in_win, _dot_nt(qm, kl), NEG_INF)
        s_ctx = _dot_nt(qm, kc)
        m = jnp.maximum(jnp.maximum(jnp.max(s_loc, axis=-1, keepdims=True),
                                    jnp.max(s_ctx, axis=-1, keepdims=True)), sk)
        p_loc = jnp.exp(s_loc - m)
        p_ctx = jnp.exp(s_ctx - m)
        l = (jnp.sum(p_loc, axis=-1, keepdims=True) + jnp.sum(p_ctx, axis=-1, keepdims=True)
             + jnp.exp(sk - m))
        o2 = _dot(p_loc.astype(BF16), vl) + _dot(p_ctx.astype(BF16), vc)
        outs.append(o2 / l)
    o_all = jnp.where(lo, outs[0], outs[1]).astype(BF16)
    for j in range(npair):
        o_ref[0, :, j * LANES:(j + 1) * LANES] = o_all[j * WA_BLOCK:(j + 1) * WA_BLOCK]


def _attn_wa(sink, q, k, v, cache_k, cache_v, l, B, L):
    nb = L // WA_BLOCK
    P = cache_k.shape[2]
    r3 = lambda a: a.reshape(B, L, a.shape[-1])
    ck = cache_k.reshape(B, DEPTH, P, WA_KV_WIDTH)
    cv = cache_v.reshape(B, DEPTH, P, WA_KV_WIDTH)
    o = pl.pallas_call(
        functools.partial(_attn_wa_body, L),
        grid=(B, nb),
        in_specs=[
            pl.BlockSpec(memory_space=pltpu.SMEM),
            pl.BlockSpec((1, WA_BLOCK, WA_WIDTH), lambda b, n: (b, n, 0)),
            pl.BlockSpec((1, L, WA_KV_WIDTH), lambda b, n: (b, 0, 0)),
            pl.BlockSpec((1, L, WA_KV_WIDTH), lambda b, n: (b, 0, 0)),
            pl.BlockSpec((1, 1, P, WA_KV_WIDTH), lambda b, n: (b, l, 0, 0)),
            pl.BlockSpec((1, 1, P, WA_KV_WIDTH), lambda b, n: (b, l, 0, 0)),
        ],
        out_specs=pl.BlockSpec((1, WA_BLOCK, WA_WIDTH), lambda b, n: (b, n, 0)),
        out_shape=jax.ShapeDtypeStruct((B, L, WA_WIDTH), BF16),
        compiler_params=_cparams(("arbitrary", "arbitrary")),
        name="attn_wa",
    )(sink, r3(q), r3(k), r3(v), ck, cv)
    return o.reshape(B * L, WA_WIDTH)


def _s5_prep_body(lr_ref, li_ref, ldt_ref, br_ref, bi_ref, a_ref, bb_ref):
    lr = lr_ref[0]
    li = li_ref[0]
    dt = jnp.exp(ldt_ref[0])
    mag = jnp.exp(lr * dt)
    ar = mag * jnp.cos(li * dt)
    ai = mag * jnp.sin(li * dt)
    den = lr * lr + li * li
    nr = ar - 1.0
    fr = (nr * lr + ai * li) / den
    fi = (ai * lr - nr * li) / den
    a_ref[0, :, 0:SSM_N] = ar
    a_ref[0, :, SSM_N:2 * SSM_N] = ai
    br = br_ref[0]
    bi = bi_ref[0]
    bb_ref[0, :, 0:SSM_N] = (fr * br - fi * bi).astype(BF16)
    bb_ref[0, :, SSM_N:2 * SSM_N] = (fr * bi + fi * br).astype(BF16)


def _s5_prep(lam_re, lam_im, log_dt, b_re, b_im):
    n = DEPTH * 2
    eye = jnp.eye(SSM_GROUPS, dtype=F32)

    def blockdiag_b(b):
        return jnp.einsum('ngph,gk->nghkp', b.reshape(n, SSM_GROUPS, SSM_STATE, SSM_GROUP_CH), eye
                          ).reshape(n, SSM_CH, SSM_N)

    lr = lam_re.reshape(n, 1, SSM_N)
    li = lam_im.reshape(n, 1, SSM_N)
    ldt = jnp.repeat(log_dt.reshape(n, SSM_GROUPS), SSM_STATE, axis=-1).reshape(n, 1, SSM_N)
    vec = pl.BlockSpec((1, 1, SSM_N), lambda i: (i, 0, 0))
    mat = pl.BlockSpec((1, SSM_CH, SSM_N), lambda i: (i, 0, 0))
    return pl.pallas_call(
        _s5_prep_body,
        grid=(n,),
        in_specs=[vec, vec, vec, mat, mat],
        out_specs=[pl.BlockSpec((1, 1, 2 * SSM_N), lambda i: (i, 0, 0)),
                   pl.BlockSpec((1, SSM_CH, 2 * SSM_N), lambda i: (i, 0, 0))],
        out_shape=[jax.ShapeDtypeStruct((n, 1, 2 * SSM_N), F32),
                   jax.ShapeDtypeStruct((n, SSM_CH, 2 * SSM_N), BF16)],
        compiler_params=_cparams(("arbitrary",)),
        name="s5_prep",
    )(lr, li, ldt, blockdiag_b(b_re), blockdiag_b(b_im))


def _s5_c_matrix(c_re, c_im):
    n = DEPTH * 2
    eye = jnp.eye(SSM_GROUPS, dtype=F32)

    def blk(c):
        return jnp.einsum('nghp,gk->ngpkh', c.reshape(n, SSM_GROUPS, SSM_GROUP_CH, SSM_STATE), eye
                          ).reshape(n, SSM_N, SSM_CH)

    return jnp.concatenate([blk(c_re), -blk(c_im)], axis=1).astype(BF16)


def _s5_scan_body(tl, u_ref, bb_ref, a_ref, c_ref, h0_ref, y_ref, hfin_ref, bu, hst):
    d = pl.program_id(0)
    i = pl.program_id(2)

    @pl.when(i == 0)
    def _():
        hst[...] = h0_ref[0]

    nch = SSM_N // LANES
    bb = bb_ref[0]
    for b in range(SCAN_BATCH):
        res = _dot(u_ref[b].astype(BF16), bb)
        for k in range(2 * nch):
            bu[k, b * tl:(b + 1) * tl, :] = res[:, k * LANES:(k + 1) * LANES]
    a = a_ref[0]
    ar = jnp.broadcast_to(a[:, 0:SSM_N], (SCAN_BATCH, SSM_N))
    ai = jnp.broadcast_to(a[:, SSM_N:2 * SSM_N], (SCAN_BATCH, SSM_N))

    def step(s, carry):
        hr, hi = carry
        t = s + d * (tl - 1 - 2 * s)
        idx = pl.ds(t, SCAN_BATCH, stride=tl)
        bur = jnp.concatenate([bu[k, idx, :] for k in range(nch)], axis=1)
        bui = jnp.concatenate([bu[nch + k, idx, :] for k in range(nch)], axis=1)
        nhr = ar * hr - ai * hi + bur
        nhi = ar * hi + ai * hr + bui
        for k in range(nch):
            bu[k, idx, :] = nhr[:, k * LANES:(k + 1) * LANES]
            bu[nch + k, idx, :] = nhi[:, k * LANES:(k + 1) * LANES]
        return nhr, nhi

    hr, hi = lax.fori_loop(0, tl, step, (hst[:, 0:SSM_N], hst[:, SSM_N:2 * SSM_N]))
    hst[:, 0:SSM_N] = hr
    hst[:, SSM_N:2 * SSM_N] = hi
    hfin_ref[0, :, 0:SSM_N] = hr
    hfin_ref[0, :, SSM_N:2 * SSM_N] = hi
    cm = c_ref[0]
    for b in range(SCAN_BATCH):
        hb = jnp.concatenate([bu[k, b * tl:(b + 1) * tl, :] for k in range(2 * nch)], axis=1)
        y_ref[0, b] = _dot(hb.astype(BF16), cm)


def _s5_scan(u, bb, a, cm, h0, B, L):
    tl = SCAN_TILE
    nt = L // tl
    tile = lambda d, i: i + d * (nt - 1 - 2 * i)
    return pl.pallas_call(
        functools.partial(_s5_scan_body, tl),
        grid=(2, B // SCAN_BATCH, nt),
        in_specs=[
            pl.BlockSpec((SCAN_BATCH, tl, SSM_CH), lambda d, b, i: (b, tile(d, i), 0)),
            pl.BlockSpec((1, SSM_CH, 2 * SSM_N), lambda d, b, i: (d, 0, 0)),
            pl.BlockSpec((1, 1, 2 * SSM_N), lambda d, b, i: (d, 0, 0)),
            pl.BlockSpec((1, 2 * SSM_N, SSM_CH), lambda d, b, i: (d, 0, 0)),
            pl.BlockSpec((1, SCAN_BATCH, 2 * SSM_N), lambda d, b, i: (d, b, 0)),
        ],
        out_specs=[
            pl.BlockSpec((1, SCAN_BATCH, tl, SSM_CH), lambda d, b, i: (d, b, tile(d, i), 0)),
            pl.BlockSpec((1, SCAN_BATCH, 2 * SSM_N), lambda d, b, i: (d, b, 0)),
        ],
        out_shape=[jax.ShapeDtypeStruct((2, B, L, SSM_CH), F32),
                   jax.ShapeDtypeStruct((2, B, 2 * SSM_N), F32)],
        scratch_shapes=[pltpu.VMEM((2 * SSM_N // LANES, SCAN_BATCH * tl, LANES), F32),
                        pltpu.VMEM((SCAN_BATCH, 2 * SSM_N), F32)],
        compiler_params=_cparams(("arbitrary", "arbitrary", "arbitrary")),
        name="s5_scan",
    )(u.reshape(B, L, SSM_CH), bb, a, cm, h0)


def _outproj_body(ona_ref, owa_ref, y_ref, u_ref, dsk_ref, wglu_ref, wout_ref, x_ref, mod_ref, g_ref,
                  wr_ref, x1_ref, h2_ref, aff_ref):
    y = dsk_ref[...] * u_ref[...] + y_ref[0] + y_ref[1]
    zg = _dot(y.astype(BF16), wglu_ref[...])
    oss = zg[:, 0:SSM_CH] * _sigmoid(zg[:, SSM_CH:2 * SSM_CH])
    mix = (_dot(ona_ref[...], wout_ref[0:NA_WIDTH, :])
           + _dot(owa_ref[...], wout_ref[NA_WIDTH:NA_WIDTH + WA_WIDTH, :])
           + _dot(oss.astype(BF16), wout_ref[NA_WIDTH + WA_WIDTH:D_MODEL, :]))
    m = mod_ref[0]
    g1 = m[:, 2 * D_MODEL:3 * D_MODEL]
    sh2 = m[:, 3 * D_MODEL:4 * D_MODEL]
    sc2 = m[:, 4 * D_MODEL:5 * D_MODEL]
    x1 = x_ref[...] + g1 * _rms(mix, g_ref[1:2, :])
    x1_ref[...] = x1
    h2 = _rms(x1, g_ref[2:3, :]) * (1.0 + sc2) + sh2
    h2_ref[...] = h2.astype(BF16)
    logits = jnp.dot(h2, wr_ref[...], preferred_element_type=F32, precision=lax.Precision.HIGHEST)
    lane = lax.broadcasted_iota(jnp.int32, logits.shape, 1)
    logits = jnp.where(lane < N_EXPERTS, logits, NEG_INF)
    mx = jnp.max(logits, axis=-1, keepdims=True)
    p = jnp.exp(logits - mx)
    aff_ref[...] = p / jnp.sum(p, axis=-1, keepdims=True)


def _outproj(ona, owa, y2, u, dsk, wglu_bf, wout_bf, x, mod_l, mod_row_fn, g, wr_pad, L):
    T = x.shape[0]
    tm = TOKEN_TILE
    tiles_per_seq = L // tm
    row = lambda w: pl.BlockSpec((tm, w), lambda i: (i, 0))
    full = lambda a: pl.BlockSpec(a.shape, lambda i: (0,) * a.ndim)
    return pl.pallas_call(
        _outproj_body,
        grid=(T // tm,),
        in_specs=[row(NA_WIDTH), row(WA_WIDTH),
                  pl.BlockSpec((2, tm, SSM_CH), lambda i: (0, i, 0)),
                  row(SSM_CH), full(dsk), full(wglu_bf), full(wout_bf), row(D_MODEL),
                  pl.BlockSpec((1, 1, 6 * D_MODEL), lambda i: (mod_row_fn(i // tiles_per_seq), 0, 0)),
                  full(g), full(wr_pad)],
        out_specs=[row(D_MODEL), row(D_MODEL), row(LANES)],
        out_shape=[jax.ShapeDtypeStruct((T, D_MODEL), F32),
                   jax.ShapeDtypeStruct((T, D_MODEL), BF16),
                   jax.ShapeDtypeStruct((T, LANES), F32)],
        compiler_params=_cparams(("arbitrary",)),
        name="outproj",
    )(ona, owa, y2.reshape(2, T, SSM_CH), u, dsk, wglu_bf, wout_bf, x, mod_l, g, wr_pad)


def _select_body(cap, aff_ref, pos_ref, post_ref, afft_ref):
    aff = aff_ref[0]
    L = aff.shape[0]
    bits = pltpu.bitcast(aff, jnp.int32)
    capf = jnp.float32(cap)

    def bisect(k, thr):
        cand = thr | jnp.left_shift(jnp.int32(1), 30 - k)
        cnt = jnp.sum(jnp.where(bits >= cand, 1.0, 0.0), axis=0, keepdims=True)
        return jnp.where(cnt >= capf, cand, thr)

    thr = lax.fori_loop(0, 31, bisect, jnp.zeros((1, LANES), jnp.int32))
    gt = jnp.where(bits > thr, 1.0, 0.0)
    eq = jnp.where(bits == thr, 1.0, 0.0)
    need = capf - jnp.sum(gt, axis=0, keepdims=True)
    ck = SEL_CHUNK
    tri = (lax.broadcasted_iota(jnp.int32, (ck, ck), 0)
           >= lax.broadcasted_iota(jnp.int32, (ck, ck), 1))
    tri = jnp.where(tri, 1.0, 0.0).astype(BF16)

    def prefix(mask):
        outs = []
        carry = jnp.zeros((1, LANES), F32)
        for j in range(L // ck):
            c = _dot(tri, mask[j * ck:(j + 1) * ck].astype(BF16)) + carry
            outs.append(c)
            carry = c[ck - 1:ck, :]
        return jnp.concatenate(outs, axis=0)

    lane = lax.broadcasted_iota(jnp.int32, (L, LANES), 1)
    tie = jnp.where(prefix(eq) <= need, eq, 0.0)
    sel = jnp.where(lane < N_EXPERTS, gt + tie, 0.0)
    pos = jnp.where(sel > 0.0, prefix(sel) - 1.0, -1.0)
    pos_ref[0] = pos
    post_ref[0] = jnp.transpose(pos)[0:N_EXPERTS, :]
    afft_ref[0] = jnp.transpose(aff)[0:N_EXPERTS, :]


def _select(aff, B, L):
    cap = EC_CAPACITY * L // N_EXPERTS
    return pl.pallas_call(
        functools.partial(_select_body, cap),
        grid=(B,),
        in_specs=[pl.BlockSpec((1, L, LANES), lambda b: (b, 0, 0))],
        out_specs=[pl.BlockSpec((1, L, LANES), lambda b: (b, 0, 0)),
                   pl.BlockSpec((1, N_EXPERTS, L), lambda b: (b, 0, 0)),
                   pl.BlockSpec((1, N_EXPERTS, L), lambda b: (b, 0, 0))],
        out_shape=[jax.ShapeDtypeStruct((B, L, LANES), F32),
                   jax.ShapeDtypeStruct((B, N_EXPERTS, L), F32),
                   jax.ShapeDtypeStruct((B, N_EXPERTS, L), F32)],
        compiler_params=_cparams(("arbitrary",)),
        name="ec_select",
    )(aff.reshape(B, L, LANES))


def _gather_body(cap, group, h_ref, post_ref, afft_ref, xs_ref, gs_ref):
    h = h_ref[0]
    L = h.shape[0]
    slot = lax.broadcasted_iota(jnp.int32, (cap, L), 0).astype(F32)
    for e0 in range(0, N_EXPERTS, group):
        onehots = []
        for e in range(e0, e0 + group):
            hit = post_ref[0, e:e + 1, :] == slot
            gs_ref[e, 0] = jnp.sum(jnp.where(hit, afft_ref[0, e:e + 1, :], 0.0), axis=1, keepdims=True)
            onehots.append(jnp.where(hit, 1.0, 0.0).astype(BF16))
        xs = _dot(jnp.concatenate(onehots, axis=0), h)
        for k in range(group):
            xs_ref[e0 + k, 0] = xs[k * cap:(k + 1) * cap].astype(BF16)


def _gather(h2, post, afft, B, L):
    cap = EC_CAPACITY * L // N_EXPERTS
    group = max(1, min(N_EXPERTS, 512 // cap))
    xs, gs = pl.pallas_call(
        functools.partial(_gather_body, cap, group),
        grid=(B,),
        in_specs=[pl.BlockSpec((1, L, D_MODEL), lambda b: (b, 0, 0)),
                  pl.BlockSpec((1, N_EXPERTS, L), lambda b: (b, 0, 0)),
                  pl.BlockSpec((1, N_EXPERTS, L), lambda b: (b, 0, 0))],
        out_specs=[pl.BlockSpec((N_EXPERTS, 1, cap, D_MODEL), lambda b: (0, b, 0, 0)),
                   pl.BlockSpec((N_EXPERTS, 1, cap, 1), lambda b: (0, b, 0, 0))],
        out_shape=[jax.ShapeDtypeStruct((N_EXPERTS, B, cap, D_MODEL), BF16),
                   jax.ShapeDtypeStruct((N_EXPERTS, B, cap, 1), F32)],
        compiler_params=_cparams(("arbitrary",)),
        name="ec_gather",
    )(h2.reshape(B, L, D_MODEL), post, afft)
    return xs.reshape(N_EXPERTS, B * cap, D_MODEL), gs.reshape(N_EXPERTS, B * cap, 1)


def _ffn_body(xs_ref, gs_ref, wg_ref, wu_ref, wd_ref, y_ref, wgb, wub, wdb):
    @pl.when(pl.program_id(1) == 0)
    def _():
        wgb[...] = wg_ref[0].astype(BF16)
        wub[...] = wu_ref[0].astype(BF16)
        wdb[...] = wd_ref[0].astype(BF16)

    x = xs_ref[0]
    a = _dot(x, wgb[...])
    u = _dot(x, wub[...])
    hm = (a * _sigmoid(a) * u).astype(BF16)
    y_ref[0] = (_dot(hm, wdb[...]) * gs_ref[0]).astype(BF16)


def _ffn(xs, gs, wg, wu, wd):
    R = xs.shape[1]
    rc = FFN_ROWS
    wspec = lambda a, b: pl.BlockSpec((1, a, b), lambda e, j: (e, 0, 0))
    return pl.pallas_call(
        _ffn_body,
        grid=(N_EXPERTS, R // rc),
        in_specs=[pl.BlockSpec((1, rc, D_MODEL), lambda e, j: (e, j, 0)),
                  pl.BlockSpec((1, rc, 1), lambda e, j: (e, j, 0)),
                  wspec(D_MODEL, EXPERT_FF), wspec(D_MODEL, EXPERT_FF), wspec(EXPERT_FF, D_MODEL)],
        out_specs=pl.BlockSpec((1, rc, D_MODEL), lambda e, j: (e, j, 0)),
        out_shape=jax.ShapeDtypeStruct((N_EXPERTS, R, D_MODEL), BF16),
        scratch_shapes=[pltpu.VMEM((D_MODEL, EXPERT_FF), BF16),
                        pltpu.VMEM((D_MODEL, EXPERT_FF), BF16),
                        pltpu.VMEM((EXPERT_FF, D_MODEL), BF16)],
        compiler_params=_cparams(("arbitrary", "arbitrary")),
        name="ec_ffn",
    )(xs, gs, wg, wu, wd)


def _combine_body(cap, pos_ref, y_ref, x1_ref, mod_ref, g_ref, o_ref):
    n = N_EXPERTS * cap
    pos = pos_ref[0]
    col = lax.broadcasted_iota(jnp.int32, (LANES, n), 1)
    expand = jnp.where(col // cap == lax.broadcasted_iota(jnp.int32, (LANES, n), 0), 1.0, 0.0)
    per_col = _dot(pos.astype(BF16), expand.astype(BF16))
    target = (lax.broadcasted_iota(jnp.int32, (1, n), 1) % cap).astype(F32)
    onehot = jnp.where(per_col == target, 1.0, 0.0).astype(BF16)
    f = _dot(onehot, y_ref[...].reshape(n, D_MODEL))
    g2 = mod_ref[0][:, 5 * D_MODEL:6 * D_MODEL]
    o_ref[...] = x1_ref[...] + g2 * _rms(f, g_ref[3:4, :])


def _combine(pos, y, x1, mod_l, mod_row_fn, g, B, L):
    cap = EC_CAPACITY * L // N_EXPERTS
    tq = TOKEN_TILE
    nq = L // tq
    return pl.pallas_call(
        functools.partial(_combine_body, cap),
        grid=(B, nq),
        in_specs=[pl.BlockSpec((1, tq, LANES), lambda b, i: (b, i, 0)),
                  pl.BlockSpec((N_EXPERTS, cap, D_MODEL), lambda b, i: (0, b, 0)),
                  pl.BlockSpec((tq, D_MODEL), lambda b, i: (b * nq + i, 0)),
                  pl.BlockSpec((1, 1, 6 * D_MODEL), lambda b, i: (mod_row_fn(b), 0, 0)),
                  pl.BlockSpec((4, D_MODEL), lambda b, i: (0, 0))],
        out_specs=pl.BlockSpec((tq, D_MODEL), lambda b, i: (b * nq + i, 0)),
        out_shape=jax.ShapeDtypeStruct((B * L, D_MODEL), F32),
        compiler_params=_cparams(("arbitrary", "arbitrary")),
        name="ec_combine",
    )(pos, y, x1, mod_l, g)


def _rope_tables(L):
    t = jnp.arange(L)
    row = (t // GRID_W).astype(F32)
    col = (t % GRID_W).astype(F32)
    half = HEAD_DIM // 4
    inv = ROPE_BASE ** (-jnp.arange(half, dtype=F32) / half)
    d = np.arange(LANES) % HEAD_DIM
    use_col = (d // (HEAD_DIM // 2)) == 1
    pos = jnp.where(use_col[None, :], col[:, None], row[:, None])
    ang = pos * inv[d % half][None, :]
    sign = np.where((d % (HEAD_DIM // 2)) < half, -1.0, 1.0).astype(np.float32)
    return jnp.cos(ang), jnp.sin(ang) * sign[None, :]


def _permute_wa_heads(a, axis, start):
    idx = np.arange(a.shape[axis])
    blk = np.concatenate([np.arange(h * HEAD_DIM, (h + 1) * HEAD_DIM) for h in WA_HEAD_ORDER])
    idx[start:start + WA_WIDTH] = start + blk
    return jnp.take(a, idx, axis=axis)


def kernel(x_prompt, x_sample, c, cache_na_k, cache_na_v, cache_wa_k, cache_wa_v, state_ssm, c_ctx, w_ada, b_ada, norm_g, w_in, w_out, na_rpb, wa_sink, ssm_lambda_re, ssm_lambda_im, ssm_log_dt, ssm_b_re, ssm_b_im, ssm_c_re, ssm_c_im, ssm_d, w_glu, w_router, w_exp_gate, w_exp_up, w_exp_down):
    Bc, Lc, _ = x_prompt.shape
    Bs, Ls, _ = x_sample.shape
    ctx_row = Bs
    n_cond = ((Bs + 1 + SUBLANES - 1) // SUBLANES) * SUBLANES
    cond = jnp.zeros((n_cond, D_MODEL), F32).at[0:Bs].set(c).at[ctx_row].set(c_ctx)
    mod = _modulation(cond, w_ada, b_ada)
    mod = mod.reshape(DEPTH, n_cond, 1, 6 * D_MODEL)

    w_in_bf = _permute_wa_heads(w_in, 2, C_WAQ).astype(BF16)
    w_out_bf = _permute_wa_heads(w_out, 1, NA_WIDTH).astype(BF16)
    w_glu_bf = w_glu.astype(BF16)
    wr_pad = jnp.pad(w_router, ((0, 0), (0, 0), (0, LANES - N_EXPERTS)))
    s5_a, s5_bb = _s5_prep(ssm_lambda_re, ssm_lambda_im, ssm_log_dt, ssm_b_re, ssm_b_im)
    s5_c = _s5_c_matrix(ssm_c_re, ssm_c_im)
    rope_tabs = _rope_tables(Ls)
    na_rows = Ls // GRID_W

    ctx_mod_row = lambda b: ctx_row + 0 * b
    lat_mod_row = lambda b: b

    xp = x_prompt.reshape(Bc * Lc, D_MODEL)
    xs = x_sample.reshape(Bs * Ls, D_MODEL)
    new_na_k, new_na_v, new_wa_k, new_wa_v, new_ssm = [], [], [], [], []
    for l in range(DEPTH):
        g = norm_g[l]
        dsk = ssm_d[l].reshape(1, SSM_CH)
        sl2 = slice(2 * l, 2 * l + 2)
        qna, kna, vna, qwa, kwa, vwa, u = _inproj(xp, mod[l], ctx_mod_row, g, w_in_bf[l], Lc, None)
        ona, owa = _attn_ctx(wa_sink[l], qna, kna, vna, qwa, kwa, vwa, Bc, Lc)
        y2, hfin = _s5_scan(u, s5_bb[sl2], s5_a[sl2], s5_c[sl2],
                            jnp.zeros((2, Bc, 2 * SSM_N), F32), Bc, Lc)
        x1, h2, aff = _outproj(ona, owa, y2, u, dsk, w_glu_bf[l], w_out_bf[l], xp, mod[l], ctx_mod_row,
                               g, wr_pad[l], Lc)
        pos, post, afft = _select(aff, Bc, Lc)
        xg, gs = _gather(h2, post, afft, Bc, Lc)
        yy = _ffn(xg, gs, w_exp_gate[l], w_exp_up[l], w_exp_down[l])
        xp = _combine(pos, yy, x1, mod[l], ctx_mod_row, g, Bc, Lc)
        new_na_k.append(kna.reshape(Bc, Lc, NA_HEADS, HEAD_DIM))
        new_na_v.append(vna.reshape(Bc, Lc, NA_HEADS, HEAD_DIM))
        new_wa_k.append(kwa.reshape(Bc, Lc, WA_KV_HEADS, HEAD_DIM))
        new_wa_v.append(vwa.reshape(Bc, Lc, WA_KV_HEADS, HEAD_DIM))
        new_ssm.append(jnp.transpose(hfin.reshape(2, Bc, 2, SSM_GROUPS, SSM_STATE), (1, 0, 2, 3, 4)))
        qna, kna, vna, qwa, kwa, vwa, u = _inproj(xs, mod[l], lat_mod_row, g, w_in_bf[l], Ls, rope_tabs)
        ona = _attn_na(qna, kna, vna, cache_na_k, cache_na_v, _na_bias(na_rpb[l], na_rows), l, Bs, Ls)
        owa = _attn_wa(wa_sink[l], qwa, kwa, vwa, cache_wa_k, cache_wa_v, l, Bs, Ls)
        h0 = jnp.transpose(state_ssm[:, l].reshape(Bs, 2, 2 * SSM_N), (1, 0, 2))
        y2, _ = _s5_scan(u, s5_bb[sl2], s5_a[sl2], s5_c[sl2], h0, Bs, Ls)
        x1, h2, aff = _outproj(ona, owa, y2, u, dsk, w_glu_bf[l], w_out_bf[l], xs, mod[l], lat_mod_row,
                               g, wr_pad[l], Ls)
        pos, post, afft = _select(aff, Bs, Ls)
        xg, gs = _gather(h2, post, afft, Bs, Ls)
        yy = _ffn(xg, gs, w_exp_gate[l], w_exp_up[l], w_exp_down[l])
        xs = _combine(pos, yy, x1, mod[l], lat_mod_row, g, Bs, Ls)
    return (xp.reshape(Bc, Lc, D_MODEL), xs.reshape(Bs, Ls, D_MODEL),
            jnp.stack(new_na_k, axis=1), jnp.stack(new_na_v, axis=1),
            jnp.stack(new_wa_k, axis=1), jnp.stack(new_wa_v, axis=1),
            jnp.stack(new_ssm, axis=1))
```

```python
import functools
import math

import numpy as np
import jax
import jax.numpy as jnp
from jax import lax
from jax.experimental import pallas as pl
from jax.experimental.pallas import tpu as pltpu

F32 = jnp.float32
BF16 = jnp.bfloat16

D_MODEL = 1024
DEPTH = 2
GRID_W = 64
HEAD_DIM = 64
NA_HEADS = 6
NA_KH = 8
NA_KW = 16
WA_HEADS = 6
WA_KV_HEADS = 2
WA_WINDOW = 128
WA_BLOCK = 128
SSM_CH = 256
SSM_GROUP_CH = 16
SSM_GROUPS = 16
SSM_STATE = 64
SSM_N = SSM_GROUPS * SSM_STATE
NA_WIDTH = NA_HEADS * HEAD_DIM
WA_WIDTH = WA_HEADS * HEAD_DIM
WA_KV_WIDTH = WA_KV_HEADS * HEAD_DIM
IN_COLS = 2048
N_EXPERTS = 16
EC_CAPACITY = 2
EXPERT_FF = 1024
ROPE_BASE = 10000.0
RMS_EPS = 1e-6
NEG_INF = -1e30
ATTN_SCALE = HEAD_DIM ** -0.5

LANES = 128
SUBLANES = 8
VMEM_LIMIT = 48 * 1024 * 1024

C_NAQ, C_NAK, C_NAV, C_WAQ, C_WAK, C_WAV, C_SSU = 0, 384, 768, 1152, 1536, 1664, 1792
WA_HEAD_ORDER = (0, 3, 1, 4, 2, 5)

NA_QROWS = 4
NA_KROWS = NA_QROWS + NA_KH
TOKEN_TILE = 256
SCAN_TILE = 128
SCAN_BATCH = SUBLANES
FFN_ROWS = 512
SEL_CHUNK = 256


def _cparams(sem):
    return pltpu.CompilerParams(dimension_semantics=sem, vmem_limit_bytes=VMEM_LIMIT)


def _sigmoid(x):
    return 1.0 / (1.0 + jnp.exp(-x))


def _rms(x, g):
    ms = jnp.mean(x * x, axis=-1, keepdims=True)
    return x * lax.rsqrt(ms + RMS_EPS) * g


def _dot(a, b):
    return jnp.dot(a, b, preferred_element_type=F32)


def _dot_nt(a, b):
    return lax.dot_general(a, b, (((1,), (1,)), ((), ())), preferred_element_type=F32)


def _mod_body(c_ref, w_ref, b_ref, o_ref):
    c = c_ref[...]
    s = c * _sigmoid(c)
    o_ref[0] = _dot(s.astype(BF16), w_ref[0].astype(BF16)) + b_ref[0]


def _modulation(cond, w_ada, b_ada):
    n = cond.shape[0]
    tn = 1024
    return pl.pallas_call(
        _mod_body,
        grid=(DEPTH, 6 * D_MODEL // tn),
        in_specs=[
            pl.BlockSpec((n, D_MODEL), lambda l, j: (0, 0)),
            pl.BlockSpec((1, D_MODEL, tn), lambda l, j: (l, 0, j)),
            pl.BlockSpec((1, 1, tn), lambda l, j: (l, 0, j)),
        ],
        out_specs=pl.BlockSpec((1, n, tn), lambda l, j: (l, 0, j)),
        out_shape=jax.ShapeDtypeStruct((DEPTH, n, 6 * D_MODEL), F32),
        compiler_params=_cparams(("arbitrary", "arbitrary")),
        name="modulation",
    )(cond, w_ada, b_ada.reshape(DEPTH, 1, 6 * D_MODEL))


def _rope_chunk(x, cos, sin_signed):
    lane = lax.broadcasted_iota(jnp.int32, x.shape, 1)
    first = (lane % 32) < 16
    rot = jnp.where(first, pltpu.roll(x, LANES - 16, 1), pltpu.roll(x, 16, 1))
    return x * cos + rot * sin_signed


def _inproj_body(rope, x_ref, mod_ref, g_ref, w_ref, *rest):
    if rope:
        cos_ref, sin_ref = rest[:2]
        rest = rest[2:]
    qna_ref, kna_ref, vna_ref, qwa_ref, kwa_ref, vwa_ref, u_ref = rest
    x = x_ref[...]
    m = mod_ref[0]
    sh1 = m[:, 0:D_MODEL]
    sc1 = m[:, D_MODEL:2 * D_MODEL]
    h = _rms(x, g_ref[0:1, :]) * (1.0 + sc1) + sh1
    z = _dot(h.astype(BF16), w_ref[...])
    qna_ref[...] = (z[:, C_NAQ:C_NAK] * ATTN_SCALE).astype(BF16)
    kna_ref[...] = z[:, C_NAK:C_NAV]
    vna_ref[...] = z[:, C_NAV:C_WAQ]
    vwa_ref[...] = z[:, C_WAV:C_SSU]
    u_ref[...] = z[:, C_SSU:IN_COLS]
    if rope:
        cos = cos_ref[...]
        sin = sin_ref[...]
        for j in range(WA_WIDTH // LANES):
            qc = _rope_chunk(z[:, C_WAQ + j * LANES:C_WAQ + (j + 1) * LANES], cos, sin)
            qwa_ref[:, j * LANES:(j + 1) * LANES] = (qc * ATTN_SCALE).astype(BF16)
        kwa_ref[...] = _rope_chunk(z[:, C_WAK:C_WAV], cos, sin)
    else:
        qwa_ref[...] = (z[:, C_WAQ:C_WAK] * ATTN_SCALE).astype(BF16)
        kwa_ref[...] = z[:, C_WAK:C_WAV]


def _inproj(x, mod_l, mod_row_fn, g, w_bf, L, rope_tabs):
    T = x.shape[0]
    tm = TOKEN_TILE
    tiles_per_seq = L // tm
    in_specs = [
        pl.BlockSpec((tm, D_MODEL), lambda i: (i, 0)),
        pl.BlockSpec((1, 1, 6 * D_MODEL), lambda i: (mod_row_fn(i // tiles_per_seq), 0, 0)),
        pl.BlockSpec((4, D_MODEL), lambda i: (0, 0)),
        pl.BlockSpec((D_MODEL, IN_COLS), lambda i: (0, 0)),
    ]
    args = [x, mod_l, g, w_bf]
    if rope_tabs is not None:
        in_specs += [pl.BlockSpec((tm, LANES), lambda i: (i % tiles_per_seq, 0))] * 2
        args += list(rope_tabs)
    widths = (NA_WIDTH, NA_WIDTH, NA_WIDTH, WA_WIDTH, WA_KV_WIDTH, WA_KV_WIDTH, SSM_CH)
    dtypes = (BF16, F32, F32, BF16, F32, F32, F32)
    return pl.pallas_call(
        functools.partial(_inproj_body, rope_tabs is not None),
        grid=(T // tm,),
        in_specs=in_specs,
        out_specs=[pl.BlockSpec((tm, w), lambda i: (i, 0)) for w in widths],
        out_shape=[jax.ShapeDtypeStruct((T, w), dt) for w, dt in zip(widths, dtypes)],
        compiler_params=_cparams(("arbitrary",)),
        name="inproj",
    )(*args)


def _lane_lo(shape):
    return lax.broadcasted_iota(jnp.int32, shape, len(shape) - 1) < HEAD_DIM


def _keep_half(q, lo, half):
    keep = jnp.where(lo, 1.0 - half, 0.0 + half).astype(q.dtype)
    return q * keep


def _attn_ctx_body(sink_ref, qna, kna, vna, qwa, kwa, vwa, ona, owa):
    L = qna.shape[1]
    lo = _lane_lo((L, LANES))
    for j in range(NA_WIDTH // LANES):
        sl = slice(j * LANES, (j + 1) * LANES)
        q2 = qna[0, :, sl]
        k2 = kna[0, :, sl].astype(BF16)
        v2 = vna[0, :, sl].astype(BF16)
        halves = []
        for half in range(2):
            qm = _keep_half(q2, lo, half)
            s = _dot_nt(qm, k2)
            m = jnp.max(s, axis=-1, keepdims=True)
            p = jnp.exp(s - m)
            l = jnp.sum(p, axis=-1, keepdims=True)
            halves.append(_dot(p.astype(BF16), v2) / l)
        ona[0, :, sl] = jnp.where(lo, halves[0], halves[1]).astype(BF16)
    kw = kwa[0].astype(BF16)
    vw = vwa[0].astype(BF16)
    for j in range(WA_WIDTH // LANES):
        sl = slice(j * LANES, (j + 1) * LANES)
        q2 = qwa[0, :, sl]
        halves = []
        for half in range(2):
            head = WA_HEAD_ORDER[2 * j + half]
            qm = _keep_half(q2, lo, half)
            s = _dot_nt(qm, kw)
            sk = sink_ref[head]
            m = jnp.maximum(jnp.max(s, axis=-1, keepdims=True), sk)
            p = jnp.exp(s - m)
            l = jnp.sum(p, axis=-1, keepdims=True) + jnp.exp(sk - m)
            halves.append(_dot(p.astype(BF16), vw) / l)
        owa[0, :, sl] = jnp.where(lo, halves[0], halves[1]).astype(BF16)


def _attn_ctx(sink, qna, kna, vna, qwa, kwa, vwa, B, L):
    def spec(w):
        return pl.BlockSpec((1, L, w), lambda b: (b, 0, 0))

    r3 = lambda a: a.reshape(B, L, a.shape[-1])
    ona, owa = pl.pallas_call(
        _attn_ctx_body,
        grid=(B,),
        in_specs=[pl.BlockSpec(memory_space=pltpu.SMEM),
                  spec(NA_WIDTH), spec(NA_WIDTH), spec(NA_WIDTH),
                  spec(WA_WIDTH), spec(WA_KV_WIDTH), spec(WA_KV_WIDTH)],
        out_specs=[spec(NA_WIDTH), spec(WA_WIDTH)],
        out_shape=[jax.ShapeDtypeStruct((B, L, NA_WIDTH), BF16),
                   jax.ShapeDtypeStruct((B, L, WA_WIDTH), BF16)],
        compiler_params=_cparams(("arbitrary",)),
        name="attn_ctx",
    )(sink, r3(qna), r3(kna), r3(vna), r3(qwa), r3(kwa), r3(vwa))
    return ona.reshape(B * L, NA_WIDTH), owa.reshape(B * L, WA_WIDTH)


def _na_key_start(rb, rows):
    return jnp.clip(rb * NA_QROWS - NA_KH // 2, 0, rows - NA_KROWS)


def _attn_na_body(rows, q_ref, k_ref, v_ref, kc_ref, vc_ref, bias_ref, o_ref):
    rb = pl.program_id(1)
    nq = NA_QROWS * GRID_W
    nk = NA_KROWS * GRID_W
    start = pl.multiple_of(_na_key_start(rb, rows) * GRID_W, GRID_W)
    lo = _lane_lo((nq, LANES))
    for j in range(NA_WIDTH // LANES):
        sl = slice(j * LANES, (j + 1) * LANES)
        q2 = q_ref[0, :, sl]
        kl = k_ref[0, pl.ds(start, nk), sl].astype(BF16)
        vl = v_ref[0, pl.ds(start, nk), sl].astype(BF16)
        kc = kc_ref[0, 0, :, sl].astype(BF16)
        vc = vc_ref[0, 0, :, sl].astype(BF16)
        halves = []
        for half in range(2):
            qm = _keep_half(q2, lo, half)
            s_loc = _dot_nt(qm, kl) + bias_ref[0, 2 * j + half]
            s_ctx = _dot_nt(qm, kc)
            m = jnp.maximum(jnp.max(s_loc, axis=-1, keepdims=True),
                            jnp.max(s_ctx, axis=-1, keepdims=True))
            p_loc = jnp.exp(s_loc - m)
            p_ctx = jnp.exp(s_ctx - m)
            l = jnp.sum(p_loc, axis=-1, keepdims=True) + jnp.sum(p_ctx, axis=-1, keepdims=True)
            o2 = _dot(p_loc.astype(BF16), vl) + _dot(p_ctx.astype(BF16), vc)
            halves.append(o2 / l)
        o_ref[0, :, sl] = jnp.where(lo, halves[0], halves[1]).astype(BF16)


def _na_bias(rpb_l, rows):
    nrb = rows // NA_QROWS
    pats = (0, 1, nrb - 1)
    cq = np.arange(GRID_W)
    col_start = np.clip(cq - NA_KW // 2, 0, GRID_W - NA_KW)
    col_ok = (cq[None, :] >= col_start[:, None]) & (cq[None, :] < col_start[:, None] + NA_KW)
    dcol = np.clip(cq[None, :] - cq[:, None], -(NA_KW - 1), NA_KW - 1) + (NA_KW - 1)
    pick = (dcol[None] == np.arange(2 * NA_KW - 1)[:, None, None]).astype(np.float32)
    tiles = jnp.einsum('hrd,dqc->hrqc', rpb_l.astype(F32), pick, precision=lax.Precision.HIGHEST)
    tiles = jnp.where(col_ok[None, None], tiles, NEG_INF)
    masked = jnp.full((NA_HEADS, GRID_W, GRID_W), NEG_INF, F32)
    out = []
    for rb in pats:
        ks = int(np.clip(rb * NA_QROWS - NA_KH // 2, 0, rows - NA_KROWS))
        stripes = []
        for i in range(NA_QROWS):
            qr = rb * NA_QROWS + i
            rs = int(np.clip(qr - NA_KH // 2, 0, rows - NA_KH))
            blocks = []
            for jj in range(NA_KROWS):
                kr = ks + jj
                blocks.append(tiles[:, kr - qr + NA_KH - 1] if rs <= kr < rs + NA_KH else masked)
            stripes.append(jnp.concatenate(blocks, axis=-1))
        out.append(jnp.concatenate(stripes, axis=-2))
    return jnp.stack(out, axis=0)


def _attn_na(q, k, v, cache_k, cache_v, bias, l, B, L):
    rows = L // GRID_W
    nrb = rows // NA_QROWS
    nq = NA_QROWS * GRID_W
    P = cache_k.shape[2]
    pat = lambda rb: jnp.where(rb == 0, 0, jnp.where(rb == nrb - 1, 2, 1))
    r3 = lambda a: a.reshape(B, L, a.shape[-1])
    ck = cache_k.reshape(B, DEPTH, P, NA_WIDTH)
    cv = cache_v.reshape(B, DEPTH, P, NA_WIDTH)
    o = pl.pallas_call(
        functools.partial(_attn_na_body, rows),
        grid=(B, nrb),
        in_specs=[
            pl.BlockSpec((1, nq, NA_WIDTH), lambda b, r: (b, r, 0)),
            pl.BlockSpec((1, L, NA_WIDTH), lambda b, r: (b, 0, 0)),
            pl.BlockSpec((1, L, NA_WIDTH), lambda b, r: (b, 0, 0)),
            pl.BlockSpec((1, 1, P, NA_WIDTH), lambda b, r: (b, l, 0, 0)),
            pl.BlockSpec((1, 1, P, NA_WIDTH), lambda b, r: (b, l, 0, 0)),
            pl.BlockSpec((1, NA_HEADS, nq, NA_KROWS * GRID_W), lambda b, r: (pat(r), 0, 0, 0)),
        ],
        out_specs=pl.BlockSpec((1, nq, NA_WIDTH), lambda b, r: (b, r, 0)),
        out_shape=jax.ShapeDtypeStruct((B, L, NA_WIDTH), BF16),
        compiler_params=_cparams(("arbitrary", "arbitrary")),
        name="attn_na",
    )(r3(q), r3(k), r3(v), ck, cv, bias)
    return o.reshape(B * L, NA_WIDTH)


def _attn_wa_body(L, sink_ref, q_ref, k_ref, v_ref, kc_ref, vc_ref, o_ref):
    n = pl.program_id(1)
    nk = 3 * WA_BLOCK
    npair = WA_WIDTH // LANES
    start = pl.multiple_of(jnp.clip((n - 1) * WA_BLOCK, 0, L - nk), WA_BLOCK)
    kl = k_ref[0, pl.ds(start, nk), :].astype(BF16)
    vl = v_ref[0, pl.ds(start, nk), :].astype(BF16)
    kc = kc_ref[0, 0].astype(BF16)
    vc = vc_ref[0, 0].astype(BF16)
    rows = npair * WA_BLOCK
    qpos = n * WA_BLOCK + lax.broadcasted_iota(jnp.int32, (rows, nk), 0) % WA_BLOCK
    kpos = start + lax.broadcasted_iota(jnp.int32, (rows, nk), 1)
    in_win = jnp.abs(kpos - qpos) <= WA_WINDOW
    lo = _lane_lo((rows, LANES))
    q_all = jnp.concatenate([q_ref[0, :, j * LANES:(j + 1) * LANES] for j in range(npair)], axis=0)
    blk = lax.broadcasted_iota(jnp.int32, (rows, 1), 0) // WA_BLOCK
    outs = []
    for g in range(WA_KV_HEADS):
        qm = _keep_half(q_all, lo, g)
        sk = jnp.zeros((rows, 1), F32)
        for j in range(npair):
            sk = jnp.where(blk == j, sink_ref[WA_HEAD_ORDER[2 * j + g]], sk)
        s_loc = jnp.where(in_win, _dot_nt(qm, kl), NEG_INF)
        s_ctx = _dot_nt(qm, kc)
        m = jnp.maximum(jnp.maximum(jnp.max(s_loc, axis=-1, keepdims=True),
                                    jnp.max(s_ctx, axis=-1, keepdims=True)), sk)
        p_loc = jnp.exp(s_loc - m)
        p_ctx = jnp.exp(s_ctx - m)
        l = (jnp.sum(p_loc, axis=-1, keepdims=True) + jnp.sum(p_ctx, axis=-1, keepdims=True)
             + jnp.exp(sk - m))
        o2 = _dot(p_loc.astype(BF16), vl) + _dot(p_ctx.astype(BF16), vc)
        outs.append(o2 / l)
    o_all = jnp.where(lo, outs[0], outs[1]).astype(BF16)
    for j in range(npair):
        o_ref[0, :, j * LANES:(j + 1) * LANES] = o_all[j * WA_BLOCK:(j + 1) * WA_BLOCK]


def _attn_wa(sink, q, k, v, cache_k, cache_v, l, B, L):
    nb = L // WA_BLOCK
    P = cache_k.shape[2]
    r3 = lambda a: a.reshape(B, L, a.shape[-1])
    ck = cache_k.reshape(B, DEPTH, P, WA_KV_WIDTH)
    cv = cache_v.reshape(B, DEPTH, P, WA_KV_WIDTH)
    o = pl.pallas_call(
        functools.partial(_attn_wa_body, L),
        grid=(B, nb),
        in_specs=[
            pl.BlockSpec(memory_space=pltpu.SMEM),
            pl.BlockSpec((1, WA_BLOCK, WA_WIDTH), lambda b, n: (b, n, 0)),
            pl.BlockSpec((1, L, WA_KV_WIDTH), lambda b, n: (b, 0, 0)),
            pl.BlockSpec((1, L, WA_KV_WIDTH), lambda b, n: (b, 0, 0)),
            pl.BlockSpec((1, 1, P, WA_KV_WIDTH), lambda b, n: (b, l, 0, 0)),
            pl.BlockSpec((1, 1, P, WA_KV_WIDTH), lambda b, n: (b, l, 0, 0)),
        ],
        out_specs=pl.BlockSpec((1, WA_BLOCK, WA_WIDTH), lambda b, n: (b, n, 0)),
        out_shape=jax.ShapeDtypeStruct((B, L, WA_WIDTH), BF16),
        compiler_params=_cparams(("arbitrary", "arbitrary")),
        name="attn_wa",
    )(sink, r3(q), r3(k), r3(v), ck, cv)
    return o.reshape(B * L, WA_WIDTH)


def _s5_prep_body(lr_ref, li_ref, ldt_ref, br_ref, bi_ref, a_ref, bb_ref):
    lr = lr_ref[0]
    li = li_ref[0]
    dt = jnp.exp(ldt_ref[0])
    mag = jnp.exp(lr * dt)
    ar = mag * jnp.cos(li * dt)
    ai = mag * jnp.sin(li * dt)
    den = lr * lr + li * li
    nr = ar - 1.0
    fr = (nr * lr + ai * li) / den
    fi = (ai * lr - nr * li) / den
    a_ref[0, :, 0:SSM_N] = ar
    a_ref[0, :, SSM_N:2 * SSM_N] = ai
    br = br_ref[0]
    bi = bi_ref[0]
    bb_ref[0, :, 0:SSM_N] = (fr * br - fi * bi).astype(BF16)
    bb_ref[0, :, SSM_N:2 * SSM_N] = (fr * bi + fi * br).astype(BF16)


def _s5_prep(lam_re, lam_im, log_dt, b_re, b_im):
    n = DEPTH * 2
    eye = jnp.eye(SSM_GROUPS, dtype=F32)

    def blockdiag_b(b):
        return jnp.einsum('ngph,gk->nghkp', b.reshape(n, SSM_GROUPS, SSM_STATE, SSM_GROUP_CH), eye
                          ).reshape(n, SSM_CH, SSM_N)

    lr = lam_re.reshape(n, 1, SSM_N)
    li = lam_im.reshape(n, 1, SSM_N)
    ldt = jnp.repeat(log_dt.reshape(n, SSM_GROUPS), SSM_STATE, axis=-1).reshape(n, 1, SSM_N)
    vec = pl.BlockSpec((1, 1, SSM_N), lambda i: (i, 0, 0))
    mat = pl.BlockSpec((1, SSM_CH, SSM_N), lambda i: (i, 0, 0))
    return pl.pallas_call(
        _s5_prep_body,
        grid=(n,),
        in_specs=[vec, vec, vec, mat, mat],
        out_specs=[pl.BlockSpec((1, 1, 2 * SSM_N), lambda i: (i, 0, 0)),
                   pl.BlockSpec((1, SSM_CH, 2 * SSM_N), lambda i: (i, 0, 0))],
        out_shape=[jax.ShapeDtypeStruct((n, 1, 2 * SSM_N), F32),
                   jax.ShapeDtypeStruct((n, SSM_CH, 2 * SSM_N), BF16)],
        compiler_params=_cparams(("arbitrary",)),
        name="s5_prep",
    )(lr, li, ldt, blockdiag_b(b_re), blockdiag_b(b_im))


def _s5_c_matrix(c_re, c_im):
    n = DEPTH * 2
    eye = jnp.eye(SSM_GROUPS, dtype=F32)

    def blk(c):
        return jnp.einsum('nghp,gk->ngpkh', c.reshape(n, SSM_GROUPS, SSM_GROUP_CH, SSM_STATE), eye
                          ).reshape(n, SSM_N, SSM_CH)

    return jnp.concatenate([blk(c_re), -blk(c_im)], axis=1).astype(BF16)


def _s5_scan_body(tl, u_ref, bb_ref, a_ref, c_ref, h0_ref, y_ref, hfin_ref, bu, hst):
    d = pl.program_id(0)
    i = pl.program_id(2)

    @pl.when(i == 0)
    def _():
        hst[...] = h0_ref[0]

    u_tb = pltpu.einshape("btc->(tb)c", u_ref[...])
    bu[...] = _dot(u_tb.astype(BF16), bb_ref[0])
    a = a_ref[0]
    ar = jnp.broadcast_to(a[:, 0:SSM_N], (SCAN_BATCH, SSM_N))
    ai = jnp.broadcast_to(a[:, SSM_N:2 * SSM_N], (SCAN_BATCH, SSM_N))

    def step(s, carry):
        hr, hi = carry
        t = s + d * (tl - 1 - 2 * s)
        rows = pl.ds(pl.multiple_of(t * SCAN_BATCH, SCAN_BATCH), SCAN_BATCH)
        nhr = ar * hr - ai * hi + bu[rows, 0:SSM_N]
        nhi = ar * hi + ai * hr + bu[rows, SSM_N:2 * SSM_N]
        bu[rows, 0:SSM_N] = nhr
        bu[rows, SSM_N:2 * SSM_N] = nhi
        return nhr, nhi

    hr, hi = lax.fori_loop(0, tl, step, (hst[:, 0:SSM_N], hst[:, SSM_N:2 * SSM_N]), unroll=2)
    hst[:, 0:SSM_N] = hr
    hst[:, SSM_N:2 * SSM_N] = hi
    hfin_ref[0, :, 0:SSM_N] = hr
    hfin_ref[0, :, SSM_N:2 * SSM_N] = hi
    y_tb = _dot(bu[...].astype(BF16), c_ref[0])
    y_ref[0] = pltpu.einshape("(tb)c->btc", y_tb, b=SCAN_BATCH)


def _s5_scan(u, bb, a, cm, h0, B, L):
    tl = SCAN_TILE
    nt = L // tl
    tile = lambda d, i: i + d * (nt - 1 - 2 * i)
    return pl.pallas_call(
        functools.partial(_s5_scan_body, tl),
        grid=(2, B // SCAN_BATCH, nt),
        in_specs=[
            pl.BlockSpec((SCAN_BATCH, tl, SSM_CH), lambda d, b, i: (b, tile(d, i), 0)),
            pl.BlockSpec((1, SSM_CH, 2 * SSM_N), lambda d, b, i: (d, 0, 0)),
            pl.BlockSpec((1, 1, 2 * SSM_N), lambda d, b, i: (d, 0, 0)),
            pl.BlockSpec((1, 2 * SSM_N, SSM_CH), lambda d, b, i: (d, 0, 0)),
            pl.BlockSpec((1, SCAN_BATCH, 2 * SSM_N), lambda d, b, i: (d, b, 0)),
        ],
        out_specs=[
            pl.BlockSpec((1, SCAN_BATCH, tl, SSM_CH), lambda d, b, i: (d, b, tile(d, i), 0)),
            pl.BlockSpec((1, SCAN_BATCH, 2 * SSM_N), lambda d, b, i: (d, b, 0)),
        ],
        out_shape=[jax.ShapeDtypeStruct((2, B, L, SSM_CH), F32),
                   jax.ShapeDtypeStruct((2, B, 2 * SSM_N), F32)],
        scratch_shapes=[pltpu.VMEM((SCAN_BATCH * tl, 2 * SSM_N), F32),
                        pltpu.VMEM((SCAN_BATCH, 2 * SSM_N), F32)],
        compiler_params=_cparams(("arbitrary", "arbitrary", "arbitrary")),
        name="s5_scan",
    )(u.reshape(B, L, SSM_CH), bb, a, cm, h0)


def _outproj_body(ona_ref, owa_ref, y_ref, u_ref, dsk_ref, wglu_ref, wout_ref, x_ref, mod_ref, g_ref,
                  wr_ref, x1_ref, h2_ref, aff_ref):
    y = dsk_ref[...] * u_ref[...] + y_ref[0] + y_ref[1]
    zg = _dot(y.astype(BF16), wglu_ref[...])
    oss = zg[:, 0:SSM_CH] * _sigmoid(zg[:, SSM_CH:2 * SSM_CH])
    mix = (_dot(ona_ref[...], wout_ref[0:NA_WIDTH, :])
           + _dot(owa_ref[...], wout_ref[NA_WIDTH:NA_WIDTH + WA_WIDTH, :])
           + _dot(oss.astype(BF16), wout_ref[NA_WIDTH + WA_WIDTH:D_MODEL, :]))
    m = mod_ref[0]
    g1 = m[:, 2 * D_MODEL:3 * D_MODEL]
    sh2 = m[:, 3 * D_MODEL:4 * D_MODEL]
    sc2 = m[:, 4 * D_MODEL:5 * D_MODEL]
    x1 = x_ref[...] + g1 * _rms(mix, g_ref[1:2, :])
    x1_ref[...] = x1
    h2 = _rms(x1, g_ref[2:3, :]) * (1.0 + sc2) + sh2
    h2_hi = h2.astype(BF16)
    h2_ref[...] = h2_hi
    h2_lo = (h2 - h2_hi.astype(F32)).astype(BF16)
    r_hi = _dot(h2_hi, wr_ref[...])
    logits = r_hi + pltpu.roll(r_hi, LANES - N_EXPERTS, 1) + _dot(h2_lo, wr_ref[...])
    lane = lax.broadcasted_iota(jnp.int32, logits.shape, 1)
    logits = jnp.where(lane < N_EXPERTS, logits, NEG_INF)
    mx = jnp.max(logits, axis=-1, keepdims=True)
    p = jnp.exp(logits - mx)
    aff_ref[...] = p / jnp.sum(p, axis=-1, keepdims=True)


def _outproj(ona, owa, y2, u, dsk, wglu_bf, wout_bf, x, mod_l, mod_row_fn, g, wr_pad, L):
    T = x.shape[0]
    tm = TOKEN_TILE
    tiles_per_seq = L // tm
    row = lambda w: pl.BlockSpec((tm, w), lambda i: (i, 0))
    full = lambda a: pl.BlockSpec(a.shape, lambda i: (0,) * a.ndim)
    return pl.pallas_call(
        _outproj_body,
        grid=(T // tm,),
        in_specs=[row(NA_WIDTH), row(WA_WIDTH),
                  pl.BlockSpec((2, tm, SSM_CH), lambda i: (0, i, 0)),
                  row(SSM_CH), full(dsk), full(wglu_bf), full(wout_bf), row(D_MODEL),
                  pl.BlockSpec((1, 1, 6 * D_MODEL), lambda i: (mod_row_fn(i // tiles_per_seq), 0, 0)),
                  full(g), full(wr_pad)],
        out_specs=[row(D_MODEL), row(D_MODEL), row(LANES)],
        out_shape=[jax.ShapeDtypeStruct((T, D_MODEL), F32),
                   jax.ShapeDtypeStruct((T, D_MODEL), BF16),
                   jax.ShapeDtypeStruct((T, LANES), F32)],
        compiler_params=_cparams(("arbitrary",)),
        name="outproj",
    )(ona, owa, y2.reshape(2, T, SSM_CH), u, dsk, wglu_bf, wout_bf, x, mod_l, g, wr_pad)


def _select_body(cap, aff_ref, pos_ref, post_ref, afft_ref):
    aff = aff_ref[0]
    L = aff.shape[0]
    bits = pltpu.bitcast(aff, jnp.int32)
    capf = jnp.float32(cap)

    def bisect(k, thr):
        cand = thr | jnp.left_shift(jnp.int32(1), 30 - k)
        cnt = jnp.sum(jnp.where(bits >= cand, 1.0, 0.0), axis=0, keepdims=True)
        return jnp.where(cnt >= capf, cand, thr)

    thr = lax.fori_loop(0, 31, bisect, jnp.zeros((1, LANES), jnp.int32))
    gt = jnp.where(bits > thr, 1.0, 0.0)
    eq = jnp.where(bits == thr, 1.0, 0.0)
    need = capf - jnp.sum(gt, axis=0, keepdims=True)
    ck = SEL_CHUNK
    tri = (lax.broadcasted_iota(jnp.int32, (ck, ck), 0)
           >= lax.broadcasted_iota(jnp.int32, (ck, ck), 1))
    tri = jnp.where(tri, 1.0, 0.0).astype(BF16)

    def prefix(mask):
        outs = []
        carry = jnp.zeros((1, LANES), F32)
        for j in range(L // ck):
            c = _dot(tri, mask[j * ck:(j + 1) * ck].astype(BF16)) + carry
            outs.append(c)
            carry = c[ck - 1:ck, :]
        return jnp.concatenate(outs, axis=0)

    lane = lax.broadcasted_iota(jnp.int32, (L, LANES), 1)
    tie = jnp.where(prefix(eq) <= need, eq, 0.0)
    sel = jnp.where(lane < N_EXPERTS, gt + tie, 0.0)
    pos = jnp.where(sel > 0.0, prefix(sel) - 1.0, -1.0)
    pos_ref[0] = pos
    post_ref[0] = jnp.transpose(pos)[0:N_EXPERTS, :]
    afft_ref[0] = jnp.transpose(aff)[0:N_EXPERTS, :]


def _select(aff, B, L):
    cap = EC_CAPACITY * L // N_EXPERTS
    return pl.pallas_call(
        functools.partial(_select_body, cap),
        grid=(B,),
        in_specs=[pl.BlockSpec((1, L, LANES), lambda b: (b, 0, 0))],
        out_specs=[pl.BlockSpec((1, L, LANES), lambda b: (b, 0, 0)),
                   pl.BlockSpec((1, N_EXPERTS, L), lambda b: (b, 0, 0)),
                   pl.BlockSpec((1, N_EXPERTS, L), lambda b: (b, 0, 0))],
        out_shape=[jax.ShapeDtypeStruct((B, L, LANES), F32),
                   jax.ShapeDtypeStruct((B, N_EXPERTS, L), F32),
                   jax.ShapeDtypeStruct((B, N_EXPERTS, L), F32)],
        compiler_params=_cparams(("arbitrary",)),
        name="ec_select",
    )(aff.reshape(B, L, LANES))


def _gather_body(cap, group, h_ref, post_ref, afft_ref, xs_ref, gs_ref):
    h = h_ref[0]
    L = h.shape[0]
    slot = lax.broadcasted_iota(jnp.int32, (cap, L), 0).astype(F32)
    for e0 in range(0, N_EXPERTS, group):
        onehots = []
        for e in range(e0, e0 + group):
            hit = post_ref[0, e:e + 1, :] == slot
            gs_ref[e, 0] = jnp.sum(jnp.where(hit, afft_ref[0, e:e + 1, :], 0.0), axis=1, keepdims=True)
            onehots.append(jnp.where(hit, 1.0, 0.0).astype(BF16))
        xs = _dot(jnp.concatenate(onehots, axis=0), h)
        for k in range(group):
            xs_ref[e0 + k, 0] = xs[k * cap:(k + 1) * cap].astype(BF16)


def _gather(h2, post, afft, B, L):
    cap = EC_CAPACITY * L // N_EXPERTS
    group = max(1, min(N_EXPERTS, 512 // cap))
    xs, gs = pl.pallas_call(
        functools.partial(_gather_body, cap, group),
        grid=(B,),
        in_specs=[pl.BlockSpec((1, L, D_MODEL), lambda b: (b, 0, 0)),
                  pl.BlockSpec((1, N_EXPERTS, L), lambda b: (b, 0, 0)),
                  pl.BlockSpec((1, N_EXPERTS, L), lambda b: (b, 0, 0))],
        out_specs=[pl.BlockSpec((N_EXPERTS, 1, cap, D_MODEL), lambda b: (0, b, 0, 0)),
                   pl.BlockSpec((N_EXPERTS, 1, cap, 1), lambda b: (0, b, 0, 0))],
        out_shape=[jax.ShapeDtypeStruct((N_EXPERTS, B, cap, D_MODEL), BF16),
                   jax.ShapeDtypeStruct((N_EXPERTS, B, cap, 1), F32)],
        compiler_params=_cparams(("arbitrary",)),
        name="ec_gather",
    )(h2.reshape(B, L, D_MODEL), post, afft)
    return xs.reshape(N_EXPERTS, B * cap, D_MODEL), gs.reshape(N_EXPERTS, B * cap, 1)


def _ffn_body(n_first, xa_ref, ga_ref, xb_ref, gb_ref, wg_ref, wu_ref, wd_ref, y_ref, wgb, wub, wdb):
    j = pl.program_id(1)

    @pl.when(j == 0)
    def _():
        wgb[...] = wg_ref[0, 0].astype(BF16)
        wub[...] = wu_ref[0, 0].astype(BF16)
        wdb[...] = wd_ref[0, 0].astype(BF16)

    def run(xs_ref, gs_ref):
        x = xs_ref[0]
        a = _dot(x, wgb[...])
        u = _dot(x, wub[...])
        hm = (a * _sigmoid(a) * u).astype(BF16)
        y_ref[0] = (_dot(hm, wdb[...]) * gs_ref[0]).astype(BF16)

    @pl.when(j < n_first)
    def _():
        run(xa_ref, ga_ref)

    @pl.when(j >= n_first)
    def _():
        run(xb_ref, gb_ref)


def _ffn(xa, ga, xb, gb, wg, wu, wd, l):
    rc = FFN_ROWS
    na = xa.shape[1] // rc
    nb = xb.shape[1] // rc
    first = lambda w: pl.BlockSpec((1, rc, w), lambda e, j: (e, jnp.minimum(j, na - 1), 0))
    second = lambda w: pl.BlockSpec((1, rc, w), lambda e, j: (e, jnp.maximum(j - na, 0), 0))
    wspec = lambda a, b: pl.BlockSpec((1, 1, a, b), lambda e, j: (l, e, 0, 0))
    return pl.pallas_call(
        functools.partial(_ffn_body, na),
        grid=(N_EXPERTS, na + nb),
        in_specs=[first(D_MODEL), first(1), second(D_MODEL), second(1),
                  wspec(D_MODEL, EXPERT_FF), wspec(D_MODEL, EXPERT_FF), wspec(EXPERT_FF, D_MODEL)],
        out_specs=pl.BlockSpec((1, rc, D_MODEL), lambda e, j: (e, j, 0)),
        out_shape=jax.ShapeDtypeStruct((N_EXPERTS, (na + nb) * rc, D_MODEL), BF16),
        scratch_shapes=[pltpu.VMEM((D_MODEL, EXPERT_FF), BF16),
                        pltpu.VMEM((D_MODEL, EXPERT_FF), BF16),
                        pltpu.VMEM((EXPERT_FF, D_MODEL), BF16)],
        compiler_params=_cparams(("arbitrary", "arbitrary")),
        name="ec_ffn",
    )(xa, ga, xb, gb, wg, wu, wd)


def _combine_body(cap, pos_ref, y_ref, x1_ref, mod_ref, g_ref, o_ref):
    n = N_EXPERTS * cap
    pos = pos_ref[0]
    col = lax.broadcasted_iota(jnp.int32, (LANES, n), 1)
    expand = jnp.where(col // cap == lax.broadcasted_iota(jnp.int32, (LANES, n), 0), 1.0, 0.0)
    per_col = _dot(pos.astype(BF16), expand.astype(BF16))
    target = (lax.broadcasted_iota(jnp.int32, (1, n), 1) % cap).astype(F32)
    onehot = jnp.where(per_col == target, 1.0, 0.0).astype(BF16)
    f = _dot(onehot, y_ref[...].reshape(n, D_MODEL))
    g2 = mod_ref[0][:, 5 * D_MODEL:6 * D_MODEL]
    o_ref[...] = x1_ref[...] + g2 * _rms(f, g_ref[3:4, :])


def _combine(pos, y, row_off, x1, mod_l, mod_row_fn, g, B, L):
    cap = EC_CAPACITY * L // N_EXPERTS
    tq = TOKEN_TILE
    nq = L // tq
    blk_off = row_off // cap
    return pl.pallas_call(
        functools.partial(_combine_body, cap),
        grid=(B, nq),
        in_specs=[pl.BlockSpec((1, tq, LANES), lambda b, i: (b, i, 0)),
                  pl.BlockSpec((N_EXPERTS, cap, D_MODEL), lambda b, i: (0, blk_off + b, 0)),
                  pl.BlockSpec((tq, D_MODEL), lambda b, i: (b * nq + i, 0)),
                  pl.BlockSpec((1, 1, 6 * D_MODEL), lambda b, i: (mod_row_fn(b), 0, 0)),
                  pl.BlockSpec((4, D_MODEL), lambda b, i: (0, 0))],
        out_specs=pl.BlockSpec((tq, D_MODEL), lambda b, i: (b * nq + i, 0)),
        out_shape=jax.ShapeDtypeStruct((B * L, D_MODEL), F32),
        compiler_params=_cparams(("arbitrary", "arbitrary")),
        name="ec_combine",
    )(pos, y, x1, mod_l, g)


def _rope_tables(L):
    t = jnp.arange(L)
    row = (t // GRID_W).astype(F32)
    col = (t % GRID_W).astype(F32)
    half = HEAD_DIM // 4
    inv = ROPE_BASE ** (-jnp.arange(half, dtype=F32) / half)
    d = np.arange(LANES) % HEAD_DIM
    use_col = (d // (HEAD_DIM // 2)) == 1
    pos = jnp.where(use_col[None, :], col[:, None], row[:, None])
    ang = pos * inv[d % half][None, :]
    sign = np.where((d % (HEAD_DIM // 2)) < half, -1.0, 1.0).astype(np.float32)
    return jnp.cos(ang), jnp.sin(ang) * sign[None, :]


def _permute_wa_heads(a, axis, start):
    idx = np.arange(a.shape[axis])
    blk = np.concatenate([np.arange(h * HEAD_DIM, (h + 1) * HEAD_DIM) for h in WA_HEAD_ORDER])
    idx[start:start + WA_WIDTH] = start + blk
    return jnp.take(a, idx, axis=axis)


def kernel(x_prompt, x_sample, c, cache_na_k, cache_na_v, cache_wa_k, cache_wa_v, state_ssm, c_ctx, w_ada, b_ada, norm_g, w_in, w_out, na_rpb, wa_sink, ssm_lambda_re, ssm_lambda_im, ssm_log_dt, ssm_b_re, ssm_b_im, ssm_c_re, ssm_c_im, ssm_d, w_glu, w_router, w_exp_gate, w_exp_up, w_exp_down):
    Bc, Lc, _ = x_prompt.shape
    Bs, Ls, _ = x_sample.shape
    ctx_row = Bs
    n_cond = ((Bs + 1 + SUBLANES - 1) // SUBLANES) * SUBLANES
    cond = jnp.zeros((n_cond, D_MODEL), F32).at[0:Bs].set(c).at[ctx_row].set(c_ctx)
    mod = _modulation(cond, w_ada, b_ada)
    mod = mod.reshape(DEPTH, n_cond, 1, 6 * D_MODEL)

    w_in_bf = _permute_wa_heads(w_in, 2, C_WAQ).astype(BF16)
    w_out_bf = _permute_wa_heads(w_out, 1, NA_WIDTH).astype(BF16)
    w_glu_bf = w_glu.astype(BF16)
    wr_hi = w_router.astype(BF16)
    wr_lo = (w_router - wr_hi.astype(F32)).astype(BF16)
    wr_pad = jnp.pad(jnp.concatenate([wr_hi, wr_lo], axis=-1), ((0, 0), (0, 0), (0, LANES - 2 * N_EXPERTS)))
    s5_a, s5_bb = _s5_prep(ssm_lambda_re, ssm_lambda_im, ssm_log_dt, ssm_b_re, ssm_b_im)
    s5_c = _s5_c_matrix(ssm_c_re, ssm_c_im)
    rope_tabs = _rope_tables(Ls)
    na_rows = Ls // GRID_W

    ctx_mod_row = lambda b: ctx_row + 0 * b
    lat_mod_row = lambda b: b

    xp = x_prompt.reshape(Bc * Lc, D_MODEL)
    xs = x_sample.reshape(Bs * Ls, D_MODEL)
    new_na_k, new_na_v, new_wa_k, new_wa_v, new_ssm = [], [], [], [], []
    for l in range(DEPTH):
        g = norm_g[l]
        dsk = ssm_d[l].reshape(1, SSM_CH)
        sl2 = slice(2 * l, 2 * l + 2)
        qna, kna, vna, qwa, kwa, vwa, u = _inproj(xp, mod[l], ctx_mod_row, g, w_in_bf[l], Lc, None)
        ona, owa = _attn_ctx(wa_sink[l], qna, kna, vna, qwa, kwa, vwa, Bc, Lc)
        y2, hfin = _s5_scan(u, s5_bb[sl2], s5_a[sl2], s5_c[sl2],
                            jnp.zeros((2, Bc, 2 * SSM_N), F32), Bc, Lc)
        x1, h2, aff = _outproj(ona, owa, y2, u, dsk, w_glu_bf[l], w_out_bf[l], xp, mod[l], ctx_mod_row,
                               g, wr_pad[l], Lc)
        pos_c, post, afft = _select(aff, Bc, Lc)
        xg_c, gs_c = _gather(h2, post, afft, Bc, Lc)
        x1_c = x1
        new_na_k.append(kna.reshape(Bc, Lc, NA_HEADS, HEAD_DIM))
        new_na_v.append(vna.reshape(Bc, Lc, NA_HEADS, HEAD_DIM))
        new_wa_k.append(kwa.reshape(Bc, Lc, WA_KV_HEADS, HEAD_DIM))
        new_wa_v.append(vwa.reshape(Bc, Lc, WA_KV_HEADS, HEAD_DIM))
        new_ssm.append(jnp.transpose(hfin.reshape(2, Bc, 2, SSM_GROUPS, SSM_STATE), (1, 0, 2, 3, 4)))
        qna, kna, vna, qwa, kwa, vwa, u = _inproj(xs, mod[l], lat_mod_row, g, w_in_bf[l], Ls, rope_tabs)
        ona = _attn_na(qna, kna, vna, cache_na_k, cache_na_v, _na_bias(na_rpb[l], na_rows), l, Bs, Ls)
        owa = _attn_wa(wa_sink[l], qwa, kwa, vwa, cache_wa_k, cache_wa_v, l, Bs, Ls)
        h0 = jnp.transpose(state_ssm[:, l].reshape(Bs, 2, 2 * SSM_N), (1, 0, 2))
        y2, _ = _s5_scan(u, s5_bb[sl2], s5_a[sl2], s5_c[sl2], h0, Bs, Ls)
        x1, h2, aff = _outproj(ona, owa, y2, u, dsk, w_glu_bf[l], w_out_bf[l], xs, mod[l], lat_mod_row,
                               g, wr_pad[l], Ls)
        pos_s, post, afft = _select(aff, Bs, Ls)
        xg_s, gs_s = _gather(h2, post, afft, Bs, Ls)
        yy = _ffn(xg_c, gs_c, xg_s, gs_s, w_exp_gate, w_exp_up, w_exp_down, l)
        xp = _combine(pos_c, yy, 0, x1_c, mod[l], ctx_mod_row, g, Bc, Lc)
        xs = _combine(pos_s, yy, xg_c.shape[1], x1, mod[l], lat_mod_row, g, Bs, Ls)
    return (xp.reshape(Bc, Lc, D_MODEL), xs.reshape(Bs, Ls, D_MODEL),
            jnp.stack(new_na_k, axis=1), jnp.stack(new_na_v, axis=1),
            jnp.stack(new_wa_k, axis=1), jnp.stack(new_wa_v, axis=1),
            jnp.stack(new_ssm, axis=1))
```

```python
import functools
import math

import numpy as np
import jax
import jax.numpy as jnp
from jax import lax
from jax.experimental import pallas as pl
from jax.experimental.pallas import tpu as pltpu

F32 = jnp.float32
BF16 = jnp.bfloat16

D_MODEL = 1024
DEPTH = 2
GRID_W = 64
HEAD_DIM = 64
NA_HEADS = 6
NA_KH = 8
NA_KW = 16
WA_HEADS = 6
WA_KV_HEADS = 2
WA_WINDOW = 128
WA_BLOCK = 128
SSM_CH = 256
SSM_GROUP_CH = 16
SSM_GROUPS = 16
SSM_STATE = 64
SSM_N = SSM_GROUPS * SSM_STATE
NA_WIDTH = NA_HEADS * HEAD_DIM
WA_WIDTH = WA_HEADS * HEAD_DIM
WA_KV_WIDTH = WA_KV_HEADS * HEAD_DIM
IN_COLS = 2048
N_EXPERTS = 16
EC_CAPACITY = 2
EXPERT_FF = 1024
ROPE_BASE = 10000.0
RMS_EPS = 1e-6
NEG_INF = -1e30
ATTN_SCALE = HEAD_DIM ** -0.5
LOG2E = math.log2(math.e)
Q_SCALE = ATTN_SCALE * LOG2E

LANES = 128
SUBLANES = 8
VMEM_LIMIT = 48 * 1024 * 1024

C_NAQ, C_NAK, C_NAV, C_WAQ, C_WAK, C_WAV, C_SSU = 0, 384, 768, 1152, 1536, 1664, 1792
WA_HEAD_ORDER = (0, 3, 1, 4, 2, 5)

NA_QROWS = 4
NA_KROWS = NA_QROWS + NA_KH
TOKEN_TILE = 256
SCAN_TILE = 128
SCAN_BATCH = SUBLANES
FFN_ROWS = 512
SEL_CHUNK = 256
SEL_GROUP = LANES // N_EXPERTS


def _cparams(sem):
    return pltpu.CompilerParams(dimension_semantics=sem, vmem_limit_bytes=VMEM_LIMIT)


def _sigmoid(x):
    return 1.0 / (1.0 + jnp.exp(-x))


def _rms(x, g):
    ms = jnp.mean(x * x, axis=-1, keepdims=True)
    return x * lax.rsqrt(ms + RMS_EPS) * g


def _dot(a, b):
    return jnp.dot(a, b, preferred_element_type=F32)


def _dot_nt(a, b):
    return lax.dot_general(a, b, (((1,), (1,)), ((), ())), preferred_element_type=F32)


def _mod_body(c_ref, w_ref, b_ref, o_ref):
    c = c_ref[...]
    s = c * _sigmoid(c)
    o_ref[0] = _dot(s.astype(BF16), w_ref[0].astype(BF16)) + b_ref[0]


def _modulation(cond, w_ada, b_ada):
    n = cond.shape[0]
    tn = 1024
    return pl.pallas_call(
        _mod_body,
        grid=(DEPTH, 6 * D_MODEL // tn),
        in_specs=[
            pl.BlockSpec((n, D_MODEL), lambda l, j: (0, 0)),
            pl.BlockSpec((1, D_MODEL, tn), lambda l, j: (l, 0, j)),
            pl.BlockSpec((1, 1, tn), lambda l, j: (l, 0, j)),
        ],
        out_specs=pl.BlockSpec((1, n, tn), lambda l, j: (l, 0, j)),
        out_shape=jax.ShapeDtypeStruct((DEPTH, n, 6 * D_MODEL), F32),
        compiler_params=_cparams(("arbitrary", "arbitrary")),
        name="modulation",
    )(cond, w_ada, b_ada.reshape(DEPTH, 1, 6 * D_MODEL))


def _rope_chunk(x, cos, sin_signed):
    lane = lax.broadcasted_iota(jnp.int32, x.shape, 1)
    first = (lane % 32) < 16
    rot = jnp.where(first, pltpu.roll(x, LANES - 16, 1), pltpu.roll(x, 16, 1))
    return x * cos + rot * sin_signed


def _inproj_body(rope, x_ref, mod_ref, g_ref, w_ref, *rest):
    if rope:
        cos_ref, sin_ref = rest[:2]
        rest = rest[2:]
    qna_ref, kna_ref, vna_ref, qwa_ref, kwa_ref, vwa_ref, u_ref = rest
    x = x_ref[...]
    m = mod_ref[0]
    sh1 = m[:, 0:D_MODEL]
    sc1 = m[:, D_MODEL:2 * D_MODEL]
    h = _rms(x, g_ref[0:1, :]) * (1.0 + sc1) + sh1
    z = _dot(h.astype(BF16), w_ref[...])
    qna_ref[...] = (z[:, C_NAQ:C_NAK] * Q_SCALE).astype(BF16)
    kna_ref[...] = z[:, C_NAK:C_NAV]
    vna_ref[...] = z[:, C_NAV:C_WAQ]
    vwa_ref[...] = z[:, C_WAV:C_SSU]
    u_ref[...] = z[:, C_SSU:IN_COLS]
    if rope:
        cos = cos_ref[...]
        sin = sin_ref[...]
        for j in range(WA_WIDTH // LANES):
            qc = _rope_chunk(z[:, C_WAQ + j * LANES:C_WAQ + (j + 1) * LANES], cos, sin)
            qwa_ref[:, j * LANES:(j + 1) * LANES] = (qc * Q_SCALE).astype(BF16)
        kwa_ref[...] = _rope_chunk(z[:, C_WAK:C_WAV], cos, sin)
    else:
        qwa_ref[...] = (z[:, C_WAQ:C_WAK] * Q_SCALE).astype(BF16)
        kwa_ref[...] = z[:, C_WAK:C_WAV]


def _inproj(x, mod_l, mod_row_fn, g, w_bf, L, rope_tabs):
    T = x.shape[0]
    tm = TOKEN_TILE
    tiles_per_seq = L // tm
    in_specs = [
        pl.BlockSpec((tm, D_MODEL), lambda i: (i, 0)),
        pl.BlockSpec((1, 1, 6 * D_MODEL), lambda i: (mod_row_fn(i // tiles_per_seq), 0, 0)),
        pl.BlockSpec((4, D_MODEL), lambda i: (0, 0)),
        pl.BlockSpec((D_MODEL, IN_COLS), lambda i: (0, 0)),
    ]
    args = [x, mod_l, g, w_bf]
    if rope_tabs is not None:
        in_specs += [pl.BlockSpec((tm, LANES), lambda i: (i % tiles_per_seq, 0))] * 2
        args += list(rope_tabs)
    widths = (NA_WIDTH, NA_WIDTH, NA_WIDTH, WA_WIDTH, WA_KV_WIDTH, WA_KV_WIDTH, SSM_CH)
    dtypes = (BF16, F32, F32, BF16, F32, F32, F32)
    return pl.pallas_call(
        functools.partial(_inproj_body, rope_tabs is not None),
        grid=(T // tm,),
        in_specs=in_specs,
        out_specs=[pl.BlockSpec((tm, w), lambda i: (i, 0)) for w in widths],
        out_shape=[jax.ShapeDtypeStruct((T, w), dt) for w, dt in zip(widths, dtypes)],
        compiler_params=_cparams(("arbitrary",)),
        name="inproj",
    )(*args)


def _lane_lo(shape):
    return lax.broadcasted_iota(jnp.int32, shape, len(shape) - 1) < HEAD_DIM


def _keep_half(q, lo, half):
    keep = jnp.where(lo, 1.0 - half, 0.0 + half).astype(q.dtype)
    return q * keep


def _values_and_ones(v, lo, half):
    one = jnp.ones_like(v)
    return (jnp.where(lo, v, one) if half == 0 else jnp.where(lo, one, v)).astype(BF16)


def _normalise(o2, extra=None):
    den = pltpu.roll(o2, HEAD_DIM, 1)
    if extra is not None:
        den = den + extra
    return o2 / den


def _attn_ctx_body(sink_ref, qna, kna, vna, qwa, kwa, vwa, ona, owa):
    L = qna.shape[1]
    lo = _lane_lo((L, LANES))
    for j in range(NA_WIDTH // LANES):
        sl = slice(j * LANES, (j + 1) * LANES)
        q2 = qna[0, :, sl]
        k2 = kna[0, :, sl].astype(BF16)
        v2 = vna[0, :, sl]
        halves = []
        for half in range(2):
            s = _dot_nt(_keep_half(q2, lo, half), k2)
            p = jnp.exp2(s - jnp.max(s, axis=-1, keepdims=True))
            halves.append(_normalise(_dot(p.astype(BF16), _values_and_ones(v2, lo, half))))
        ona[0, :, sl] = jnp.where(lo, halves[0], halves[1]).astype(BF16)
    kw = kwa[0].astype(BF16)
    vw = vwa[0]
    for j in range(WA_WIDTH // LANES):
        sl = slice(j * LANES, (j + 1) * LANES)
        q2 = qwa[0, :, sl]
        halves = []
        for half in range(2):
            s = _dot_nt(_keep_half(q2, lo, half), kw)
            sk = sink_ref[WA_HEAD_ORDER[2 * j + half]] * LOG2E
            m = jnp.maximum(jnp.max(s, axis=-1, keepdims=True), sk)
            p = jnp.exp2(s - m)
            o2 = _dot(p.astype(BF16), _values_and_ones(vw, lo, half))
            halves.append(_normalise(o2, jnp.exp2(sk - m)))
        owa[0, :, sl] = jnp.where(lo, halves[0], halves[1]).astype(BF16)


def _attn_ctx(sink, qna, kna, vna, qwa, kwa, vwa, B, L):
    def spec(w):
        return pl.BlockSpec((1, L, w), lambda b: (b, 0, 0))

    r3 = lambda a: a.reshape(B, L, a.shape[-1])
    ona, owa = pl.pallas_call(
        _attn_ctx_body,
        grid=(B,),
        in_specs=[pl.BlockSpec(memory_space=pltpu.SMEM),
                  spec(NA_WIDTH), spec(NA_WIDTH), spec(NA_WIDTH),
                  spec(WA_WIDTH), spec(WA_KV_WIDTH), spec(WA_KV_WIDTH)],
        out_specs=[spec(NA_WIDTH), spec(WA_WIDTH)],
        out_shape=[jax.ShapeDtypeStruct((B, L, NA_WIDTH), BF16),
                   jax.ShapeDtypeStruct((B, L, WA_WIDTH), BF16)],
        compiler_params=_cparams(("arbitrary",)),
        name="attn_ctx",
    )(sink, r3(qna), r3(kna), r3(vna), r3(qwa), r3(kwa), r3(vwa))
    return ona.reshape(B * L, NA_WIDTH), owa.reshape(B * L, WA_WIDTH)


def _na_key_start(rb, rows):
    return jnp.clip(rb * NA_QROWS - NA_KH // 2, 0, rows - NA_KROWS)


def _attn_na_body(rows, q_ref, k_ref, v_ref, kc_ref, vc_ref, bias_ref, o_ref):
    rb = pl.program_id(1)
    nq = NA_QROWS * GRID_W
    nk = NA_KROWS * GRID_W
    start = pl.multiple_of(_na_key_start(rb, rows) * GRID_W, GRID_W)
    lo = _lane_lo((nq, LANES))
    lo_v = _lane_lo((1, LANES))
    for j in range(NA_WIDTH // LANES):
        sl = slice(j * LANES, (j + 1) * LANES)
        q2 = q_ref[0, :, sl]
        kl = k_ref[0, pl.ds(start, nk), sl].astype(BF16)
        vl = v_ref[0, pl.ds(start, nk), sl]
        kc = kc_ref[0, 0, :, sl].astype(BF16)
        vc = vc_ref[0, 0, :, sl]
        halves = []
        for half in range(2):
            qm = _keep_half(q2, lo, half)
            s_loc = _dot_nt(qm, kl) + bias_ref[0, 2 * j + half]
            s_ctx = _dot_nt(qm, kc)
            m = jnp.maximum(jnp.max(s_loc, axis=-1, keepdims=True),
                            jnp.max(s_ctx, axis=-1, keepdims=True))
            o2 = (_dot(jnp.exp2(s_loc - m).astype(BF16), _values_and_ones(vl, lo_v, half))
                  + _dot(jnp.exp2(s_ctx - m).astype(BF16), _values_and_ones(vc, lo_v, half)))
            halves.append(_normalise(o2))
        o_ref[0, :, sl] = jnp.where(lo, halves[0], halves[1]).astype(BF16)


def _na_bias(rpb_l, rows):
    nrb = rows // NA_QROWS
    pats = (0, 1, nrb - 1)
    cq = np.arange(GRID_W)
    col_start = np.clip(cq - NA_KW // 2, 0, GRID_W - NA_KW)
    col_ok = (cq[None, :] >= col_start[:, None]) & (cq[None, :] < col_start[:, None] + NA_KW)
    dcol = np.clip(cq[None, :] - cq[:, None], -(NA_KW - 1), NA_KW - 1) + (NA_KW - 1)
    pick = (dcol[None] == np.arange(2 * NA_KW - 1)[:, None, None]).astype(np.float32)
    exact = lax.Precision.HIGHEST
    tiles = jnp.einsum('hrd,dqc->hrqc', rpb_l.astype(F32) * LOG2E, pick, precision=exact)
    tiles = jnp.where(col_ok[None, None], tiles, NEG_INF)
    n_rel = 2 * NA_KH - 1
    tiles = jnp.concatenate([tiles, jnp.full((NA_HEADS, 1, GRID_W, GRID_W), NEG_INF, F32)], axis=1)
    which = np.zeros((len(pats), NA_QROWS, NA_KROWS, n_rel + 1), np.float32)
    for pi, rb in enumerate(pats):
        ks = int(np.clip(rb * NA_QROWS - NA_KH // 2, 0, rows - NA_KROWS))
        for i in range(NA_QROWS):
            qr = rb * NA_QROWS + i
            rs = int(np.clip(qr - NA_KH // 2, 0, rows - NA_KH))
            for jj in range(NA_KROWS):
                kr = ks + jj
                which[pi, i, jj, kr - qr + NA_KH - 1 if rs <= kr < rs + NA_KH else n_rel] = 1.0
    out = jnp.einsum('pijr,hrqc->phiqjc', which, tiles, precision=exact)
    return out.reshape(len(pats), NA_HEADS, NA_QROWS * GRID_W, NA_KROWS * GRID_W)


def _attn_na(q, k, v, cache_k, cache_v, bias, l, B, L):
    rows = L // GRID_W
    nrb = rows // NA_QROWS
    nq = NA_QROWS * GRID_W
    P = cache_k.shape[2]
    pat = lambda rb: jnp.where(rb == 0, 0, jnp.where(rb == nrb - 1, 2, 1))
    r3 = lambda a: a.reshape(B, L, a.shape[-1])
    ck = cache_k.reshape(B, DEPTH, P, NA_WIDTH)
    cv = cache_v.reshape(B, DEPTH, P, NA_WIDTH)
    o = pl.pallas_call(
        functools.partial(_attn_na_body, rows),
        grid=(B, nrb),
        in_specs=[
            pl.BlockSpec((1, nq, NA_WIDTH), lambda b, r: (b, r, 0)),
            pl.BlockSpec((1, L, NA_WIDTH), lambda b, r: (b, 0, 0)),
            pl.BlockSpec((1, L, NA_WIDTH), lambda b, r: (b, 0, 0)),
            pl.BlockSpec((1, 1, P, NA_WIDTH), lambda b, r: (b, l, 0, 0)),
            pl.BlockSpec((1, 1, P, NA_WIDTH), lambda b, r: (b, l, 0, 0)),
            pl.BlockSpec((1, NA_HEADS, nq, NA_KROWS * GRID_W), lambda b, r: (pat(r), 0, 0, 0)),
        ],
        out_specs=pl.BlockSpec((1, nq, NA_WIDTH), lambda b, r: (b, r, 0)),
        out_shape=jax.ShapeDtypeStruct((B, L, NA_WIDTH), BF16),
        compiler_params=_cparams(("arbitrary", "arbitrary")),
        name="attn_na",
    )(r3(q), r3(k), r3(v), ck, cv, bias)
    return o.reshape(B * L, NA_WIDTH)


def _attn_wa_body(L, sink_ref, q_ref, k_ref, v_ref, kc_ref, vc_ref, o_ref):
    n = pl.program_id(1)
    nk = 3 * WA_BLOCK
    npair = WA_WIDTH // LANES
    start = pl.multiple_of(jnp.clip((n - 1) * WA_BLOCK, 0, L - nk), WA_BLOCK)
    kl = k_ref[0, pl.ds(start, nk), :].astype(BF16)
    vl = v_ref[0, pl.ds(start, nk), :]
    kc = kc_ref[0, 0].astype(BF16)
    vc = vc_ref[0, 0]
    lo_v = _lane_lo((1, LANES))
    rows = npair * WA_BLOCK
    qpos = n * WA_BLOCK + lax.broadcasted_iota(jnp.int32, (rows, nk), 0) % WA_BLOCK
    kpos = start + lax.broadcasted_iota(jnp.int32, (rows, nk), 1)
    in_win = jnp.abs(kpos - qpos) <= WA_WINDOW
    lo = _lane_lo((rows, LANES))
    q_all = jnp.concatenate([q_ref[0, :, j * LANES:(j + 1) * LANES] for j in range(npair)], axis=0)
    blk = lax.broadcasted_iota(jnp.int32, (rows, 1), 0) // WA_BLOCK
    outs = []
    for g in range(WA_KV_HEADS):
        qm = _keep_half(q_all, lo, g)
        sk = jnp.zeros((rows, 1), F32)
        for j in range(npair):
            sk = jnp.where(blk == j, sink_ref[WA_HEAD_ORDER[2 * j + g]] * LOG2E, sk)
        s_loc = jnp.where(in_win, _dot_nt(qm, kl), NEG_INF)
        s_ctx = _dot_nt(qm, kc)
        m = jnp.maximum(jnp.maximum(jnp.max(s_loc, axis=-1, keepdims=True),
                                    jnp.max(s_ctx, axis=-1, keepdims=True)), sk)
        o2 = (_dot(jnp.exp2(s_loc - m).astype(BF16), _values_and_ones(vl, lo_v, g))
              + _dot(jnp.exp2(s_ctx - m).astype(BF16), _values_and_ones(vc, lo_v, g)))
        outs.append(_normalise(o2, jnp.exp2(sk - m)))
    o_all = jnp.where(lo, outs[0], outs[1]).astype(BF16)
    for j in range(npair):
        o_ref[0, :, j * LANES:(j + 1) * LANES] = o_all[j * WA_BLOCK:(j + 1) * WA_BLOCK]


def _attn_wa(sink, q, k, v, cache_k, cache_v, l, B, L):
    nb = L // WA_BLOCK
    P = cache_k.shape[2]
    r3 = lambda a: a.reshape(B, L, a.shape[-1])
    ck = cache_k.reshape(B, DEPTH, P, WA_KV_WIDTH)
    cv = cache_v.reshape(B, DEPTH, P, WA_KV_WIDTH)
    o = pl.pallas_call(
        functools.partial(_attn_wa_body, L),
        grid=(B, nb),
        in_specs=[
            pl.BlockSpec(memory_space=pltpu.SMEM),
            pl.BlockSpec((1, WA_BLOCK, WA_WIDTH), lambda b, n: (b, n, 0)),
            pl.BlockSpec((1, L, WA_KV_WIDTH), lambda b, n: (b, 0, 0)),
            pl.BlockSpec((1, L, WA_KV_WIDTH), lambda b, n: (b, 0, 0)),
            pl.BlockSpec((1, 1, P, WA_KV_WIDTH), lambda b, n: (b, l, 0, 0)),
            pl.BlockSpec((1, 1, P, WA_KV_WIDTH), lambda b, n: (b, l, 0, 0)),
        ],
        out_specs=pl.BlockSpec((1, WA_BLOCK, WA_WIDTH), lambda b, n: (b, n, 0)),
        out_shape=jax.ShapeDtypeStruct((B, L, WA_WIDTH), BF16),
        compiler_params=_cparams(("arbitrary", "arbitrary")),
        name="attn_wa",
    )(sink, r3(q), r3(k), r3(v), ck, cv)
    return o.reshape(B * L, WA_WIDTH)


def _s5_prep_body(lr_ref, li_ref, ldt_ref, br_ref, bi_ref, a_ref, bb_ref):
    lr = lr_ref[0]
    li = li_ref[0]
    dt = jnp.exp(ldt_ref[0])
    mag = jnp.exp(lr * dt)
    ar = mag * jnp.cos(li * dt)
    ai = mag * jnp.sin(li * dt)
    den = lr * lr + li * li
    nr = ar - 1.0
    fr = (nr * lr + ai * li) / den
    fi = (ai * lr - nr * li) / den
    a_ref[0, :, 0:SSM_N] = ar
    a_ref[0, :, SSM_N:2 * SSM_N] = ai
    br = br_ref[0]
    bi = bi_ref[0]
    bb_ref[0, :, 0:SSM_N] = (fr * br - fi * bi).astype(BF16)
    bb_ref[0, :, SSM_N:2 * SSM_N] = (fr * bi + fi * br).astype(BF16)


def _s5_prep(lam_re, lam_im, log_dt, b_re, b_im):
    n = DEPTH * 2
    eye = jnp.eye(SSM_GROUPS, dtype=F32)

    def blockdiag_b(b):
        return jnp.einsum('ngph,gk->nghkp', b.reshape(n, SSM_GROUPS, SSM_STATE, SSM_GROUP_CH), eye
                          ).reshape(n, SSM_CH, SSM_N)

    lr = lam_re.reshape(n, 1, SSM_N)
    li = lam_im.reshape(n, 1, SSM_N)
    ldt = jnp.repeat(log_dt.reshape(n, SSM_GROUPS), SSM_STATE, axis=-1).reshape(n, 1, SSM_N)
    vec = pl.BlockSpec((1, 1, SSM_N), lambda i: (i, 0, 0))
    mat = pl.BlockSpec((1, SSM_CH, SSM_N), lambda i: (i, 0, 0))
    return pl.pallas_call(
        _s5_prep_body,
        grid=(n,),
        in_specs=[vec, vec, vec, mat, mat],
        out_specs=[pl.BlockSpec((1, 1, 2 * SSM_N), lambda i: (i, 0, 0)),
                   pl.BlockSpec((1, SSM_CH, 2 * SSM_N), lambda i: (i, 0, 0))],
        out_shape=[jax.ShapeDtypeStruct((n, 1, 2 * SSM_N), F32),
                   jax.ShapeDtypeStruct((n, SSM_CH, 2 * SSM_N), BF16)],
        compiler_params=_cparams(("arbitrary",)),
        name="s5_prep",
    )(lr, li, ldt, blockdiag_b(b_re), blockdiag_b(b_im))


def _s5_c_matrix(c_re, c_im):
    n = DEPTH * 2
    eye = jnp.eye(SSM_GROUPS, dtype=F32)

    def blk(c):
        return jnp.einsum('nghp,gk->ngpkh', c.reshape(n, SSM_GROUPS, SSM_GROUP_CH, SSM_STATE), eye
                          ).reshape(n, SSM_N, SSM_CH)

    return jnp.concatenate([blk(c_re), -blk(c_im)], axis=1).astype(BF16)


def _s5_scan_body(tl, u_ref, bb_ref, a_ref, c_ref, h0_ref, y_ref, hfin_ref, bu, hst):
    d = pl.program_id(0)
    i = pl.program_id(2)

    @pl.when(i == 0)
    def _():
        hst[...] = h0_ref[0]

    u_tb = jnp.transpose(u_ref[...], (1, 0, 2)).reshape(tl * SCAN_BATCH, SSM_CH)
    bu[...] = _dot(u_tb.astype(BF16), bb_ref[0])
    a = a_ref[0]
    ar = jnp.broadcast_to(a[:, 0:SSM_N], (SCAN_BATCH, SSM_N))
    ai = jnp.broadcast_to(a[:, SSM_N:2 * SSM_N], (SCAN_BATCH, SSM_N))

    def step(s, carry):
        hr, hi = carry
        t = s + d * (tl - 1 - 2 * s)
        rows = pl.ds(pl.multiple_of(t * SCAN_BATCH, SCAN_BATCH), SCAN_BATCH)
        nhr = ar * hr - ai * hi + bu[rows, 0:SSM_N]
        nhi = ar * hi + ai * hr + bu[rows, SSM_N:2 * SSM_N]
        bu[rows, 0:SSM_N] = nhr
        bu[rows, SSM_N:2 * SSM_N] = nhi
        return nhr, nhi

    hr, hi = lax.fori_loop(0, tl, step, (hst[:, 0:SSM_N], hst[:, SSM_N:2 * SSM_N]), unroll=2)
    hst[:, 0:SSM_N] = hr
    hst[:, SSM_N:2 * SSM_N] = hi
    hfin_ref[0, :, 0:SSM_N] = hr
    hfin_ref[0, :, SSM_N:2 * SSM_N] = hi
    y_tb = _dot(bu[...].astype(BF16), c_ref[0])
    y_ref[0] = jnp.transpose(y_tb.reshape(tl, SCAN_BATCH, SSM_CH), (1, 0, 2))


def _s5_scan(u, bb, a, cm, h0, B, L):
    tl = SCAN_TILE
    nt = L // tl
    tile = lambda d, i: i + d * (nt - 1 - 2 * i)
    return pl.pallas_call(
        functools.partial(_s5_scan_body, tl),
        grid=(2, B // SCAN_BATCH, nt),
        in_specs=[
            pl.BlockSpec((SCAN_BATCH, tl, SSM_CH), lambda d, b, i: (b, tile(d, i), 0)),
            pl.BlockSpec((1, SSM_CH, 2 * SSM_N), lambda d, b, i: (d, 0, 0)),
            pl.BlockSpec((1, 1, 2 * SSM_N), lambda d, b, i: (d, 0, 0)),
            pl.BlockSpec((1, 2 * SSM_N, SSM_CH), lambda d, b, i: (d, 0, 0)),
            pl.BlockSpec((1, SCAN_BATCH, 2 * SSM_N), lambda d, b, i: (d, b, 0)),
        ],
        out_specs=[
            pl.BlockSpec((1, SCAN_BATCH, tl, SSM_CH), lambda d, b, i: (d, b, tile(d, i), 0)),
            pl.BlockSpec((1, SCAN_BATCH, 2 * SSM_N), lambda d, b, i: (d, b, 0)),
        ],
        out_shape=[jax.ShapeDtypeStruct((2, B, L, SSM_CH), F32),
                   jax.ShapeDtypeStruct((2, B, 2 * SSM_N), F32)],
        scratch_shapes=[pltpu.VMEM((SCAN_BATCH * tl, 2 * SSM_N), F32),
                        pltpu.VMEM((SCAN_BATCH, 2 * SSM_N), F32)],
        compiler_params=_cparams(("arbitrary", "arbitrary", "arbitrary")),
        name="s5_scan",
    )(u.reshape(B, L, SSM_CH), bb, a, cm, h0)


def _outproj_body(ona_ref, owa_ref, y_ref, u_ref, dsk_ref, wglu_ref, wout_ref, x_ref, mod_ref, g_ref,
                  wr_ref, x1_ref, h2_ref, aff_ref):
    y = dsk_ref[...] * u_ref[...] + y_ref[0] + y_ref[1]
    zg = _dot(y.astype(BF16), wglu_ref[...])
    oss = zg[:, 0:SSM_CH] * _sigmoid(zg[:, SSM_CH:2 * SSM_CH])
    mix = (_dot(ona_ref[...], wout_ref[0:NA_WIDTH, :])
           + _dot(owa_ref[...], wout_ref[NA_WIDTH:NA_WIDTH + WA_WIDTH, :])
           + _dot(oss.astype(BF16), wout_ref[NA_WIDTH + WA_WIDTH:D_MODEL, :]))
    m = mod_ref[0]
    g1 = m[:, 2 * D_MODEL:3 * D_MODEL]
    sh2 = m[:, 3 * D_MODEL:4 * D_MODEL]
    sc2 = m[:, 4 * D_MODEL:5 * D_MODEL]
    x1 = x_ref[...] + g1 * _rms(mix, g_ref[1:2, :])
    x1_ref[...] = x1
    h2 = _rms(x1, g_ref[2:3, :]) * (1.0 + sc2) + sh2
    h2_hi = h2.astype(BF16)
    h2_ref[...] = h2_hi
    h2_lo = (h2 - h2_hi.astype(F32)).astype(BF16)
    r_hi = _dot(h2_hi, wr_ref[...])
    logits = r_hi + pltpu.roll(r_hi, LANES - N_EXPERTS, 1) + _dot(h2_lo, wr_ref[...])
    lane = lax.broadcasted_iota(jnp.int32, logits.shape, 1)
    logits = jnp.where(lane < N_EXPERTS, logits, NEG_INF)
    mx = jnp.max(logits, axis=-1, keepdims=True)
    p = jnp.exp(logits - mx)
    aff_ref[...] = p / jnp.sum(p, axis=-1, keepdims=True)


def _outproj(ona, owa, y2, u, dsk, wglu_bf, wout_bf, x, mod_l, mod_row_fn, g, wr_pad, L):
    T = x.shape[0]
    tm = TOKEN_TILE
    tiles_per_seq = L // tm
    row = lambda w: pl.BlockSpec((tm, w), lambda i: (i, 0))
    full = lambda a: pl.BlockSpec(a.shape, lambda i: (0,) * a.ndim)
    return pl.pallas_call(
        _outproj_body,
        grid=(T // tm,),
        in_specs=[row(NA_WIDTH), row(WA_WIDTH),
                  pl.BlockSpec((2, tm, SSM_CH), lambda i: (0, i, 0)),
                  row(SSM_CH), full(dsk), full(wglu_bf), full(wout_bf), row(D_MODEL),
                  pl.BlockSpec((1, 1, 6 * D_MODEL), lambda i: (mod_row_fn(i // tiles_per_seq), 0, 0)),
                  full(g), full(wr_pad)],
        out_specs=[row(D_MODEL), row(D_MODEL), row(LANES)],
        out_shape=[jax.ShapeDtypeStruct((T, D_MODEL), F32),
                   jax.ShapeDtypeStruct((T, D_MODEL), BF16),
                   jax.ShapeDtypeStruct((T, LANES), F32)],
        compiler_params=_cparams(("arbitrary",)),
        name="outproj",
    )(ona, owa, y2.reshape(2, T, SSM_CH), u, dsk, wglu_bf, wout_bf, x, mod_l, g, wr_pad)


def _select_body(cap, aff_ref, pos_ref, post_ref, afft_ref):
    L = aff_ref.shape[1]
    aff = aff_ref[0]
    for r in range(1, SEL_GROUP):
        aff = aff + pltpu.roll(aff_ref[r], r * N_EXPERTS, 1)
    bits = pltpu.bitcast(aff, jnp.int32)
    capf = jnp.float32(cap)

    def bisect(k, thr):
        cand = thr | jnp.left_shift(jnp.int32(1), 30 - k)
        cnt = jnp.sum(jnp.where(bits >= cand, 1.0, 0.0), axis=0, keepdims=True)
        return jnp.where(cnt >= capf, cand, thr)

    thr = lax.fori_loop(0, 31, bisect, jnp.zeros((1, LANES), jnp.int32))
    gt = jnp.where(bits > thr, 1.0, 0.0)
    eq = jnp.where(bits == thr, 1.0, 0.0)
    need = capf - jnp.sum(gt, axis=0, keepdims=True)
    ck = SEL_CHUNK
    tri = (lax.broadcasted_iota(jnp.int32, (ck, ck), 0)
           >= lax.broadcasted_iota(jnp.int32, (ck, ck), 1))
    tri = jnp.where(tri, 1.0, 0.0).astype(BF16)

    def prefix(mask):
        outs = []
        carry = jnp.zeros((1, LANES), F32)
        for j in range(L // ck):
            c = _dot(tri, mask[j * ck:(j + 1) * ck].astype(BF16)) + carry
            outs.append(c)
            carry = c[ck - 1:ck, :]
        return jnp.concatenate(outs, axis=0)

    tie = jnp.where(prefix(eq) <= need, eq, 0.0)
    sel = gt + tie
    pos = jnp.where(sel > 0.0, prefix(sel) - 1.0, -1.0)
    pos_ref[0] = pos
    post_ref[0] = jnp.transpose(pos)
    afft_ref[0] = jnp.transpose(aff)


def _select(aff, B, L):
    cap = EC_CAPACITY * L // N_EXPERTS
    ng = B // SEL_GROUP
    return pl.pallas_call(
        functools.partial(_select_body, cap),
        grid=(ng,),
        in_specs=[pl.BlockSpec((SEL_GROUP, L, LANES), lambda i: (i, 0, 0))],
        out_specs=[pl.BlockSpec((1, L, LANES), lambda i: (i, 0, 0)),
                   pl.BlockSpec((1, LANES, L), lambda i: (i, 0, 0)),
                   pl.BlockSpec((1, LANES, L), lambda i: (i, 0, 0))],
        out_shape=[jax.ShapeDtypeStruct((ng, L, LANES), F32),
                   jax.ShapeDtypeStruct((ng, LANES, L), F32),
                   jax.ShapeDtypeStruct((ng, LANES, L), F32)],
        compiler_params=_cparams(("arbitrary",)),
        name="ec_select",
    )(aff.reshape(B, L, LANES))


def _gather_body(cap, group, h_ref, post_ref, afft_ref, xs_ref, gs_ref):
    h = h_ref[0]
    L = h.shape[0]
    slot = lax.broadcasted_iota(jnp.int32, (cap, L), 0).astype(F32)
    for e0 in range(0, N_EXPERTS, group):
        onehots = []
        for e in range(e0, e0 + group):
            hit = post_ref[0, e:e + 1, :] == slot
            gs_ref[e, 0] = jnp.sum(jnp.where(hit, afft_ref[0, e:e + 1, :], 0.0), axis=1, keepdims=True)
            onehots.append(jnp.where(hit, 1.0, 0.0).astype(BF16))
        xs = _dot(jnp.concatenate(onehots, axis=0), h)
        for k in range(group):
            xs_ref[e0 + k, 0] = xs[k * cap:(k + 1) * cap].astype(BF16)


def _gather(h2, post, afft, B, L):
    cap = EC_CAPACITY * L // N_EXPERTS
    group = max(1, min(N_EXPERTS, 512 // cap))
    xs, gs = pl.pallas_call(
        functools.partial(_gather_body, cap, group),
        grid=(B,),
        in_specs=[pl.BlockSpec((1, L, D_MODEL), lambda b: (b, 0, 0)),
                  pl.BlockSpec((1, N_EXPERTS, L), lambda b: (b // SEL_GROUP, b % SEL_GROUP, 0)),
                  pl.BlockSpec((1, N_EXPERTS, L), lambda b: (b // SEL_GROUP, b % SEL_GROUP, 0))],
        out_specs=[pl.BlockSpec((N_EXPERTS, 1, cap, D_MODEL), lambda b: (0, b, 0, 0)),
                   pl.BlockSpec((N_EXPERTS, 1, cap, 1), lambda b: (0, b, 0, 0))],
        out_shape=[jax.ShapeDtypeStruct((N_EXPERTS, B, cap, D_MODEL), BF16),
                   jax.ShapeDtypeStruct((N_EXPERTS, B, cap, 1), F32)],
        compiler_params=_cparams(("arbitrary",)),
        name="ec_gather",
    )(h2.reshape(B, L, D_MODEL), post, afft)
    return xs.reshape(N_EXPERTS, B * cap, D_MODEL), gs.reshape(N_EXPERTS, B * cap, 1)


def _ffn_body(n_first, xa_ref, ga_ref, xb_ref, gb_ref, wg_ref, wu_ref, wd_ref, y_ref, wgb, wub, wdb):
    j = pl.program_id(1)

    @pl.when(j == 0)
    def _():
        wgb[...] = wg_ref[0, 0].astype(BF16)
        wub[...] = wu_ref[0, 0].astype(BF16)
        wdb[...] = wd_ref[0, 0].astype(BF16)

    def run(xs_ref, gs_ref):
        x = xs_ref[0]
        a = _dot(x, wgb[...])
        u = _dot(x, wub[...])
        hm = (a * _sigmoid(a) * u).astype(BF16)
        y_ref[0] = (_dot(hm, wdb[...]) * gs_ref[0]).astype(BF16)

    @pl.when(j < n_first)
    def _():
        run(xa_ref, ga_ref)

    @pl.when(j >= n_first)
    def _():
        run(xb_ref, gb_ref)


def _ffn(xa, ga, xb, gb, wg, wu, wd, l):
    rc = FFN_ROWS
    na = xa.shape[1] // rc
    nb = xb.shape[1] // rc
    first = lambda w: pl.BlockSpec((1, rc, w), lambda e, j: (e, jnp.minimum(j, na - 1), 0))
    second = lambda w: pl.BlockSpec((1, rc, w), lambda e, j: (e, jnp.maximum(j - na, 0), 0))
    wspec = lambda a, b: pl.BlockSpec((1, 1, a, b), lambda e, j: (l, e, 0, 0))
    return pl.pallas_call(
        functools.partial(_ffn_body, na),
        grid=(N_EXPERTS, na + nb),
        in_specs=[first(D_MODEL), first(1), second(D_MODEL), second(1),
                  wspec(D_MODEL, EXPERT_FF), wspec(D_MODEL, EXPERT_FF), wspec(EXPERT_FF, D_MODEL)],
        out_specs=pl.BlockSpec((1, rc, D_MODEL), lambda e, j: (e, j, 0)),
        out_shape=jax.ShapeDtypeStruct((N_EXPERTS, (na + nb) * rc, D_MODEL), BF16),
        scratch_shapes=[pltpu.VMEM((D_MODEL, EXPERT_FF), BF16),
                        pltpu.VMEM((D_MODEL, EXPERT_FF), BF16),
                        pltpu.VMEM((EXPERT_FF, D_MODEL), BF16)],
        compiler_params=_cparams(("arbitrary", "arbitrary")),
        name="ec_ffn",
    )(xa, ga, xb, gb, wg, wu, wd)


def _combine_body(cap, pos_ref, y_ref, x1_ref, mod_ref, g_ref, o_ref):
    n = N_EXPERTS * cap
    pos = pos_ref[0]
    lane0 = (pl.program_id(0) % SEL_GROUP) * N_EXPERTS
    col = lax.broadcasted_iota(jnp.int32, (LANES, n), 1)
    expand = jnp.where(lane0 + col // cap == lax.broadcasted_iota(jnp.int32, (LANES, n), 0), 1.0, 0.0)
    per_col = _dot(pos.astype(BF16), expand.astype(BF16))
    target = (lax.broadcasted_iota(jnp.int32, (1, n), 1) % cap).astype(F32)
    onehot = jnp.where(per_col == target, 1.0, 0.0).astype(BF16)
    f = _dot(onehot, y_ref[...].reshape(n, D_MODEL))
    g2 = mod_ref[0][:, 5 * D_MODEL:6 * D_MODEL]
    o_ref[...] = x1_ref[...] + g2 * _rms(f, g_ref[3:4, :])


def _combine(pos, y, row_off, x1, mod_l, mod_row_fn, g, B, L):
    cap = EC_CAPACITY * L // N_EXPERTS
    tq = TOKEN_TILE
    nq = L // tq
    blk_off = row_off // cap
    return pl.pallas_call(
        functools.partial(_combine_body, cap),
        grid=(B, nq),
        in_specs=[pl.BlockSpec((1, tq, LANES), lambda b, i: (b // SEL_GROUP, i, 0)),
                  pl.BlockSpec((N_EXPERTS, cap, D_MODEL), lambda b, i: (0, blk_off + b, 0)),
                  pl.BlockSpec((tq, D_MODEL), lambda b, i: (b * nq + i, 0)),
                  pl.BlockSpec((1, 1, 6 * D_MODEL), lambda b, i: (mod_row_fn(b), 0, 0)),
                  pl.BlockSpec((4, D_MODEL), lambda b, i: (0, 0))],
        out_specs=pl.BlockSpec((tq, D_MODEL), lambda b, i: (b * nq + i, 0)),
        out_shape=jax.ShapeDtypeStruct((B * L, D_MODEL), F32),
        compiler_params=_cparams(("arbitrary", "arbitrary")),
        name="ec_combine",
    )(pos, y, x1, mod_l, g)


def _rope_tables(L):
    t = jnp.arange(L)
    row = (t // GRID_W).astype(F32)
    col = (t % GRID_W).astype(F32)
    half = HEAD_DIM // 4
    inv = ROPE_BASE ** (-jnp.arange(half, dtype=F32) / half)
    d = np.arange(LANES) % HEAD_DIM
    use_col = (d // (HEAD_DIM // 2)) == 1
    pos = jnp.where(use_col[None, :], col[:, None], row[:, None])
    ang = pos * inv[d % half][None, :]
    sign = np.where((d % (HEAD_DIM // 2)) < half, -1.0, 1.0).astype(np.float32)
    return jnp.cos(ang), jnp.sin(ang) * sign[None, :]


def _permute_wa_heads(a, axis, start):
    cut = lambda lo, hi: lax.slice_in_dim(a, lo, hi, axis=axis)
    heads = [cut(start + h * HEAD_DIM, start + (h + 1) * HEAD_DIM) for h in WA_HEAD_ORDER]
    return jnp.concatenate([cut(0, start)] + heads + [cut(start + WA_WIDTH, a.shape[axis])], axis=axis)


def kernel(x_prompt, x_sample, c, cache_na_k, cache_na_v, cache_wa_k, cache_wa_v, state_ssm, c_ctx, w_ada, b_ada, norm_g, w_in, w_out, na_rpb, wa_sink, ssm_lambda_re, ssm_lambda_im, ssm_log_dt, ssm_b_re, ssm_b_im, ssm_c_re, ssm_c_im, ssm_d, w_glu, w_router, w_exp_gate, w_exp_up, w_exp_down):
    Bc, Lc, _ = x_prompt.shape
    Bs, Ls, _ = x_sample.shape
    ctx_row = Bs
    n_cond = ((Bs + 1 + SUBLANES - 1) // SUBLANES) * SUBLANES
    cond = jnp.zeros((n_cond, D_MODEL), F32).at[0:Bs].set(c).at[ctx_row].set(c_ctx)
    mod = _modulation(cond, w_ada, b_ada)
    mod = mod.reshape(DEPTH, n_cond, 1, 6 * D_MODEL)

    w_in_bf = _permute_wa_heads(w_in, 2, C_WAQ).astype(BF16)
    w_out_bf = _permute_wa_heads(w_out, 1, NA_WIDTH).astype(BF16)
    w_glu_bf = w_glu.astype(BF16)
    wr_hi = w_router.astype(BF16)
    wr_lo = (w_router - wr_hi.astype(F32)).astype(BF16)
    wr_pad = jnp.pad(jnp.concatenate([wr_hi, wr_lo], axis=-1), ((0, 0), (0, 0), (0, LANES - 2 * N_EXPERTS)))
    s5_a, s5_bb = _s5_prep(ssm_lambda_re, ssm_lambda_im, ssm_log_dt, ssm_b_re, ssm_b_im)
    s5_c = _s5_c_matrix(ssm_c_re, ssm_c_im)
    rope_tabs = _rope_tables(Ls)
    na_rows = Ls // GRID_W

    ctx_mod_row = lambda b: ctx_row + 0 * b
    lat_mod_row = lambda b: b

    xp = x_prompt.reshape(Bc * Lc, D_MODEL)
    xs = x_sample.reshape(Bs * Ls, D_MODEL)
    new_na_k, new_na_v, new_wa_k, new_wa_v, new_ssm = [], [], [], [], []
    for l in range(DEPTH):
        g = norm_g[l]
        dsk = ssm_d[l].reshape(1, SSM_CH)
        sl2 = slice(2 * l, 2 * l + 2)
        qna, kna, vna, qwa, kwa, vwa, u = _inproj(xp, mod[l], ctx_mod_row, g, w_in_bf[l], Lc, None)
        ona, owa = _attn_ctx(wa_sink[l], qna, kna, vna, qwa, kwa, vwa, Bc, Lc)
        y2, hfin = _s5_scan(u, s5_bb[sl2], s5_a[sl2], s5_c[sl2],
                            jnp.zeros((2, Bc, 2 * SSM_N), F32), Bc, Lc)
        x1, h2, aff = _outproj(ona, owa, y2, u, dsk, w_glu_bf[l], w_out_bf[l], xp, mod[l], ctx_mod_row,
                               g, wr_pad[l], Lc)
        pos_c, post, afft = _select(aff, Bc, Lc)
        xg_c, gs_c = _gather(h2, post, afft, Bc, Lc)
        x1_c = x1
        new_na_k.append(kna.reshape(Bc, Lc, NA_HEADS, HEAD_DIM))
        new_na_v.append(vna.reshape(Bc, Lc, NA_HEADS, HEAD_DIM))
        new_wa_k.append(kwa.reshape(Bc, Lc, WA_KV_HEADS, HEAD_DIM))
        new_wa_v.append(vwa.reshape(Bc, Lc, WA_KV_HEADS, HEAD_DIM))
        new_ssm.append(jnp.transpose(hfin.reshape(2, Bc, 2, SSM_GROUPS, SSM_STATE), (1, 0, 2, 3, 4)))
        qna, kna, vna, qwa, kwa, vwa, u = _inproj(xs, mod[l], lat_mod_row, g, w_in_bf[l], Ls, rope_tabs)
        ona = _attn_na(qna, kna, vna, cache_na_k, cache_na_v, _na_bias(na_rpb[l], na_rows), l, Bs, Ls)
        owa = _attn_wa(wa_sink[l], qwa, kwa, vwa, cache_wa_k, cache_wa_v, l, Bs, Ls)
        h0 = jnp.transpose(state_ssm[:, l].reshape(Bs, 2, 2 * SSM_N), (1, 0, 2))
        y2, _ = _s5_scan(u, s5_bb[sl2], s5_a[sl2], s5_c[sl2], h0, Bs, Ls)
        x1, h2, aff = _outproj(ona, owa, y2, u, dsk, w_glu_bf[l], w_out_bf[l], xs, mod[l], lat_mod_row,
                               g, wr_pad[l], Ls)
        pos_s, post, afft = _select(aff, Bs, Ls)
        xg_s, gs_s = _gather(h2, post, afft, Bs, Ls)
        yy = _ffn(xg_c, gs_c, xg_s, gs_s, w_exp_gate, w_exp_up, w_exp_down, l)
        xp = _combine(pos_c, yy, 0, x1_c, mod[l], ctx_mod_row, g, Bc, Lc)
        xs = _combine(pos_s, yy, xg_c.shape[1], x1, mod[l], lat_mod_row, g, Bs, Ls)
    return (xp.reshape(Bc, Lc, D_MODEL), xs.reshape(Bs, Ls, D_MODEL),
            jnp.stack(new_na_k, axis=1), jnp.stack(new_na_v, axis=1),
            jnp.stack(new_wa_k, axis=1), jnp.stack(new_wa_v, axis=1),
            jnp.stack(new_ssm, axis=1))
```

```python
import functools
import math

import numpy as np
import jax
import jax.numpy as jnp
from jax import lax
from jax.experimental import pallas as pl
from jax.experimental.pallas import tpu as pltpu

F32 = jnp.float32
BF16 = jnp.bfloat16

D_MODEL = 1024
DEPTH = 2
GRID_W = 64
HEAD_DIM = 64
NA_HEADS = 6
NA_KH = 8
NA_KW = 16
WA_HEADS = 6
WA_KV_HEADS = 2
WA_WINDOW = 128
WA_BLOCK = 128
SSM_CH = 256
SSM_GROUP_CH = 16
SSM_GROUPS = 16
SSM_STATE = 64
SSM_N = SSM_GROUPS * SSM_STATE
NA_WIDTH = NA_HEADS * HEAD_DIM
WA_WIDTH = WA_HEADS * HEAD_DIM
WA_KV_WIDTH = WA_KV_HEADS * HEAD_DIM
IN_COLS = 2048
N_EXPERTS = 16
EC_CAPACITY = 2
EXPERT_FF = 1024
ROPE_BASE = 10000.0
RMS_EPS = 1e-6
NEG_INF = -1e30
ATTN_SCALE = HEAD_DIM ** -0.5
LOG2E = math.log2(math.e)
Q_SCALE = ATTN_SCALE * LOG2E

LANES = 128
SUBLANES = 8
VMEM_LIMIT = 48 * 1024 * 1024

C_NAQ, C_NAK, C_NAV, C_WAQ, C_WAK, C_WAV, C_SSU = 0, 384, 768, 1152, 1536, 1664, 1792
WA_HEAD_ORDER = (0, 3, 1, 4, 2, 5)

NA_QROWS = 4
NA_KROWS = NA_QROWS + NA_KH
TOKEN_TILE = 256
SCAN_TILE = 128
SCAN_BATCH = SUBLANES
FFN_ROWS = 512
SEL_CHUNK = 256
SEL_GROUP = LANES // N_EXPERTS
BF16_ROWS = 2 * SUBLANES
COMBINE_WINDOW = 96


def _cparams(sem):
    return pltpu.CompilerParams(dimension_semantics=sem, vmem_limit_bytes=VMEM_LIMIT)


def _sigmoid(x):
    return 1.0 / (1.0 + jnp.exp(-x))


def _rms(x, g):
    ms = jnp.mean(x * x, axis=-1, keepdims=True)
    return x * lax.rsqrt(ms + RMS_EPS) * g


def _dot(a, b):
    return jnp.dot(a, b, preferred_element_type=F32)


def _dot_nt(a, b):
    return lax.dot_general(a, b, (((1,), (1,)), ((), ())), preferred_element_type=F32)


def _mod_body(c_ref, w_ref, b_ref, o_ref):
    c = c_ref[...]
    s = c * _sigmoid(c)
    o_ref[0] = _dot(s.astype(BF16), w_ref[0].astype(BF16)) + b_ref[0]


def _modulation(cond, w_ada, b_ada):
    n = cond.shape[0]
    tn = 1024
    return pl.pallas_call(
        _mod_body,
        grid=(DEPTH, 6 * D_MODEL // tn),
        in_specs=[
            pl.BlockSpec((n, D_MODEL), lambda l, j: (0, 0)),
            pl.BlockSpec((1, D_MODEL, tn), lambda l, j: (l, 0, j)),
            pl.BlockSpec((1, 1, tn), lambda l, j: (l, 0, j)),
        ],
        out_specs=pl.BlockSpec((1, n, tn), lambda l, j: (l, 0, j)),
        out_shape=jax.ShapeDtypeStruct((DEPTH, n, 6 * D_MODEL), F32),
        compiler_params=_cparams(("arbitrary", "arbitrary")),
        name="modulation",
    )(cond, w_ada, b_ada.reshape(DEPTH, 1, 6 * D_MODEL))


def _rope_chunk(x, cos, sin_signed):
    lane = lax.broadcasted_iota(jnp.int32, x.shape, 1)
    first = (lane % 32) < 16
    rot = jnp.where(first, pltpu.roll(x, LANES - 16, 1), pltpu.roll(x, 16, 1))
    return x * cos + rot * sin_signed


def _inproj_body(rope, x_ref, mod_ref, g_ref, w_ref, *rest):
    if rope:
        cos_ref, sin_ref = rest[:2]
        rest = rest[2:]
    qna_ref, kna_ref, vna_ref, qwa_ref, kwa_ref, vwa_ref, u_ref = rest
    x = x_ref[...]
    m = mod_ref[0]
    sh1 = m[:, 0:D_MODEL]
    sc1 = m[:, D_MODEL:2 * D_MODEL]
    h = _rms(x, g_ref[0:1, :]) * (1.0 + sc1) + sh1
    z = _dot(h.astype(BF16), w_ref[...])
    qna_ref[...] = (z[:, C_NAQ:C_NAK] * Q_SCALE).astype(BF16)
    kna_ref[...] = z[:, C_NAK:C_NAV]
    vna_ref[...] = z[:, C_NAV:C_WAQ]
    vwa_ref[...] = z[:, C_WAV:C_SSU]
    u_ref[...] = z[:, C_SSU:IN_COLS]
    if rope:
        cos = cos_ref[...]
        sin = sin_ref[...]
        for j in range(WA_WIDTH // LANES):
            qc = _rope_chunk(z[:, C_WAQ + j * LANES:C_WAQ + (j + 1) * LANES], cos, sin)
            qwa_ref[:, j * LANES:(j + 1) * LANES] = (qc * Q_SCALE).astype(BF16)
        kwa_ref[...] = _rope_chunk(z[:, C_WAK:C_WAV], cos, sin)
    else:
        qwa_ref[...] = (z[:, C_WAQ:C_WAK] * Q_SCALE).astype(BF16)
        kwa_ref[...] = z[:, C_WAK:C_WAV]


def _inproj(x, mod_l, mod_row_fn, g, w_bf, L, rope_tabs):
    T = x.shape[0]
    tm = TOKEN_TILE
    tiles_per_seq = L // tm
    in_specs = [
        pl.BlockSpec((tm, D_MODEL), lambda i: (i, 0)),
        pl.BlockSpec((1, 1, 6 * D_MODEL), lambda i: (mod_row_fn(i // tiles_per_seq), 0, 0)),
        pl.BlockSpec((4, D_MODEL), lambda i: (0, 0)),
        pl.BlockSpec((D_MODEL, IN_COLS), lambda i: (0, 0)),
    ]
    args = [x, mod_l, g, w_bf]
    if rope_tabs is not None:
        in_specs += [pl.BlockSpec((tm, LANES), lambda i: (i % tiles_per_seq, 0))] * 2
        args += list(rope_tabs)
    widths = (NA_WIDTH, NA_WIDTH, NA_WIDTH, WA_WIDTH, WA_KV_WIDTH, WA_KV_WIDTH, SSM_CH)
    dtypes = (BF16, F32, F32, BF16, F32, F32, F32)
    return pl.pallas_call(
        functools.partial(_inproj_body, rope_tabs is not None),
        grid=(T // tm,),
        in_specs=in_specs,
        out_specs=[pl.BlockSpec((tm, w), lambda i: (i, 0)) for w in widths],
        out_shape=[jax.ShapeDtypeStruct((T, w), dt) for w, dt in zip(widths, dtypes)],
        compiler_params=_cparams(("arbitrary",)),
        name="inproj",
    )(*args)


def _lane_lo(shape):
    return lax.broadcasted_iota(jnp.int32, shape, len(shape) - 1) < HEAD_DIM


def _keep_half(q, lo, half):
    keep = jnp.where(lo, 1.0 - half, 0.0 + half).astype(q.dtype)
    return q * keep


def _values_and_ones(v, lo, half):
    one = jnp.ones_like(v)
    return (jnp.where(lo, v, one) if half == 0 else jnp.where(lo, one, v)).astype(BF16)


def _normalise(o2, extra=None):
    den = pltpu.roll(o2, HEAD_DIM, 1)
    if extra is not None:
        den = den + extra
    return o2 / den


def _attn_ctx_body(sink_ref, qna, kna, vna, qwa, kwa, vwa, ona, owa):
    L = qna.shape[1]
    lo = _lane_lo((L, LANES))
    for j in range(NA_WIDTH // LANES):
        sl = slice(j * LANES, (j + 1) * LANES)
        q2 = qna[0, :, sl]
        k2 = kna[0, :, sl].astype(BF16)
        v2 = vna[0, :, sl].astype(BF16)
        halves = []
        for half in range(2):
            s = _dot_nt(_keep_half(q2, lo, half), k2)
            p = jnp.exp2(s - jnp.max(s, axis=-1, keepdims=True))
            halves.append(_dot(p.astype(BF16), v2) / jnp.sum(p, axis=-1, keepdims=True))
        ona[0, :, sl] = jnp.where(lo, halves[0], halves[1]).astype(BF16)
    kw = kwa[0].astype(BF16)
    vw = vwa[0].astype(BF16)
    for j in range(WA_WIDTH // LANES):
        sl = slice(j * LANES, (j + 1) * LANES)
        q2 = qwa[0, :, sl]
        halves = []
        for half in range(2):
            s = _dot_nt(_keep_half(q2, lo, half), kw)
            sk = sink_ref[WA_HEAD_ORDER[2 * j + half]] * LOG2E
            m = jnp.maximum(jnp.max(s, axis=-1, keepdims=True), sk)
            p = jnp.exp2(s - m)
            l = jnp.sum(p, axis=-1, keepdims=True) + jnp.exp2(sk - m)
            halves.append(_dot(p.astype(BF16), vw) / l)
        owa[0, :, sl] = jnp.where(lo, halves[0], halves[1]).astype(BF16)


def _attn_ctx(sink, qna, kna, vna, qwa, kwa, vwa, B, L):
    def spec(w):
        return pl.BlockSpec((1, L, w), lambda b: (b, 0, 0))

    r3 = lambda a: a.reshape(B, L, a.shape[-1])
    ona, owa = pl.pallas_call(
        _attn_ctx_body,
        grid=(B,),
        in_specs=[pl.BlockSpec(memory_space=pltpu.SMEM),
                  spec(NA_WIDTH), spec(NA_WIDTH), spec(NA_WIDTH),
                  spec(WA_WIDTH), spec(WA_KV_WIDTH), spec(WA_KV_WIDTH)],
        out_specs=[spec(NA_WIDTH), spec(WA_WIDTH)],
        out_shape=[jax.ShapeDtypeStruct((B, L, NA_WIDTH), BF16),
                   jax.ShapeDtypeStruct((B, L, WA_WIDTH), BF16)],
        compiler_params=_cparams(("arbitrary",)),
        name="attn_ctx",
    )(sink, r3(qna), r3(kna), r3(vna), r3(qwa), r3(kwa), r3(vwa))
    return ona.reshape(B * L, NA_WIDTH), owa.reshape(B * L, WA_WIDTH)


def _na_key_start(rb, rows):
    return jnp.clip(rb * NA_QROWS - NA_KH // 2, 0, rows - NA_KROWS)


def _attn_na_body(rows, q_ref, k_ref, v_ref, kc_ref, vc_ref, bias_ref, o_ref):
    rb = pl.program_id(1)
    nq = NA_QROWS * GRID_W
    nk = NA_KROWS * GRID_W
    start = pl.multiple_of(_na_key_start(rb, rows) * GRID_W, GRID_W)
    lo = _lane_lo((nq, LANES))
    lo_v = _lane_lo((1, LANES))
    for j in range(NA_WIDTH // LANES):
        sl = slice(j * LANES, (j + 1) * LANES)
        q2 = q_ref[0, :, sl]
        kl = k_ref[0, pl.ds(start, nk), sl].astype(BF16)
        vl = v_ref[0, pl.ds(start, nk), sl]
        kc = kc_ref[0, 0, :, sl].astype(BF16)
        vc = vc_ref[0, 0, :, sl]
        halves = []
        for half in range(2):
            qm = _keep_half(q2, lo, half)
            s_loc = _dot_nt(qm, kl) + bias_ref[0, 2 * j + half]
            s_ctx = _dot_nt(qm, kc)
            m = jnp.maximum(jnp.max(s_loc, axis=-1, keepdims=True),
                            jnp.max(s_ctx, axis=-1, keepdims=True))
            o2 = (_dot(jnp.exp2(s_loc - m).astype(BF16), _values_and_ones(vl, lo_v, half))
                  + _dot(jnp.exp2(s_ctx - m).astype(BF16), _values_and_ones(vc, lo_v, half)))
            halves.append(_normalise(o2))
        o_ref[0, :, sl] = jnp.where(lo, halves[0], halves[1]).astype(BF16)


def _na_bias(rpb_l, rows):
    nrb = rows // NA_QROWS
    pats = (0, 1, nrb - 1)
    cq = np.arange(GRID_W)
    col_start = np.clip(cq - NA_KW // 2, 0, GRID_W - NA_KW)
    col_ok = (cq[None, :] >= col_start[:, None]) & (cq[None, :] < col_start[:, None] + NA_KW)
    dcol = np.clip(cq[None, :] - cq[:, None], -(NA_KW - 1), NA_KW - 1) + (NA_KW - 1)
    pick = (dcol[None] == np.arange(2 * NA_KW - 1)[:, None, None]).astype(np.float32)
    exact = lax.Precision.HIGHEST
    tiles = jnp.einsum('hrd,dqc->hrqc', rpb_l.astype(F32) * LOG2E, pick, precision=exact)
    tiles = jnp.where(col_ok[None, None], tiles, NEG_INF)
    n_rel = 2 * NA_KH - 1
    tiles = jnp.concatenate([tiles, jnp.full((NA_HEADS, 1, GRID_W, GRID_W), NEG_INF, F32)], axis=1)
    which = np.zeros((len(pats), NA_QROWS, NA_KROWS, n_rel + 1), np.float32)
    for pi, rb in enumerate(pats):
        ks = int(np.clip(rb * NA_QROWS - NA_KH // 2, 0, rows - NA_KROWS))
        for i in range(NA_QROWS):
            qr = rb * NA_QROWS + i
            rs = int(np.clip(qr - NA_KH // 2, 0, rows - NA_KH))
            for jj in range(NA_KROWS):
                kr = ks + jj
                which[pi, i, jj, kr - qr + NA_KH - 1 if rs <= kr < rs + NA_KH else n_rel] = 1.0
    out = jnp.einsum('pijr,hrqc->phiqjc', which, tiles, precision=exact)
    return out.reshape(len(pats), NA_HEADS, NA_QROWS * GRID_W, NA_KROWS * GRID_W)


def _attn_na(q, k, v, cache_k, cache_v, bias, l, B, L):
    rows = L // GRID_W
    nrb = rows // NA_QROWS
    nq = NA_QROWS * GRID_W
    P = cache_k.shape[2]
    pat = lambda rb: jnp.where(rb == 0, 0, jnp.where(rb == nrb - 1, 2, 1))
    r3 = lambda a: a.reshape(B, L, a.shape[-1])
    ck = cache_k.reshape(B, DEPTH, P, NA_WIDTH)
    cv = cache_v.reshape(B, DEPTH, P, NA_WIDTH)
    o = pl.pallas_call(
        functools.partial(_attn_na_body, rows),
        grid=(B, nrb),
        in_specs=[
            pl.BlockSpec((1, nq, NA_WIDTH), lambda b, r: (b, r, 0)),
            pl.BlockSpec((1, L, NA_WIDTH), lambda b, r: (b, 0, 0)),
            pl.BlockSpec((1, L, NA_WIDTH), lambda b, r: (b, 0, 0)),
            pl.BlockSpec((1, 1, P, NA_WIDTH), lambda b, r: (b, l, 0, 0)),
            pl.BlockSpec((1, 1, P, NA_WIDTH), lambda b, r: (b, l, 0, 0)),
            pl.BlockSpec((1, NA_HEADS, nq, NA_KROWS * GRID_W), lambda b, r: (pat(r), 0, 0, 0)),
        ],
        out_specs=pl.BlockSpec((1, nq, NA_WIDTH), lambda b, r: (b, r, 0)),
        out_shape=jax.ShapeDtypeStruct((B, L, NA_WIDTH), BF16),
        compiler_params=_cparams(("arbitrary", "arbitrary")),
        name="attn_na",
    )(r3(q), r3(k), r3(v), ck, cv, bias)
    return o.reshape(B * L, NA_WIDTH)


def _attn_wa_body(L, sink_ref, q_ref, k_ref, v_ref, kc_ref, vc_ref, o_ref):
    n = pl.program_id(1)
    nk = 3 * WA_BLOCK
    npair = WA_WIDTH // LANES
    start = pl.multiple_of(jnp.clip((n - 1) * WA_BLOCK, 0, L - nk), WA_BLOCK)
    kl = k_ref[0, pl.ds(start, nk), :].astype(BF16)
    vl = v_ref[0, pl.ds(start, nk), :]
    kc = kc_ref[0, 0].astype(BF16)
    vc = vc_ref[0, 0]
    lo_v = _lane_lo((1, LANES))
    rows = npair * WA_BLOCK
    qpos = n * WA_BLOCK + lax.broadcasted_iota(jnp.int32, (rows, nk), 0) % WA_BLOCK
    kpos = start + lax.broadcasted_iota(jnp.int32, (rows, nk), 1)
    in_win = jnp.abs(kpos - qpos) <= WA_WINDOW
    lo = _lane_lo((rows, LANES))
    q_all = jnp.concatenate([q_ref[0, :, j * LANES:(j + 1) * LANES] for j in range(npair)], axis=0)
    blk = lax.broadcasted_iota(jnp.int32, (rows, 1), 0) // WA_BLOCK
    outs = []
    for g in range(WA_KV_HEADS):
        qm = _keep_half(q_all, lo, g)
        sk = jnp.zeros((rows, 1), F32)
        for j in range(npair):
            sk = jnp.where(blk == j, sink_ref[WA_HEAD_ORDER[2 * j + g]] * LOG2E, sk)
        s_loc = jnp.where(in_win, _dot_nt(qm, kl), NEG_INF)
        s_ctx = _dot_nt(qm, kc)
        m = jnp.maximum(jnp.maximum(jnp.max(s_loc, axis=-1, keepdims=True),
                                    jnp.max(s_ctx, axis=-1, keepdims=True)), sk)
        o2 = (_dot(jnp.exp2(s_loc - m).astype(BF16), _values_and_ones(vl, lo_v, g))
              + _dot(jnp.exp2(s_ctx - m).astype(BF16), _values_and_ones(vc, lo_v, g)))
        outs.append(_normalise(o2, jnp.exp2(sk - m)))
    o_all = jnp.where(lo, outs[0], outs[1]).astype(BF16)
    for j in range(npair):
        o_ref[0, :, j * LANES:(j + 1) * LANES] = o_all[j * WA_BLOCK:(j + 1) * WA_BLOCK]


def _attn_wa(sink, q, k, v, cache_k, cache_v, l, B, L):
    nb = L // WA_BLOCK
    P = cache_k.shape[2]
    r3 = lambda a: a.reshape(B, L, a.shape[-1])
    ck = cache_k.reshape(B, DEPTH, P, WA_KV_WIDTH)
    cv = cache_v.reshape(B, DEPTH, P, WA_KV_WIDTH)
    o = pl.pallas_call(
        functools.partial(_attn_wa_body, L),
        grid=(B, nb),
        in_specs=[
            pl.BlockSpec(memory_space=pltpu.SMEM),
            pl.BlockSpec((1, WA_BLOCK, WA_WIDTH), lambda b, n: (b, n, 0)),
            pl.BlockSpec((1, L, WA_KV_WIDTH), lambda b, n: (b, 0, 0)),
            pl.BlockSpec((1, L, WA_KV_WIDTH), lambda b, n: (b, 0, 0)),
            pl.BlockSpec((1, 1, P, WA_KV_WIDTH), lambda b, n: (b, l, 0, 0)),
            pl.BlockSpec((1, 1, P, WA_KV_WIDTH), lambda b, n: (b, l, 0, 0)),
        ],
        out_specs=pl.BlockSpec((1, WA_BLOCK, WA_WIDTH), lambda b, n: (b, n, 0)),
        out_shape=jax.ShapeDtypeStruct((B, L, WA_WIDTH), BF16),
        compiler_params=_cparams(("arbitrary", "arbitrary")),
        name="attn_wa",
    )(sink, r3(q), r3(k), r3(v), ck, cv)
    return o.reshape(B * L, WA_WIDTH)


def _s5_prep_body(lr_ref, li_ref, ldt_ref, br_ref, bi_ref, a_ref, bb_ref):
    lr = lr_ref[0]
    li = li_ref[0]
    dt = jnp.exp(ldt_ref[0])
    mag = jnp.exp(lr * dt)
    ar = mag * jnp.cos(li * dt)
    ai = mag * jnp.sin(li * dt)
    den = lr * lr + li * li
    nr = ar - 1.0
    fr = (nr * lr + ai * li) / den
    fi = (ai * lr - nr * li) / den
    a_ref[0, :, 0:SSM_N] = ar
    a_ref[0, :, SSM_N:2 * SSM_N] = ai
    br = br_ref[0]
    bi = bi_ref[0]
    bb_ref[0, :, 0:SSM_N] = (fr * br - fi * bi).astype(BF16)
    bb_ref[0, :, SSM_N:2 * SSM_N] = (fr * bi + fi * br).astype(BF16)


def _s5_prep(lam_re, lam_im, log_dt, b_re, b_im):
    n = DEPTH * 2
    eye = jnp.eye(SSM_GROUPS, dtype=F32)

    def blockdiag_b(b):
        return jnp.einsum('ngph,gk->nghkp', b.reshape(n, SSM_GROUPS, SSM_STATE, SSM_GROUP_CH), eye
                          ).reshape(n, SSM_CH, SSM_N)

    lr = lam_re.reshape(n, 1, SSM_N)
    li = lam_im.reshape(n, 1, SSM_N)
    ldt = jnp.repeat(log_dt.reshape(n, SSM_GROUPS), SSM_STATE, axis=-1).reshape(n, 1, SSM_N)
    vec = pl.BlockSpec((1, 1, SSM_N), lambda i: (i, 0, 0))
    mat = pl.BlockSpec((1, SSM_CH, SSM_N), lambda i: (i, 0, 0))
    return pl.pallas_call(
        _s5_prep_body,
        grid=(n,),
        in_specs=[vec, vec, vec, mat, mat],
        out_specs=[pl.BlockSpec((1, 1, 2 * SSM_N), lambda i: (i, 0, 0)),
                   pl.BlockSpec((1, SSM_CH, 2 * SSM_N), lambda i: (i, 0, 0))],
        out_shape=[jax.ShapeDtypeStruct((n, 1, 2 * SSM_N), F32),
                   jax.ShapeDtypeStruct((n, SSM_CH, 2 * SSM_N), BF16)],
        compiler_params=_cparams(("arbitrary",)),
        name="s5_prep",
    )(lr, li, ldt, blockdiag_b(b_re), blockdiag_b(b_im))


def _s5_c_matrix(c_re, c_im):
    n = DEPTH * 2
    eye = jnp.eye(SSM_GROUPS, dtype=F32)

    def blk(c):
        return jnp.einsum('nghp,gk->ngpkh', c.reshape(n, SSM_GROUPS, SSM_GROUP_CH, SSM_STATE), eye
                          ).reshape(n, SSM_N, SSM_CH)

    return jnp.concatenate([blk(c_re), -blk(c_im)], axis=1).astype(BF16)


def _s5_scan_body(tl, u_ref, bb_ref, a_ref, c_ref, h0_ref, y_ref, hfin_ref, bu, hst):
    d = pl.program_id(0)
    i = pl.program_id(2)

    @pl.when(i == 0)
    def _():
        hst[...] = h0_ref[0]

    u_tb = jnp.transpose(u_ref[...], (1, 0, 2)).reshape(tl * SCAN_BATCH, SSM_CH)
    bu[...] = _dot(u_tb.astype(BF16), bb_ref[0])
    a = a_ref[0]
    ar = jnp.broadcast_to(a[:, 0:SSM_N], (SCAN_BATCH, SSM_N))
    ai = jnp.broadcast_to(a[:, SSM_N:2 * SSM_N], (SCAN_BATCH, SSM_N))

    def step(s, carry):
        hr, hi = carry
        t = s + d * (tl - 1 - 2 * s)
        rows = pl.ds(pl.multiple_of(t * SCAN_BATCH, SCAN_BATCH), SCAN_BATCH)
        nhr = ar * hr - ai * hi + bu[rows, 0:SSM_N]
        nhi = ar * hi + ai * hr + bu[rows, SSM_N:2 * SSM_N]
        bu[rows, 0:SSM_N] = nhr
        bu[rows, SSM_N:2 * SSM_N] = nhi
        return nhr, nhi

    hr, hi = lax.fori_loop(0, tl, step, (hst[:, 0:SSM_N], hst[:, SSM_N:2 * SSM_N]), unroll=2)
    hst[:, 0:SSM_N] = hr
    hst[:, SSM_N:2 * SSM_N] = hi
    hfin_ref[0, :, 0:SSM_N] = hr
    hfin_ref[0, :, SSM_N:2 * SSM_N] = hi
    y_tb = _dot(bu[...].astype(BF16), c_ref[0])
    y_ref[0] = jnp.transpose(y_tb.reshape(tl, SCAN_BATCH, SSM_CH), (1, 0, 2))


def _s5_scan(u, bb, a, cm, h0, B, L):
    tl = SCAN_TILE
    nt = L // tl
    tile = lambda d, i: i + d * (nt - 1 - 2 * i)
    return pl.pallas_call(
        functools.partial(_s5_scan_body, tl),
        grid=(2, B // SCAN_BATCH, nt),
        in_specs=[
            pl.BlockSpec((SCAN_BATCH, tl, SSM_CH), lambda d, b, i: (b, tile(d, i), 0)),
            pl.BlockSpec((1, SSM_CH, 2 * SSM_N), lambda d, b, i: (d, 0, 0)),
            pl.BlockSpec((1, 1, 2 * SSM_N), lambda d, b, i: (d, 0, 0)),
            pl.BlockSpec((1, 2 * SSM_N, SSM_CH), lambda d, b, i: (d, 0, 0)),
            pl.BlockSpec((1, SCAN_BATCH, 2 * SSM_N), lambda d, b, i: (d, b, 0)),
        ],
        out_specs=[
            pl.BlockSpec((1, SCAN_BATCH, tl, SSM_CH), lambda d, b, i: (d, b, tile(d, i), 0)),
            pl.BlockSpec((1, SCAN_BATCH, 2 * SSM_N), lambda d, b, i: (d, b, 0)),
        ],
        out_shape=[jax.ShapeDtypeStruct((2, B, L, SSM_CH), F32),
                   jax.ShapeDtypeStruct((2, B, 2 * SSM_N), F32)],
        scratch_shapes=[pltpu.VMEM((SCAN_BATCH * tl, 2 * SSM_N), F32),
                        pltpu.VMEM((SCAN_BATCH, 2 * SSM_N), F32)],
        compiler_params=_cparams(("arbitrary", "arbitrary", "arbitrary")),
        name="s5_scan",
    )(u.reshape(B, L, SSM_CH), bb, a, cm, h0)


def _outproj_body(ona_ref, owa_ref, y_ref, u_ref, dsk_ref, wglu_ref, wout_ref, x_ref, mod_ref, g_ref,
                  wr_ref, x1_ref, h2_ref, aff_ref):
    y = dsk_ref[...] * u_ref[...] + y_ref[0] + y_ref[1]
    zg = _dot(y.astype(BF16), wglu_ref[...])
    oss = zg[:, 0:SSM_CH] * _sigmoid(zg[:, SSM_CH:2 * SSM_CH])
    mix = (_dot(ona_ref[...], wout_ref[0:NA_WIDTH, :])
           + _dot(owa_ref[...], wout_ref[NA_WIDTH:NA_WIDTH + WA_WIDTH, :])
           + _dot(oss.astype(BF16), wout_ref[NA_WIDTH + WA_WIDTH:D_MODEL, :]))
    m = mod_ref[0]
    g1 = m[:, 2 * D_MODEL:3 * D_MODEL]
    sh2 = m[:, 3 * D_MODEL:4 * D_MODEL]
    sc2 = m[:, 4 * D_MODEL:5 * D_MODEL]
    x1 = x_ref[...] + g1 * _rms(mix, g_ref[1:2, :])
    x1_ref[...] = x1
    h2 = _rms(x1, g_ref[2:3, :]) * (1.0 + sc2) + sh2
    h2_hi = h2.astype(BF16)
    h2_ref[...] = h2_hi
    h2_lo = (h2 - h2_hi.astype(F32)).astype(BF16)
    r_hi = _dot(h2_hi, wr_ref[...])
    logits = r_hi + pltpu.roll(r_hi, LANES - N_EXPERTS, 1) + _dot(h2_lo, wr_ref[...])
    lane = lax.broadcasted_iota(jnp.int32, logits.shape, 1)
    logits = jnp.where(lane < N_EXPERTS, logits, NEG_INF)
    mx = jnp.max(logits, axis=-1, keepdims=True)
    p = jnp.exp(logits - mx)
    aff_ref[...] = p / jnp.sum(p, axis=-1, keepdims=True)


def _outproj(ona, owa, y2, u, dsk, wglu_bf, wout_bf, x, mod_l, mod_row_fn, g, wr_pad, L):
    T = x.shape[0]
    tm = TOKEN_TILE
    tiles_per_seq = L // tm
    row = lambda w: pl.BlockSpec((tm, w), lambda i: (i, 0))
    full = lambda a: pl.BlockSpec(a.shape, lambda i: (0,) * a.ndim)
    return pl.pallas_call(
        _outproj_body,
        grid=(T // tm,),
        in_specs=[row(NA_WIDTH), row(WA_WIDTH),
                  pl.BlockSpec((2, tm, SSM_CH), lambda i: (0, i, 0)),
                  row(SSM_CH), full(dsk), full(wglu_bf), full(wout_bf), row(D_MODEL),
                  pl.BlockSpec((1, 1, 6 * D_MODEL), lambda i: (mod_row_fn(i // tiles_per_seq), 0, 0)),
                  full(g), full(wr_pad)],
        out_specs=[row(D_MODEL), row(D_MODEL), row(LANES)],
        out_shape=[jax.ShapeDtypeStruct((T, D_MODEL), F32),
                   jax.ShapeDtypeStruct((T, D_MODEL), BF16),
                   jax.ShapeDtypeStruct((T, LANES), F32)],
        compiler_params=_cparams(("arbitrary",)),
        name="outproj",
    )(ona, owa, y2.reshape(2, T, SSM_CH), u, dsk, wglu_bf, wout_bf, x, mod_l, g, wr_pad)


def _select_body(cap, aff_ref, pos_ref, post_ref, afft_ref, starts_ref):
    L = aff_ref.shape[1]
    aff = aff_ref[0]
    for r in range(1, SEL_GROUP):
        aff = aff + pltpu.roll(aff_ref[r], r * N_EXPERTS, 1)
    bits = pltpu.bitcast(aff, jnp.int32)
    capf = jnp.float32(cap)

    def bisect(k, thr):
        cand = thr | jnp.left_shift(jnp.int32(1), 30 - k)
        cnt = jnp.sum(jnp.where(bits >= cand, 1.0, 0.0), axis=0, keepdims=True)
        return jnp.where(cnt >= capf, cand, thr)

    thr = lax.fori_loop(0, 31, bisect, jnp.zeros((1, LANES), jnp.int32))
    gt = jnp.where(bits > thr, 1.0, 0.0)
    eq = jnp.where(bits == thr, 1.0, 0.0)
    need = capf - jnp.sum(gt, axis=0, keepdims=True)
    ck = SEL_CHUNK
    tri = (lax.broadcasted_iota(jnp.int32, (ck, ck), 0)
           >= lax.broadcasted_iota(jnp.int32, (ck, ck), 1))
    tri = jnp.where(tri, 1.0, 0.0).astype(BF16)

    def prefix(mask):
        outs = []
        carries = [jnp.zeros((1, LANES), F32)]
        for j in range(L // ck):
            c = _dot(tri, mask[j * ck:(j + 1) * ck].astype(BF16)) + carries[-1]
            outs.append(c)
            carries.append(c[ck - 1:ck, :])
        return jnp.concatenate(outs, axis=0), jnp.concatenate(carries, axis=0)

    tie = jnp.where(prefix(eq)[0] <= need, eq, 0.0)
    sel = gt + tie
    rank, starts = prefix(sel)
    pos = jnp.where(sel > 0.0, rank - 1.0, -1.0)
    pos_ref[0] = pos
    post_ref[0] = jnp.transpose(pos)
    afft_ref[0] = jnp.transpose(aff)
    starts_ref[0] = starts.astype(jnp.int32)


def _select(aff, B, L):
    cap = EC_CAPACITY * L // N_EXPERTS
    ng = B // SEL_GROUP
    return pl.pallas_call(
        functools.partial(_select_body, cap),
        grid=(ng,),
        in_specs=[pl.BlockSpec((SEL_GROUP, L, LANES), lambda i: (i, 0, 0))],
        out_specs=[pl.BlockSpec((1, L, LANES), lambda i: (i, 0, 0)),
                   pl.BlockSpec((1, LANES, L), lambda i: (i, 0, 0)),
                   pl.BlockSpec((1, LANES, L), lambda i: (i, 0, 0)),
                   pl.BlockSpec((1, L // SEL_CHUNK + 1, LANES), lambda i: (i, 0, 0))],
        out_shape=[jax.ShapeDtypeStruct((ng, L, LANES), F32),
                   jax.ShapeDtypeStruct((ng, LANES, L), F32),
                   jax.ShapeDtypeStruct((ng, LANES, L), F32),
                   jax.ShapeDtypeStruct((ng, L // SEL_CHUNK + 1, LANES), jnp.int32)],
        compiler_params=_cparams(("arbitrary",)),
        name="ec_select",
    )(aff.reshape(B, L, LANES))


def _window_rows(base, w):
    return pl.ds(base, w) if isinstance(base, int) else pl.ds(pl.multiple_of(base, BF16_ROWS), w)


def _gather_body(cap, win, starts_ref, h_ref, post_ref, afft_ref, xs_ref, gs_ref):
    b = pl.program_id(0)
    j = pl.program_id(1)
    grp = b // SEL_GROUP
    lane0 = (b % SEL_GROUP) * N_EXPERTS

    @pl.when(j == 0)
    def _():
        xs_ref[...] = jnp.zeros_like(xs_ref)
        gs_ref[...] = jnp.zeros_like(gs_ref)

    h = h_ref[0]
    ck = h.shape[0]

    def run(w, bases):
        slot = lax.broadcasted_iota(jnp.int32, (w, ck), 0)
        onehots = []
        for e in range(N_EXPERTS):
            rows = _window_rows(bases[e], w)
            hit = post_ref[0, e:e + 1, :] == (slot + bases[e]).astype(F32)
            gs_ref[e, 0, rows, :] += jnp.sum(jnp.where(hit, afft_ref[0, e:e + 1, :], 0.0), axis=1, keepdims=True)
            onehots.append(jnp.where(hit, 1.0, 0.0).astype(BF16))
        xs = _dot(jnp.concatenate(onehots, axis=0), h).astype(BF16)
        for e in range(N_EXPERTS):
            rows = _window_rows(bases[e], w)
            xs_ref[e, 0, rows, :] += xs[e * w:(e + 1) * w]

    if win == cap:
        run(cap, [0] * N_EXPERTS)
        return
    bases = []
    fits = None
    for e in range(N_EXPERTS):
        s0 = starts_ref[grp, j, lane0 + e]
        s1 = starts_ref[grp, j + 1, lane0 + e]
        base = jnp.minimum((s0 // BF16_ROWS) * BF16_ROWS, cap - win)
        bases.append(base)
        ok = s1 - base <= win
        fits = ok if fits is None else jnp.logical_and(fits, ok)

    @pl.when(fits)
    def _():
        run(win, bases)

    @pl.when(jnp.logical_not(fits))
    def _():
        run(cap, [0] * N_EXPERTS)


def _gather(starts, h2, post, afft, B, L):
    cap = EC_CAPACITY * L // N_EXPERTS
    ck = SEL_CHUNK
    win = min(cap, COMBINE_WINDOW)
    lane_blk = lambda b, j, s: (b // SEL_GROUP, b % SEL_GROUP, j)
    grid_spec = pltpu.PrefetchScalarGridSpec(
        num_scalar_prefetch=1,
        grid=(B, L // ck),
        in_specs=[pl.BlockSpec((1, ck, D_MODEL), lambda b, j, s: (b, j, 0)),
                  pl.BlockSpec((1, N_EXPERTS, ck), lane_blk),
                  pl.BlockSpec((1, N_EXPERTS, ck), lane_blk)],
        out_specs=[pl.BlockSpec((N_EXPERTS, 1, cap, D_MODEL), lambda b, j, s: (0, b, 0, 0)),
                   pl.BlockSpec((N_EXPERTS, 1, cap, 1), lambda b, j, s: (0, b, 0, 0))],
    )
    xs, gs = pl.pallas_call(
        functools.partial(_gather_body, cap, win),
        grid_spec=grid_spec,
        out_shape=[jax.ShapeDtypeStruct((N_EXPERTS, B, cap, D_MODEL), BF16),
                   jax.ShapeDtypeStruct((N_EXPERTS, B, cap, 1), F32)],
        compiler_params=_cparams(("arbitrary", "arbitrary")),
        name="ec_gather",
    )(starts, h2.reshape(B, L, D_MODEL), post, afft)
    return xs.reshape(N_EXPERTS, B * cap, D_MODEL), gs.reshape(N_EXPERTS, B * cap, 1)


def _ffn_body(n_first, xa_ref, ga_ref, xb_ref, gb_ref, wg_ref, wu_ref, wd_ref, y_ref, wgb, wub, wdb):
    j = pl.program_id(1)

    @pl.when(j == 0)
    def _():
        wgb[...] = wg_ref[0, 0].astype(BF16)
        wub[...] = wu_ref[0, 0].astype(BF16)
        wdb[...] = wd_ref[0, 0].astype(BF16)

    def run(xs_ref, gs_ref):
        x = xs_ref[0]
        a = _dot(x, wgb[...])
        u = _dot(x, wub[...])
        hm = (a * _sigmoid(a) * u).astype(BF16)
        y_ref[0] = (_dot(hm, wdb[...]) * gs_ref[0]).astype(BF16)

    @pl.when(j < n_first)
    def _():
        run(xa_ref, ga_ref)

    @pl.when(j >= n_first)
    def _():
        run(xb_ref, gb_ref)


def _ffn(xa, ga, xb, gb, wg, wu, wd, l):
    rc = FFN_ROWS
    na = xa.shape[1] // rc
    nb = xb.shape[1] // rc
    first = lambda w: pl.BlockSpec((1, rc, w), lambda e, j: (e, jnp.minimum(j, na - 1), 0))
    second = lambda w: pl.BlockSpec((1, rc, w), lambda e, j: (e, jnp.maximum(j - na, 0), 0))
    wspec = lambda a, b: pl.BlockSpec((1, 1, a, b), lambda e, j: (l, e, 0, 0))
    return pl.pallas_call(
        functools.partial(_ffn_body, na),
        grid=(N_EXPERTS, na + nb),
        in_specs=[first(D_MODEL), first(1), second(D_MODEL), second(1),
                  wspec(D_MODEL, EXPERT_FF), wspec(D_MODEL, EXPERT_FF), wspec(EXPERT_FF, D_MODEL)],
        out_specs=pl.BlockSpec((1, rc, D_MODEL), lambda e, j: (e, j, 0)),
        out_shape=jax.ShapeDtypeStruct((N_EXPERTS, (na + nb) * rc, D_MODEL), BF16),
        scratch_shapes=[pltpu.VMEM((D_MODEL, EXPERT_FF), BF16),
                        pltpu.VMEM((D_MODEL, EXPERT_FF), BF16),
                        pltpu.VMEM((EXPERT_FF, D_MODEL), BF16)],
        compiler_params=_cparams(("arbitrary", "arbitrary")),
        name="ec_ffn",
    )(xa, ga, xb, gb, wg, wu, wd)


def _combine_body(cap, win, starts_ref, pos_ref, y_ref, x1_ref, mod_ref, g_ref, o_ref):
    b = pl.program_id(0)
    i = pl.program_id(1)
    grp = b // SEL_GROUP
    lane0 = (b % SEL_GROUP) * N_EXPERTS
    pos = pos_ref[0].astype(BF16)
    g2 = mod_ref[0][:, 5 * D_MODEL:6 * D_MODEL]

    def finish(w, bases, values):
        n = N_EXPERTS * w
        col = lax.broadcasted_iota(jnp.int32, (1, n), 1)
        expert = jnp.zeros((1, n), jnp.int32)
        for e in range(1, N_EXPERTS):
            expert = expert + jnp.where(col >= e * w, 1, 0)
        target = col - expert * w
        if bases is not None:
            for e in range(N_EXPERTS):
                target = jnp.where(expert == e, target + bases[e], target)
        expand = jnp.where(lax.broadcasted_iota(jnp.int32, (LANES, n), 0) == expert + lane0, 1.0, 0.0)
        per_col = _dot(pos, expand.astype(BF16))
        onehot = jnp.where(per_col == target.astype(F32), 1.0, 0.0).astype(BF16)
        f = _dot(onehot, values)
        o_ref[...] = x1_ref[...] + g2 * _rms(f, g_ref[3:4, :])

    if win == cap:
        finish(cap, None, y_ref[...].reshape(N_EXPERTS * cap, D_MODEL))
        return
    bases = []
    fits = None
    for e in range(N_EXPERTS):
        s0 = starts_ref[grp, i, lane0 + e]
        s1 = starts_ref[grp, i + 1, lane0 + e]
        base = jnp.minimum((s0 // BF16_ROWS) * BF16_ROWS, cap - win)
        bases.append(base)
        ok = s1 - base <= win
        fits = ok if fits is None else jnp.logical_and(fits, ok)

    @pl.when(fits)
    def _():
        rows = [y_ref[e, pl.ds(pl.multiple_of(bases[e], BF16_ROWS), win), :] for e in range(N_EXPERTS)]
        finish(win, bases, jnp.concatenate(rows, axis=0))

    @pl.when(jnp.logical_not(fits))
    def _():
        finish(cap, None, y_ref[...].reshape(N_EXPERTS * cap, D_MODEL))


def _combine(starts, pos, y, row_off, x1, mod_l, mod_row_fn, g, B, L):
    cap = EC_CAPACITY * L // N_EXPERTS
    tq = TOKEN_TILE
    assert tq == SEL_CHUNK
    nq = L // tq
    blk_off = row_off // cap
    win = min(cap, COMBINE_WINDOW)
    grid_spec = pltpu.PrefetchScalarGridSpec(
        num_scalar_prefetch=1,
        grid=(B, nq),
        in_specs=[pl.BlockSpec((1, tq, LANES), lambda b, i, s: (b // SEL_GROUP, i, 0)),
                  pl.BlockSpec((N_EXPERTS, cap, D_MODEL), lambda b, i, s: (0, blk_off + b, 0)),
                  pl.BlockSpec((tq, D_MODEL), lambda b, i, s: (b * nq + i, 0)),
                  pl.BlockSpec((1, 1, 6 * D_MODEL), lambda b, i, s: (mod_row_fn(b), 0, 0)),
                  pl.BlockSpec((4, D_MODEL), lambda b, i, s: (0, 0))],
        out_specs=pl.BlockSpec((tq, D_MODEL), lambda b, i, s: (b * nq + i, 0)),
    )
    return pl.pallas_call(
        functools.partial(_combine_body, cap, win),
        grid_spec=grid_spec,
        out_shape=jax.ShapeDtypeStruct((B * L, D_MODEL), F32),
        compiler_params=_cparams(("arbitrary", "arbitrary")),
        name="ec_combine",
    )(starts, pos, y, x1, mod_l, g)


def _rope_tables(L):
    t = jnp.arange(L)
    row = (t // GRID_W).astype(F32)
    col = (t % GRID_W).astype(F32)
    half = HEAD_DIM // 4
    inv = ROPE_BASE ** (-jnp.arange(half, dtype=F32) / half)
    d = np.arange(LANES) % HEAD_DIM
    use_col = (d // (HEAD_DIM // 2)) == 1
    pos = jnp.where(use_col[None, :], col[:, None], row[:, None])
    ang = pos * inv[d % half][None, :]
    sign = np.where((d % (HEAD_DIM // 2)) < half, -1.0, 1.0).astype(np.float32)
    return jnp.cos(ang), jnp.sin(ang) * sign[None, :]


def _permute_wa_heads(a, axis, start):
    cut = lambda lo, hi: lax.slice_in_dim(a, lo, hi, axis=axis)
    heads = [cut(start + h * HEAD_DIM, start + (h + 1) * HEAD_DIM) for h in WA_HEAD_ORDER]
    return jnp.concatenate([cut(0, start)] + heads + [cut(start + WA_WIDTH, a.shape[axis])], axis=axis)


def kernel(x_prompt, x_sample, c, cache_na_k, cache_na_v, cache_wa_k, cache_wa_v, state_ssm, c_ctx, w_ada, b_ada, norm_g, w_in, w_out, na_rpb, wa_sink, ssm_lambda_re, ssm_lambda_im, ssm_log_dt, ssm_b_re, ssm_b_im, ssm_c_re, ssm_c_im, ssm_d, w_glu, w_router, w_exp_gate, w_exp_up, w_exp_down):
    Bc, Lc, _ = x_prompt.shape
    Bs, Ls, _ = x_sample.shape
    ctx_row = Bs
    n_cond = ((Bs + 1 + SUBLANES - 1) // SUBLANES) * SUBLANES
    cond = jnp.zeros((n_cond, D_MODEL), F32).at[0:Bs].set(c).at[ctx_row].set(c_ctx)
    mod = _modulation(cond, w_ada, b_ada)
    mod = mod.reshape(DEPTH, n_cond, 1, 6 * D_MODEL)

    w_in_bf = _permute_wa_heads(w_in, 2, C_WAQ).astype(BF16)
    w_out_bf = _permute_wa_heads(w_out, 1, NA_WIDTH).astype(BF16)
    w_glu_bf = w_glu.astype(BF16)
    wr_hi = w_router.astype(BF16)
    wr_lo = (w_router - wr_hi.astype(F32)).astype(BF16)
    wr_pad = jnp.pad(jnp.concatenate([wr_hi, wr_lo], axis=-1), ((0, 0), (0, 0), (0, LANES - 2 * N_EXPERTS)))
    s5_a, s5_bb = _s5_prep(ssm_lambda_re, ssm_lambda_im, ssm_log_dt, ssm_b_re, ssm_b_im)
    s5_c = _s5_c_matrix(ssm_c_re, ssm_c_im)
    rope_tabs = _rope_tables(Ls)
    na_rows = Ls // GRID_W

    ctx_mod_row = lambda b: ctx_row + 0 * b
    lat_mod_row = lambda b: b

    xp = x_prompt.reshape(Bc * Lc, D_MODEL)
    xs = x_sample.reshape(Bs * Ls, D_MODEL)
    new_na_k, new_na_v, new_wa_k, new_wa_v, new_ssm = [], [], [], [], []
    for l in range(DEPTH):
        g = norm_g[l]
        dsk = ssm_d[l].reshape(1, SSM_CH)
        sl2 = slice(2 * l, 2 * l + 2)
        qna, kna, vna, qwa, kwa, vwa, u = _inproj(xp, mod[l], ctx_mod_row, g, w_in_bf[l], Lc, None)
        ona, owa = _attn_ctx(wa_sink[l], qna, kna, vna, qwa, kwa, vwa, Bc, Lc)
        y2, hfin = _s5_scan(u, s5_bb[sl2], s5_a[sl2], s5_c[sl2],
                            jnp.zeros((2, Bc, 2 * SSM_N), F32), Bc, Lc)
        x1, h2, aff = _outproj(ona, owa, y2, u, dsk, w_glu_bf[l], w_out_bf[l], xp, mod[l], ctx_mod_row,
                               g, wr_pad[l], Lc)
        pos_c, post, afft, starts_c = _select(aff, Bc, Lc)
        xg_c, gs_c = _gather(starts_c, h2, post, afft, Bc, Lc)
        x1_c = x1
        new_na_k.append(kna.reshape(Bc, Lc, NA_HEADS, HEAD_DIM))
        new_na_v.append(vna.reshape(Bc, Lc, NA_HEADS, HEAD_DIM))
        new_wa_k.append(kwa.reshape(Bc, Lc, WA_KV_HEADS, HEAD_DIM))
        new_wa_v.append(vwa.reshape(Bc, Lc, WA_KV_HEADS, HEAD_DIM))
        new_ssm.append(jnp.transpose(hfin.reshape(2, Bc, 2, SSM_GROUPS, SSM_STATE), (1, 0, 2, 3, 4)))
        qna, kna, vna, qwa, kwa, vwa, u = _inproj(xs, mod[l], lat_mod_row, g, w_in_bf[l], Ls, rope_tabs)
        ona = _attn_na(qna, kna, vna, cache_na_k, cache_na_v, _na_bias(na_rpb[l], na_rows), l, Bs, Ls)
        owa = _attn_wa(wa_sink[l], qwa, kwa, vwa, cache_wa_k, cache_wa_v, l, Bs, Ls)
        h0 = jnp.transpose(state_ssm[:, l].reshape(Bs, 2, 2 * SSM_N), (1, 0, 2))
        y2, _ = _s5_scan(u, s5_bb[sl2], s5_a[sl2], s5_c[sl2], h0, Bs, Ls)
        x1, h2, aff = _outproj(ona, owa, y2, u, dsk, w_glu_bf[l], w_out_bf[l], xs, mod[l], lat_mod_row,
                               g, wr_pad[l], Ls)
        pos_s, post, afft, starts_s = _select(aff, Bs, Ls)
        xg_s, gs_s = _gather(starts_s, h2, post, afft, Bs, Ls)
        yy = _ffn(xg_c, gs_c, xg_s, gs_s, w_exp_gate, w_exp_up, w_exp_down, l)
        xp = _combine(starts_c, pos_c, yy, 0, x1_c, mod[l], ctx_mod_row, g, Bc, Lc)
        xs = _combine(starts_s, pos_s, yy, xg_c.shape[1], x1, mod[l], lat_mod_row, g, Bs, Ls)
    return (xp.reshape(Bc, Lc, D_MODEL), xs.reshape(Bs, Ls, D_MODEL),
            jnp.stack(new_na_k, axis=1), jnp.stack(new_na_v, axis=1),
            jnp.stack(new_wa_k, axis=1), jnp.stack(new_wa_v, axis=1),
            jnp.stack(new_ssm, axis=1))
```

```python
import functools
import math

import numpy as np
import jax
import jax.numpy as jnp
from jax import lax
from jax.experimental import pallas as pl
from jax.experimental.pallas import tpu as pltpu

F32 = jnp.float32
BF16 = jnp.bfloat16

D_MODEL = 1024
DEPTH = 2
GRID_W = 64
HEAD_DIM = 64
NA_HEADS = 6
NA_KH = 8
NA_KW = 16
WA_HEADS = 6
WA_KV_HEADS = 2
WA_WINDOW = 128
WA_BLOCK = 128
SSM_CH = 256
SSM_GROUP_CH = 16
SSM_GROUPS = 16
SSM_STATE = 64
SSM_N = SSM_GROUPS * SSM_STATE
NA_WIDTH = NA_HEADS * HEAD_DIM
WA_WIDTH = WA_HEADS * HEAD_DIM
WA_KV_WIDTH = WA_KV_HEADS * HEAD_DIM
IN_COLS = 2048
N_EXPERTS = 16
EC_CAPACITY = 2
EXPERT_FF = 1024
ROPE_BASE = 10000.0
RMS_EPS = 1e-6
NEG_INF = -1e30
ATTN_SCALE = HEAD_DIM ** -0.5
LOG2E = math.log2(math.e)
Q_SCALE = ATTN_SCALE * LOG2E

LANES = 128
SUBLANES = 8
VMEM_LIMIT = 48 * 1024 * 1024

C_NAQ, C_NAK, C_NAV, C_WAQ, C_WAK, C_WAV, C_SSU = 0, 384, 768, 1152, 1536, 1664, 1792
WA_HEAD_ORDER = (0, 3, 1, 4, 2, 5)

NA_QROWS = 4
NA_KROWS = NA_QROWS + NA_KH
NA_REL = 2 * NA_KH - 1
NA_PAIRS = 3 * NA_REL
TOKEN_TILE = 256
SCAN_TILE = 128
SCAN_BATCH = SUBLANES
FFN_ROWS = 512
SEL_CHUNK = 256
SEL_GROUP = LANES // N_EXPERTS
BF16_ROWS = 2 * SUBLANES
COMBINE_WINDOW = 64


def _cparams(sem):
    return pltpu.CompilerParams(dimension_semantics=sem, vmem_limit_bytes=VMEM_LIMIT)


def _sigmoid(x):
    return 1.0 / (1.0 + jnp.exp(-x))


def _rms(x, g):
    ms = jnp.mean(x * x, axis=-1, keepdims=True)
    return x * lax.rsqrt(ms + RMS_EPS) * g


def _dot(a, b):
    return jnp.dot(a, b, preferred_element_type=F32)


def _dot_nt(a, b):
    return lax.dot_general(a, b, (((1,), (1,)), ((), ())), preferred_element_type=F32)


def _mod_body(c_ref, w_ref, b_ref, o_ref):
    c = c_ref[...]
    s = c * _sigmoid(c)
    o_ref[0] = _dot(s.astype(BF16), w_ref[0].astype(BF16)) + b_ref[0]


def _modulation(cond, w_ada, b_ada):
    n = cond.shape[0]
    tn = 1024
    return pl.pallas_call(
        _mod_body,
        grid=(DEPTH, 6 * D_MODEL // tn),
        in_specs=[
            pl.BlockSpec((n, D_MODEL), lambda l, j: (0, 0)),
            pl.BlockSpec((1, D_MODEL, tn), lambda l, j: (l, 0, j)),
            pl.BlockSpec((1, 1, tn), lambda l, j: (l, 0, j)),
        ],
        out_specs=pl.BlockSpec((1, n, tn), lambda l, j: (l, 0, j)),
        out_shape=jax.ShapeDtypeStruct((DEPTH, n, 6 * D_MODEL), F32),
        compiler_params=_cparams(("arbitrary", "arbitrary")),
        name="modulation",
    )(cond, w_ada, b_ada.reshape(DEPTH, 1, 6 * D_MODEL))


def _rope_chunk(x, cos, sin_signed):
    lane = lax.broadcasted_iota(jnp.int32, x.shape, 1)
    first = (lane % 32) < 16
    rot = jnp.where(first, pltpu.roll(x, LANES - 16, 1), pltpu.roll(x, 16, 1))
    return x * cos + rot * sin_signed


INPROJ_KV_OUTPUTS = (1, 2, 4, 5)


def _inproj_body(rope, n_alias, x_ref, mod_ref, g_ref, w_ref, *rest):
    if rope:
        cos_ref, sin_ref = rest[:2]
        rest = rest[2:]
    qna_ref, kna_ref, vna_ref, qwa_ref, kwa_ref, vwa_ref, u_ref = rest[n_alias:]

    def put(ref, val):
        ref[...] = val.reshape(ref.shape)

    x = x_ref[...]
    m = mod_ref[0, 0]
    sh1 = m[:, 0:D_MODEL]
    sc1 = m[:, D_MODEL:2 * D_MODEL]
    h = _rms(x, g_ref[0, 0:1, :]) * (1.0 + sc1) + sh1
    z = _dot(h.astype(BF16), w_ref[0])
    qna_ref[...] = (z[:, C_NAQ:C_NAK] * Q_SCALE).astype(BF16)
    put(kna_ref, z[:, C_NAK:C_NAV])
    put(vna_ref, z[:, C_NAV:C_WAQ])
    put(vwa_ref, z[:, C_WAV:C_SSU])
    u_ref[...] = z[:, C_SSU:IN_COLS]
    if rope:
        cos = cos_ref[...]
        sin = sin_ref[...]
        for j in range(WA_WIDTH // LANES):
            qc = _rope_chunk(z[:, C_WAQ + j * LANES:C_WAQ + (j + 1) * LANES], cos, sin)
            qwa_ref[:, j * LANES:(j + 1) * LANES] = (qc * Q_SCALE).astype(BF16)
        put(kwa_ref, _rope_chunk(z[:, C_WAK:C_WAV], cos, sin))
    else:
        qwa_ref[...] = (z[:, C_WAQ:C_WAK] * Q_SCALE).astype(BF16)
        put(kwa_ref, z[:, C_WAK:C_WAV])


def _inproj(x, mod, mod_row_fn, norm_g, w_bf, l, B, L, rope_tabs, kv_layers=None):
    T = x.shape[0]
    tm = TOKEN_TILE
    tiles_per_seq = L // tm
    in_specs = [
        pl.BlockSpec((tm, D_MODEL), lambda i: (i, 0)),
        pl.BlockSpec((1, 1, 1, 6 * D_MODEL), lambda i: (l, mod_row_fn(i // tiles_per_seq), 0, 0)),
        pl.BlockSpec((1, 4, D_MODEL), lambda i: (l, 0, 0)),
        pl.BlockSpec((1, D_MODEL, IN_COLS), lambda i: (l, 0, 0)),
    ]
    args = [x, mod, norm_g, w_bf]
    if rope_tabs is not None:
        in_specs += [pl.BlockSpec((tm, LANES), lambda i: (i % tiles_per_seq, 0))] * 2
        args += list(rope_tabs)
    widths = (NA_WIDTH, NA_WIDTH, NA_WIDTH, WA_WIDTH, WA_KV_WIDTH, WA_KV_WIDTH, SSM_CH)
    dtypes = (BF16, F32, F32, BF16, F32, F32, F32)
    out_specs = [pl.BlockSpec((tm, w), lambda i: (i, 0)) for w in widths]
    out_shape = [jax.ShapeDtypeStruct((T, w), dt) for w, dt in zip(widths, dtypes)]
    aliases = {}
    n_alias = 0
    if kv_layers is not None:
        for o in INPROJ_KV_OUTPUTS:
            out_specs[o] = pl.BlockSpec((1, 1, tm, widths[o]),
                                        lambda i: (i // tiles_per_seq, l, i % tiles_per_seq, 0))
            out_shape[o] = jax.ShapeDtypeStruct((B, DEPTH, L, widths[o]), dtypes[o])
        n_alias = len(kv_layers)
        for k, (o, prev) in enumerate(zip(INPROJ_KV_OUTPUTS, kv_layers)):
            aliases[len(args)] = o
            in_specs.append(pl.BlockSpec(memory_space=pl.ANY))
            args.append(prev)
    return pl.pallas_call(
        functools.partial(_inproj_body, rope_tabs is not None, n_alias),
        grid=(T // tm,),
        in_specs=in_specs,
        out_specs=out_specs,
        out_shape=out_shape,
        input_output_aliases=aliases,
        compiler_params=_cparams(("arbitrary",)),
        name="inproj",
    )(*args)


def _lane_lo(shape):
    return lax.broadcasted_iota(jnp.int32, shape, len(shape) - 1) < HEAD_DIM


def _keep_half(q, lo, half):
    keep = jnp.where(lo, 1.0 - half, 0.0 + half).astype(q.dtype)
    return q * keep


def _values_and_ones(v, lo, half):
    one = jnp.ones_like(v)
    return (jnp.where(lo, v, one) if half == 0 else jnp.where(lo, one, v)).astype(BF16)


def _normalise(o2, extra=None):
    den = pltpu.roll(o2, HEAD_DIM, 1)
    if extra is not None:
        den = den + extra
    return o2 / den


def _attn_ctx_body(sink_ref, qna, kna, vna, qwa, kwa, vwa, ona, owa):
    L = qna.shape[1]
    lo = _lane_lo((L, LANES))
    for j in range(NA_WIDTH // LANES):
        sl = slice(j * LANES, (j + 1) * LANES)
        q2 = qna[0, :, sl]
        k2 = kna[0, 0, :, sl].astype(BF16)
        v2 = vna[0, 0, :, sl].astype(BF16)
        halves = []
        for half in range(2):
            s = _dot_nt(_keep_half(q2, lo, half), k2)
            p = jnp.exp2(s - jnp.max(s, axis=-1, keepdims=True))
            halves.append(_dot(p.astype(BF16), v2) / jnp.sum(p, axis=-1, keepdims=True))
        ona[0, :, sl] = jnp.where(lo, halves[0], halves[1]).astype(BF16)
    kw = kwa[0, 0].astype(BF16)
    vw = vwa[0, 0].astype(BF16)
    for j in range(WA_WIDTH // LANES):
        sl = slice(j * LANES, (j + 1) * LANES)
        q2 = qwa[0, :, sl]
        halves = []
        for half in range(2):
            s = _dot_nt(_keep_half(q2, lo, half), kw)
            sk = sink_ref[WA_HEAD_ORDER[2 * j + half]] * LOG2E
            m = jnp.maximum(jnp.max(s, axis=-1, keepdims=True), sk)
            p = jnp.exp2(s - m)
            l = jnp.sum(p, axis=-1, keepdims=True) + jnp.exp2(sk - m)
            halves.append(_dot(p.astype(BF16), vw) / l)
        owa[0, :, sl] = jnp.where(lo, halves[0], halves[1]).astype(BF16)


def _attn_ctx(sink, qna, kna, vna, qwa, kwa, vwa, l, B, L):
    def spec(w):
        return pl.BlockSpec((1, L, w), lambda b: (b, 0, 0))

    def layer_spec(w):
        return pl.BlockSpec((1, 1, L, w), lambda b: (b, l, 0, 0))

    r3 = lambda a: a.reshape(B, L, a.shape[-1])
    ona, owa = pl.pallas_call(
        _attn_ctx_body,
        grid=(B,),
        in_specs=[pl.BlockSpec(memory_space=pltpu.SMEM),
                  spec(NA_WIDTH), layer_spec(NA_WIDTH), layer_spec(NA_WIDTH),
                  spec(WA_WIDTH), layer_spec(WA_KV_WIDTH), layer_spec(WA_KV_WIDTH)],
        out_specs=[spec(NA_WIDTH), spec(WA_WIDTH)],
        out_shape=[jax.ShapeDtypeStruct((B, L, NA_WIDTH), BF16),
                   jax.ShapeDtypeStruct((B, L, WA_WIDTH), BF16)],
        compiler_params=_cparams(("arbitrary",)),
        name="attn_ctx",
    )(sink, r3(qna), kna, vna, r3(qwa), kwa, vwa)
    return ona.reshape(B * L, NA_WIDTH), owa.reshape(B * L, WA_WIDTH)


def _na_key_start(rb, rows):
    return jnp.clip(rb * NA_QROWS - NA_KH // 2, 0, rows - NA_KROWS)


def _attn_na_body(rows, idx_ref, q_ref, k_ref, v_ref, kc_ref, vc_ref, pairs_ref, o_ref):
    rb = pl.program_id(1)
    nq = NA_QROWS * GRID_W
    nk = NA_KROWS * GRID_W
    start = pl.multiple_of(_na_key_start(rb, rows) * GRID_W, GRID_W)
    lo = _lane_lo((nq, LANES))
    lo_v = _lane_lo((1, LANES))
    npair = NA_KROWS // 2
    pat = jnp.where(rb == 0, 0, jnp.where(rb == rows // NA_QROWS - 1, 2, 1))
    entry = [[idx_ref[(pat * NA_QROWS + i) * npair + jj] for jj in range(npair)] for i in range(NA_QROWS)]

    def bias(head):
        return jnp.concatenate(
            [jnp.concatenate([pairs_ref[head, entry[i][jj]] for jj in range(npair)], axis=1)
             for i in range(NA_QROWS)], axis=0)

    for j in range(NA_WIDTH // LANES):
        sl = slice(j * LANES, (j + 1) * LANES)
        q2 = q_ref[0, :, sl]
        kl = k_ref[0, pl.ds(start, nk), sl].astype(BF16)
        vl = v_ref[0, pl.ds(start, nk), sl]
        kc = kc_ref[0, 0, :, sl].astype(BF16)
        vc = vc_ref[0, 0, :, sl]
        halves = []
        for half in range(2):
            qm = _keep_half(q2, lo, half)
            s_loc = _dot_nt(qm, kl) + bias(2 * j + half)
            s_ctx = _dot_nt(qm, kc)
            m = jnp.maximum(jnp.max(s_loc, axis=-1, keepdims=True),
                            jnp.max(s_ctx, axis=-1, keepdims=True))
            o2 = (_dot(jnp.exp2(s_loc - m).astype(BF16), _values_and_ones(vl, lo_v, half))
                  + _dot(jnp.exp2(s_ctx - m).astype(BF16), _values_and_ones(vc, lo_v, half)))
            halves.append(_normalise(o2))
        o_ref[0, :, sl] = jnp.where(lo, halves[0], halves[1]).astype(BF16)


def _na_bias_pairs(rpb_l):
    cq = np.arange(GRID_W)
    col_start = np.clip(cq - NA_KW // 2, 0, GRID_W - NA_KW)
    col_ok = (cq[None, :] >= col_start[:, None]) & (cq[None, :] < col_start[:, None] + NA_KW)
    dcol = np.clip(cq[None, :] - cq[:, None], -(NA_KW - 1), NA_KW - 1) + (NA_KW - 1)
    pick = (dcol[None] == np.arange(2 * NA_KW - 1)[:, None, None]).astype(np.float32)
    tiles = jnp.einsum('hrd,dqc->hrqc', rpb_l.astype(F32) * LOG2E, pick, precision=lax.Precision.HIGHEST)
    tiles = jnp.where(col_ok[None, None], tiles, NEG_INF)
    masked = jnp.full_like(tiles, NEG_INF)
    both = jnp.concatenate([tiles[:, :-1], tiles[:, 1:]], axis=-1)
    right = jnp.concatenate([masked, tiles], axis=-1)
    left = jnp.concatenate([tiles, masked], axis=-1)
    none = jnp.concatenate([masked[:, :1], masked[:, :1]], axis=-1)
    return jnp.concatenate([both, right, left, none], axis=1)


def _na_pair_index(rows):
    nrb = rows // NA_QROWS
    idx = np.zeros((3, NA_QROWS, NA_KROWS // 2), np.int32)
    for pi, rb in enumerate((0, 1, nrb - 1)):
        ks = int(np.clip(rb * NA_QROWS - NA_KH // 2, 0, rows - NA_KROWS))
        for i in range(NA_QROWS):
            qr = rb * NA_QROWS + i
            rs = int(np.clip(qr - NA_KH // 2, 0, rows - NA_KH))
            for jj in range(NA_KROWS // 2):
                kr = ks + 2 * jj
                r = kr - qr + NA_KH - 1
                ok_l = rs <= kr < rs + NA_KH
                ok_r = rs <= kr + 1 < rs + NA_KH
                if ok_l and ok_r:
                    idx[pi, i, jj] = r
                elif ok_r:
                    idx[pi, i, jj] = (NA_REL - 1) + (r + 1)
                elif ok_l:
                    idx[pi, i, jj] = (NA_REL - 1) + NA_REL + r
                else:
                    idx[pi, i, jj] = NA_PAIRS - 1
    return jnp.asarray(idx.reshape(-1))


def _attn_na(q, k, v, cache_k, cache_v, pairs, l, B, L):
    rows = L // GRID_W
    nrb = rows // NA_QROWS
    nq = NA_QROWS * GRID_W
    P = cache_k.shape[2]
    r3 = lambda a: a.reshape(B, L, a.shape[-1])
    ck = cache_k.reshape(B, DEPTH, P, NA_WIDTH)
    cv = cache_v.reshape(B, DEPTH, P, NA_WIDTH)
    grid_spec = pltpu.PrefetchScalarGridSpec(
        num_scalar_prefetch=1,
        grid=(B, nrb),
        in_specs=[
            pl.BlockSpec((1, nq, NA_WIDTH), lambda b, r, s: (b, r, 0)),
            pl.BlockSpec((1, L, NA_WIDTH), lambda b, r, s: (b, 0, 0)),
            pl.BlockSpec((1, L, NA_WIDTH), lambda b, r, s: (b, 0, 0)),
            pl.BlockSpec((1, 1, P, NA_WIDTH), lambda b, r, s: (b, l, 0, 0)),
            pl.BlockSpec((1, 1, P, NA_WIDTH), lambda b, r, s: (b, l, 0, 0)),
            pl.BlockSpec(pairs.shape, lambda b, r, s: (0, 0, 0, 0)),
        ],
        out_specs=pl.BlockSpec((1, nq, NA_WIDTH), lambda b, r, s: (b, r, 0)),
    )
    o = pl.pallas_call(
        functools.partial(_attn_na_body, rows),
        grid_spec=grid_spec,
        out_shape=jax.ShapeDtypeStruct((B, L, NA_WIDTH), BF16),
        compiler_params=_cparams(("arbitrary", "arbitrary")),
        name="attn_na",
    )(_na_pair_index(rows), r3(q), r3(k), r3(v), ck, cv, pairs)
    return o.reshape(B * L, NA_WIDTH)


def _attn_wa_body(L, sink_ref, q_ref, k_ref, v_ref, kc_ref, vc_ref, o_ref):
    n = pl.program_id(1)
    nk = 3 * WA_BLOCK
    npair = WA_WIDTH // LANES
    start = pl.multiple_of(jnp.clip((n - 1) * WA_BLOCK, 0, L - nk), WA_BLOCK)
    kl = k_ref[0, pl.ds(start, nk), :].astype(BF16)
    vl = v_ref[0, pl.ds(start, nk), :]
    kc = kc_ref[0, 0].astype(BF16)
    vc = vc_ref[0, 0]
    lo_v = _lane_lo((1, LANES))
    rows = npair * WA_BLOCK
    qpos = n * WA_BLOCK + lax.broadcasted_iota(jnp.int32, (rows, nk), 0) % WA_BLOCK
    kpos = start + lax.broadcasted_iota(jnp.int32, (rows, nk), 1)
    in_win = jnp.abs(kpos - qpos) <= WA_WINDOW
    lo = _lane_lo((rows, LANES))
    q_all = jnp.concatenate([q_ref[0, :, j * LANES:(j + 1) * LANES] for j in range(npair)], axis=0)
    blk = lax.broadcasted_iota(jnp.int32, (rows, 1), 0) // WA_BLOCK
    outs = []
    for g in range(WA_KV_HEADS):
        qm = _keep_half(q_all, lo, g)
        sk = jnp.zeros((rows, 1), F32)
        for j in range(npair):
            sk = jnp.where(blk == j, sink_ref[WA_HEAD_ORDER[2 * j + g]] * LOG2E, sk)
        s_loc = jnp.where(in_win, _dot_nt(qm, kl), NEG_INF)
        s_ctx = _dot_nt(qm, kc)
        m = jnp.maximum(jnp.maximum(jnp.max(s_loc, axis=-1, keepdims=True),
                                    jnp.max(s_ctx, axis=-1, keepdims=True)), sk)
        o2 = (_dot(jnp.exp2(s_loc - m).astype(BF16), _values_and_ones(vl, lo_v, g))
              + _dot(jnp.exp2(s_ctx - m).astype(BF16), _values_and_ones(vc, lo_v, g)))
        outs.append(_normalise(o2, jnp.exp2(sk - m)))
    o_all = jnp.where(lo, outs[0], outs[1]).astype(BF16)
    for j in range(npair):
        o_ref[0, :, j * LANES:(j + 1) * LANES] = o_all[j * WA_BLOCK:(j + 1) * WA_BLOCK]


def _attn_wa(sink, q, k, v, cache_k, cache_v, l, B, L):
    nb = L // WA_BLOCK
    P = cache_k.shape[2]
    r3 = lambda a: a.reshape(B, L, a.shape[-1])
    ck = cache_k.reshape(B, DEPTH, P, WA_KV_WIDTH)
    cv = cache_v.reshape(B, DEPTH, P, WA_KV_WIDTH)
    o = pl.pallas_call(
        functools.partial(_attn_wa_body, L),
        grid=(B, nb),
        in_specs=[
            pl.BlockSpec(memory_space=pltpu.SMEM),
            pl.BlockSpec((1, WA_BLOCK, WA_WIDTH), lambda b, n: (b, n, 0)),
            pl.BlockSpec((1, L, WA_KV_WIDTH), lambda b, n: (b, 0, 0)),
            pl.BlockSpec((1, L, WA_KV_WIDTH), lambda b, n: (b, 0, 0)),
            pl.BlockSpec((1, 1, P, WA_KV_WIDTH), lambda b, n: (b, l, 0, 0)),
            pl.BlockSpec((1, 1, P, WA_KV_WIDTH), lambda b, n: (b, l, 0, 0)),
        ],
        out_specs=pl.BlockSpec((1, WA_BLOCK, WA_WIDTH), lambda b, n: (b, n, 0)),
        out_shape=jax.ShapeDtypeStruct((B, L, WA_WIDTH), BF16),
        compiler_params=_cparams(("arbitrary", "arbitrary")),
        name="attn_wa",
    )(sink, r3(q), r3(k), r3(v), ck, cv)
    return o.reshape(B * L, WA_WIDTH)


def _s5_prep_body(lr_ref, li_ref, ldt_ref, br_ref, bi_ref, a_ref, bb_ref):
    lr = lr_ref[0]
    li = li_ref[0]
    dt = jnp.exp(ldt_ref[0])
    mag = jnp.exp(lr * dt)
    ar = mag * jnp.cos(li * dt)
    ai = mag * jnp.sin(li * dt)
    den = lr * lr + li * li
    nr = ar - 1.0
    fr = (nr * lr + ai * li) / den
    fi = (ai * lr - nr * li) / den
    a_ref[0, :, 0:SSM_N] = ar
    a_ref[0, :, SSM_N:2 * SSM_N] = ai
    br = br_ref[0]
    bi = bi_ref[0]
    bb_ref[0, :, 0:SSM_N] = (fr * br - fi * bi).astype(BF16)
    bb_ref[0, :, SSM_N:2 * SSM_N] = (fr * bi + fi * br).astype(BF16)


def _s5_prep(lam_re, lam_im, log_dt, b_re, b_im):
    n = DEPTH * 2
    eye = jnp.eye(SSM_GROUPS, dtype=F32)

    def blockdiag_b(b):
        return jnp.einsum('ngph,gk->nghkp', b.reshape(n, SSM_GROUPS, SSM_STATE, SSM_GROUP_CH), eye
                          ).reshape(n, SSM_CH, SSM_N)

    lr = lam_re.reshape(n, 1, SSM_N)
    li = lam_im.reshape(n, 1, SSM_N)
    ldt = jnp.repeat(log_dt.reshape(n, SSM_GROUPS), SSM_STATE, axis=-1).reshape(n, 1, SSM_N)
    vec = pl.BlockSpec((1, 1, SSM_N), lambda i: (i, 0, 0))
    mat = pl.BlockSpec((1, SSM_CH, SSM_N), lambda i: (i, 0, 0))
    return pl.pallas_call(
        _s5_prep_body,
        grid=(n,),
        in_specs=[vec, vec, vec, mat, mat],
        out_specs=[pl.BlockSpec((1, 1, 2 * SSM_N), lambda i: (i, 0, 0)),
                   pl.BlockSpec((1, SSM_CH, 2 * SSM_N), lambda i: (i, 0, 0))],
        out_shape=[jax.ShapeDtypeStruct((n, 1, 2 * SSM_N), F32),
                   jax.ShapeDtypeStruct((n, SSM_CH, 2 * SSM_N), BF16)],
        compiler_params=_cparams(("arbitrary",)),
        name="s5_prep",
    )(lr, li, ldt, blockdiag_b(b_re), blockdiag_b(b_im))


def _s5_c_matrix(c_re, c_im):
    n = DEPTH * 2
    eye = jnp.eye(SSM_GROUPS, dtype=F32)

    def blk(c):
        return jnp.einsum('nghp,gk->ngpkh', c.reshape(n, SSM_GROUPS, SSM_GROUP_CH, SSM_STATE), eye
                          ).reshape(n, SSM_N, SSM_CH)

    return jnp.concatenate([blk(c_re), -blk(c_im)], axis=1).astype(BF16)


def _s5_scan_body(tl, u_ref, bb_ref, a_ref, c_ref, h0_ref, y_ref, hfin_ref, bu, hst):
    d = pl.program_id(0)
    i = pl.program_id(2)

    @pl.when(i == 0)
    def _():
        hst[...] = h0_ref[0]

    u_tb = jnp.transpose(u_ref[...], (1, 0, 2)).reshape(tl * SCAN_BATCH, SSM_CH)
    bu[...] = _dot(u_tb.astype(BF16), bb_ref[0])
    a = a_ref[0]
    ar = jnp.broadcast_to(a[:, 0:SSM_N], (SCAN_BATCH, SSM_N))
    ai = jnp.broadcast_to(a[:, SSM_N:2 * SSM_N], (SCAN_BATCH, SSM_N))

    def step(s, carry):
        hr, hi = carry
        t = s + d * (tl - 1 - 2 * s)
        rows = pl.ds(pl.multiple_of(t * SCAN_BATCH, SCAN_BATCH), SCAN_BATCH)
        nhr = ar * hr - ai * hi + bu[rows, 0:SSM_N]
        nhi = ar * hi + ai * hr + bu[rows, SSM_N:2 * SSM_N]
        bu[rows, 0:SSM_N] = nhr
        bu[rows, SSM_N:2 * SSM_N] = nhi
        return nhr, nhi

    hr, hi = lax.fori_loop(0, tl, step, (hst[:, 0:SSM_N], hst[:, SSM_N:2 * SSM_N]), unroll=2)
    hst[:, 0:SSM_N] = hr
    hst[:, SSM_N:2 * SSM_N] = hi
    hfin_ref[0, :, 0:SSM_N] = hr
    hfin_ref[0, :, SSM_N:2 * SSM_N] = hi
    y_tb = _dot(bu[...].astype(BF16), c_ref[0])
    y_ref[0] = jnp.transpose(y_tb.reshape(tl, SCAN_BATCH, SSM_CH), (1, 0, 2))


def _s5_scan(u, bb, a, cm, h0, B, L):
    tl = SCAN_TILE
    nt = L // tl
    tile = lambda d, i: i + d * (nt - 1 - 2 * i)
    return pl.pallas_call(
        functools.partial(_s5_scan_body, tl),
        grid=(2, B // SCAN_BATCH, nt),
        in_specs=[
            pl.BlockSpec((SCAN_BATCH, tl, SSM_CH), lambda d, b, i: (b, tile(d, i), 0)),
            pl.BlockSpec((1, SSM_CH, 2 * SSM_N), lambda d, b, i: (d, 0, 0)),
            pl.BlockSpec((1, 1, 2 * SSM_N), lambda d, b, i: (d, 0, 0)),
            pl.BlockSpec((1, 2 * SSM_N, SSM_CH), lambda d, b, i: (d, 0, 0)),
            pl.BlockSpec((1, SCAN_BATCH, 2 * SSM_N), lambda d, b, i: (d, b, 0)),
        ],
        out_specs=[
            pl.BlockSpec((1, SCAN_BATCH, tl, SSM_CH), lambda d, b, i: (d, b, tile(d, i), 0)),
            pl.BlockSpec((1, SCAN_BATCH, 2 * SSM_N), lambda d, b, i: (d, b, 0)),
        ],
        out_shape=[jax.ShapeDtypeStruct((2, B, L, SSM_CH), F32),
                   jax.ShapeDtypeStruct((2, B, 2 * SSM_N), F32)],
        scratch_shapes=[pltpu.VMEM((SCAN_BATCH * tl, 2 * SSM_N), F32),
                        pltpu.VMEM((SCAN_BATCH, 2 * SSM_N), F32)],
        compiler_params=_cparams(("arbitrary", "arbitrary", "arbitrary")),
        name="s5_scan",
    )(u.reshape(B, L, SSM_CH), bb, a, cm, h0)


def _outproj_body(ona_ref, owa_ref, y_ref, u_ref, dsk_ref, wglu_ref, wout_ref, x_ref, mod_ref, g_ref,
                  wr_ref, x1_ref, h2_ref, aff_ref):
    y = dsk_ref[...] * u_ref[...] + y_ref[0] + y_ref[1]
    zg = _dot(y.astype(BF16), wglu_ref[...])
    oss = zg[:, 0:SSM_CH] * _sigmoid(zg[:, SSM_CH:2 * SSM_CH])
    mix = (_dot(ona_ref[...], wout_ref[0:NA_WIDTH, :])
           + _dot(owa_ref[...], wout_ref[NA_WIDTH:NA_WIDTH + WA_WIDTH, :])
           + _dot(oss.astype(BF16), wout_ref[NA_WIDTH + WA_WIDTH:D_MODEL, :]))
    m = mod_ref[0]
    g1 = m[:, 2 * D_MODEL:3 * D_MODEL]
    sh2 = m[:, 3 * D_MODEL:4 * D_MODEL]
    sc2 = m[:, 4 * D_MODEL:5 * D_MODEL]
    x1 = x_ref[...] + g1 * _rms(mix, g_ref[1:2, :])
    x1_ref[...] = x1
    h2 = _rms(x1, g_ref[2:3, :]) * (1.0 + sc2) + sh2
    h2_hi = h2.astype(BF16)
    h2_ref[...] = h2_hi
    h2_lo = (h2 - h2_hi.astype(F32)).astype(BF16)
    r_hi = _dot(h2_hi, wr_ref[...])
    logits = r_hi + pltpu.roll(r_hi, LANES - N_EXPERTS, 1) + _dot(h2_lo, wr_ref[...])
    lane = lax.broadcasted_iota(jnp.int32, logits.shape, 1)
    logits = jnp.where(lane < N_EXPERTS, logits, NEG_INF)
    mx = jnp.max(logits, axis=-1, keepdims=True)
    p = jnp.exp(logits - mx)
    aff_ref[...] = p / jnp.sum(p, axis=-1, keepdims=True)


def _outproj(ona, owa, y2, u, dsk, wglu_bf, wout_bf, x, mod_l, mod_row_fn, g, wr_pad, L):
    T = x.shape[0]
    tm = TOKEN_TILE
    tiles_per_seq = L // tm
    row = lambda w: pl.BlockSpec((tm, w), lambda i: (i, 0))
    full = lambda a: pl.BlockSpec(a.shape, lambda i: (0,) * a.ndim)
    return pl.pallas_call(
        _outproj_body,
        grid=(T // tm,),
        in_specs=[row(NA_WIDTH), row(WA_WIDTH),
                  pl.BlockSpec((2, tm, SSM_CH), lambda i: (0, i, 0)),
                  row(SSM_CH), full(dsk), full(wglu_bf), full(wout_bf), row(D_MODEL),
                  pl.BlockSpec((1, 1, 6 * D_MODEL), lambda i: (mod_row_fn(i // tiles_per_seq), 0, 0)),
                  full(g), full(wr_pad)],
        out_specs=[row(D_MODEL), row(D_MODEL), row(LANES)],
        out_shape=[jax.ShapeDtypeStruct((T, D_MODEL), F32),
                   jax.ShapeDtypeStruct((T, D_MODEL), BF16),
                   jax.ShapeDtypeStruct((T, LANES), F32)],
        compiler_params=_cparams(("arbitrary",)),
        name="outproj",
    )(ona, owa, y2.reshape(2, T, SSM_CH), u, dsk, wglu_bf, wout_bf, x, mod_l, g, wr_pad)


def _select_body(cap, aff_ref, pos_ref, post_ref, afft_ref, starts_ref):
    L = aff_ref.shape[1]
    aff = aff_ref[0]
    for r in range(1, SEL_GROUP):
        aff = aff + pltpu.roll(aff_ref[r], r * N_EXPERTS, 1)
    bits = pltpu.bitcast(aff, jnp.int32)
    capf = jnp.float32(cap)

    def bisect(k, thr):
        cand = thr | jnp.left_shift(jnp.int32(1), 30 - k)
        cnt = jnp.sum(jnp.where(bits >= cand, 1.0, 0.0), axis=0, keepdims=True)
        return jnp.where(cnt >= capf, cand, thr)

    thr = lax.fori_loop(0, 31, bisect, jnp.zeros((1, LANES), jnp.int32))
    gt = jnp.where(bits > thr, 1.0, 0.0)
    eq = jnp.where(bits == thr, 1.0, 0.0)
    need = capf - jnp.sum(gt, axis=0, keepdims=True)
    ck = SEL_CHUNK
    tri = (lax.broadcasted_iota(jnp.int32, (ck, ck), 0)
           >= lax.broadcasted_iota(jnp.int32, (ck, ck), 1))
    tri = jnp.where(tri, 1.0, 0.0).astype(BF16)

    def prefix(mask):
        outs = []
        carries = [jnp.zeros((1, LANES), F32)]
        for j in range(L // ck):
            c = _dot(tri, mask[j * ck:(j + 1) * ck].astype(BF16)) + carries[-1]
            outs.append(c)
            carries.append(c[ck - 1:ck, :])
        return jnp.concatenate(outs, axis=0), jnp.concatenate(carries, axis=0)

    tie = jnp.where(prefix(eq)[0] <= need, eq, 0.0)
    sel = gt + tie
    rank, starts = prefix(sel)
    pos = jnp.where(sel > 0.0, rank - 1.0, -1.0)
    pos_ref[0] = pos
    post_ref[0] = jnp.transpose(pos)
    afft_ref[0] = jnp.transpose(aff)
    starts_ref[0] = starts.astype(jnp.int32)


def _select(aff, B, L):
    cap = EC_CAPACITY * L // N_EXPERTS
    ng = B // SEL_GROUP
    return pl.pallas_call(
        functools.partial(_select_body, cap),
        grid=(ng,),
        in_specs=[pl.BlockSpec((SEL_GROUP, L, LANES), lambda i: (i, 0, 0))],
        out_specs=[pl.BlockSpec((1, L, LANES), lambda i: (i, 0, 0)),
                   pl.BlockSpec((1, LANES, L), lambda i: (i, 0, 0)),
                   pl.BlockSpec((1, LANES, L), lambda i: (i, 0, 0)),
                   pl.BlockSpec((1, L // SEL_CHUNK + 1, LANES), lambda i: (i, 0, 0))],
        out_shape=[jax.ShapeDtypeStruct((ng, L, LANES), F32),
                   jax.ShapeDtypeStruct((ng, LANES, L), F32),
                   jax.ShapeDtypeStruct((ng, LANES, L), F32),
                   jax.ShapeDtypeStruct((ng, L // SEL_CHUNK + 1, LANES), jnp.int32)],
        compiler_params=_cparams(("arbitrary",)),
        name="ec_select",
    )(aff.reshape(B, L, LANES))


def _window_rows(base, w):
    return pl.ds(base, w) if isinstance(base, int) else pl.ds(pl.multiple_of(base, BF16_ROWS), w)


def _gather_body(cap, win, starts_ref, h_ref, post_ref, afft_ref, xs_ref, gs_ref):
    b = pl.program_id(0)
    j = pl.program_id(1)
    grp = b // SEL_GROUP
    lane0 = (b % SEL_GROUP) * N_EXPERTS

    @pl.when(j == 0)
    def _():
        xs_ref[...] = jnp.zeros_like(xs_ref)
        gs_ref[...] = jnp.zeros_like(gs_ref)

    h = h_ref[0]
    ck = h.shape[0]

    def run(w, bases):
        slot = lax.broadcasted_iota(jnp.int32, (w, ck), 0)
        onehots = []
        for e in range(N_EXPERTS):
            rows = _window_rows(bases[e], w)
            hit = post_ref[0, e:e + 1, :] == (slot + bases[e]).astype(F32)
            gs_ref[e, 0, rows, :] += jnp.sum(jnp.where(hit, afft_ref[0, e:e + 1, :], 0.0), axis=1, keepdims=True)
            onehots.append(jnp.where(hit, 1.0, 0.0).astype(BF16))
        xs = _dot(jnp.concatenate(onehots, axis=0), h).astype(BF16)
        for e in range(N_EXPERTS):
            rows = _window_rows(bases[e], w)
            xs_ref[e, 0, rows, :] += xs[e * w:(e + 1) * w]

    if win == cap:
        run(cap, [0] * N_EXPERTS)
        return
    bases = []
    fits = None
    for e in range(N_EXPERTS):
        s0 = starts_ref[grp, j, lane0 + e]
        s1 = starts_ref[grp, j + 1, lane0 + e]
        base = jnp.minimum((s0 // BF16_ROWS) * BF16_ROWS, cap - win)
        bases.append(base)
        ok = s1 - base <= win
        fits = ok if fits is None else jnp.logical_and(fits, ok)

    @pl.when(fits)
    def _():
        run(win, bases)

    @pl.when(jnp.logical_not(fits))
    def _():
        run(cap, [0] * N_EXPERTS)


def _gather(starts, h2, post, afft, B, L):
    cap = EC_CAPACITY * L // N_EXPERTS
    ck = SEL_CHUNK
    win = min(cap, COMBINE_WINDOW)
    lane_blk = lambda b, j, s: (b // SEL_GROUP, b % SEL_GROUP, j)
    grid_spec = pltpu.PrefetchScalarGridSpec(
        num_scalar_prefetch=1,
        grid=(B, L // ck),
        in_specs=[pl.BlockSpec((1, ck, D_MODEL), lambda b, j, s: (b, j, 0)),
                  pl.BlockSpec((1, N_EXPERTS, ck), lane_blk),
                  pl.BlockSpec((1, N_EXPERTS, ck), lane_blk)],
        out_specs=[pl.BlockSpec((N_EXPERTS, 1, cap, D_MODEL), lambda b, j, s: (0, b, 0, 0)),
                   pl.BlockSpec((N_EXPERTS, 1, cap, 1), lambda b, j, s: (0, b, 0, 0))],
    )
    xs, gs = pl.pallas_call(
        functools.partial(_gather_body, cap, win),
        grid_spec=grid_spec,
        out_shape=[jax.ShapeDtypeStruct((N_EXPERTS, B, cap, D_MODEL), BF16),
                   jax.ShapeDtypeStruct((N_EXPERTS, B, cap, 1), F32)],
        compiler_params=_cparams(("arbitrary", "arbitrary")),
        name="ec_gather",
    )(starts, h2.reshape(B, L, D_MODEL), post, afft)
    return xs.reshape(N_EXPERTS, B * cap, D_MODEL), gs.reshape(N_EXPERTS, B * cap, 1)


def _ffn_body(n_first, xa_ref, ga_ref, xb_ref, gb_ref, wg_ref, wu_ref, wd_ref, y_ref, wgb, wub, wdb):
    j = pl.program_id(1)

    @pl.when(j == 0)
    def _():
        wgb[...] = wg_ref[0, 0].astype(BF16)
        wub[...] = wu_ref[0, 0].astype(BF16)
        wdb[...] = wd_ref[0, 0].astype(BF16)

    def run(xs_ref, gs_ref):
        x = xs_ref[0]
        a = _dot(x, wgb[...])
        u = _dot(x, wub[...])
        hm = (a * _sigmoid(a) * u).astype(BF16)
        y_ref[0] = (_dot(hm, wdb[...]) * gs_ref[0]).astype(BF16)

    @pl.when(j < n_first)
    def _():
        run(xa_ref, ga_ref)

    @pl.when(j >= n_first)
    def _():
        run(xb_ref, gb_ref)


def _ffn(xa, ga, xb, gb, wg, wu, wd, l):
    rc = FFN_ROWS
    na = xa.shape[1] // rc
    nb = xb.shape[1] // rc
    first = lambda w: pl.BlockSpec((1, rc, w), lambda e, j: (e, jnp.minimum(j, na - 1), 0))
    second = lambda w: pl.BlockSpec((1, rc, w), lambda e, j: (e, jnp.maximum(j - na, 0), 0))
    wspec = lambda a, b: pl.BlockSpec((1, 1, a, b), lambda e, j: (l, e, 0, 0))
    return pl.pallas_call(
        functools.partial(_ffn_body, na),
        grid=(N_EXPERTS, na + nb),
        in_specs=[first(D_MODEL), first(1), second(D_MODEL), second(1),
                  wspec(D_MODEL, EXPERT_FF), wspec(D_MODEL, EXPERT_FF), wspec(EXPERT_FF, D_MODEL)],
        out_specs=pl.BlockSpec((1, rc, D_MODEL), lambda e, j: (e, j, 0)),
        out_shape=jax.ShapeDtypeStruct((N_EXPERTS, (na + nb) * rc, D_MODEL), BF16),
        scratch_shapes=[pltpu.VMEM((D_MODEL, EXPERT_FF), BF16),
                        pltpu.VMEM((D_MODEL, EXPERT_FF), BF16),
                        pltpu.VMEM((EXPERT_FF, D_MODEL), BF16)],
        compiler_params=_cparams(("arbitrary", "arbitrary")),
        name="ec_ffn",
    )(xa, ga, xb, gb, wg, wu, wd)


def _combine_body(cap, win, starts_ref, pos_ref, y_ref, x1_ref, mod_ref, g_ref, o_ref):
    b = pl.program_id(0)
    i = pl.program_id(1)
    grp = b // SEL_GROUP
    lane0 = (b % SEL_GROUP) * N_EXPERTS
    pos = pos_ref[0].astype(BF16)
    g2 = mod_ref[0][:, 5 * D_MODEL:6 * D_MODEL]

    def finish(w, bases, values):
        n = N_EXPERTS * w
        col = lax.broadcasted_iota(jnp.int32, (1, n), 1)
        expert = jnp.zeros((1, n), jnp.int32)
        for e in range(1, N_EXPERTS):
            expert = expert + jnp.where(col >= e * w, 1, 0)
        target = col - expert * w
        if bases is not None:
            for e in range(N_EXPERTS):
                target = jnp.where(expert == e, target + bases[e], target)
        expand = jnp.where(lax.broadcasted_iota(jnp.int32, (LANES, n), 0) == expert + lane0, 1.0, 0.0)
        per_col = _dot(pos, expand.astype(BF16))
        onehot = jnp.where(per_col == target.astype(F32), 1.0, 0.0).astype(BF16)
        f = _dot(onehot, values)
        o_ref[...] = x1_ref[...] + g2 * _rms(f, g_ref[3:4, :])

    if win == cap:
        finish(cap, None, y_ref[...].reshape(N_EXPERTS * cap, D_MODEL))
        return
    bases = []
    fits = None
    for e in range(N_EXPERTS):
        s0 = starts_ref[grp, i, lane0 + e]
        s1 = starts_ref[grp, i + 1, lane0 + e]
        base = jnp.minimum((s0 // BF16_ROWS) * BF16_ROWS, cap - win)
        bases.append(base)
        ok = s1 - base <= win
        fits = ok if fits is None else jnp.logical_and(fits, ok)

    @pl.when(fits)
    def _():
        rows = [y_ref[e, pl.ds(pl.multiple_of(bases[e], BF16_ROWS), win), :] for e in range(N_EXPERTS)]
        finish(win, bases, jnp.concatenate(rows, axis=0))

    @pl.when(jnp.logical_not(fits))
    def _():
        finish(cap, None, y_ref[...].reshape(N_EXPERTS * cap, D_MODEL))


def _combine(starts, pos, y, row_off, x1, mod_l, mod_row_fn, g, B, L):
    cap = EC_CAPACITY * L // N_EXPERTS
    tq = TOKEN_TILE
    assert tq == SEL_CHUNK
    nq = L // tq
    blk_off = row_off // cap
    win = min(cap, COMBINE_WINDOW)
    grid_spec = pltpu.PrefetchScalarGridSpec(
        num_scalar_prefetch=1,
        grid=(B, nq),
        in_specs=[pl.BlockSpec((1, tq, LANES), lambda b, i, s: (b // SEL_GROUP, i, 0)),
                  pl.BlockSpec((N_EXPERTS, cap, D_MODEL), lambda b, i, s: (0, blk_off + b, 0)),
                  pl.BlockSpec((tq, D_MODEL), lambda b, i, s: (b * nq + i, 0)),
                  pl.BlockSpec((1, 1, 6 * D_MODEL), lambda b, i, s: (mod_row_fn(b), 0, 0)),
                  pl.BlockSpec((4, D_MODEL), lambda b, i, s: (0, 0))],
        out_specs=pl.BlockSpec((tq, D_MODEL), lambda b, i, s: (b * nq + i, 0)),
    )
    return pl.pallas_call(
        functools.partial(_combine_body, cap, win),
        grid_spec=grid_spec,
        out_shape=jax.ShapeDtypeStruct((B * L, D_MODEL), F32),
        compiler_params=_cparams(("arbitrary", "arbitrary")),
        name="ec_combine",
    )(starts, pos, y, x1, mod_l, g)


def _rope_tables(L):
    t = jnp.arange(L)
    row = (t // GRID_W).astype(F32)
    col = (t % GRID_W).astype(F32)
    half = HEAD_DIM // 4
    inv = ROPE_BASE ** (-jnp.arange(half, dtype=F32) / half)
    d = np.arange(LANES) % HEAD_DIM
    use_col = (d // (HEAD_DIM // 2)) == 1
    pos = jnp.where(use_col[None, :], col[:, None], row[:, None])
    ang = pos * inv[d % half][None, :]
    sign = np.where((d % (HEAD_DIM // 2)) < half, -1.0, 1.0).astype(np.float32)
    return jnp.cos(ang), jnp.sin(ang) * sign[None, :]


def _permute_wa_heads(a, axis, start):
    cut = lambda lo, hi: lax.slice_in_dim(a, lo, hi, axis=axis)
    heads = [cut(start + h * HEAD_DIM, start + (h + 1) * HEAD_DIM) for h in WA_HEAD_ORDER]
    return jnp.concatenate([cut(0, start)] + heads + [cut(start + WA_WIDTH, a.shape[axis])], axis=axis)


def kernel(x_prompt, x_sample, c, cache_na_k, cache_na_v, cache_wa_k, cache_wa_v, state_ssm, c_ctx, w_ada, b_ada, norm_g, w_in, w_out, na_rpb, wa_sink, ssm_lambda_re, ssm_lambda_im, ssm_log_dt, ssm_b_re, ssm_b_im, ssm_c_re, ssm_c_im, ssm_d, w_glu, w_router, w_exp_gate, w_exp_up, w_exp_down):
    Bc, Lc, _ = x_prompt.shape
    Bs, Ls, _ = x_sample.shape
    ctx_row = Bs
    n_cond = ((Bs + 1 + SUBLANES - 1) // SUBLANES) * SUBLANES
    cond = jnp.zeros((n_cond, D_MODEL), F32).at[0:Bs].set(c).at[ctx_row].set(c_ctx)
    mod = _modulation(cond, w_ada, b_ada)
    mod = mod.reshape(DEPTH, n_cond, 1, 6 * D_MODEL)

    w_in_bf = _permute_wa_heads(w_in, 2, C_WAQ).astype(BF16)
    w_out_bf = _permute_wa_heads(w_out, 1, NA_WIDTH).astype(BF16)
    w_glu_bf = w_glu.astype(BF16)
    wr_hi = w_router.astype(BF16)
    wr_lo = (w_router - wr_hi.astype(F32)).astype(BF16)
    wr_pad = jnp.pad(jnp.concatenate([wr_hi, wr_lo], axis=-1), ((0, 0), (0, 0), (0, LANES - 2 * N_EXPERTS)))
    s5_a, s5_bb = _s5_prep(ssm_lambda_re, ssm_lambda_im, ssm_log_dt, ssm_b_re, ssm_b_im)
    s5_c = _s5_c_matrix(ssm_c_re, ssm_c_im)
    rope_tabs = _rope_tables(Ls)

    ctx_mod_row = lambda b: ctx_row + 0 * b
    lat_mod_row = lambda b: b

    xp = x_prompt.reshape(Bc * Lc, D_MODEL)
    xs = x_sample.reshape(Bs * Ls, D_MODEL)
    new_kv, new_ssm = (), []
    for l in range(DEPTH):
        g = norm_g[l]
        dsk = ssm_d[l].reshape(1, SSM_CH)
        sl2 = slice(2 * l, 2 * l + 2)
        qna, kna, vna, qwa, kwa, vwa, u = _inproj(xp, mod, ctx_mod_row, norm_g, w_in_bf, l, Bc, Lc, None, new_kv)
        new_kv = (kna, vna, kwa, vwa)
        ona, owa = _attn_ctx(wa_sink[l], qna, kna, vna, qwa, kwa, vwa, l, Bc, Lc)
        y2, hfin = _s5_scan(u, s5_bb[sl2], s5_a[sl2], s5_c[sl2],
                            jnp.zeros((2, Bc, 2 * SSM_N), F32), Bc, Lc)
        x1, h2, aff = _outproj(ona, owa, y2, u, dsk, w_glu_bf[l], w_out_bf[l], xp, mod[l], ctx_mod_row,
                               g, wr_pad[l], Lc)
        pos_c, post, afft, starts_c = _select(aff, Bc, Lc)
        xg_c, gs_c = _gather(starts_c, h2, post, afft, Bc, Lc)
        x1_c = x1
        new_ssm.append(jnp.transpose(hfin.reshape(2, Bc, 2, SSM_GROUPS, SSM_STATE), (1, 0, 2, 3, 4)))
        qna, kna, vna, qwa, kwa, vwa, u = _inproj(xs, mod, lat_mod_row, norm_g, w_in_bf, l, Bs, Ls, rope_tabs)
        ona = _attn_na(qna, kna, vna, cache_na_k, cache_na_v, _na_bias_pairs(na_rpb[l]), l, Bs, Ls)
        owa = _attn_wa(wa_sink[l], qwa, kwa, vwa, cache_wa_k, cache_wa_v, l, Bs, Ls)
        h0 = jnp.transpose(state_ssm[:, l].reshape(Bs, 2, 2 * SSM_N), (1, 0, 2))
        y2, _ = _s5_scan(u, s5_bb[sl2], s5_a[sl2], s5_c[sl2], h0, Bs, Ls)
        x1, h2, aff = _outproj(ona, owa, y2, u, dsk, w_glu_bf[l], w_out_bf[l], xs, mod[l], lat_mod_row,
                               g, wr_pad[l], Ls)
        pos_s, post, afft, starts_s = _select(aff, Bs, Ls)
        xg_s, gs_s = _gather(starts_s, h2, post, afft, Bs, Ls)
        yy = _ffn(xg_c, gs_c, xg_s, gs_s, w_exp_gate, w_exp_up, w_exp_down, l)
        xp = _combine(starts_c, pos_c, yy, 0, x1_c, mod[l], ctx_mod_row, g, Bc, Lc)
        xs = _combine(starts_s, pos_s, yy, xg_c.shape[1], x1, mod[l], lat_mod_row, g, Bs, Ls)
    return (xp.reshape(Bc, Lc, D_MODEL), xs.reshape(Bs, Ls, D_MODEL),
            new_kv[0].reshape(Bc, DEPTH, Lc, NA_HEADS, HEAD_DIM),
            new_kv[1].reshape(Bc, DEPTH, Lc, NA_HEADS, HEAD_DIM),
            new_kv[2].reshape(Bc, DEPTH, Lc, WA_KV_HEADS, HEAD_DIM),
            new_kv[3].reshape(Bc, DEPTH, Lc, WA_KV_HEADS, HEAD_DIM),
            jnp.stack(new_ssm, axis=1))
```

```python
import functools
import math

import numpy as np
import jax
import jax.numpy as jnp
from jax import lax
from jax.experimental import pallas as pl
from jax.experimental.pallas import tpu as pltpu

F32 = jnp.float32
BF16 = jnp.bfloat16

D_MODEL = 1024
DEPTH = 2
GRID_W = 64
HEAD_DIM = 64
NA_HEADS = 6
NA_KH = 8
NA_KW = 16
WA_HEADS = 6
WA_KV_HEADS = 2
WA_WINDOW = 128
WA_QBLOCK = 256
SSM_CH = 256
SSM_GROUP_CH = 16
SSM_GROUPS = 16
SSM_STATE = 64
SSM_N = SSM_GROUPS * SSM_STATE
NA_WIDTH = NA_HEADS * HEAD_DIM
WA_WIDTH = WA_HEADS * HEAD_DIM
WA_KV_WIDTH = WA_KV_HEADS * HEAD_DIM
IN_COLS = 2048
N_EXPERTS = 16
EC_CAPACITY = 2
EXPERT_FF = 1024
ROPE_BASE = 10000.0
RMS_EPS = 1e-6
NEG_INF = -1e30
ATTN_SCALE = HEAD_DIM ** -0.5
LOG2E = math.log2(math.e)
Q_SCALE = ATTN_SCALE * LOG2E

LANES = 128
SUBLANES = 8
VMEM_LIMIT = 48 * 1024 * 1024

C_NAQ, C_NAK, C_NAV, C_WAQ, C_WAK, C_WAV, C_SSU = 0, 384, 768, 1152, 1536, 1664, 1792
WA_HEAD_ORDER = (0, 3, 1, 4, 2, 5)

NA_QROWS = 4
NA_KROWS = NA_QROWS + NA_KH
NA_REL = 2 * NA_KH - 1
NA_PAIRS = 3 * NA_REL
TOKEN_TILE = 256
PROJ_TILE = 512
SCAN_TILE = 128
SCAN_BATCH = SUBLANES
FFN_ROWS = 512
SEL_CHUNK = 256
SEL_GROUP = LANES // N_EXPERTS
BF16_ROWS = 2 * SUBLANES
COMBINE_WINDOW = 64


def _cparams(sem):
    return pltpu.CompilerParams(dimension_semantics=sem, vmem_limit_bytes=VMEM_LIMIT)


def _sigmoid(x):
    return 1.0 / (1.0 + jnp.exp(-x))


def _rms(x, g):
    ms = jnp.mean(x * x, axis=-1, keepdims=True)
    return x * lax.rsqrt(ms + RMS_EPS) * g


def _dot(a, b):
    return jnp.dot(a, b, preferred_element_type=F32)


def _dot_nt(a, b):
    return lax.dot_general(a, b, (((1,), (1,)), ((), ())), preferred_element_type=F32)


def _mod_body(c_ref, w_ref, b_ref, o_ref):
    c = c_ref[...]
    s = c * _sigmoid(c)
    o_ref[0] = _dot(s.astype(BF16), w_ref[0].astype(BF16)) + b_ref[0]


def _modulation(cond, w_ada, b_ada):
    n = cond.shape[0]
    tn = 1024
    return pl.pallas_call(
        _mod_body,
        grid=(DEPTH, 6 * D_MODEL // tn),
        in_specs=[
            pl.BlockSpec((n, D_MODEL), lambda l, j: (0, 0)),
            pl.BlockSpec((1, D_MODEL, tn), lambda l, j: (l, 0, j)),
            pl.BlockSpec((1, 1, tn), lambda l, j: (l, 0, j)),
        ],
        out_specs=pl.BlockSpec((1, n, tn), lambda l, j: (l, 0, j)),
        out_shape=jax.ShapeDtypeStruct((DEPTH, n, 6 * D_MODEL), F32),
        compiler_params=_cparams(("arbitrary", "arbitrary")),
        name="modulation",
    )(cond, w_ada, b_ada.reshape(DEPTH, 1, 6 * D_MODEL))


def _rope_chunk(x, cos, sin_signed):
    lane = lax.broadcasted_iota(jnp.int32, x.shape, 1)
    first = (lane % 32) < 16
    rot = jnp.where(first, pltpu.roll(x, LANES - 16, 1), pltpu.roll(x, 16, 1))
    return x * cos + rot * sin_signed


INPROJ_KV_OUTPUTS = (1, 2, 4, 5)


def _inproj_body(rope, n_alias, x_ref, mod_ref, g_ref, w_ref, *rest):
    if rope:
        cos_ref, sin_ref = rest[:2]
        rest = rest[2:]
    qna_ref, kna_ref, vna_ref, qwa_ref, kwa_ref, vwa_ref, u_ref = rest[n_alias:]

    def put(ref, val):
        ref[...] = val.reshape(ref.shape)

    x = x_ref[...]
    m = mod_ref[0, 0]
    sh1 = m[:, 0:D_MODEL]
    sc1 = m[:, D_MODEL:2 * D_MODEL]
    h = _rms(x, g_ref[0, 0:1, :]) * (1.0 + sc1) + sh1
    z = _dot(h.astype(BF16), w_ref[0])
    qna_ref[...] = (z[:, C_NAQ:C_NAK] * Q_SCALE).astype(BF16)
    put(kna_ref, z[:, C_NAK:C_NAV])
    put(vna_ref, z[:, C_NAV:C_WAQ])
    put(vwa_ref, z[:, C_WAV:C_SSU])
    u_ref[...] = z[:, C_SSU:IN_COLS]
    if rope:
        cos = cos_ref[...]
        sin = sin_ref[...]
        for j in range(WA_WIDTH // LANES):
            qc = _rope_chunk(z[:, C_WAQ + j * LANES:C_WAQ + (j + 1) * LANES], cos, sin)
            qwa_ref[:, j * LANES:(j + 1) * LANES] = (qc * Q_SCALE).astype(BF16)
        put(kwa_ref, _rope_chunk(z[:, C_WAK:C_WAV], cos, sin))
    else:
        qwa_ref[...] = (z[:, C_WAQ:C_WAK] * Q_SCALE).astype(BF16)
        put(kwa_ref, z[:, C_WAK:C_WAV])


def _inproj(x, mod, mod_row_fn, norm_g, w_bf, l, B, L, rope_tabs, kv_layers=None):
    T = x.shape[0]
    tm = min(PROJ_TILE, L)
    tiles_per_seq = L // tm
    in_specs = [
        pl.BlockSpec((tm, D_MODEL), lambda i: (i, 0)),
        pl.BlockSpec((1, 1, 1, 6 * D_MODEL), lambda i: (l, mod_row_fn(i // tiles_per_seq), 0, 0)),
        pl.BlockSpec((1, 4, D_MODEL), lambda i: (l, 0, 0)),
        pl.BlockSpec((1, D_MODEL, IN_COLS), lambda i: (l, 0, 0)),
    ]
    args = [x, mod, norm_g, w_bf]
    if rope_tabs is not None:
        in_specs += [pl.BlockSpec((tm, LANES), lambda i: (i % tiles_per_seq, 0))] * 2
        args += list(rope_tabs)
    widths = (NA_WIDTH, NA_WIDTH, NA_WIDTH, WA_WIDTH, WA_KV_WIDTH, WA_KV_WIDTH, SSM_CH)
    dtypes = (BF16, F32, F32, BF16, F32, F32, F32)
    out_specs = [pl.BlockSpec((tm, w), lambda i: (i, 0)) for w in widths]
    out_shape = [jax.ShapeDtypeStruct((T, w), dt) for w, dt in zip(widths, dtypes)]
    aliases = {}
    n_alias = 0
    if kv_layers is not None:
        for o in INPROJ_KV_OUTPUTS:
            out_specs[o] = pl.BlockSpec((1, 1, tm, widths[o]),
                                        lambda i: (i // tiles_per_seq, l, i % tiles_per_seq, 0))
            out_shape[o] = jax.ShapeDtypeStruct((B, DEPTH, L, widths[o]), dtypes[o])
        n_alias = len(kv_layers)
        for k, (o, prev) in enumerate(zip(INPROJ_KV_OUTPUTS, kv_layers)):
            aliases[len(args)] = o
            in_specs.append(pl.BlockSpec(memory_space=pl.ANY))
            args.append(prev)
    return pl.pallas_call(
        functools.partial(_inproj_body, rope_tabs is not None, n_alias),
        grid=(T // tm,),
        in_specs=in_specs,
        out_specs=out_specs,
        out_shape=out_shape,
        input_output_aliases=aliases,
        compiler_params=_cparams(("arbitrary",)),
        name="inproj",
    )(*args)


def _lane_lo(shape):
    return lax.broadcasted_iota(jnp.int32, shape, len(shape) - 1) < HEAD_DIM


def _keep_half(q, lo, half):
    keep = jnp.where(lo, 1.0 - half, 0.0 + half).astype(q.dtype)
    return q * keep


def _values_and_ones(v, lo, half):
    one = jnp.ones_like(v)
    return (jnp.where(lo, v, one) if half == 0 else jnp.where(lo, one, v)).astype(BF16)


def _normalise(o2, extra=None):
    den = pltpu.roll(o2, HEAD_DIM, 1)
    if extra is not None:
        den = den + extra
    return o2 / den


def _attn_ctx_body(sink_ref, qna, kna, vna, qwa, kwa, vwa, ona, owa):
    L = qna.shape[1]
    lo = _lane_lo((L, LANES))
    for j in range(NA_WIDTH // LANES):
        sl = slice(j * LANES, (j + 1) * LANES)
        q2 = qna[0, :, sl]
        k2 = kna[0, 0, :, sl].astype(BF16)
        v2 = vna[0, 0, :, sl].astype(BF16)
        halves = []
        for half in range(2):
            s = _dot_nt(_keep_half(q2, lo, half), k2)
            p = jnp.exp2(s - jnp.max(s, axis=-1, keepdims=True))
            halves.append(_dot(p.astype(BF16), v2) / jnp.sum(p, axis=-1, keepdims=True))
        ona[0, :, sl] = jnp.where(lo, halves[0], halves[1]).astype(BF16)
    kw = kwa[0, 0].astype(BF16)
    vw = vwa[0, 0].astype(BF16)
    for j in range(WA_WIDTH // LANES):
        sl = slice(j * LANES, (j + 1) * LANES)
        q2 = qwa[0, :, sl]
        halves = []
        for half in range(2):
            s = _dot_nt(_keep_half(q2, lo, half), kw)
            sk = sink_ref[WA_HEAD_ORDER[2 * j + half]] * LOG2E
            m = jnp.maximum(jnp.max(s, axis=-1, keepdims=True), sk)
            p = jnp.exp2(s - m)
            l = jnp.sum(p, axis=-1, keepdims=True) + jnp.exp2(sk - m)
            halves.append(_dot(p.astype(BF16), vw) / l)
        owa[0, :, sl] = jnp.where(lo, halves[0], halves[1]).astype(BF16)


def _attn_ctx(sink, qna, kna, vna, qwa, kwa, vwa, l, B, L):
    def spec(w):
        return pl.BlockSpec((1, L, w), lambda b: (b, 0, 0))

    def layer_spec(w):
        return pl.BlockSpec((1, 1, L, w), lambda b: (b, l, 0, 0))

    r3 = lambda a: a.reshape(B, L, a.shape[-1])
    ona, owa = pl.pallas_call(
        _attn_ctx_body,
        grid=(B,),
        in_specs=[pl.BlockSpec(memory_space=pltpu.SMEM),
                  spec(NA_WIDTH), layer_spec(NA_WIDTH), layer_spec(NA_WIDTH),
                  spec(WA_WIDTH), layer_spec(WA_KV_WIDTH), layer_spec(WA_KV_WIDTH)],
        out_specs=[spec(NA_WIDTH), spec(WA_WIDTH)],
        out_shape=[jax.ShapeDtypeStruct((B, L, NA_WIDTH), BF16),
                   jax.ShapeDtypeStruct((B, L, WA_WIDTH), BF16)],
        compiler_params=_cparams(("arbitrary",)),
        name="attn_ctx",
    )(sink, r3(qna), kna, vna, r3(qwa), kwa, vwa)
    return ona.reshape(B * L, NA_WIDTH), owa.reshape(B * L, WA_WIDTH)


def _na_key_start(rb, rows):
    return jnp.clip(rb * NA_QROWS - NA_KH // 2, 0, rows - NA_KROWS)


def _attn_na_body(rows, idx_ref, q_ref, k_ref, v_ref, kc_ref, vc_ref, pairs_ref, o_ref):
    rb = pl.program_id(1)
    nq = NA_QROWS * GRID_W
    nk = NA_KROWS * GRID_W
    start = pl.multiple_of(_na_key_start(rb, rows) * GRID_W, GRID_W)
    lo = _lane_lo((nq, LANES))
    lo_v = _lane_lo((1, LANES))
    npair = NA_KROWS // 2
    pat = jnp.where(rb == 0, 0, jnp.where(rb == rows // NA_QROWS - 1, 2, 1))
    entry = [[idx_ref[(pat * NA_QROWS + i) * npair + jj] for jj in range(npair)] for i in range(NA_QROWS)]

    def bias(head):
        return jnp.concatenate(
            [jnp.concatenate([pairs_ref[head, entry[i][jj]] for jj in range(npair)], axis=1)
             for i in range(NA_QROWS)], axis=0)

    for j in range(NA_WIDTH // LANES):
        sl = slice(j * LANES, (j + 1) * LANES)
        q2 = q_ref[0, :, sl]
        kl = k_ref[0, pl.ds(start, nk), sl].astype(BF16)
        vl = v_ref[0, pl.ds(start, nk), sl]
        kc = kc_ref[0, 0, :, sl].astype(BF16)
        vc = vc_ref[0, 0, :, sl]
        halves = []
        for half in range(2):
            qm = _keep_half(q2, lo, half)
            s_loc = _dot_nt(qm, kl) + bias(2 * j + half)
            s_ctx = _dot_nt(qm, kc)
            m = jnp.maximum(jnp.max(s_loc, axis=-1, keepdims=True),
                            jnp.max(s_ctx, axis=-1, keepdims=True))
            o2 = (_dot(jnp.exp2(s_loc - m).astype(BF16), _values_and_ones(vl, lo_v, half))
                  + _dot(jnp.exp2(s_ctx - m).astype(BF16), _values_and_ones(vc, lo_v, half)))
            halves.append(_normalise(o2))
        o_ref[0, :, sl] = jnp.where(lo, halves[0], halves[1]).astype(BF16)


def _na_bias_pairs(rpb_l):
    cq = np.arange(GRID_W)
    col_start = np.clip(cq - NA_KW // 2, 0, GRID_W - NA_KW)
    col_ok = (cq[None, :] >= col_start[:, None]) & (cq[None, :] < col_start[:, None] + NA_KW)
    dcol = np.clip(cq[None, :] - cq[:, None], -(NA_KW - 1), NA_KW - 1) + (NA_KW - 1)
    pick = (dcol[None] == np.arange(2 * NA_KW - 1)[:, None, None]).astype(np.float32)
    tiles = jnp.einsum('hrd,dqc->hrqc', rpb_l.astype(F32) * LOG2E, pick, precision=lax.Precision.HIGHEST)
    tiles = jnp.where(col_ok[None, None], tiles, NEG_INF)
    masked = jnp.full_like(tiles, NEG_INF)
    both = jnp.concatenate([tiles[:, :-1], tiles[:, 1:]], axis=-1)
    right = jnp.concatenate([masked, tiles], axis=-1)
    left = jnp.concatenate([tiles, masked], axis=-1)
    none = jnp.concatenate([masked[:, :1], masked[:, :1]], axis=-1)
    return jnp.concatenate([both, right, left, none], axis=1)


def _na_pair_index(rows):
    nrb = rows // NA_QROWS
    idx = np.zeros((3, NA_QROWS, NA_KROWS // 2), np.int32)
    for pi, rb in enumerate((0, 1, nrb - 1)):
        ks = int(np.clip(rb * NA_QROWS - NA_KH // 2, 0, rows - NA_KROWS))
        for i in range(NA_QROWS):
            qr = rb * NA_QROWS + i
            rs = int(np.clip(qr - NA_KH // 2, 0, rows - NA_KH))
            for jj in range(NA_KROWS // 2):
                kr = ks + 2 * jj
                r = kr - qr + NA_KH - 1
                ok_l = rs <= kr < rs + NA_KH
                ok_r = rs <= kr + 1 < rs + NA_KH
                if ok_l and ok_r:
                    idx[pi, i, jj] = r
                elif ok_r:
                    idx[pi, i, jj] = (NA_REL - 1) + (r + 1)
                elif ok_l:
                    idx[pi, i, jj] = (NA_REL - 1) + NA_REL + r
                else:
                    idx[pi, i, jj] = NA_PAIRS - 1
    return jnp.asarray(idx.reshape(-1))


def _attn_na(q, k, v, cache_k, cache_v, pairs, l, B, L):
    rows = L // GRID_W
    nrb = rows // NA_QROWS
    nq = NA_QROWS * GRID_W
    P = cache_k.shape[2]
    r3 = lambda a: a.reshape(B, L, a.shape[-1])
    ck = cache_k.reshape(B, DEPTH, P, NA_WIDTH)
    cv = cache_v.reshape(B, DEPTH, P, NA_WIDTH)
    grid_spec = pltpu.PrefetchScalarGridSpec(
        num_scalar_prefetch=1,
        grid=(B, nrb),
        in_specs=[
            pl.BlockSpec((1, nq, NA_WIDTH), lambda b, r, s: (b, r, 0)),
            pl.BlockSpec((1, L, NA_WIDTH), lambda b, r, s: (b, 0, 0)),
            pl.BlockSpec((1, L, NA_WIDTH), lambda b, r, s: (b, 0, 0)),
            pl.BlockSpec((1, 1, P, NA_WIDTH), lambda b, r, s: (b, l, 0, 0)),
            pl.BlockSpec((1, 1, P, NA_WIDTH), lambda b, r, s: (b, l, 0, 0)),
            pl.BlockSpec(pairs.shape, lambda b, r, s: (0, 0, 0, 0)),
        ],
        out_specs=pl.BlockSpec((1, nq, NA_WIDTH), lambda b, r, s: (b, r, 0)),
    )
    o = pl.pallas_call(
        functools.partial(_attn_na_body, rows),
        grid_spec=grid_spec,
        out_shape=jax.ShapeDtypeStruct((B, L, NA_WIDTH), BF16),
        compiler_params=_cparams(("arbitrary", "arbitrary")),
        name="attn_na",
    )(_na_pair_index(rows), r3(q), r3(k), r3(v), ck, cv, pairs)
    return o.reshape(B * L, NA_WIDTH)


def _attn_wa_body(L, sink_ref, q_ref, k_ref, v_ref, kc_ref, vc_ref, o_ref):
    n = pl.program_id(1)
    nk = WA_QBLOCK + 2 * WA_WINDOW
    npair = WA_WIDTH // LANES
    start = pl.multiple_of(jnp.clip(n * WA_QBLOCK - WA_WINDOW, 0, L - nk), WA_WINDOW)
    kl = k_ref[0, pl.ds(start, nk), :].astype(BF16)
    vl = v_ref[0, pl.ds(start, nk), :]
    kc = kc_ref[0, 0].astype(BF16)
    vc = vc_ref[0, 0]
    lo_v = _lane_lo((1, LANES))
    rows = npair * WA_QBLOCK
    qpos = n * WA_QBLOCK + lax.broadcasted_iota(jnp.int32, (rows, nk), 0) % WA_QBLOCK
    kpos = start + lax.broadcasted_iota(jnp.int32, (rows, nk), 1)
    in_win = jnp.abs(kpos - qpos) <= WA_WINDOW
    lo = _lane_lo((rows, LANES))
    q_all = jnp.concatenate([q_ref[0, :, j * LANES:(j + 1) * LANES] for j in range(npair)], axis=0)
    blk = lax.broadcasted_iota(jnp.int32, (rows, 1), 0) // WA_QBLOCK
    scores = []
    for g in range(WA_KV_HEADS):
        qm = _keep_half(q_all, lo, g)
        scores.append((jnp.where(in_win, _dot_nt(qm, kl), NEG_INF), _dot_nt(qm, kc)))
    probs = []
    for g in range(WA_KV_HEADS):
        s_loc, s_ctx = scores[g]
        sk = jnp.zeros((rows, 1), F32)
        for j in range(npair):
            sk = jnp.where(blk == j, sink_ref[WA_HEAD_ORDER[2 * j + g]] * LOG2E, sk)
        m = jnp.maximum(jnp.maximum(jnp.max(s_loc, axis=-1, keepdims=True),
                                    jnp.max(s_ctx, axis=-1, keepdims=True)), sk)
        probs.append((jnp.exp2(s_loc - m).astype(BF16), jnp.exp2(s_ctx - m).astype(BF16), jnp.exp2(sk - m)))
    outs = []
    for g in range(WA_KV_HEADS):
        p_loc, p_ctx, p_sink = probs[g]
        o2 = _dot(p_loc, _values_and_ones(vl, lo_v, g)) + _dot(p_ctx, _values_and_ones(vc, lo_v, g))
        outs.append(_normalise(o2, p_sink))
    o_all = jnp.where(lo, outs[0], outs[1]).astype(BF16)
    for j in range(npair):
        o_ref[0, :, j * LANES:(j + 1) * LANES] = o_all[j * WA_QBLOCK:(j + 1) * WA_QBLOCK]


def _attn_wa(sink, q, k, v, cache_k, cache_v, l, B, L):
    nb = L // WA_QBLOCK
    P = cache_k.shape[2]
    r3 = lambda a: a.reshape(B, L, a.shape[-1])
    ck = cache_k.reshape(B, DEPTH, P, WA_KV_WIDTH)
    cv = cache_v.reshape(B, DEPTH, P, WA_KV_WIDTH)
    o = pl.pallas_call(
        functools.partial(_attn_wa_body, L),
        grid=(B, nb),
        in_specs=[
            pl.BlockSpec(memory_space=pltpu.SMEM),
            pl.BlockSpec((1, WA_QBLOCK, WA_WIDTH), lambda b, n: (b, n, 0)),
            pl.BlockSpec((1, L, WA_KV_WIDTH), lambda b, n: (b, 0, 0)),
            pl.BlockSpec((1, L, WA_KV_WIDTH), lambda b, n: (b, 0, 0)),
            pl.BlockSpec((1, 1, P, WA_KV_WIDTH), lambda b, n: (b, l, 0, 0)),
            pl.BlockSpec((1, 1, P, WA_KV_WIDTH), lambda b, n: (b, l, 0, 0)),
        ],
        out_specs=pl.BlockSpec((1, WA_QBLOCK, WA_WIDTH), lambda b, n: (b, n, 0)),
        out_shape=jax.ShapeDtypeStruct((B, L, WA_WIDTH), BF16),
        compiler_params=_cparams(("arbitrary", "arbitrary")),
        name="attn_wa",
    )(sink, r3(q), r3(k), r3(v), ck, cv)
    return o.reshape(B * L, WA_WIDTH)


def _s5_prep_body(lr_ref, li_ref, ldt_ref, br_ref, bi_ref, a_ref, bb_ref):
    lr = lr_ref[0]
    li = li_ref[0]
    dt = jnp.exp(ldt_ref[0])
    mag = jnp.exp(lr * dt)
    ar = mag * jnp.cos(li * dt)
    ai = mag * jnp.sin(li * dt)
    den = lr * lr + li * li
    nr = ar - 1.0
    fr = (nr * lr + ai * li) / den
    fi = (ai * lr - nr * li) / den
    a_ref[0, :, 0:SSM_N] = ar
    a_ref[0, :, SSM_N:2 * SSM_N] = ai
    br = br_ref[0]
    bi = bi_ref[0]
    bb_ref[0, :, 0:SSM_N] = (fr * br - fi * bi).astype(BF16)
    bb_ref[0, :, SSM_N:2 * SSM_N] = (fr * bi + fi * br).astype(BF16)


def _s5_prep(lam_re, lam_im, log_dt, b_re, b_im):
    n = DEPTH * 2
    eye = jnp.eye(SSM_GROUPS, dtype=F32)

    def blockdiag_b(b):
        return jnp.einsum('ngph,gk->nghkp', b.reshape(n, SSM_GROUPS, SSM_STATE, SSM_GROUP_CH), eye
                          ).reshape(n, SSM_CH, SSM_N)

    lr = lam_re.reshape(n, 1, SSM_N)
    li = lam_im.reshape(n, 1, SSM_N)
    ldt = jnp.repeat(log_dt.reshape(n, SSM_GROUPS), SSM_STATE, axis=-1).reshape(n, 1, SSM_N)
    vec = pl.BlockSpec((1, 1, SSM_N), lambda i: (i, 0, 0))
    mat = pl.BlockSpec((1, SSM_CH, SSM_N), lambda i: (i, 0, 0))
    return pl.pallas_call(
        _s5_prep_body,
        grid=(n,),
        in_specs=[vec, vec, vec, mat, mat],
        out_specs=[pl.BlockSpec((1, 1, 2 * SSM_N), lambda i: (i, 0, 0)),
                   pl.BlockSpec((1, SSM_CH, 2 * SSM_N), lambda i: (i, 0, 0))],
        out_shape=[jax.ShapeDtypeStruct((n, 1, 2 * SSM_N), F32),
                   jax.ShapeDtypeStruct((n, SSM_CH, 2 * SSM_N), BF16)],
        compiler_params=_cparams(("arbitrary",)),
        name="s5_prep",
    )(lr, li, ldt, blockdiag_b(b_re), blockdiag_b(b_im))


def _s5_c_matrix(c_re, c_im):
    n = DEPTH * 2
    eye = jnp.eye(SSM_GROUPS, dtype=F32)

    def blk(c):
        return jnp.einsum('nghp,gk->ngpkh', c.reshape(n, SSM_GROUPS, SSM_GROUP_CH, SSM_STATE), eye
                          ).reshape(n, SSM_N, SSM_CH)

    return jnp.concatenate([blk(c_re), -blk(c_im)], axis=1).astype(BF16)


def _s5_scan_body(tl, u_ref, bb_ref, a_ref, c_ref, h0_ref, y_ref, hfin_ref, bu, hb, hst):
    d = pl.program_id(0)
    i = pl.program_id(2)
    sb = SCAN_BATCH
    re = slice(0, SSM_N)
    im = slice(SSM_N, 2 * SSM_N)

    @pl.when(i == 0)
    def _():
        hst[...] = h0_ref[0]

    u_tb = jnp.transpose(u_ref[...], (1, 0, 2)).reshape(tl * sb, SSM_CH)
    u_bf = u_tb.astype(BF16)
    half = tl * sb // 2
    bu[0:half, :] = _dot(u_bf[0:half], bb_ref[0])
    bu[half:, :] = _dot(u_bf[half:], bb_ref[0])
    a = a_ref[0]
    ar = jnp.broadcast_to(a[:, re], (sb, SSM_N))
    ai = jnp.broadcast_to(a[:, im], (sb, SSM_N))

    def scan(reverse):
        def pair(s, carry):
            hr, hi = carry
            t0 = tl - 1 - 2 * s if reverse else 2 * s
            t1 = t0 - 1 if reverse else t0 + 1
            r0 = pl.ds(pl.multiple_of(t0 * sb, sb), sb)
            r1 = pl.ds(pl.multiple_of(t1 * sb, sb), sb)
            h0r = ar * hr - ai * hi + bu[r0, re]
            h0i = ar * hi + ai * hr + bu[r0, im]
            h1r = ar * h0r - ai * h0i + bu[r1, re]
            h1i = ar * h0i + ai * h0r + bu[r1, im]
            rows = pl.ds(pl.multiple_of((t1 if reverse else t0) * sb, 2 * sb), 2 * sb)
            early, late = ((h1r, h1i), (h0r, h0i)) if reverse else ((h0r, h0i), (h1r, h1i))
            hb[rows, re] = jnp.concatenate([early[0], late[0]], axis=0).astype(BF16)
            hb[rows, im] = jnp.concatenate([early[1], late[1]], axis=0).astype(BF16)
            return h1r, h1i

        hr, hi = lax.fori_loop(0, tl // 2, pair, (hst[:, re], hst[:, im]))
        hst[:, re] = hr
        hst[:, im] = hi
        hfin_ref[0, :, re] = hr
        hfin_ref[0, :, im] = hi

    @pl.when(d == 0)
    def _():
        scan(False)

    @pl.when(d == 1)
    def _():
        scan(True)

    y_tb = jnp.concatenate([_dot(hb[0:half, :], c_ref[0]), _dot(hb[half:, :], c_ref[0])], axis=0)
    y_ref[0] = jnp.transpose(y_tb.reshape(tl, sb, SSM_CH), (1, 0, 2))


def _s5_scan(u, bb, a, cm, h0, B, L):
    tl = SCAN_TILE
    nt = L // tl
    tile = lambda d, i: i + d * (nt - 1 - 2 * i)
    return pl.pallas_call(
        functools.partial(_s5_scan_body, tl),
        grid=(2, B // SCAN_BATCH, nt),
        in_specs=[
            pl.BlockSpec((SCAN_BATCH, tl, SSM_CH), lambda d, b, i: (b, tile(d, i), 0)),
            pl.BlockSpec((1, SSM_CH, 2 * SSM_N), lambda d, b, i: (d, 0, 0)),
            pl.BlockSpec((1, 1, 2 * SSM_N), lambda d, b, i: (d, 0, 0)),
            pl.BlockSpec((1, 2 * SSM_N, SSM_CH), lambda d, b, i: (d, 0, 0)),
            pl.BlockSpec((1, SCAN_BATCH, 2 * SSM_N), lambda d, b, i: (d, b, 0)),
        ],
        out_specs=[
            pl.BlockSpec((1, SCAN_BATCH, tl, SSM_CH), lambda d, b, i: (d, b, tile(d, i), 0)),
            pl.BlockSpec((1, SCAN_BATCH, 2 * SSM_N), lambda d, b, i: (d, b, 0)),
        ],
        out_shape=[jax.ShapeDtypeStruct((2, B, L, SSM_CH), F32),
                   jax.ShapeDtypeStruct((2, B, 2 * SSM_N), F32)],
        scratch_shapes=[pltpu.VMEM((SCAN_BATCH * tl, 2 * SSM_N), F32),
                        pltpu.VMEM((SCAN_BATCH * tl, 2 * SSM_N), BF16),
                        pltpu.VMEM((SCAN_BATCH, 2 * SSM_N), F32)],
        compiler_params=_cparams(("arbitrary", "arbitrary", "arbitrary")),
        name="s5_scan",
    )(u.reshape(B, L, SSM_CH), bb, a, cm, h0)


def _outproj_body(ona_ref, owa_ref, y_ref, u_ref, dsk_ref, wglu_ref, wout_ref, x_ref, mod_ref, g_ref,
                  wr_ref, x1_ref, h2_ref, aff_ref):
    y = dsk_ref[...] * u_ref[...] + y_ref[0] + y_ref[1]
    zg = _dot(y.astype(BF16), wglu_ref[...])
    oss = zg[:, 0:SSM_CH] * _sigmoid(zg[:, SSM_CH:2 * SSM_CH])
    mix = (_dot(ona_ref[...], wout_ref[0:NA_WIDTH, :])
           + _dot(owa_ref[...], wout_ref[NA_WIDTH:NA_WIDTH + WA_WIDTH, :])
           + _dot(oss.astype(BF16), wout_ref[NA_WIDTH + WA_WIDTH:D_MODEL, :]))
    m = mod_ref[0]
    g1 = m[:, 2 * D_MODEL:3 * D_MODEL]
    sh2 = m[:, 3 * D_MODEL:4 * D_MODEL]
    sc2 = m[:, 4 * D_MODEL:5 * D_MODEL]
    x1 = x_ref[...] + g1 * _rms(mix, g_ref[1:2, :])
    x1_ref[...] = x1
    h2 = _rms(x1, g_ref[2:3, :]) * (1.0 + sc2) + sh2
    h2_hi = h2.astype(BF16)
    h2_ref[...] = h2_hi
    h2_lo = (h2 - h2_hi.astype(F32)).astype(BF16)
    r_hi = _dot(h2_hi, wr_ref[...])
    logits = r_hi + pltpu.roll(r_hi, LANES - N_EXPERTS, 1) + _dot(h2_lo, wr_ref[...])
    lane = lax.broadcasted_iota(jnp.int32, logits.shape, 1)
    logits = jnp.where(lane < N_EXPERTS, logits, NEG_INF)
    mx = jnp.max(logits, axis=-1, keepdims=True)
    p = jnp.exp(logits - mx)
    aff_ref[...] = p / jnp.sum(p, axis=-1, keepdims=True)


def _outproj(ona, owa, y2, u, dsk, wglu_bf, wout_bf, x, mod_l, mod_row_fn, g, wr_pad, L):
    T = x.shape[0]
    tm = min(PROJ_TILE, L)
    tiles_per_seq = L // tm
    row = lambda w: pl.BlockSpec((tm, w), lambda i: (i, 0))
    full = lambda a: pl.BlockSpec(a.shape, lambda i: (0,) * a.ndim)
    return pl.pallas_call(
        _outproj_body,
        grid=(T // tm,),
        in_specs=[row(NA_WIDTH), row(WA_WIDTH),
                  pl.BlockSpec((2, tm, SSM_CH), lambda i: (0, i, 0)),
                  row(SSM_CH), full(dsk), full(wglu_bf), full(wout_bf), row(D_MODEL),
                  pl.BlockSpec((1, 1, 6 * D_MODEL), lambda i: (mod_row_fn(i // tiles_per_seq), 0, 0)),
                  full(g), full(wr_pad)],
        out_specs=[row(D_MODEL), row(D_MODEL), row(LANES)],
        out_shape=[jax.ShapeDtypeStruct((T, D_MODEL), F32),
                   jax.ShapeDtypeStruct((T, D_MODEL), BF16),
                   jax.ShapeDtypeStruct((T, LANES), F32)],
        compiler_params=_cparams(("arbitrary",)),
        name="outproj",
    )(ona, owa, y2.reshape(2, T, SSM_CH), u, dsk, wglu_bf, wout_bf, x, mod_l, g, wr_pad)


def _select_body(cap, aff_ref, pos_ref, post_ref, afft_ref, starts_ref):
    L = aff_ref.shape[1]
    aff = aff_ref[0]
    for r in range(1, SEL_GROUP):
        aff = aff + pltpu.roll(aff_ref[r], r * N_EXPERTS, 1)
    bits = pltpu.bitcast(aff, jnp.int32)
    capf = jnp.float32(cap)

    def bisect(k, thr):
        cand = thr | jnp.left_shift(jnp.int32(1), 30 - k)
        cnt = jnp.sum(jnp.where(bits >= cand, 1.0, 0.0), axis=0, keepdims=True)
        return jnp.where(cnt >= capf, cand, thr)

    thr = lax.fori_loop(0, 31, bisect, jnp.zeros((1, LANES), jnp.int32))
    gt = jnp.where(bits > thr, 1.0, 0.0)
    eq = jnp.where(bits == thr, 1.0, 0.0)
    need = capf - jnp.sum(gt, axis=0, keepdims=True)
    ck = SEL_CHUNK
    tri = (lax.broadcasted_iota(jnp.int32, (ck, ck), 0)
           >= lax.broadcasted_iota(jnp.int32, (ck, ck), 1))
    tri = jnp.where(tri, 1.0, 0.0).astype(BF16)

    def prefix(mask):
        outs = []
        carries = [jnp.zeros((1, LANES), F32)]
        for j in range(L // ck):
            c = _dot(tri, mask[j * ck:(j + 1) * ck].astype(BF16)) + carries[-1]
            outs.append(c)
            carries.append(c[ck - 1:ck, :])
        return jnp.concatenate(outs, axis=0), jnp.concatenate(carries, axis=0)

    tie = jnp.where(prefix(eq)[0] <= need, eq, 0.0)
    sel = gt + tie
    rank, starts = prefix(sel)
    pos = jnp.where(sel > 0.0, rank - 1.0, -1.0)
    pos_ref[0] = pos
    post_ref[0] = jnp.transpose(pos)
    afft_ref[0] = jnp.transpose(aff)
    starts_ref[0] = starts.astype(jnp.int32)


def _select(aff, B, L):
    cap = EC_CAPACITY * L // N_EXPERTS
    ng = B // SEL_GROUP
    return pl.pallas_call(
        functools.partial(_select_body, cap),
        grid=(ng,),
        in_specs=[pl.BlockSpec((SEL_GROUP, L, LANES), lambda i: (i, 0, 0))],
        out_specs=[pl.BlockSpec((1, L, LANES), lambda i: (i, 0, 0)),
                   pl.BlockSpec((1, LANES, L), lambda i: (i, 0, 0)),
                   pl.BlockSpec((1, LANES, L), lambda i: (i, 0, 0)),
                   pl.BlockSpec((1, L // SEL_CHUNK + 1, LANES), lambda i: (i, 0, 0))],
        out_shape=[jax.ShapeDtypeStruct((ng, L, LANES), F32),
                   jax.ShapeDtypeStruct((ng, LANES, L), F32),
                   jax.ShapeDtypeStruct((ng, LANES, L), F32),
                   jax.ShapeDtypeStruct((ng, L // SEL_CHUNK + 1, LANES), jnp.int32)],
        compiler_params=_cparams(("arbitrary",)),
        name="ec_select",
    )(aff.reshape(B, L, LANES))


def _window_rows(base, w):
    return pl.ds(base, w) if isinstance(base, int) else pl.ds(pl.multiple_of(base, BF16_ROWS), w)


def _gather_body(cap, win, starts_ref, h_ref, post_ref, afft_ref, xs_ref, gs_ref):
    b = pl.program_id(0)
    j = pl.program_id(1)
    grp = b // SEL_GROUP
    lane0 = (b % SEL_GROUP) * N_EXPERTS

    @pl.when(j == 0)
    def _():
        xs_ref[...] = jnp.zeros_like(xs_ref)
        gs_ref[...] = jnp.zeros_like(gs_ref)

    h = h_ref[0]
    ck = h.shape[0]

    def run(w, bases):
        slot = lax.broadcasted_iota(jnp.int32, (w, ck), 0)
        onehots = []
        for e in range(N_EXPERTS):
            rows = _window_rows(bases[e], w)
            hit = post_ref[0, e:e + 1, :] == (slot + bases[e]).astype(F32)
            gs_ref[e, 0, rows, :] += jnp.sum(jnp.where(hit, afft_ref[0, e:e + 1, :], 0.0), axis=1, keepdims=True)
            onehots.append(jnp.where(hit, 1.0, 0.0).astype(BF16))
        xs = _dot(jnp.concatenate(onehots, axis=0), h).astype(BF16)
        for e in range(N_EXPERTS):
            rows = _window_rows(bases[e], w)
            xs_ref[e, 0, rows, :] += xs[e * w:(e + 1) * w]

    if win == cap:
        run(cap, [0] * N_EXPERTS)
        return
    bases = []
    fits = None
    for e in range(N_EXPERTS):
        s0 = starts_ref[grp, j, lane0 + e]
        s1 = starts_ref[grp, j + 1, lane0 + e]
        base = jnp.minimum((s0 // BF16_ROWS) * BF16_ROWS, cap - win)
        bases.append(base)
        ok = s1 - base <= win
        fits = ok if fits is None else jnp.logical_and(fits, ok)

    @pl.when(fits)
    def _():
        run(win, bases)

    @pl.when(jnp.logical_not(fits))
    def _():
        run(cap, [0] * N_EXPERTS)


def _gather(starts, h2, post, afft, B, L):
    cap = EC_CAPACITY * L // N_EXPERTS
    ck = SEL_CHUNK
    win = min(cap, COMBINE_WINDOW)
    lane_blk = lambda b, j, s: (b // SEL_GROUP, b % SEL_GROUP, j)
    grid_spec = pltpu.PrefetchScalarGridSpec(
        num_scalar_prefetch=1,
        grid=(B, L // ck),
        in_specs=[pl.BlockSpec((1, ck, D_MODEL), lambda b, j, s: (b, j, 0)),
                  pl.BlockSpec((1, N_EXPERTS, ck), lane_blk),
                  pl.BlockSpec((1, N_EXPERTS, ck), lane_blk)],
        out_specs=[pl.BlockSpec((N_EXPERTS, 1, cap, D_MODEL), lambda b, j, s: (0, b, 0, 0)),
                   pl.BlockSpec((N_EXPERTS, 1, cap, 1), lambda b, j, s: (0, b, 0, 0))],
    )
    xs, gs = pl.pallas_call(
        functools.partial(_gather_body, cap, win),
        grid_spec=grid_spec,
        out_shape=[jax.ShapeDtypeStruct((N_EXPERTS, B, cap, D_MODEL), BF16),
                   jax.ShapeDtypeStruct((N_EXPERTS, B, cap, 1), F32)],
        compiler_params=_cparams(("arbitrary", "arbitrary")),
        name="ec_gather",
    )(starts, h2.reshape(B, L, D_MODEL), post, afft)
    return xs.reshape(N_EXPERTS, B * cap, D_MODEL), gs.reshape(N_EXPERTS, B * cap, 1)


def _ffn_body(n_first, xa_ref, ga_ref, xb_ref, gb_ref, wg_ref, wu_ref, wd_ref, y_ref, wgb, wub, wdb):
    j = pl.program_id(1)

    @pl.when(j == 0)
    def _():
        wgb[...] = wg_ref[0, 0].astype(BF16)
        wub[...] = wu_ref[0, 0].astype(BF16)
        wdb[...] = wd_ref[0, 0].astype(BF16)

    def run(xs_ref, gs_ref):
        x = xs_ref[0]
        a = _dot(x, wgb[...])
        u = _dot(x, wub[...])
        hm = (a * _sigmoid(a) * u).astype(BF16)
        y_ref[0] = (_dot(hm, wdb[...]) * gs_ref[0]).astype(BF16)

    @pl.when(j < n_first)
    def _():
        run(xa_ref, ga_ref)

    @pl.when(j >= n_first)
    def _():
        run(xb_ref, gb_ref)


def _ffn(xa, ga, xb, gb, wg, wu, wd, l):
    rc = FFN_ROWS
    na = xa.shape[1] // rc
    nb = xb.shape[1] // rc
    first = lambda w: pl.BlockSpec((1, rc, w), lambda e, j: (e, jnp.minimum(j, na - 1), 0))
    second = lambda w: pl.BlockSpec((1, rc, w), lambda e, j: (e, jnp.maximum(j - na, 0), 0))
    wspec = lambda a, b: pl.BlockSpec((1, 1, a, b), lambda e, j: (l, e, 0, 0))
    return pl.pallas_call(
        functools.partial(_ffn_body, na),
        grid=(N_EXPERTS, na + nb),
        in_specs=[first(D_MODEL), first(1), second(D_MODEL), second(1),
                  wspec(D_MODEL, EXPERT_FF), wspec(D_MODEL, EXPERT_FF), wspec(EXPERT_FF, D_MODEL)],
        out_specs=pl.BlockSpec((1, rc, D_MODEL), lambda e, j: (e, j, 0)),
        out_shape=jax.ShapeDtypeStruct((N_EXPERTS, (na + nb) * rc, D_MODEL), BF16),
        scratch_shapes=[pltpu.VMEM((D_MODEL, EXPERT_FF), BF16),
                        pltpu.VMEM((D_MODEL, EXPERT_FF), BF16),
                        pltpu.VMEM((EXPERT_FF, D_MODEL), BF16)],
        compiler_params=_cparams(("arbitrary", "arbitrary")),
        name="ec_ffn",
    )(xa, ga, xb, gb, wg, wu, wd)


def _combine_body(cap, win, starts_ref, pos_ref, y_ref, x1_ref, mod_ref, g_ref, o_ref):
    b = pl.program_id(0)
    i = pl.program_id(1)
    grp = b // SEL_GROUP
    lane0 = (b % SEL_GROUP) * N_EXPERTS
    pos = pos_ref[0].astype(BF16)
    g2 = mod_ref[0][:, 5 * D_MODEL:6 * D_MODEL]

    def finish(w, bases, values):
        n = N_EXPERTS * w
        col = lax.broadcasted_iota(jnp.int32, (1, n), 1)
        expert = jnp.zeros((1, n), jnp.int32)
        for e in range(1, N_EXPERTS):
            expert = expert + jnp.where(col >= e * w, 1, 0)
        target = col - expert * w
        if bases is not None:
            for e in range(N_EXPERTS):
                target = jnp.where(expert == e, target + bases[e], target)
        expand = jnp.where(lax.broadcasted_iota(jnp.int32, (LANES, n), 0) == expert + lane0, 1.0, 0.0)
        per_col = _dot(pos, expand.astype(BF16))
        onehot = jnp.where(per_col == target.astype(F32), 1.0, 0.0).astype(BF16)
        f = _dot(onehot, values)
        o_ref[...] = x1_ref[...] + g2 * _rms(f, g_ref[3:4, :])

    if win == cap:
        finish(cap, None, y_ref[...].reshape(N_EXPERTS * cap, D_MODEL))
        return
    bases = []
    fits = None
    for e in range(N_EXPERTS):
        s0 = starts_ref[grp, i, lane0 + e]
        s1 = starts_ref[grp, i + 1, lane0 + e]
        base = jnp.minimum((s0 // BF16_ROWS) * BF16_ROWS, cap - win)
        bases.append(base)
        ok = s1 - base <= win
        fits = ok if fits is None else jnp.logical_and(fits, ok)

    @pl.when(fits)
    def _():
        rows = [y_ref[e, pl.ds(pl.multiple_of(bases[e], BF16_ROWS), win), :] for e in range(N_EXPERTS)]
        finish(win, bases, jnp.concatenate(rows, axis=0))

    @pl.when(jnp.logical_not(fits))
    def _():
        finish(cap, None, y_ref[...].reshape(N_EXPERTS * cap, D_MODEL))


def _combine(starts, pos, y, row_off, x1, mod_l, mod_row_fn, g, B, L):
    cap = EC_CAPACITY * L // N_EXPERTS
    tq = TOKEN_TILE
    assert tq == SEL_CHUNK
    nq = L // tq
    blk_off = row_off // cap
    win = min(cap, COMBINE_WINDOW)
    grid_spec = pltpu.PrefetchScalarGridSpec(
        num_scalar_prefetch=1,
        grid=(B, nq),
        in_specs=[pl.BlockSpec((1, tq, LANES), lambda b, i, s: (b // SEL_GROUP, i, 0)),
                  pl.BlockSpec((N_EXPERTS, cap, D_MODEL), lambda b, i, s: (0, blk_off + b, 0)),
                  pl.BlockSpec((tq, D_MODEL), lambda b, i, s: (b * nq + i, 0)),
                  pl.BlockSpec((1, 1, 6 * D_MODEL), lambda b, i, s: (mod_row_fn(b), 0, 0)),
                  pl.BlockSpec((4, D_MODEL), lambda b, i, s: (0, 0))],
        out_specs=pl.BlockSpec((tq, D_MODEL), lambda b, i, s: (b * nq + i, 0)),
    )
    return pl.pallas_call(
        functools.partial(_combine_body, cap, win),
        grid_spec=grid_spec,
        out_shape=jax.ShapeDtypeStruct((B * L, D_MODEL), F32),
        compiler_params=_cparams(("arbitrary", "arbitrary")),
        name="ec_combine",
    )(starts, pos, y, x1, mod_l, g)


def _rope_tables(L):
    t = jnp.arange(L)
    row = (t // GRID_W).astype(F32)
    col = (t % GRID_W).astype(F32)
    half = HEAD_DIM // 4
    inv = ROPE_BASE ** (-jnp.arange(half, dtype=F32) / half)
    d = np.arange(LANES) % HEAD_DIM
    use_col = (d // (HEAD_DIM // 2)) == 1
    pos = jnp.where(use_col[None, :], col[:, None], row[:, None])
    ang = pos * inv[d % half][None, :]
    sign = np.where((d % (HEAD_DIM // 2)) < half, -1.0, 1.0).astype(np.float32)
    return jnp.cos(ang), jnp.sin(ang) * sign[None, :]


def _permute_wa_heads(a, axis, start):
    cut = lambda lo, hi: lax.slice_in_dim(a, lo, hi, axis=axis)
    heads = [cut(start + h * HEAD_DIM, start + (h + 1) * HEAD_DIM) for h in WA_HEAD_ORDER]
    return jnp.concatenate([cut(0, start)] + heads + [cut(start + WA_WIDTH, a.shape[axis])], axis=axis)


def kernel(x_prompt, x_sample, c, cache_na_k, cache_na_v, cache_wa_k, cache_wa_v, state_ssm, c_ctx, w_ada, b_ada, norm_g, w_in, w_out, na_rpb, wa_sink, ssm_lambda_re, ssm_lambda_im, ssm_log_dt, ssm_b_re, ssm_b_im, ssm_c_re, ssm_c_im, ssm_d, w_glu, w_router, w_exp_gate, w_exp_up, w_exp_down):
    Bc, Lc, _ = x_prompt.shape
    Bs, Ls, _ = x_sample.shape
    ctx_row = Bs
    n_cond = ((Bs + 1 + SUBLANES - 1) // SUBLANES) * SUBLANES
    cond = jnp.zeros((n_cond, D_MODEL), F32).at[0:Bs].set(c).at[ctx_row].set(c_ctx)
    mod = _modulation(cond, w_ada, b_ada)
    mod = mod.reshape(DEPTH, n_cond, 1, 6 * D_MODEL)

    w_in_bf = _permute_wa_heads(w_in, 2, C_WAQ).astype(BF16)
    w_out_bf = _permute_wa_heads(w_out, 1, NA_WIDTH).astype(BF16)
    w_glu_bf = w_glu.astype(BF16)
    wr_hi = w_router.astype(BF16)
    wr_lo = (w_router - wr_hi.astype(F32)).astype(BF16)
    wr_pad = jnp.pad(jnp.concatenate([wr_hi, wr_lo], axis=-1), ((0, 0), (0, 0), (0, LANES - 2 * N_EXPERTS)))
    s5_a, s5_bb = _s5_prep(ssm_lambda_re, ssm_lambda_im, ssm_log_dt, ssm_b_re, ssm_b_im)
    s5_c = _s5_c_matrix(ssm_c_re, ssm_c_im)
    rope_tabs = _rope_tables(Ls)

    ctx_mod_row = lambda b: ctx_row + 0 * b
    lat_mod_row = lambda b: b

    xp = x_prompt.reshape(Bc * Lc, D_MODEL)
    xs = x_sample.reshape(Bs * Ls, D_MODEL)
    new_kv, new_ssm = (), []
    for l in range(DEPTH):
        g = norm_g[l]
        dsk = ssm_d[l].reshape(1, SSM_CH)
        sl2 = slice(2 * l, 2 * l + 2)
        qna, kna, vna, qwa, kwa, vwa, u = _inproj(xp, mod, ctx_mod_row, norm_g, w_in_bf, l, Bc, Lc, None, new_kv)
        new_kv = (kna, vna, kwa, vwa)
        ona, owa = _attn_ctx(wa_sink[l], qna, kna, vna, qwa, kwa, vwa, l, Bc, Lc)
        y2, hfin = _s5_scan(u, s5_bb[sl2], s5_a[sl2], s5_c[sl2],
                            jnp.zeros((2, Bc, 2 * SSM_N), F32), Bc, Lc)
        x1, h2, aff = _outproj(ona, owa, y2, u, dsk, w_glu_bf[l], w_out_bf[l], xp, mod[l], ctx_mod_row,
                               g, wr_pad[l], Lc)
        pos_c, post, afft, starts_c = _select(aff, Bc, Lc)
        xg_c, gs_c = _gather(starts_c, h2, post, afft, Bc, Lc)
        x1_c = x1
        new_ssm.append(jnp.transpose(hfin.reshape(2, Bc, 2, SSM_GROUPS, SSM_STATE), (1, 0, 2, 3, 4)))
        qna, kna, vna, qwa, kwa, vwa, u = _inproj(xs, mod, lat_mod_row, norm_g, w_in_bf, l, Bs, Ls, rope_tabs)
        ona = _attn_na(qna, kna, vna, cache_na_k, cache_na_v, _na_bias_pairs(na_rpb[l]), l, Bs, Ls)
        owa = _attn_wa(wa_sink[l], qwa, kwa, vwa, cache_wa_k, cache_wa_v, l, Bs, Ls)
        h0 = jnp.transpose(state_ssm[:, l].reshape(Bs, 2, 2 * SSM_N), (1, 0, 2))
        y2, _ = _s5_scan(u, s5_bb[sl2], s5_a[sl2], s5_c[sl2], h0, Bs, Ls)
        x1, h2, aff = _outproj(ona, owa, y2, u, dsk, w_glu_bf[l], w_out_bf[l], xs, mod[l], lat_mod_row,
                               g, wr_pad[l], Ls)
        pos_s, post, afft, starts_s = _select(aff, Bs, Ls)
        xg_s, gs_s = _gather(starts_s, h2, post, afft, Bs, Ls)
        yy = _ffn(xg_c, gs_c, xg_s, gs_s, w_exp_gate, w_exp_up, w_exp_down, l)
        xp = _combine(starts_c, pos_c, yy, 0, x1_c, mod[l], ctx_mod_row, g, Bc, Lc)
        xs = _combine(starts_s, pos_s, yy, xg_c.shape[1], x1, mod[l], lat_mod_row, g, Bs, Ls)
    return (xp.reshape(Bc, Lc, D_MODEL), xs.reshape(Bs, Ls, D_MODEL),
            new_kv[0].reshape(Bc, DEPTH, Lc, NA_HEADS, HEAD_DIM),
            new_kv[1].reshape(Bc, DEPTH, Lc, NA_HEADS, HEAD_DIM),
            new_kv[2].reshape(Bc, DEPTH, Lc, WA_KV_HEADS, HEAD_DIM),
            new_kv[3].reshape(Bc, DEPTH, Lc, WA_KV_HEADS, HEAD_DIM),
            jnp.stack(new_ssm, axis=1))
```

```python
import functools
import math

import numpy as np
import jax
import jax.numpy as jnp
from jax import lax
from jax.experimental import pallas as pl
from jax.experimental.pallas import tpu as pltpu

F32 = jnp.float32
BF16 = jnp.bfloat16

D_MODEL = 1024
DEPTH = 2
GRID_W = 64
HEAD_DIM = 64
NA_HEADS = 6
NA_KH = 8
NA_KW = 16
WA_HEADS = 6
WA_KV_HEADS = 2
WA_WINDOW = 128
WA_QBLOCK = 256
SSM_CH = 256
SSM_GROUP_CH = 16
SSM_GROUPS = 16
SSM_STATE = 64
SSM_N = SSM_GROUPS * SSM_STATE
NA_WIDTH = NA_HEADS * HEAD_DIM
WA_WIDTH = WA_HEADS * HEAD_DIM
WA_KV_WIDTH = WA_KV_HEADS * HEAD_DIM
IN_COLS = 2048
N_EXPERTS = 16
EC_CAPACITY = 2
EXPERT_FF = 1024
ROPE_BASE = 10000.0
RMS_EPS = 1e-6
NEG_INF = -1e30
ATTN_SCALE = HEAD_DIM ** -0.5
LOG2E = math.log2(math.e)
Q_SCALE = ATTN_SCALE * LOG2E

LANES = 128
SUBLANES = 8
VMEM_LIMIT = 48 * 1024 * 1024

C_NAQ, C_NAK, C_NAV, C_WAQ, C_WAK, C_WAV, C_SSU = 0, 384, 768, 1152, 1536, 1664, 1792
WA_HEAD_ORDER = (0, 3, 1, 4, 2, 5)

NA_QROWS = 4
NA_KROWS = NA_QROWS + NA_KH
NA_REL = 2 * NA_KH - 1
NA_PAIRS = 3 * NA_REL
TOKEN_TILE = 256
OUTPROJ_SUBTILE = 128
PROJ_TILE = 512
SCAN_TILE = 128
SCAN_BATCH = SUBLANES
FFN_ROWS = 512
SEL_CHUNK = 256
SEL_GROUP = LANES // N_EXPERTS
BF16_ROWS = 2 * SUBLANES
COMBINE_WINDOW = 64


def _cparams(sem):
    return pltpu.CompilerParams(dimension_semantics=sem, vmem_limit_bytes=VMEM_LIMIT)


def _sigmoid(x):
    return 1.0 / (1.0 + jnp.exp(-x))


def _rms(x, g):
    ms = jnp.mean(x * x, axis=-1, keepdims=True)
    return x * lax.rsqrt(ms + RMS_EPS) * g


def _dot(a, b):
    return jnp.dot(a, b, preferred_element_type=F32)


def _dot_nt(a, b):
    return lax.dot_general(a, b, (((1,), (1,)), ((), ())), preferred_element_type=F32)


def _mod_body(c_ref, w_ref, b_ref, o_ref):
    c = c_ref[...]
    s = c * _sigmoid(c)
    o_ref[0] = _dot(s.astype(BF16), w_ref[0].astype(BF16)) + b_ref[0]


def _modulation(cond, w_ada, b_ada):
    n = cond.shape[0]
    tn = 1024
    return pl.pallas_call(
        _mod_body,
        grid=(DEPTH, 6 * D_MODEL // tn),
        in_specs=[
            pl.BlockSpec((n, D_MODEL), lambda l, j: (0, 0)),
            pl.BlockSpec((1, D_MODEL, tn), lambda l, j: (l, 0, j)),
            pl.BlockSpec((1, 1, tn), lambda l, j: (l, 0, j)),
        ],
        out_specs=pl.BlockSpec((1, n, tn), lambda l, j: (l, 0, j)),
        out_shape=jax.ShapeDtypeStruct((DEPTH, n, 6 * D_MODEL), F32),
        compiler_params=_cparams(("arbitrary", "arbitrary")),
        name="modulation",
    )(cond, w_ada, b_ada.reshape(DEPTH, 1, 6 * D_MODEL))


def _rope_chunk(x, cos, sin_signed):
    lane = lax.broadcasted_iota(jnp.int32, x.shape, 1)
    first = (lane % 32) < 16
    rot = jnp.where(first, pltpu.roll(x, LANES - 16, 1), pltpu.roll(x, 16, 1))
    return x * cos + rot * sin_signed


INPROJ_KV_OUTPUTS = (1, 2, 4, 5)


def _inproj_body(rope, n_alias, x_ref, mod_ref, g_ref, w_ref, *rest):
    if rope:
        cos_ref, sin_ref = rest[:2]
        rest = rest[2:]
    qna_ref, kna_ref, vna_ref, qwa_ref, kwa_ref, vwa_ref, u_ref = rest[n_alias:]

    def put(ref, val):
        ref[...] = val.reshape(ref.shape)

    x = x_ref[...]
    m = mod_ref[0, 0]
    sh1 = m[:, 0:D_MODEL]
    sc1 = m[:, D_MODEL:2 * D_MODEL]
    h = _rms(x, g_ref[0, 0:1, :]) * (1.0 + sc1) + sh1
    z = _dot(h.astype(BF16), w_ref[0])
    qna_ref[...] = (z[:, C_NAQ:C_NAK] * Q_SCALE).astype(BF16)
    put(kna_ref, z[:, C_NAK:C_NAV])
    put(vna_ref, z[:, C_NAV:C_WAQ])
    put(vwa_ref, z[:, C_WAV:C_SSU])
    u_ref[...] = z[:, C_SSU:IN_COLS]
    if rope:
        cos = cos_ref[...]
        sin = sin_ref[...]
        for j in range(WA_WIDTH // LANES):
            qc = _rope_chunk(z[:, C_WAQ + j * LANES:C_WAQ + (j + 1) * LANES], cos, sin)
            qwa_ref[:, j * LANES:(j + 1) * LANES] = (qc * Q_SCALE).astype(BF16)
        put(kwa_ref, _rope_chunk(z[:, C_WAK:C_WAV], cos, sin))
    else:
        qwa_ref[...] = (z[:, C_WAQ:C_WAK] * Q_SCALE).astype(BF16)
        put(kwa_ref, z[:, C_WAK:C_WAV])


def _inproj(x, mod, mod_row_fn, norm_g, w_bf, l, B, L, rope_tabs, kv_layers=None):
    T = x.shape[0]
    tm = min(PROJ_TILE, L)
    tiles_per_seq = L // tm
    in_specs = [
        pl.BlockSpec((tm, D_MODEL), lambda i: (i, 0)),
        pl.BlockSpec((1, 1, 1, 6 * D_MODEL), lambda i: (l, mod_row_fn(i // tiles_per_seq), 0, 0)),
        pl.BlockSpec((1, 4, D_MODEL), lambda i: (l, 0, 0)),
        pl.BlockSpec((1, D_MODEL, IN_COLS), lambda i: (l, 0, 0)),
    ]
    args = [x, mod, norm_g, w_bf]
    if rope_tabs is not None:
        in_specs += [pl.BlockSpec((tm, LANES), lambda i: (i % tiles_per_seq, 0))] * 2
        args += list(rope_tabs)
    widths = (NA_WIDTH, NA_WIDTH, NA_WIDTH, WA_WIDTH, WA_KV_WIDTH, WA_KV_WIDTH, SSM_CH)
    dtypes = (BF16, F32, F32, BF16, F32, F32, F32)
    out_specs = [pl.BlockSpec((tm, w), lambda i: (i, 0)) for w in widths]
    out_shape = [jax.ShapeDtypeStruct((T, w), dt) for w, dt in zip(widths, dtypes)]
    aliases = {}
    n_alias = 0
    if kv_layers is not None:
        for o in INPROJ_KV_OUTPUTS:
            out_specs[o] = pl.BlockSpec((1, 1, tm, widths[o]),
                                        lambda i: (i // tiles_per_seq, l, i % tiles_per_seq, 0))
            out_shape[o] = jax.ShapeDtypeStruct((B, DEPTH, L, widths[o]), dtypes[o])
        n_alias = len(kv_layers)
        for k, (o, prev) in enumerate(zip(INPROJ_KV_OUTPUTS, kv_layers)):
            aliases[len(args)] = o
            in_specs.append(pl.BlockSpec(memory_space=pl.ANY))
            args.append(prev)
    return pl.pallas_call(
        functools.partial(_inproj_body, rope_tabs is not None, n_alias),
        grid=(T // tm,),
        in_specs=in_specs,
        out_specs=out_specs,
        out_shape=out_shape,
        input_output_aliases=aliases,
        compiler_params=_cparams(("arbitrary",)),
        name="inproj",
    )(*args)


def _lane_lo(shape):
    return lax.broadcasted_iota(jnp.int32, shape, len(shape) - 1) < HEAD_DIM


def _keep_half(q, lo, half):
    keep = jnp.where(lo, 1.0 - half, 0.0 + half).astype(q.dtype)
    return q * keep


def _values_and_ones(v, lo, half):
    one = jnp.ones_like(v)
    return (jnp.where(lo, v, one) if half == 0 else jnp.where(lo, one, v)).astype(BF16)


def _normalise(o2, extra=None):
    den = pltpu.roll(o2, HEAD_DIM, 1)
    if extra is not None:
        den = den + extra
    return o2 / den


def _attn_ctx_body(sink_ref, qna, kna, vna, qwa, kwa, vwa, ona, owa):
    L = qna.shape[1]
    lo = _lane_lo((L, LANES))
    for j in range(NA_WIDTH // LANES):
        sl = slice(j * LANES, (j + 1) * LANES)
        q2 = qna[0, :, sl]
        k2 = kna[0, 0, :, sl].astype(BF16)
        v2 = vna[0, 0, :, sl].astype(BF16)
        halves = []
        for half in range(2):
            s = _dot_nt(_keep_half(q2, lo, half), k2)
            p = jnp.exp2(s - jnp.max(s, axis=-1, keepdims=True))
            halves.append(_dot(p.astype(BF16), v2) / jnp.sum(p, axis=-1, keepdims=True))
        ona[0, :, sl] = jnp.where(lo, halves[0], halves[1]).astype(BF16)
    kw = kwa[0, 0].astype(BF16)
    vw = vwa[0, 0].astype(BF16)
    for j in range(WA_WIDTH // LANES):
        sl = slice(j * LANES, (j + 1) * LANES)
        q2 = qwa[0, :, sl]
        halves = []
        for half in range(2):
            s = _dot_nt(_keep_half(q2, lo, half), kw)
            sk = sink_ref[WA_HEAD_ORDER[2 * j + half]] * LOG2E
            m = jnp.maximum(jnp.max(s, axis=-1, keepdims=True), sk)
            p = jnp.exp2(s - m)
            l = jnp.sum(p, axis=-1, keepdims=True) + jnp.exp2(sk - m)
            halves.append(_dot(p.astype(BF16), vw) / l)
        owa[0, :, sl] = jnp.where(lo, halves[0], halves[1]).astype(BF16)


def _attn_ctx(sink, qna, kna, vna, qwa, kwa, vwa, l, B, L):
    def spec(w):
        return pl.BlockSpec((1, L, w), lambda b: (b, 0, 0))

    def layer_spec(w):
        return pl.BlockSpec((1, 1, L, w), lambda b: (b, l, 0, 0))

    r3 = lambda a: a.reshape(B, L, a.shape[-1])
    ona, owa = pl.pallas_call(
        _attn_ctx_body,
        grid=(B,),
        in_specs=[pl.BlockSpec(memory_space=pltpu.SMEM),
                  spec(NA_WIDTH), layer_spec(NA_WIDTH), layer_spec(NA_WIDTH),
                  spec(WA_WIDTH), layer_spec(WA_KV_WIDTH), layer_spec(WA_KV_WIDTH)],
        out_specs=[spec(NA_WIDTH), spec(WA_WIDTH)],
        out_shape=[jax.ShapeDtypeStruct((B, L, NA_WIDTH), BF16),
                   jax.ShapeDtypeStruct((B, L, WA_WIDTH), BF16)],
        compiler_params=_cparams(("arbitrary",)),
        name="attn_ctx",
    )(sink, r3(qna), kna, vna, r3(qwa), kwa, vwa)
    return ona.reshape(B * L, NA_WIDTH), owa.reshape(B * L, WA_WIDTH)


def _na_key_start(rb, rows):
    return jnp.clip(rb * NA_QROWS - NA_KH // 2, 0, rows - NA_KROWS)


def _attn_na_body(rows, idx_ref, q_ref, k_ref, v_ref, kc_ref, vc_ref, pairs_ref, o_ref):
    rb = pl.program_id(1)
    nq = NA_QROWS * GRID_W
    nk = NA_KROWS * GRID_W
    start = pl.multiple_of(_na_key_start(rb, rows) * GRID_W, GRID_W)
    lo = _lane_lo((nq, LANES))
    lo_v = _lane_lo((1, LANES))
    npair = NA_KROWS // 2
    pat = jnp.where(rb == 0, 0, jnp.where(rb == rows // NA_QROWS - 1, 2, 1))
    entry = [[idx_ref[(pat * NA_QROWS + i) * npair + jj] for jj in range(npair)] for i in range(NA_QROWS)]

    def bias(head):
        return jnp.concatenate(
            [jnp.concatenate([pairs_ref[head, entry[i][jj]] for jj in range(npair)], axis=1)
             for i in range(NA_QROWS)], axis=0)

    for j in range(NA_WIDTH // LANES):
        sl = slice(j * LANES, (j + 1) * LANES)
        q2 = q_ref[0, :, sl]
        kl = k_ref[0, pl.ds(start, nk), sl].astype(BF16)
        vl = v_ref[0, pl.ds(start, nk), sl]
        kc = kc_ref[0, 0, :, sl].astype(BF16)
        vc = vc_ref[0, 0, :, sl]
        scores = []
        for half in range(2):
            qm = _keep_half(q2, lo, half)
            scores.append((_dot_nt(qm, kl) + bias(2 * j + half), _dot_nt(qm, kc)))
        probs = []
        for s_loc, s_ctx in scores:
            m = jnp.maximum(jnp.max(s_loc, axis=-1, keepdims=True),
                            jnp.max(s_ctx, axis=-1, keepdims=True))
            probs.append((jnp.exp2(s_loc - m).astype(BF16), jnp.exp2(s_ctx - m).astype(BF16)))
        halves = []
        for half, (p_loc, p_ctx) in enumerate(probs):
            o2 = (_dot(p_loc, _values_and_ones(vl, lo_v, half))
                  + _dot(p_ctx, _values_and_ones(vc, lo_v, half)))
            halves.append(_normalise(o2))
        o_ref[0, :, sl] = jnp.where(lo, halves[0], halves[1]).astype(BF16)


def _na_bias_pairs(rpb_l):
    cq = np.arange(GRID_W)
    col_start = np.clip(cq - NA_KW // 2, 0, GRID_W - NA_KW)
    col_ok = (cq[None, :] >= col_start[:, None]) & (cq[None, :] < col_start[:, None] + NA_KW)
    dcol = np.clip(cq[None, :] - cq[:, None], -(NA_KW - 1), NA_KW - 1) + (NA_KW - 1)
    pick = (dcol[None] == np.arange(2 * NA_KW - 1)[:, None, None]).astype(np.float32)
    tiles = jnp.einsum('hrd,dqc->hrqc', rpb_l.astype(F32) * LOG2E, pick, precision=lax.Precision.HIGHEST)
    tiles = jnp.where(col_ok[None, None], tiles, NEG_INF)
    masked = jnp.full_like(tiles, NEG_INF)
    both = jnp.concatenate([tiles[:, :-1], tiles[:, 1:]], axis=-1)
    right = jnp.concatenate([masked, tiles], axis=-1)
    left = jnp.concatenate([tiles, masked], axis=-1)
    none = jnp.concatenate([masked[:, :1], masked[:, :1]], axis=-1)
    return jnp.concatenate([both, right, left, none], axis=1)


def _na_pair_index(rows):
    nrb = rows // NA_QROWS
    idx = np.zeros((3, NA_QROWS, NA_KROWS // 2), np.int32)
    for pi, rb in enumerate((0, 1, nrb - 1)):
        ks = int(np.clip(rb * NA_QROWS - NA_KH // 2, 0, rows - NA_KROWS))
        for i in range(NA_QROWS):
            qr = rb * NA_QROWS + i
            rs = int(np.clip(qr - NA_KH // 2, 0, rows - NA_KH))
            for jj in range(NA_KROWS // 2):
                kr = ks + 2 * jj
                r = kr - qr + NA_KH - 1
                ok_l = rs <= kr < rs + NA_KH
                ok_r = rs <= kr + 1 < rs + NA_KH
                if ok_l and ok_r:
                    idx[pi, i, jj] = r
                elif ok_r:
                    idx[pi, i, jj] = (NA_REL - 1) + (r + 1)
                elif ok_l:
                    idx[pi, i, jj] = (NA_REL - 1) + NA_REL + r
                else:
                    idx[pi, i, jj] = NA_PAIRS - 1
    return jnp.asarray(idx.reshape(-1))


def _attn_na(q, k, v, cache_k, cache_v, pairs, l, B, L):
    rows = L // GRID_W
    nrb = rows // NA_QROWS
    nq = NA_QROWS * GRID_W
    P = cache_k.shape[2]
    r3 = lambda a: a.reshape(B, L, a.shape[-1])
    ck = cache_k.reshape(B, DEPTH, P, NA_WIDTH)
    cv = cache_v.reshape(B, DEPTH, P, NA_WIDTH)
    grid_spec = pltpu.PrefetchScalarGridSpec(
        num_scalar_prefetch=1,
        grid=(B, nrb),
        in_specs=[
            pl.BlockSpec((1, nq, NA_WIDTH), lambda b, r, s: (b, r, 0)),
            pl.BlockSpec((1, L, NA_WIDTH), lambda b, r, s: (b, 0, 0)),
            pl.BlockSpec((1, L, NA_WIDTH), lambda b, r, s: (b, 0, 0)),
            pl.BlockSpec((1, 1, P, NA_WIDTH), lambda b, r, s: (b, l, 0, 0)),
            pl.BlockSpec((1, 1, P, NA_WIDTH), lambda b, r, s: (b, l, 0, 0)),
            pl.BlockSpec(pairs.shape, lambda b, r, s: (0, 0, 0, 0)),
        ],
        out_specs=pl.BlockSpec((1, nq, NA_WIDTH), lambda b, r, s: (b, r, 0)),
    )
    o = pl.pallas_call(
        functools.partial(_attn_na_body, rows),
        grid_spec=grid_spec,
        out_shape=jax.ShapeDtypeStruct((B, L, NA_WIDTH), BF16),
        compiler_params=_cparams(("arbitrary", "arbitrary")),
        name="attn_na",
    )(_na_pair_index(rows), r3(q), r3(k), r3(v), ck, cv, pairs)
    return o.reshape(B * L, NA_WIDTH)


def _attn_wa_body(L, sink_ref, q_ref, k_ref, v_ref, kc_ref, vc_ref, o_ref):
    n = pl.program_id(1)
    nk = WA_QBLOCK + 2 * WA_WINDOW
    npair = WA_WIDTH // LANES
    start = pl.multiple_of(jnp.clip(n * WA_QBLOCK - WA_WINDOW, 0, L - nk), WA_WINDOW)
    kl = k_ref[0, pl.ds(start, nk), :].astype(BF16)
    vl = v_ref[0, pl.ds(start, nk), :]
    kc = kc_ref[0, 0].astype(BF16)
    vc = vc_ref[0, 0]
    lo_v = _lane_lo((1, LANES))
    rows = npair * WA_QBLOCK
    qpos = n * WA_QBLOCK + lax.broadcasted_iota(jnp.int32, (rows, nk), 0) % WA_QBLOCK
    kpos = start + lax.broadcasted_iota(jnp.int32, (rows, nk), 1)
    in_win = jnp.abs(kpos - qpos) <= WA_WINDOW
    lo = _lane_lo((rows, LANES))
    q_all = jnp.concatenate([q_ref[0, :, j * LANES:(j + 1) * LANES] for j in range(npair)], axis=0)
    blk = lax.broadcasted_iota(jnp.int32, (rows, 1), 0) // WA_QBLOCK
    scores = []
    for g in range(WA_KV_HEADS):
        qm = _keep_half(q_all, lo, g)
        scores.append((jnp.where(in_win, _dot_nt(qm, kl), NEG_INF), _dot_nt(qm, kc)))
    probs = []
    for g in range(WA_KV_HEADS):
        s_loc, s_ctx = scores[g]
        sk = jnp.zeros((rows, 1), F32)
        for j in range(npair):
            sk = jnp.where(blk == j, sink_ref[WA_HEAD_ORDER[2 * j + g]] * LOG2E, sk)
        m = jnp.maximum(jnp.maximum(jnp.max(s_loc, axis=-1, keepdims=True),
                                    jnp.max(s_ctx, axis=-1, keepdims=True)), sk)
        probs.append((jnp.exp2(s_loc - m).astype(BF16), jnp.exp2(s_ctx - m).astype(BF16), jnp.exp2(sk - m)))
    outs = []
    for g in range(WA_KV_HEADS):
        p_loc, p_ctx, p_sink = probs[g]
        o2 = _dot(p_loc, _values_and_ones(vl, lo_v, g)) + _dot(p_ctx, _values_and_ones(vc, lo_v, g))
        outs.append(_normalise(o2, p_sink))
    o_all = jnp.where(lo, outs[0], outs[1]).astype(BF16)
    for j in range(npair):
        o_ref[0, :, j * LANES:(j + 1) * LANES] = o_all[j * WA_QBLOCK:(j + 1) * WA_QBLOCK]


def _attn_wa(sink, q, k, v, cache_k, cache_v, l, B, L):
    nb = L // WA_QBLOCK
    P = cache_k.shape[2]
    r3 = lambda a: a.reshape(B, L, a.shape[-1])
    ck = cache_k.reshape(B, DEPTH, P, WA_KV_WIDTH)
    cv = cache_v.reshape(B, DEPTH, P, WA_KV_WIDTH)
    o = pl.pallas_call(
        functools.partial(_attn_wa_body, L),
        grid=(B, nb),
        in_specs=[
            pl.BlockSpec(memory_space=pltpu.SMEM),
            pl.BlockSpec((1, WA_QBLOCK, WA_WIDTH), lambda b, n: (b, n, 0)),
            pl.BlockSpec((1, L, WA_KV_WIDTH), lambda b, n: (b, 0, 0)),
            pl.BlockSpec((1, L, WA_KV_WIDTH), lambda b, n: (b, 0, 0)),
            pl.BlockSpec((1, 1, P, WA_KV_WIDTH), lambda b, n: (b, l, 0, 0)),
            pl.BlockSpec((1, 1, P, WA_KV_WIDTH), lambda b, n: (b, l, 0, 0)),
        ],
        out_specs=pl.BlockSpec((1, WA_QBLOCK, WA_WIDTH), lambda b, n: (b, n, 0)),
        out_shape=jax.ShapeDtypeStruct((B, L, WA_WIDTH), BF16),
        compiler_params=_cparams(("arbitrary", "arbitrary")),
        name="attn_wa",
    )(sink, r3(q), r3(k), r3(v), ck, cv)
    return o.reshape(B * L, WA_WIDTH)


def _s5_prep_body(lr_ref, li_ref, ldt_ref, br_ref, bi_ref, a_ref, bb_ref):
    lr = lr_ref[0]
    li = li_ref[0]
    dt = jnp.exp(ldt_ref[0])
    mag = jnp.exp(lr * dt)
    ar = mag * jnp.cos(li * dt)
    ai = mag * jnp.sin(li * dt)
    den = lr * lr + li * li
    nr = ar - 1.0
    fr = (nr * lr + ai * li) / den
    fi = (ai * lr - nr * li) / den
    a_ref[0, :, 0:SSM_N] = ar
    a_ref[0, :, SSM_N:2 * SSM_N] = ai
    br = br_ref[0]
    bi = bi_ref[0]
    bb_ref[0, :, 0:SSM_N] = (fr * br - fi * bi).astype(BF16)
    bb_ref[0, :, SSM_N:2 * SSM_N] = (fr * bi + fi * br).astype(BF16)


def _s5_prep(lam_re, lam_im, log_dt, b_re, b_im):
    n = DEPTH * 2
    eye = jnp.eye(SSM_GROUPS, dtype=F32)

    def blockdiag_b(b):
        return jnp.einsum('ngph,gk->nghkp', b.reshape(n, SSM_GROUPS, SSM_STATE, SSM_GROUP_CH), eye
                          ).reshape(n, SSM_CH, SSM_N)

    lr = lam_re.reshape(n, 1, SSM_N)
    li = lam_im.reshape(n, 1, SSM_N)
    ldt = jnp.repeat(log_dt.reshape(n, SSM_GROUPS), SSM_STATE, axis=-1).reshape(n, 1, SSM_N)
    vec = pl.BlockSpec((1, 1, SSM_N), lambda i: (i, 0, 0))
    mat = pl.BlockSpec((1, SSM_CH, SSM_N), lambda i: (i, 0, 0))
    return pl.pallas_call(
        _s5_prep_body,
        grid=(n,),
        in_specs=[vec, vec, vec, mat, mat],
        out_specs=[pl.BlockSpec((1, 1, 2 * SSM_N), lambda i: (i, 0, 0)),
                   pl.BlockSpec((1, SSM_CH, 2 * SSM_N), lambda i: (i, 0, 0))],
        out_shape=[jax.ShapeDtypeStruct((n, 1, 2 * SSM_N), F32),
                   jax.ShapeDtypeStruct((n, SSM_CH, 2 * SSM_N), BF16)],
        compiler_params=_cparams(("arbitrary",)),
        name="s5_prep",
    )(lr, li, ldt, blockdiag_b(b_re), blockdiag_b(b_im))


def _s5_c_matrix(c_re, c_im):
    n = DEPTH * 2
    eye = jnp.eye(SSM_GROUPS, dtype=F32)

    def blk(c):
        return jnp.einsum('nghp,gk->ngpkh', c.reshape(n, SSM_GROUPS, SSM_GROUP_CH, SSM_STATE), eye
                          ).reshape(n, SSM_N, SSM_CH)

    return jnp.concatenate([blk(c_re), -blk(c_im)], axis=1).astype(BF16)


def _s5_scan_body(tl, u_ref, bb_ref, a_ref, c_ref, h0_ref, y_ref, hfin_ref, bu, hb, hst):
    d = pl.program_id(0)
    i = pl.program_id(2)
    sb = SCAN_BATCH
    re = slice(0, SSM_N)
    im = slice(SSM_N, 2 * SSM_N)

    @pl.when(i == 0)
    def _():
        hst[...] = h0_ref[0]

    u_tb = jnp.transpose(u_ref[...], (1, 0, 2)).reshape(tl * sb, SSM_CH)
    u_bf = u_tb.astype(BF16)
    half = tl * sb // 2
    bu[0:half, :] = _dot(u_bf[0:half], bb_ref[0])
    bu[half:, :] = _dot(u_bf[half:], bb_ref[0])
    a = a_ref[0]
    ar = jnp.broadcast_to(a[:, re], (sb, SSM_N))
    ai = jnp.broadcast_to(a[:, im], (sb, SSM_N))

    def scan(reverse):
        def pair(s, carry):
            hr, hi = carry
            t0 = tl - 1 - 2 * s if reverse else 2 * s
            t1 = t0 - 1 if reverse else t0 + 1
            r0 = pl.ds(pl.multiple_of(t0 * sb, sb), sb)
            r1 = pl.ds(pl.multiple_of(t1 * sb, sb), sb)
            h0r = ar * hr - ai * hi + bu[r0, re]
            h0i = ar * hi + ai * hr + bu[r0, im]
            h1r = ar * h0r - ai * h0i + bu[r1, re]
            h1i = ar * h0i + ai * h0r + bu[r1, im]
            rows = pl.ds(pl.multiple_of((t1 if reverse else t0) * sb, 2 * sb), 2 * sb)
            early, late = ((h1r, h1i), (h0r, h0i)) if reverse else ((h0r, h0i), (h1r, h1i))
            hb[rows, re] = jnp.concatenate([early[0], late[0]], axis=0).astype(BF16)
            hb[rows, im] = jnp.concatenate([early[1], late[1]], axis=0).astype(BF16)
            return h1r, h1i

        hr, hi = lax.fori_loop(0, tl // 2, pair, (hst[:, re], hst[:, im]))
        hst[:, re] = hr
        hst[:, im] = hi
        hfin_ref[0, :, re] = hr
        hfin_ref[0, :, im] = hi

    @pl.when(d == 0)
    def _():
        scan(False)

    @pl.when(d == 1)
    def _():
        scan(True)

    y_tb = jnp.concatenate([_dot(hb[0:half, :], c_ref[0]), _dot(hb[half:, :], c_ref[0])], axis=0)
    y_ref[0] = jnp.transpose(y_tb.reshape(tl, sb, SSM_CH), (1, 0, 2))


def _s5_scan(u, bb, a, cm, h0, B, L):
    tl = SCAN_TILE
    nt = L // tl
    tile = lambda d, i: i + d * (nt - 1 - 2 * i)
    return pl.pallas_call(
        functools.partial(_s5_scan_body, tl),
        grid=(2, B // SCAN_BATCH, nt),
        in_specs=[
            pl.BlockSpec((SCAN_BATCH, tl, SSM_CH), lambda d, b, i: (b, tile(d, i), 0)),
            pl.BlockSpec((1, SSM_CH, 2 * SSM_N), lambda d, b, i: (d, 0, 0)),
            pl.BlockSpec((1, 1, 2 * SSM_N), lambda d, b, i: (d, 0, 0)),
            pl.BlockSpec((1, 2 * SSM_N, SSM_CH), lambda d, b, i: (d, 0, 0)),
            pl.BlockSpec((1, SCAN_BATCH, 2 * SSM_N), lambda d, b, i: (d, b, 0)),
        ],
        out_specs=[
            pl.BlockSpec((1, SCAN_BATCH, tl, SSM_CH), lambda d, b, i: (d, b, tile(d, i), 0)),
            pl.BlockSpec((1, SCAN_BATCH, 2 * SSM_N), lambda d, b, i: (d, b, 0)),
        ],
        out_shape=[jax.ShapeDtypeStruct((2, B, L, SSM_CH), F32),
                   jax.ShapeDtypeStruct((2, B, 2 * SSM_N), F32)],
        scratch_shapes=[pltpu.VMEM((SCAN_BATCH * tl, 2 * SSM_N), F32),
                        pltpu.VMEM((SCAN_BATCH * tl, 2 * SSM_N), BF16),
                        pltpu.VMEM((SCAN_BATCH, 2 * SSM_N), F32)],
        compiler_params=_cparams(("arbitrary", "arbitrary", "arbitrary")),
        name="s5_scan",
    )(u.reshape(B, L, SSM_CH), bb, a, cm, h0)


def _outproj_body(ona_ref, owa_ref, y_ref, u_ref, dsk_ref, wglu_ref, wout_ref, x_ref, mod_ref, g_ref,
                  wr_ref, x1_ref, h2_ref, aff_ref):
    m = mod_ref[0]
    g1 = m[:, 2 * D_MODEL:3 * D_MODEL]
    sh2 = m[:, 3 * D_MODEL:4 * D_MODEL]
    sc2 = m[:, 4 * D_MODEL:5 * D_MODEL]
    tm = x_ref.shape[0]
    sub = min(tm, OUTPROJ_SUBTILE)
    parts = [slice(r, r + sub) for r in range(0, tm, sub)]
    mixes = []
    for rs in parts:
        y = dsk_ref[...] * u_ref[rs, :] + y_ref[0, rs, :] + y_ref[1, rs, :]
        zg = _dot(y.astype(BF16), wglu_ref[...])
        oss = zg[:, 0:SSM_CH] * _sigmoid(zg[:, SSM_CH:2 * SSM_CH])
        mixes.append(_dot(ona_ref[rs, :], wout_ref[0:NA_WIDTH, :])
                     + _dot(owa_ref[rs, :], wout_ref[NA_WIDTH:NA_WIDTH + WA_WIDTH, :])
                     + _dot(oss.astype(BF16), wout_ref[NA_WIDTH + WA_WIDTH:D_MODEL, :]))
    split = []
    for rs, mix in zip(parts, mixes):
        x1 = x_ref[rs, :] + g1 * _rms(mix, g_ref[1:2, :])
        x1_ref[rs, :] = x1
        h2 = _rms(x1, g_ref[2:3, :]) * (1.0 + sc2) + sh2
        h2_hi = h2.astype(BF16)
        h2_ref[rs, :] = h2_hi
        split.append((h2_hi, (h2 - h2_hi.astype(F32)).astype(BF16)))
    for rs, (h2_hi, h2_lo) in zip(parts, split):
        r_hi = _dot(h2_hi, wr_ref[...])
        logits = r_hi + pltpu.roll(r_hi, LANES - N_EXPERTS, 1) + _dot(h2_lo, wr_ref[...])
        lane = lax.broadcasted_iota(jnp.int32, logits.shape, 1)
        logits = jnp.where(lane < N_EXPERTS, logits, NEG_INF)
        mx = jnp.max(logits, axis=-1, keepdims=True)
        p = jnp.exp(logits - mx)
        aff_ref[rs, :] = p / jnp.sum(p, axis=-1, keepdims=True)


def _outproj(ona, owa, y2, u, dsk, wglu_bf, wout_bf, x, mod_l, mod_row_fn, g, wr_pad, L):
    T = x.shape[0]
    tm = min(PROJ_TILE, L)
    tiles_per_seq = L // tm
    row = lambda w: pl.BlockSpec((tm, w), lambda i: (i, 0))
    full = lambda a: pl.BlockSpec(a.shape, lambda i: (0,) * a.ndim)
    return pl.pallas_call(
        _outproj_body,
        grid=(T // tm,),
        in_specs=[row(NA_WIDTH), row(WA_WIDTH),
                  pl.BlockSpec((2, tm, SSM_CH), lambda i: (0, i, 0)),
                  row(SSM_CH), full(dsk), full(wglu_bf), full(wout_bf), row(D_MODEL),
                  pl.BlockSpec((1, 1, 6 * D_MODEL), lambda i: (mod_row_fn(i // tiles_per_seq), 0, 0)),
                  full(g), full(wr_pad)],
        out_specs=[row(D_MODEL), row(D_MODEL), row(LANES)],
        out_shape=[jax.ShapeDtypeStruct((T, D_MODEL), F32),
                   jax.ShapeDtypeStruct((T, D_MODEL), BF16),
                   jax.ShapeDtypeStruct((T, LANES), F32)],
        compiler_params=_cparams(("arbitrary",)),
        name="outproj",
    )(ona, owa, y2.reshape(2, T, SSM_CH), u, dsk, wglu_bf, wout_bf, x, mod_l, g, wr_pad)


def _select_body(cap, aff_ref, pos_ref, post_ref, afft_ref, starts_ref):
    L = aff_ref.shape[1]
    aff = aff_ref[0]
    for r in range(1, SEL_GROUP):
        aff = aff + pltpu.roll(aff_ref[r], r * N_EXPERTS, 1)
    capf = jnp.float32(cap)

    def bisect(k, thr):
        cand = thr | jnp.left_shift(jnp.int32(1), 30 - k)
        cnt = jnp.sum(jnp.where(aff >= pltpu.bitcast(cand, F32), 1.0, 0.0), axis=0, keepdims=True)
        return jnp.where(cnt >= capf, cand, thr)

    thr = lax.fori_loop(0, 31, bisect, jnp.zeros((1, LANES), jnp.int32))
    kth = pltpu.bitcast(thr, F32)
    gt = jnp.where(aff > kth, 1.0, 0.0)
    eq = jnp.where(aff == kth, 1.0, 0.0)
    need = capf - jnp.sum(gt, axis=0, keepdims=True)
    ck = SEL_CHUNK
    tri = (lax.broadcasted_iota(jnp.int32, (ck, ck), 0)
           >= lax.broadcasted_iota(jnp.int32, (ck, ck), 1))
    tri = jnp.where(tri, 1.0, 0.0).astype(BF16)

    def prefix(mask):
        outs = []
        carries = [jnp.zeros((1, LANES), F32)]
        for j in range(L // ck):
            c = _dot(tri, mask[j * ck:(j + 1) * ck].astype(BF16)) + carries[-1]
            outs.append(c)
            carries.append(c[ck - 1:ck, :])
        return jnp.concatenate(outs, axis=0), jnp.concatenate(carries, axis=0)

    tie = jnp.where(prefix(eq)[0] <= need, eq, 0.0)
    sel = gt + tie
    rank, starts = prefix(sel)
    pos = jnp.where(sel > 0.0, rank - 1.0, -1.0)
    pos_ref[0] = pos
    post_ref[0] = jnp.transpose(pos)
    afft_ref[0] = jnp.transpose(aff)
    starts_ref[0] = starts.astype(jnp.int32)


def _select(aff, B, L):
    cap = EC_CAPACITY * L // N_EXPERTS
    ng = B // SEL_GROUP
    return pl.pallas_call(
        functools.partial(_select_body, cap),
        grid=(ng,),
        in_specs=[pl.BlockSpec((SEL_GROUP, L, LANES), lambda i: (i, 0, 0))],
        out_specs=[pl.BlockSpec((1, L, LANES), lambda i: (i, 0, 0)),
                   pl.BlockSpec((1, LANES, L), lambda i: (i, 0, 0)),
                   pl.BlockSpec((1, LANES, L), lambda i: (i, 0, 0)),
                   pl.BlockSpec((1, L // SEL_CHUNK + 1, LANES), lambda i: (i, 0, 0))],
        out_shape=[jax.ShapeDtypeStruct((ng, L, LANES), F32),
                   jax.ShapeDtypeStruct((ng, LANES, L), F32),
                   jax.ShapeDtypeStruct((ng, LANES, L), F32),
                   jax.ShapeDtypeStruct((ng, L // SEL_CHUNK + 1, LANES), jnp.int32)],
        compiler_params=_cparams(("arbitrary",)),
        name="ec_select",
    )(aff.reshape(B, L, LANES))


def _window_rows(base, w):
    return pl.ds(base, w) if isinstance(base, int) else pl.ds(pl.multiple_of(base, BF16_ROWS), w)


def _gather_body(cap, win, starts_ref, h_ref, post_ref, afft_ref, xs_ref, gs_ref):
    b = pl.program_id(0)
    j = pl.program_id(1)
    grp = b // SEL_GROUP
    lane0 = (b % SEL_GROUP) * N_EXPERTS

    @pl.when(j == 0)
    def _():
        xs_ref[...] = jnp.zeros_like(xs_ref)
        gs_ref[...] = jnp.zeros_like(gs_ref)

    h = h_ref[0]
    ck = h.shape[0]

    def run(w, bases):
        slot = lax.broadcasted_iota(jnp.int32, (w, ck), 0)
        onehots = []
        for e in range(N_EXPERTS):
            rows = _window_rows(bases[e], w)
            hit = post_ref[0, e:e + 1, :] == (slot + bases[e]).astype(F32)
            gs_ref[e, 0, rows, :] += jnp.sum(jnp.where(hit, afft_ref[0, e:e + 1, :], 0.0), axis=1, keepdims=True)
            onehots.append(jnp.where(hit, 1.0, 0.0).astype(BF16))
        xs = _dot(jnp.concatenate(onehots, axis=0), h).astype(BF16)
        for e in range(N_EXPERTS):
            rows = _window_rows(bases[e], w)
            xs_ref[e, 0, rows, :] += xs[e * w:(e + 1) * w]

    if win == cap:
        run(cap, [0] * N_EXPERTS)
        return
    bases = []
    fits = None
    for e in range(N_EXPERTS):
        s0 = starts_ref[grp, j, lane0 + e]
        s1 = starts_ref[grp, j + 1, lane0 + e]
        base = jnp.minimum((s0 // BF16_ROWS) * BF16_ROWS, cap - win)
        bases.append(base)
        ok = s1 - base <= win
        fits = ok if fits is None else jnp.logical_and(fits, ok)

    @pl.when(fits)
    def _():
        run(win, bases)

    @pl.when(jnp.logical_not(fits))
    def _():
        run(cap, [0] * N_EXPERTS)


def _gather(starts, h2, post, afft, B, L):
    cap = EC_CAPACITY * L // N_EXPERTS
    ck = SEL_CHUNK
    win = min(cap, COMBINE_WINDOW)
    lane_blk = lambda b, j, s: (b // SEL_GROUP, b % SEL_GROUP, j)
    grid_spec = pltpu.PrefetchScalarGridSpec(
        num_scalar_prefetch=1,
        grid=(B, L // ck),
        in_specs=[pl.BlockSpec((1, ck, D_MODEL), lambda b, j, s: (b, j, 0)),
                  pl.BlockSpec((1, N_EXPERTS, ck), lane_blk),
                  pl.BlockSpec((1, N_EXPERTS, ck), lane_blk)],
        out_specs=[pl.BlockSpec((N_EXPERTS, 1, cap, D_MODEL), lambda b, j, s: (0, b, 0, 0)),
                   pl.BlockSpec((N_EXPERTS, 1, cap, 1), lambda b, j, s: (0, b, 0, 0))],
    )
    xs, gs = pl.pallas_call(
        functools.partial(_gather_body, cap, win),
        grid_spec=grid_spec,
        out_shape=[jax.ShapeDtypeStruct((N_EXPERTS, B, cap, D_MODEL), BF16),
                   jax.ShapeDtypeStruct((N_EXPERTS, B, cap, 1), F32)],
        compiler_params=_cparams(("arbitrary", "arbitrary")),
        name="ec_gather",
    )(starts, h2.reshape(B, L, D_MODEL), post, afft)
    return xs.reshape(N_EXPERTS, B * cap, D_MODEL), gs.reshape(N_EXPERTS, B * cap, 1)


def _ffn_body(n_first, xa_ref, ga_ref, xb_ref, gb_ref, wg_ref, wu_ref, wd_ref, y_ref, wgb, wub, wdb):
    j = pl.program_id(1)

    @pl.when(j == 0)
    def _():
        wgb[...] = wg_ref[0, 0].astype(BF16)
        wub[...] = wu_ref[0, 0].astype(BF16)
        wdb[...] = wd_ref[0, 0].astype(BF16)

    def run(xs_ref, gs_ref):
        x = xs_ref[0]
        a = _dot(x, wgb[...])
        u = _dot(x, wub[...])
        hm = (a * _sigmoid(a) * u).astype(BF16)
        y_ref[0] = (_dot(hm, wdb[...]) * gs_ref[0]).astype(BF16)

    @pl.when(j < n_first)
    def _():
        run(xa_ref, ga_ref)

    @pl.when(j >= n_first)
    def _():
        run(xb_ref, gb_ref)


def _ffn(xa, ga, xb, gb, wg, wu, wd, l):
    rc = FFN_ROWS
    na = xa.shape[1] // rc
    nb = xb.shape[1] // rc
    first = lambda w: pl.BlockSpec((1, rc, w), lambda e, j: (e, jnp.minimum(j, na - 1), 0))
    second = lambda w: pl.BlockSpec((1, rc, w), lambda e, j: (e, jnp.maximum(j - na, 0), 0))
    wspec = lambda a, b: pl.BlockSpec((1, 1, a, b), lambda e, j: (l, e, 0, 0))
    return pl.pallas_call(
        functools.partial(_ffn_body, na),
        grid=(N_EXPERTS, na + nb),
        in_specs=[first(D_MODEL), first(1), second(D_MODEL), second(1),
                  wspec(D_MODEL, EXPERT_FF), wspec(D_MODEL, EXPERT_FF), wspec(EXPERT_FF, D_MODEL)],
        out_specs=pl.BlockSpec((1, rc, D_MODEL), lambda e, j: (e, j, 0)),
        out_shape=jax.ShapeDtypeStruct((N_EXPERTS, (na + nb) * rc, D_MODEL), BF16),
        scratch_shapes=[pltpu.VMEM((D_MODEL, EXPERT_FF), BF16),
                        pltpu.VMEM((D_MODEL, EXPERT_FF), BF16),
                        pltpu.VMEM((EXPERT_FF, D_MODEL), BF16)],
        compiler_params=_cparams(("arbitrary", "arbitrary")),
        name="ec_ffn",
    )(xa, ga, xb, gb, wg, wu, wd)


def _combine_body(cap, win, starts_ref, pos_ref, y_ref, x1_ref, mod_ref, g_ref, o_ref):
    b = pl.program_id(0)
    i = pl.program_id(1)
    grp = b // SEL_GROUP
    lane0 = (b % SEL_GROUP) * N_EXPERTS
    pos = pos_ref[0].astype(BF16)
    g2 = mod_ref[0][:, 5 * D_MODEL:6 * D_MODEL]

    def finish(w, bases, values):
        n = N_EXPERTS * w
        col = lax.broadcasted_iota(jnp.int32, (1, n), 1)
        expert = jnp.zeros((1, n), jnp.int32)
        for e in range(1, N_EXPERTS):
            expert = expert + jnp.where(col >= e * w, 1, 0)
        target = col - expert * w
        if bases is not None:
            for e in range(N_EXPERTS):
                target = jnp.where(expert == e, target + bases[e], target)
        expand = jnp.where(lax.broadcasted_iota(jnp.int32, (LANES, n), 0) == expert + lane0, 1.0, 0.0)
        per_col = _dot(pos, expand.astype(BF16))
        onehot = jnp.where(per_col == target.astype(F32), 1.0, 0.0).astype(BF16)
        f = _dot(onehot, values)
        o_ref[...] = x1_ref[...] + g2 * _rms(f, g_ref[3:4, :])

    if win == cap:
        finish(cap, None, y_ref[...].reshape(N_EXPERTS * cap, D_MODEL))
        return
    bases = []
    fits = None
    for e in range(N_EXPERTS):
        s0 = starts_ref[grp, i, lane0 + e]
        s1 = starts_ref[grp, i + 1, lane0 + e]
        base = jnp.minimum((s0 // BF16_ROWS) * BF16_ROWS, cap - win)
        bases.append(base)
        ok = s1 - base <= win
        fits = ok if fits is None else jnp.logical_and(fits, ok)

    @pl.when(fits)
    def _():
        rows = [y_ref[e, pl.ds(pl.multiple_of(bases[e], BF16_ROWS), win), :] for e in range(N_EXPERTS)]
        finish(win, bases, jnp.concatenate(rows, axis=0))

    @pl.when(jnp.logical_not(fits))
    def _():
        finish(cap, None, y_ref[...].reshape(N_EXPERTS * cap, D_MODEL))


def _combine(starts, pos, y, row_off, x1, mod_l, mod_row_fn, g, B, L):
    cap = EC_CAPACITY * L // N_EXPERTS
    tq = TOKEN_TILE
    assert tq == SEL_CHUNK
    nq = L // tq
    blk_off = row_off // cap
    win = min(cap, COMBINE_WINDOW)
    grid_spec = pltpu.PrefetchScalarGridSpec(
        num_scalar_prefetch=1,
        grid=(B, nq),
        in_specs=[pl.BlockSpec((1, tq, LANES), lambda b, i, s: (b // SEL_GROUP, i, 0)),
                  pl.BlockSpec((N_EXPERTS, cap, D_MODEL), lambda b, i, s: (0, blk_off + b, 0)),
                  pl.BlockSpec((tq, D_MODEL), lambda b, i, s: (b * nq + i, 0)),
                  pl.BlockSpec((1, 1, 6 * D_MODEL), lambda b, i, s: (mod_row_fn(b), 0, 0)),
                  pl.BlockSpec((4, D_MODEL), lambda b, i, s: (0, 0))],
        out_specs=pl.BlockSpec((tq, D_MODEL), lambda b, i, s: (b * nq + i, 0)),
    )
    return pl.pallas_call(
        functools.partial(_combine_body, cap, win),
        grid_spec=grid_spec,
        out_shape=jax.ShapeDtypeStruct((B * L, D_MODEL), F32),
        compiler_params=_cparams(("arbitrary", "arbitrary")),
        name="ec_combine",
    )(starts, pos, y, x1, mod_l, g)


def _rope_tables(L):
    t = jnp.arange(L)
    row = (t // GRID_W).astype(F32)
    col = (t % GRID_W).astype(F32)
    half = HEAD_DIM // 4
    inv = ROPE_BASE ** (-jnp.arange(half, dtype=F32) / half)
    d = np.arange(LANES) % HEAD_DIM
    use_col = (d // (HEAD_DIM // 2)) == 1
    pos = jnp.where(use_col[None, :], col[:, None], row[:, None])
    ang = pos * inv[d % half][None, :]
    sign = np.where((d % (HEAD_DIM // 2)) < half, -1.0, 1.0).astype(np.float32)
    return jnp.cos(ang), jnp.sin(ang) * sign[None, :]


def _permute_wa_heads(a, axis, start):
    cut = lambda lo, hi: lax.slice_in_dim(a, lo, hi, axis=axis)
    heads = [cut(start + h * HEAD_DIM, start + (h + 1) * HEAD_DIM) for h in WA_HEAD_ORDER]
    return jnp.concatenate([cut(0, start)] + heads + [cut(start + WA_WIDTH, a.shape[axis])], axis=axis)


def kernel(x_prompt, x_sample, c, cache_na_k, cache_na_v, cache_wa_k, cache_wa_v, state_ssm, c_ctx, w_ada, b_ada, norm_g, w_in, w_out, na_rpb, wa_sink, ssm_lambda_re, ssm_lambda_im, ssm_log_dt, ssm_b_re, ssm_b_im, ssm_c_re, ssm_c_im, ssm_d, w_glu, w_router, w_exp_gate, w_exp_up, w_exp_down):
    Bc, Lc, _ = x_prompt.shape
    Bs, Ls, _ = x_sample.shape
    ctx_row = Bs
    n_cond = ((Bs + 1 + SUBLANES - 1) // SUBLANES) * SUBLANES
    cond = jnp.zeros((n_cond, D_MODEL), F32).at[0:Bs].set(c).at[ctx_row].set(c_ctx)
    mod = _modulation(cond, w_ada, b_ada)
    mod = mod.reshape(DEPTH, n_cond, 1, 6 * D_MODEL)

    w_in_bf = _permute_wa_heads(w_in, 2, C_WAQ).astype(BF16)
    w_out_bf = _permute_wa_heads(w_out, 1, NA_WIDTH).astype(BF16)
    w_glu_bf = w_glu.astype(BF16)
    wr_hi = w_router.astype(BF16)
    wr_lo = (w_router - wr_hi.astype(F32)).astype(BF16)
    wr_pad = jnp.pad(jnp.concatenate([wr_hi, wr_lo], axis=-1), ((0, 0), (0, 0), (0, LANES - 2 * N_EXPERTS)))
    s5_a, s5_bb = _s5_prep(ssm_lambda_re, ssm_lambda_im, ssm_log_dt, ssm_b_re, ssm_b_im)
    s5_c = _s5_c_matrix(ssm_c_re, ssm_c_im)
    rope_tabs = _rope_tables(Ls)

    ctx_mod_row = lambda b: ctx_row + 0 * b
    lat_mod_row = lambda b: b

    xp = x_prompt.reshape(Bc * Lc, D_MODEL)
    xs = x_sample.reshape(Bs * Ls, D_MODEL)
    new_kv = tuple(jnp.zeros((Bc, DEPTH, Lc, w), F32) for w in (NA_WIDTH, NA_WIDTH, WA_KV_WIDTH, WA_KV_WIDTH))
    new_ssm = []
    for l in range(DEPTH):
        g = norm_g[l]
        dsk = ssm_d[l].reshape(1, SSM_CH)
        sl2 = slice(2 * l, 2 * l + 2)
        qna, kna, vna, qwa, kwa, vwa, u = _inproj(xp, mod, ctx_mod_row, norm_g, w_in_bf, l, Bc, Lc, None, new_kv)
        new_kv = (kna, vna, kwa, vwa)
        ona, owa = _attn_ctx(wa_sink[l], qna, kna, vna, qwa, kwa, vwa, l, Bc, Lc)
        y2, hfin = _s5_scan(u, s5_bb[sl2], s5_a[sl2], s5_c[sl2],
                            jnp.zeros((2, Bc, 2 * SSM_N), F32), Bc, Lc)
        x1, h2, aff = _outproj(ona, owa, y2, u, dsk, w_glu_bf[l], w_out_bf[l], xp, mod[l], ctx_mod_row,
                               g, wr_pad[l], Lc)
        pos_c, post, afft, starts_c = _select(aff, Bc, Lc)
        xg_c, gs_c = _gather(starts_c, h2, post, afft, Bc, Lc)
        x1_c = x1
        new_ssm.append(jnp.transpose(hfin.reshape(2, Bc, 2, SSM_GROUPS, SSM_STATE), (1, 0, 2, 3, 4)))
        qna, kna, vna, qwa, kwa, vwa, u = _inproj(xs, mod, lat_mod_row, norm_g, w_in_bf, l, Bs, Ls, rope_tabs)
        ona = _attn_na(qna, kna, vna, cache_na_k, cache_na_v, _na_bias_pairs(na_rpb[l]), l, Bs, Ls)
        owa = _attn_wa(wa_sink[l], qwa, kwa, vwa, cache_wa_k, cache_wa_v, l, Bs, Ls)
        h0 = jnp.transpose(state_ssm[:, l].reshape(Bs, 2, 2 * SSM_N), (1, 0, 2))
        y2, _ = _s5_scan(u, s5_bb[sl2], s5_a[sl2], s5_c[sl2], h0, Bs, Ls)
        x1, h2, aff = _outproj(ona, owa, y2, u, dsk, w_glu_bf[l], w_out_bf[l], xs, mod[l], lat_mod_row,
                               g, wr_pad[l], Ls)
        pos_s, post, afft, starts_s = _select(aff, Bs, Ls)
        xg_s, gs_s = _gather(starts_s, h2, post, afft, Bs, Ls)
        yy = _ffn(xg_c, gs_c, xg_s, gs_s, w_exp_gate, w_exp_up, w_exp_down, l)
        xp = _combine(starts_c, pos_c, yy, 0, x1_c, mod[l], ctx_mod_row, g, Bc, Lc)
        xs = _combine(starts_s, pos_s, yy, xg_c.shape[1], x1, mod[l], lat_mod_row, g, Bs, Ls)
    return (xp.reshape(Bc, Lc, D_MODEL), xs.reshape(Bs, Ls, D_MODEL),
            new_kv[0].reshape(Bc, DEPTH, Lc, NA_HEADS, HEAD_DIM),
            new_kv[1].reshape(Bc, DEPTH, Lc, NA_HEADS, HEAD_DIM),
            new_kv[2].reshape(Bc, DEPTH, Lc, WA_KV_HEADS, HEAD_DIM),
            new_kv[3].reshape(Bc, DEPTH, Lc, WA_KV_HEADS, HEAD_DIM),
            jnp.stack(new_ssm, axis=1))
```

```python
import functools
import math

import numpy as np
import jax
import jax.numpy as jnp
from jax import lax
from jax.experimental import pallas as pl
from jax.experimental.pallas import tpu as pltpu

F32 = jnp.float32
BF16 = jnp.bfloat16

D_MODEL = 1024
DEPTH = 2
GRID_W = 64
HEAD_DIM = 64
NA_HEADS = 6
NA_KH = 8
NA_KW = 16
WA_HEADS = 6
WA_KV_HEADS = 2
WA_WINDOW = 128
WA_QBLOCK = 256
SSM_CH = 256
SSM_GROUP_CH = 16
SSM_GROUPS = 16
SSM_STATE = 64
SSM_N = SSM_GROUPS * SSM_STATE
NA_WIDTH = NA_HEADS * HEAD_DIM
WA_WIDTH = WA_HEADS * HEAD_DIM
WA_KV_WIDTH = WA_KV_HEADS * HEAD_DIM
IN_COLS = 2048
N_EXPERTS = 16
EC_CAPACITY = 2
EXPERT_FF = 1024
ROPE_BASE = 10000.0
RMS_EPS = 1e-6
NEG_INF = -1e30
ATTN_SCALE = HEAD_DIM ** -0.5
LOG2E = math.log2(math.e)
Q_SCALE = ATTN_SCALE * LOG2E

LANES = 128
SUBLANES = 8
VMEM_LIMIT = 48 * 1024 * 1024

C_NAQ, C_NAK, C_NAV, C_WAQ, C_WAK, C_WAV, C_SSU = 0, 384, 768, 1152, 1536, 1664, 1792
WA_HEAD_ORDER = (0, 3, 1, 4, 2, 5)

NA_QROWS = 4
NA_KROWS = NA_QROWS + NA_KH
NA_REL = 2 * NA_KH - 1
NA_PAIRS = 3 * NA_REL
TOKEN_TILE = 256
OUTPROJ_SUBTILE = 128
PROJ_TILE = 512
SCAN_TILE = 128
SMALL_STEP_REQUESTS = 4
CTX_ATTN_REQUESTS = 4
SCAN_BATCH = SUBLANES
FFN_ROWS = 512
SEL_CHUNK = 256
SEL_GROUP = LANES // N_EXPERTS
BF16_ROWS = 2 * SUBLANES
COMBINE_WINDOW = 64


def _cparams(sem):
    return pltpu.CompilerParams(dimension_semantics=sem, vmem_limit_bytes=VMEM_LIMIT)


def _sigmoid(x):
    return 1.0 / (1.0 + jnp.exp(-x))


def _rms(x, g):
    ms = jnp.mean(x * x, axis=-1, keepdims=True)
    return x * lax.rsqrt(ms + RMS_EPS) * g


def _dot(a, b):
    return jnp.dot(a, b, preferred_element_type=F32)


def _dot_nt(a, b):
    return lax.dot_general(a, b, (((1,), (1,)), ((), ())), preferred_element_type=F32)


def _mod_body(c_ref, w_ref, b_ref, o_ref):
    c = c_ref[...]
    s = c * _sigmoid(c)
    o_ref[0] = _dot(s.astype(BF16), w_ref[0].astype(BF16)) + b_ref[0]


def _modulation(cond, w_ada, b_ada):
    n = cond.shape[0]
    tn = 1024
    return pl.pallas_call(
        _mod_body,
        grid=(DEPTH, 6 * D_MODEL // tn),
        in_specs=[
            pl.BlockSpec((n, D_MODEL), lambda l, j: (0, 0)),
            pl.BlockSpec((1, D_MODEL, tn), lambda l, j: (l, 0, j)),
            pl.BlockSpec((1, 1, tn), lambda l, j: (l, 0, j)),
        ],
        out_specs=pl.BlockSpec((1, n, tn), lambda l, j: (l, 0, j)),
        out_shape=jax.ShapeDtypeStruct((DEPTH, n, 6 * D_MODEL), F32),
        compiler_params=_cparams(("arbitrary", "arbitrary")),
        name="modulation",
    )(cond, w_ada, b_ada.reshape(DEPTH, 1, 6 * D_MODEL))


def _rope_chunk(x, cos, sin_signed):
    lane = lax.broadcasted_iota(jnp.int32, x.shape, 1)
    first = (lane % 32) < 16
    rot = jnp.where(first, pltpu.roll(x, LANES - 16, 1), pltpu.roll(x, 16, 1))
    return x * cos + rot * sin_signed


INPROJ_KV_OUTPUTS = (1, 2, 4, 5)


def _inproj_body(rope, n_alias, x_ref, mod_ref, g_ref, w_ref, *rest):
    if rope:
        cos_ref, sin_ref = rest[:2]
        rest = rest[2:]
    qna_ref, kna_ref, vna_ref, qwa_ref, kwa_ref, vwa_ref, u_ref = rest[n_alias:]

    def put(ref, val):
        ref[...] = val.reshape(ref.shape)

    x = x_ref[...]
    m = mod_ref[0, 0]
    sh1 = m[:, 0:D_MODEL]
    sc1 = m[:, D_MODEL:2 * D_MODEL]
    h = _rms(x, g_ref[0, 0:1, :]) * (1.0 + sc1) + sh1
    z = _dot(h.astype(BF16), w_ref[0])
    qna_ref[...] = (z[:, C_NAQ:C_NAK] * Q_SCALE).astype(BF16)
    put(kna_ref, z[:, C_NAK:C_NAV])
    put(vna_ref, z[:, C_NAV:C_WAQ])
    put(vwa_ref, z[:, C_WAV:C_SSU])
    u_ref[...] = z[:, C_SSU:IN_COLS]
    if rope:
        cos = cos_ref[...]
        sin = sin_ref[...]
        for j in range(WA_WIDTH // LANES):
            qc = _rope_chunk(z[:, C_WAQ + j * LANES:C_WAQ + (j + 1) * LANES], cos, sin)
            qwa_ref[:, j * LANES:(j + 1) * LANES] = (qc * Q_SCALE).astype(BF16)
        put(kwa_ref, _rope_chunk(z[:, C_WAK:C_WAV], cos, sin))
    else:
        qwa_ref[...] = (z[:, C_WAQ:C_WAK] * Q_SCALE).astype(BF16)
        put(kwa_ref, z[:, C_WAK:C_WAV])


def _inproj(x, mod, mod_row_fn, norm_g, w_bf, l, B, L, rope_tabs, kv_layers=None):
    T = x.shape[0]
    tm = min(PROJ_TILE, L)
    tiles_per_seq = L // tm
    in_specs = [
        pl.BlockSpec((tm, D_MODEL), lambda i: (i, 0)),
        pl.BlockSpec((1, 1, 1, 6 * D_MODEL), lambda i: (l, mod_row_fn(i // tiles_per_seq), 0, 0)),
        pl.BlockSpec((1, 4, D_MODEL), lambda i: (l, 0, 0)),
        pl.BlockSpec((1, D_MODEL, IN_COLS), lambda i: (l, 0, 0)),
    ]
    args = [x, mod, norm_g, w_bf]
    if rope_tabs is not None:
        in_specs += [pl.BlockSpec((tm, LANES), lambda i: (i % tiles_per_seq, 0))] * 2
        args += list(rope_tabs)
    widths = (NA_WIDTH, NA_WIDTH, NA_WIDTH, WA_WIDTH, WA_KV_WIDTH, WA_KV_WIDTH, SSM_CH)
    dtypes = (BF16, F32, F32, BF16, F32, F32, F32)
    out_specs = [pl.BlockSpec((tm, w), lambda i: (i, 0)) for w in widths]
    out_shape = [jax.ShapeDtypeStruct((T, w), dt) for w, dt in zip(widths, dtypes)]
    aliases = {}
    n_alias = 0
    if kv_layers is not None:
        for o in INPROJ_KV_OUTPUTS:
            out_specs[o] = pl.BlockSpec((1, 1, tm, widths[o]),
                                        lambda i: (i // tiles_per_seq, l, i % tiles_per_seq, 0))
            out_shape[o] = jax.ShapeDtypeStruct((B, DEPTH, L, widths[o]), dtypes[o])
        n_alias = len(kv_layers)
        for k, (o, prev) in enumerate(zip(INPROJ_KV_OUTPUTS, kv_layers)):
            aliases[len(args)] = o
            in_specs.append(pl.BlockSpec(memory_space=pl.ANY))
            args.append(prev)
    return pl.pallas_call(
        functools.partial(_inproj_body, rope_tabs is not None, n_alias),
        grid=(T // tm,),
        in_specs=in_specs,
        out_specs=out_specs,
        out_shape=out_shape,
        input_output_aliases=aliases,
        compiler_params=_cparams(("arbitrary",)),
        name="inproj",
    )(*args)


def _lane_lo(shape):
    return lax.broadcasted_iota(jnp.int32, shape, len(shape) - 1) < HEAD_DIM


def _keep_half(q, lo, half):
    keep = jnp.where(lo, 1.0 - half, 0.0 + half).astype(q.dtype)
    return q * keep


def _values_and_ones(v, lo, half):
    one = jnp.ones_like(v)
    return (jnp.where(lo, v, one) if half == 0 else jnp.where(lo, one, v)).astype(BF16)


def _normalise(o2, extra=None):
    den = pltpu.roll(o2, HEAD_DIM, 1)
    if extra is not None:
        den = den + extra
    return o2 / den


def _attn_ctx_body(sink_ref, qna, kna, vna, qwa, kwa, vwa, ona, owa):
    R, L = qna.shape[0], qna.shape[1]
    npair = WA_WIDTH // LANES
    lo = _lane_lo((L, LANES))
    lo_all = _lane_lo((npair * L, LANES))
    blk = lax.broadcasted_iota(jnp.int32, (npair * L, 1), 0) // L
    for r in range(R):
        for j in range(NA_WIDTH // LANES):
            sl = slice(j * LANES, (j + 1) * LANES)
            q2 = qna[r, :, sl]
            k2 = kna[r, 0, :, sl].astype(BF16)
            v2 = vna[r, 0, :, sl].astype(BF16)
            halves = []
            for half in range(2):
                s = _dot_nt(_keep_half(q2, lo, half), k2)
                p = jnp.exp2(s - jnp.max(s, axis=-1, keepdims=True))
                halves.append(_dot(p.astype(BF16), v2) / jnp.sum(p, axis=-1, keepdims=True))
            ona[r, :, sl] = jnp.where(lo, halves[0], halves[1]).astype(BF16)
        kw = kwa[r, 0].astype(BF16)
        vw = vwa[r, 0].astype(BF16)
        q_all = jnp.concatenate([qwa[r, :, j * LANES:(j + 1) * LANES] for j in range(npair)], axis=0)
        outs = []
        for g in range(WA_KV_HEADS):
            s = _dot_nt(_keep_half(q_all, lo_all, g), kw)
            sk = jnp.zeros((npair * L, 1), F32)
            for j in range(npair):
                sk = jnp.where(blk == j, sink_ref[WA_HEAD_ORDER[2 * j + g]] * LOG2E, sk)
            m = jnp.maximum(jnp.max(s, axis=-1, keepdims=True), sk)
            p = jnp.exp2(s - m)
            l = jnp.sum(p, axis=-1, keepdims=True) + jnp.exp2(sk - m)
            outs.append(_dot(p.astype(BF16), vw) / l)
        o_all = jnp.where(lo_all, outs[0], outs[1]).astype(BF16)
        for j in range(npair):
            owa[r, :, j * LANES:(j + 1) * LANES] = o_all[j * L:(j + 1) * L]


def _attn_ctx(sink, qna, kna, vna, qwa, kwa, vwa, l, B, L):
    R = CTX_ATTN_REQUESTS

    def spec(w):
        return pl.BlockSpec((R, L, w), lambda b: (b, 0, 0))

    def layer_spec(w):
        return pl.BlockSpec((R, 1, L, w), lambda b: (b, l, 0, 0))

    r3 = lambda a: a.reshape(B, L, a.shape[-1])
    ona, owa = pl.pallas_call(
        _attn_ctx_body,
        grid=(B // R,),
        in_specs=[pl.BlockSpec(memory_space=pltpu.SMEM),
                  spec(NA_WIDTH), layer_spec(NA_WIDTH), layer_spec(NA_WIDTH),
                  spec(WA_WIDTH), layer_spec(WA_KV_WIDTH), layer_spec(WA_KV_WIDTH)],
        out_specs=[spec(NA_WIDTH), spec(WA_WIDTH)],
        out_shape=[jax.ShapeDtypeStruct((B, L, NA_WIDTH), BF16),
                   jax.ShapeDtypeStruct((B, L, WA_WIDTH), BF16)],
        compiler_params=_cparams(("arbitrary",)),
        name="attn_ctx",
    )(sink, r3(qna), kna, vna, r3(qwa), kwa, vwa)
    return ona.reshape(B * L, NA_WIDTH), owa.reshape(B * L, WA_WIDTH)


def _na_key_start(rb, rows):
    return jnp.clip(rb * NA_QROWS - NA_KH // 2, 0, rows - NA_KROWS)


def _attn_na_body(rows, idx_ref, q_ref, k_ref, v_ref, kc_ref, vc_ref, pairs_ref, o_ref):
    rb = pl.program_id(1)
    nq = NA_QROWS * GRID_W
    nk = NA_KROWS * GRID_W
    start = pl.multiple_of(_na_key_start(rb, rows) * GRID_W, GRID_W)
    lo = _lane_lo((nq, LANES))
    lo_v = _lane_lo((1, LANES))
    npair = NA_KROWS // 2
    pat = jnp.where(rb == 0, 0, jnp.where(rb == rows // NA_QROWS - 1, 2, 1))
    entry = [[idx_ref[(pat * NA_QROWS + i) * npair + jj] for jj in range(npair)] for i in range(NA_QROWS)]

    def bias(head):
        return jnp.concatenate(
            [jnp.concatenate([pairs_ref[head, entry[i][jj]] for jj in range(npair)], axis=1)
             for i in range(NA_QROWS)], axis=0)

    for j in range(NA_WIDTH // LANES):
        sl = slice(j * LANES, (j + 1) * LANES)
        q2 = q_ref[0, :, sl]
        kl = k_ref[0, pl.ds(start, nk), sl].astype(BF16)
        vl = v_ref[0, pl.ds(start, nk), sl]
        kc = kc_ref[0, 0, :, sl].astype(BF16)
        vc = vc_ref[0, 0, :, sl]
        scores = []
        for half in range(2):
            qm = _keep_half(q2, lo, half)
            scores.append((_dot_nt(qm, kl) + bias(2 * j + half), _dot_nt(qm, kc)))
        probs = []
        for s_loc, s_ctx in scores:
            m = jnp.maximum(jnp.max(s_loc, axis=-1, keepdims=True),
                            jnp.max(s_ctx, axis=-1, keepdims=True))
            probs.append((jnp.exp2(s_loc - m).astype(BF16), jnp.exp2(s_ctx - m).astype(BF16)))
        halves = []
        for half, (p_loc, p_ctx) in enumerate(probs):
            o2 = (_dot(p_loc, _values_and_ones(vl, lo_v, half))
                  + _dot(p_ctx, _values_and_ones(vc, lo_v, half)))
            halves.append(_normalise(o2))
        o_ref[0, :, sl] = jnp.where(lo, halves[0], halves[1]).astype(BF16)


def _na_bias_pairs(rpb_l):
    cq = np.arange(GRID_W)
    col_start = np.clip(cq - NA_KW // 2, 0, GRID_W - NA_KW)
    col_ok = (cq[None, :] >= col_start[:, None]) & (cq[None, :] < col_start[:, None] + NA_KW)
    dcol = np.clip(cq[None, :] - cq[:, None], -(NA_KW - 1), NA_KW - 1) + (NA_KW - 1)
    pick = (dcol[None] == np.arange(2 * NA_KW - 1)[:, None, None]).astype(np.float32)
    tiles = jnp.einsum('hrd,dqc->hrqc', rpb_l.astype(F32) * LOG2E, pick, precision=lax.Precision.HIGHEST)
    tiles = jnp.where(col_ok[None, None], tiles, NEG_INF)
    masked = jnp.full_like(tiles, NEG_INF)
    both = jnp.concatenate([tiles[:, :-1], tiles[:, 1:]], axis=-1)
    right = jnp.concatenate([masked, tiles], axis=-1)
    left = jnp.concatenate([tiles, masked], axis=-1)
    none = jnp.concatenate([masked[:, :1], masked[:, :1]], axis=-1)
    return jnp.concatenate([both, right, left, none], axis=1)


def _na_pair_index(rows):
    nrb = rows // NA_QROWS
    idx = np.zeros((3, NA_QROWS, NA_KROWS // 2), np.int32)
    for pi, rb in enumerate((0, 1, nrb - 1)):
        ks = int(np.clip(rb * NA_QROWS - NA_KH // 2, 0, rows - NA_KROWS))
        for i in range(NA_QROWS):
            qr = rb * NA_QROWS + i
            rs = int(np.clip(qr - NA_KH // 2, 0, rows - NA_KH))
            for jj in range(NA_KROWS // 2):
                kr = ks + 2 * jj
                r = kr - qr + NA_KH - 1
                ok_l = rs <= kr < rs + NA_KH
                ok_r = rs <= kr + 1 < rs + NA_KH
                if ok_l and ok_r:
                    idx[pi, i, jj] = r
                elif ok_r:
                    idx[pi, i, jj] = (NA_REL - 1) + (r + 1)
                elif ok_l:
                    idx[pi, i, jj] = (NA_REL - 1) + NA_REL + r
                else:
                    idx[pi, i, jj] = NA_PAIRS - 1
    return jnp.asarray(idx.reshape(-1))


def _attn_na(q, k, v, cache_k, cache_v, pairs, l, B, L):
    rows = L // GRID_W
    nrb = rows // NA_QROWS
    nq = NA_QROWS * GRID_W
    P = cache_k.shape[2]
    r3 = lambda a: a.reshape(B, L, a.shape[-1])
    ck = cache_k.reshape(B, DEPTH, P, NA_WIDTH)
    cv = cache_v.reshape(B, DEPTH, P, NA_WIDTH)
    grid_spec = pltpu.PrefetchScalarGridSpec(
        num_scalar_prefetch=1,
        grid=(B, nrb),
        in_specs=[
            pl.BlockSpec((1, nq, NA_WIDTH), lambda b, r, s: (b, r, 0)),
            pl.BlockSpec((1, L, NA_WIDTH), lambda b, r, s: (b, 0, 0)),
            pl.BlockSpec((1, L, NA_WIDTH), lambda b, r, s: (b, 0, 0)),
            pl.BlockSpec((1, 1, P, NA_WIDTH), lambda b, r, s: (b, l, 0, 0)),
            pl.BlockSpec((1, 1, P, NA_WIDTH), lambda b, r, s: (b, l, 0, 0)),
            pl.BlockSpec(pairs.shape, lambda b, r, s: (0, 0, 0, 0)),
        ],
        out_specs=pl.BlockSpec((1, nq, NA_WIDTH), lambda b, r, s: (b, r, 0)),
    )
    o = pl.pallas_call(
        functools.partial(_attn_na_body, rows),
        grid_spec=grid_spec,
        out_shape=jax.ShapeDtypeStruct((B, L, NA_WIDTH), BF16),
        compiler_params=_cparams(("arbitrary", "arbitrary")),
        name="attn_na",
    )(_na_pair_index(rows), r3(q), r3(k), r3(v), ck, cv, pairs)
    return o.reshape(B * L, NA_WIDTH)


def _attn_wa_body(L, sink_ref, q_ref, k_ref, v_ref, kc_ref, vc_ref, o_ref):
    n = pl.program_id(1)
    nk = WA_QBLOCK + 2 * WA_WINDOW
    npair = WA_WIDTH // LANES
    start = pl.multiple_of(jnp.clip(n * WA_QBLOCK - WA_WINDOW, 0, L - nk), WA_WINDOW)
    kl = k_ref[0, pl.ds(start, nk), :].astype(BF16)
    vl = v_ref[0, pl.ds(start, nk), :]
    kc = kc_ref[0, 0].astype(BF16)
    vc = vc_ref[0, 0]
    lo_v = _lane_lo((1, LANES))
    rows = npair * WA_QBLOCK
    qpos = n * WA_QBLOCK + lax.broadcasted_iota(jnp.int32, (rows, nk), 0) % WA_QBLOCK
    kpos = start + lax.broadcasted_iota(jnp.int32, (rows, nk), 1)
    in_win = jnp.abs(kpos - qpos) <= WA_WINDOW
    lo = _lane_lo((rows, LANES))
    q_all = jnp.concatenate([q_ref[0, :, j * LANES:(j + 1) * LANES] for j in range(npair)], axis=0)
    blk = lax.broadcasted_iota(jnp.int32, (rows, 1), 0) // WA_QBLOCK
    scores = []
    for g in range(WA_KV_HEADS):
        qm = _keep_half(q_all, lo, g)
        scores.append((jnp.where(in_win, _dot_nt(qm, kl), NEG_INF), _dot_nt(qm, kc)))
    probs = []
    for g in range(WA_KV_HEADS):
        s_loc, s_ctx = scores[g]
        sk = jnp.zeros((rows, 1), F32)
        for j in range(npair):
            sk = jnp.where(blk == j, sink_ref[WA_HEAD_ORDER[2 * j + g]] * LOG2E, sk)
        m = jnp.maximum(jnp.maximum(jnp.max(s_loc, axis=-1, keepdims=True),
                                    jnp.max(s_ctx, axis=-1, keepdims=True)), sk)
        probs.append((jnp.exp2(s_loc - m).astype(BF16), jnp.exp2(s_ctx - m).astype(BF16), jnp.exp2(sk - m)))
    outs = []
    for g in range(WA_KV_HEADS):
        p_loc, p_ctx, p_sink = probs[g]
        o2 = _dot(p_loc, _values_and_ones(vl, lo_v, g)) + _dot(p_ctx, _values_and_ones(vc, lo_v, g))
        outs.append(_normalise(o2, p_sink))
    o_all = jnp.where(lo, outs[0], outs[1]).astype(BF16)
    for j in range(npair):
        o_ref[0, :, j * LANES:(j + 1) * LANES] = o_all[j * WA_QBLOCK:(j + 1) * WA_QBLOCK]


def _attn_wa(sink, q, k, v, cache_k, cache_v, l, B, L):
    nb = L // WA_QBLOCK
    P = cache_k.shape[2]
    r3 = lambda a: a.reshape(B, L, a.shape[-1])
    ck = cache_k.reshape(B, DEPTH, P, WA_KV_WIDTH)
    cv = cache_v.reshape(B, DEPTH, P, WA_KV_WIDTH)
    o = pl.pallas_call(
        functools.partial(_attn_wa_body, L),
        grid=(B, nb),
        in_specs=[
            pl.BlockSpec(memory_space=pltpu.SMEM),
            pl.BlockSpec((1, WA_QBLOCK, WA_WIDTH), lambda b, n: (b, n, 0)),
            pl.BlockSpec((1, L, WA_KV_WIDTH), lambda b, n: (b, 0, 0)),
            pl.BlockSpec((1, L, WA_KV_WIDTH), lambda b, n: (b, 0, 0)),
            pl.BlockSpec((1, 1, P, WA_KV_WIDTH), lambda b, n: (b, l, 0, 0)),
            pl.BlockSpec((1, 1, P, WA_KV_WIDTH), lambda b, n: (b, l, 0, 0)),
        ],
        out_specs=pl.BlockSpec((1, WA_QBLOCK, WA_WIDTH), lambda b, n: (b, n, 0)),
        out_shape=jax.ShapeDtypeStruct((B, L, WA_WIDTH), BF16),
        compiler_params=_cparams(("arbitrary", "arbitrary")),
        name="attn_wa",
    )(sink, r3(q), r3(k), r3(v), ck, cv)
    return o.reshape(B * L, WA_WIDTH)


def _s5_prep_body(lr_ref, li_ref, ldt_ref, br_ref, bi_ref, a_ref, bb_ref):
    lr = lr_ref[0]
    li = li_ref[0]
    dt = jnp.exp(ldt_ref[0])
    mag = jnp.exp(lr * dt)
    ar = mag * jnp.cos(li * dt)
    ai = mag * jnp.sin(li * dt)
    den = lr * lr + li * li
    nr = ar - 1.0
    fr = (nr * lr + ai * li) / den
    fi = (ai * lr - nr * li) / den
    a_ref[0, :, 0:SSM_N] = ar
    a_ref[0, :, SSM_N:2 * SSM_N] = ai
    br = br_ref[0]
    bi = bi_ref[0]
    bb_ref[0, :, 0:SSM_N] = (fr * br - fi * bi).astype(BF16)
    bb_ref[0, :, SSM_N:2 * SSM_N] = (fr * bi + fi * br).astype(BF16)


def _s5_prep(lam_re, lam_im, log_dt, b_re, b_im):
    n = DEPTH * 2
    eye = jnp.eye(SSM_GROUPS, dtype=F32)

    def blockdiag_b(b):
        return jnp.einsum('ngph,gk->nghkp', b.reshape(n, SSM_GROUPS, SSM_STATE, SSM_GROUP_CH), eye
                          ).reshape(n, SSM_CH, SSM_N)

    lr = lam_re.reshape(n, 1, SSM_N)
    li = lam_im.reshape(n, 1, SSM_N)
    ldt = jnp.repeat(log_dt.reshape(n, SSM_GROUPS), SSM_STATE, axis=-1).reshape(n, 1, SSM_N)
    vec = pl.BlockSpec((1, 1, SSM_N), lambda i: (i, 0, 0))
    mat = pl.BlockSpec((1, SSM_CH, SSM_N), lambda i: (i, 0, 0))
    return pl.pallas_call(
        _s5_prep_body,
        grid=(n,),
        in_specs=[vec, vec, vec, mat, mat],
        out_specs=[pl.BlockSpec((1, 1, 2 * SSM_N), lambda i: (i, 0, 0)),
                   pl.BlockSpec((1, SSM_CH, 2 * SSM_N), lambda i: (i, 0, 0))],
        out_shape=[jax.ShapeDtypeStruct((n, 1, 2 * SSM_N), F32),
                   jax.ShapeDtypeStruct((n, SSM_CH, 2 * SSM_N), BF16)],
        compiler_params=_cparams(("arbitrary",)),
        name="s5_prep",
    )(lr, li, ldt, blockdiag_b(b_re), blockdiag_b(b_im))


def _s5_c_matrix(c_re, c_im):
    n = DEPTH * 2
    eye = jnp.eye(SSM_GROUPS, dtype=F32)

    def blk(c):
        return jnp.einsum('nghp,gk->ngpkh', c.reshape(n, SSM_GROUPS, SSM_GROUP_CH, SSM_STATE), eye
                          ).reshape(n, SSM_N, SSM_CH)

    return jnp.concatenate([blk(c_re), -blk(c_im)], axis=1).astype(BF16)


def _s5_scan_body(tl, u_ref, bb_ref, a_ref, c_ref, h0_ref, y_ref, hfin_ref, bu, hb, hst):
    d = pl.program_id(0)
    i = pl.program_id(2)
    sb = SCAN_BATCH
    re = slice(0, SSM_N)
    im = slice(SSM_N, 2 * SSM_N)

    @pl.when(i == 0)
    def _():
        hst[...] = h0_ref[0]

    u_tb = jnp.transpose(u_ref[...], (1, 0, 2)).reshape(tl * sb, SSM_CH)
    u_bf = u_tb.astype(BF16)
    half = tl * sb // 2
    bu[0:half, :] = _dot(u_bf[0:half], bb_ref[0])
    bu[half:, :] = _dot(u_bf[half:], bb_ref[0])
    a = a_ref[0]
    ar = jnp.broadcast_to(a[:, re], (sb, SSM_N))
    ai = jnp.broadcast_to(a[:, im], (sb, SSM_N))

    def scan(reverse):
        def pair(s, carry):
            hr, hi = carry
            t0 = tl - 1 - 2 * s if reverse else 2 * s
            t1 = t0 - 1 if reverse else t0 + 1
            r0 = pl.ds(pl.multiple_of(t0 * sb, sb), sb)
            r1 = pl.ds(pl.multiple_of(t1 * sb, sb), sb)
            h0r = ar * hr - ai * hi + bu[r0, re]
            h0i = ar * hi + ai * hr + bu[r0, im]
            h1r = ar * h0r - ai * h0i + bu[r1, re]
            h1i = ar * h0i + ai * h0r + bu[r1, im]
            rows = pl.ds(pl.multiple_of((t1 if reverse else t0) * sb, 2 * sb), 2 * sb)
            early, late = ((h1r, h1i), (h0r, h0i)) if reverse else ((h0r, h0i), (h1r, h1i))
            hb[rows, re] = jnp.concatenate([early[0], late[0]], axis=0).astype(BF16)
            hb[rows, im] = jnp.concatenate([early[1], late[1]], axis=0).astype(BF16)
            return h1r, h1i

        hr, hi = lax.fori_loop(0, tl // 2, pair, (hst[:, re], hst[:, im]))
        hst[:, re] = hr
        hst[:, im] = hi
        hfin_ref[0, :, re] = hr
        hfin_ref[0, :, im] = hi

    @pl.when(d == 0)
    def _():
        scan(False)

    @pl.when(d == 1)
    def _():
        scan(True)

    y_tb = jnp.concatenate([_dot(hb[0:half, :], c_ref[0]), _dot(hb[half:, :], c_ref[0])], axis=0)
    y_ref[0] = jnp.transpose(y_tb.reshape(tl, sb, SSM_CH), (1, 0, 2))


def _s5_scan(u, bb, a, cm, h0, B, L):
    tl = SCAN_TILE
    nt = L // tl
    tile = lambda d, i: i + d * (nt - 1 - 2 * i)
    return pl.pallas_call(
        functools.partial(_s5_scan_body, tl),
        grid=(2, B // SCAN_BATCH, nt),
        in_specs=[
            pl.BlockSpec((SCAN_BATCH, tl, SSM_CH), lambda d, b, i: (b, tile(d, i), 0)),
            pl.BlockSpec((1, SSM_CH, 2 * SSM_N), lambda d, b, i: (d, 0, 0)),
            pl.BlockSpec((1, 1, 2 * SSM_N), lambda d, b, i: (d, 0, 0)),
            pl.BlockSpec((1, 2 * SSM_N, SSM_CH), lambda d, b, i: (d, 0, 0)),
            pl.BlockSpec((1, SCAN_BATCH, 2 * SSM_N), lambda d, b, i: (d, b, 0)),
        ],
        out_specs=[
            pl.BlockSpec((1, SCAN_BATCH, tl, SSM_CH), lambda d, b, i: (d, b, tile(d, i), 0)),
            pl.BlockSpec((1, SCAN_BATCH, 2 * SSM_N), lambda d, b, i: (d, b, 0)),
        ],
        out_shape=[jax.ShapeDtypeStruct((2, B, L, SSM_CH), F32),
                   jax.ShapeDtypeStruct((2, B, 2 * SSM_N), F32)],
        scratch_shapes=[pltpu.VMEM((SCAN_BATCH * tl, 2 * SSM_N), F32),
                        pltpu.VMEM((SCAN_BATCH * tl, 2 * SSM_N), BF16),
                        pltpu.VMEM((SCAN_BATCH, 2 * SSM_N), F32)],
        compiler_params=_cparams(("arbitrary", "arbitrary", "arbitrary")),
        name="s5_scan",
    )(u.reshape(B, L, SSM_CH), bb, a, cm, h0)


def _outproj_body(ona_ref, owa_ref, y_ref, u_ref, dsk_ref, wglu_ref, wout_ref, x_ref, mod_ref, g_ref,
                  wr_ref, x1_ref, h2_ref, aff_ref):
    m = mod_ref[0]
    g1 = m[:, 2 * D_MODEL:3 * D_MODEL]
    sh2 = m[:, 3 * D_MODEL:4 * D_MODEL]
    sc2 = m[:, 4 * D_MODEL:5 * D_MODEL]
    tm = x_ref.shape[0]
    sub = min(tm, OUTPROJ_SUBTILE)
    parts = [slice(r, r + sub) for r in range(0, tm, sub)]
    mixes = []
    for rs in parts:
        y = dsk_ref[...] * u_ref[rs, :] + y_ref[0, rs, :] + y_ref[1, rs, :]
        zg = _dot(y.astype(BF16), wglu_ref[...])
        oss = zg[:, 0:SSM_CH] * _sigmoid(zg[:, SSM_CH:2 * SSM_CH])
        mixes.append(_dot(ona_ref[rs, :], wout_ref[0:NA_WIDTH, :])
                     + _dot(owa_ref[rs, :], wout_ref[NA_WIDTH:NA_WIDTH + WA_WIDTH, :])
                     + _dot(oss.astype(BF16), wout_ref[NA_WIDTH + WA_WIDTH:D_MODEL, :]))
    split = []
    for rs, mix in zip(parts, mixes):
        x1 = x_ref[rs, :] + g1 * _rms(mix, g_ref[1:2, :])
        x1_ref[rs, :] = x1
        h2 = _rms(x1, g_ref[2:3, :]) * (1.0 + sc2) + sh2
        h2_hi = h2.astype(BF16)
        h2_ref[rs, :] = h2_hi
        split.append((h2_hi, (h2 - h2_hi.astype(F32)).astype(BF16)))
    for rs, (h2_hi, h2_lo) in zip(parts, split):
        r_hi = _dot(h2_hi, wr_ref[...])
        logits = r_hi + pltpu.roll(r_hi, LANES - N_EXPERTS, 1) + _dot(h2_lo, wr_ref[...])
        lane = lax.broadcasted_iota(jnp.int32, logits.shape, 1)
        logits = jnp.where(lane < N_EXPERTS, logits, NEG_INF)
        mx = jnp.max(logits, axis=-1, keepdims=True)
        p = jnp.exp(logits - mx)
        aff_ref[rs, :] = p / jnp.sum(p, axis=-1, keepdims=True)


def _outproj(ona, owa, y2, u, dsk, wglu_bf, wout_bf, x, mod_l, mod_row_fn, g, wr_pad, L):
    T = x.shape[0]
    tm = min(PROJ_TILE, L)
    tiles_per_seq = L // tm
    row = lambda w: pl.BlockSpec((tm, w), lambda i: (i, 0))
    full = lambda a: pl.BlockSpec(a.shape, lambda i: (0,) * a.ndim)
    return pl.pallas_call(
        _outproj_body,
        grid=(T // tm,),
        in_specs=[row(NA_WIDTH), row(WA_WIDTH),
                  pl.BlockSpec((2, tm, SSM_CH), lambda i: (0, i, 0)),
                  row(SSM_CH), full(dsk), full(wglu_bf), full(wout_bf), row(D_MODEL),
                  pl.BlockSpec((1, 1, 6 * D_MODEL), lambda i: (mod_row_fn(i // tiles_per_seq), 0, 0)),
                  full(g), full(wr_pad)],
        out_specs=[row(D_MODEL), row(D_MODEL), row(LANES)],
        out_shape=[jax.ShapeDtypeStruct((T, D_MODEL), F32),
                   jax.ShapeDtypeStruct((T, D_MODEL), BF16),
                   jax.ShapeDtypeStruct((T, LANES), F32)],
        compiler_params=_cparams(("arbitrary",)),
        name="outproj",
    )(ona, owa, y2.reshape(2, T, SSM_CH), u, dsk, wglu_bf, wout_bf, x, mod_l, g, wr_pad)


def _select_body(cap, aff_ref, pos_ref, post_ref, afft_ref, starts_ref):
    L = aff_ref.shape[1]
    aff = aff_ref[0]
    for r in range(1, SEL_GROUP):
        aff = aff + pltpu.roll(aff_ref[r], r * N_EXPERTS, 1)
    capf = jnp.float32(cap)

    def bisect(k, thr):
        cand = thr | jnp.left_shift(jnp.int32(1), 30 - k)
        cnt = jnp.sum(jnp.where(aff >= pltpu.bitcast(cand, F32), 1.0, 0.0), axis=0, keepdims=True)
        return jnp.where(cnt >= capf, cand, thr)

    thr = lax.fori_loop(0, 31, bisect, jnp.zeros((1, LANES), jnp.int32))
    kth = pltpu.bitcast(thr, F32)
    gt = jnp.where(aff > kth, 1.0, 0.0)
    eq = jnp.where(aff == kth, 1.0, 0.0)
    need = capf - jnp.sum(gt, axis=0, keepdims=True)
    ck = SEL_CHUNK
    tri = (lax.broadcasted_iota(jnp.int32, (ck, ck), 0)
           >= lax.broadcasted_iota(jnp.int32, (ck, ck), 1))
    tri = jnp.where(tri, 1.0, 0.0).astype(BF16)

    def prefix(mask):
        outs = []
        carries = [jnp.zeros((1, LANES), F32)]
        for j in range(L // ck):
            c = _dot(tri, mask[j * ck:(j + 1) * ck].astype(BF16)) + carries[-1]
            outs.append(c)
            carries.append(c[ck - 1:ck, :])
        return jnp.concatenate(outs, axis=0), jnp.concatenate(carries, axis=0)

    tie = jnp.where(prefix(eq)[0] <= need, eq, 0.0)
    sel = gt + tie
    rank, starts = prefix(sel)
    pos = jnp.where(sel > 0.0, rank - 1.0, -1.0)
    pos_ref[0] = pos
    post_ref[0] = jnp.transpose(pos)
    afft_ref[0] = jnp.transpose(aff)
    starts_ref[0] = starts.astype(jnp.int32)


def _select(aff, B, L):
    cap = EC_CAPACITY * L // N_EXPERTS
    ng = B // SEL_GROUP
    return pl.pallas_call(
        functools.partial(_select_body, cap),
        grid=(ng,),
        in_specs=[pl.BlockSpec((SEL_GROUP, L, LANES), lambda i: (i, 0, 0))],
        out_specs=[pl.BlockSpec((1, L, LANES), lambda i: (i, 0, 0)),
                   pl.BlockSpec((1, LANES, L), lambda i: (i, 0, 0)),
                   pl.BlockSpec((1, LANES, L), lambda i: (i, 0, 0)),
                   pl.BlockSpec((1, L // SEL_CHUNK + 1, LANES), lambda i: (i, 0, 0))],
        out_shape=[jax.ShapeDtypeStruct((ng, L, LANES), F32),
                   jax.ShapeDtypeStruct((ng, LANES, L), F32),
                   jax.ShapeDtypeStruct((ng, LANES, L), F32),
                   jax.ShapeDtypeStruct((ng, L // SEL_CHUNK + 1, LANES), jnp.int32)],
        compiler_params=_cparams(("arbitrary",)),
        name="ec_select",
    )(aff.reshape(B, L, LANES))


def _window_rows(base, w):
    return pl.ds(base, w) if isinstance(base, int) else pl.ds(pl.multiple_of(base, BF16_ROWS), w)


def _gather_body(cap, win, starts_ref, h_ref, post_ref, afft_ref, xs_ref, gs_ref):
    b = pl.program_id(0)
    j = pl.program_id(1)
    grp = b // SEL_GROUP
    lane0 = (b % SEL_GROUP) * N_EXPERTS

    @pl.when(j == 0)
    def _():
        xs_ref[...] = jnp.zeros_like(xs_ref)
        gs_ref[...] = jnp.zeros_like(gs_ref)

    n_req, ck = h_ref.shape[0], h_ref.shape[1]

    def run(w, bases, r=0):
        h = h_ref[r]
        slot = lax.broadcasted_iota(jnp.int32, (w, ck), 0)
        onehots = []
        for e in range(N_EXPERTS):
            rows = _window_rows(bases[e], w)
            row = r * N_EXPERTS + e
            hit = post_ref[0, row:row + 1, :] == (slot + bases[e]).astype(F32)
            gs_ref[e, r, rows, :] += jnp.sum(jnp.where(hit, afft_ref[0, row:row + 1, :], 0.0), axis=1, keepdims=True)
            onehots.append(jnp.where(hit, 1.0, 0.0).astype(BF16))
        xs = _dot(jnp.concatenate(onehots, axis=0), h).astype(BF16)
        for e in range(N_EXPERTS):
            rows = _window_rows(bases[e], w)
            xs_ref[e, r, rows, :] += xs[e * w:(e + 1) * w]

    if win == cap:
        for r in range(n_req):
            run(cap, [0] * N_EXPERTS, r)
        return
    bases = []
    fits = None
    for e in range(N_EXPERTS):
        s0 = starts_ref[grp, j, lane0 + e]
        s1 = starts_ref[grp, j + 1, lane0 + e]
        base = jnp.minimum((s0 // BF16_ROWS) * BF16_ROWS, cap - win)
        bases.append(base)
        ok = s1 - base <= win
        fits = ok if fits is None else jnp.logical_and(fits, ok)

    @pl.when(fits)
    def _():
        run(win, bases)

    @pl.when(jnp.logical_not(fits))
    def _():
        run(cap, [0] * N_EXPERTS)


def _gather(starts, h2, post, afft, B, L):
    cap = EC_CAPACITY * L // N_EXPERTS
    ck = SEL_CHUNK
    win = min(cap, COMBINE_WINDOW)
    nr = SMALL_STEP_REQUESTS if win == cap else 1
    per_group = SEL_GROUP // nr
    lane_blk = lambda b, j, s: (b // per_group, b % per_group, j)
    grid_spec = pltpu.PrefetchScalarGridSpec(
        num_scalar_prefetch=1,
        grid=(B // nr, L // ck),
        in_specs=[pl.BlockSpec((nr, ck, D_MODEL), lambda b, j, s: (b, j, 0)),
                  pl.BlockSpec((1, nr * N_EXPERTS, ck), lane_blk),
                  pl.BlockSpec((1, nr * N_EXPERTS, ck), lane_blk)],
        out_specs=[pl.BlockSpec((N_EXPERTS, nr, cap, D_MODEL), lambda b, j, s: (0, b, 0, 0)),
                   pl.BlockSpec((N_EXPERTS, nr, cap, 1), lambda b, j, s: (0, b, 0, 0))],
    )
    xs, gs = pl.pallas_call(
        functools.partial(_gather_body, cap, win),
        grid_spec=grid_spec,
        out_shape=[jax.ShapeDtypeStruct((N_EXPERTS, B, cap, D_MODEL), BF16),
                   jax.ShapeDtypeStruct((N_EXPERTS, B, cap, 1), F32)],
        compiler_params=_cparams(("arbitrary", "arbitrary")),
        name="ec_gather",
    )(starts, h2.reshape(B, L, D_MODEL), post, afft)
    return xs.reshape(N_EXPERTS, B * cap, D_MODEL), gs.reshape(N_EXPERTS, B * cap, 1)


def _ffn_body(n_first, xa_ref, ga_ref, xb_ref, gb_ref, wg_ref, wu_ref, wd_ref, y_ref, wgb, wub, wdb):
    j = pl.program_id(1)

    @pl.when(j == 0)
    def _():
        wgb[...] = wg_ref[0, 0].astype(BF16)
        wub[...] = wu_ref[0, 0].astype(BF16)
        wdb[...] = wd_ref[0, 0].astype(BF16)

    def run(xs_ref, gs_ref):
        x = xs_ref[0]
        a = _dot(x, wgb[...])
        u = _dot(x, wub[...])
        hm = (a * _sigmoid(a) * u).astype(BF16)
        y_ref[0] = (_dot(hm, wdb[...]) * gs_ref[0]).astype(BF16)

    @pl.when(j < n_first)
    def _():
        run(xa_ref, ga_ref)

    @pl.when(j >= n_first)
    def _():
        run(xb_ref, gb_ref)


def _ffn(xa, ga, xb, gb, wg, wu, wd, l):
    rc = FFN_ROWS
    na = xa.shape[1] // rc
    nb = xb.shape[1] // rc
    first = lambda w: pl.BlockSpec((1, rc, w), lambda e, j: (e, jnp.minimum(j, na - 1), 0))
    second = lambda w: pl.BlockSpec((1, rc, w), lambda e, j: (e, jnp.maximum(j - na, 0), 0))
    wspec = lambda a, b: pl.BlockSpec((1, 1, a, b), lambda e, j: (l, e, 0, 0))
    return pl.pallas_call(
        functools.partial(_ffn_body, na),
        grid=(N_EXPERTS, na + nb),
        in_specs=[first(D_MODEL), first(1), second(D_MODEL), second(1),
                  wspec(D_MODEL, EXPERT_FF), wspec(D_MODEL, EXPERT_FF), wspec(EXPERT_FF, D_MODEL)],
        out_specs=pl.BlockSpec((1, rc, D_MODEL), lambda e, j: (e, j, 0)),
        out_shape=jax.ShapeDtypeStruct((N_EXPERTS, (na + nb) * rc, D_MODEL), BF16),
        scratch_shapes=[pltpu.VMEM((D_MODEL, EXPERT_FF), BF16),
                        pltpu.VMEM((D_MODEL, EXPERT_FF), BF16),
                        pltpu.VMEM((EXPERT_FF, D_MODEL), BF16)],
        compiler_params=_cparams(("arbitrary", "arbitrary")),
        name="ec_ffn",
    )(xa, ga, xb, gb, wg, wu, wd)


def _combine_body(cap, win, starts_ref, pos_ref, y_ref, x1_ref, mod_ref, g_ref, o_ref):
    b = pl.program_id(0)
    i = pl.program_id(1)
    pos = pos_ref[0].astype(BF16)
    tq = pos.shape[0]
    g2 = mod_ref[0][:, 5 * D_MODEL:6 * D_MODEL]

    def finish(w, bases, values, lane0, rows=slice(None)):
        n = N_EXPERTS * w
        col = lax.broadcasted_iota(jnp.int32, (1, n), 1)
        expert = jnp.zeros((1, n), jnp.int32)
        for e in range(1, N_EXPERTS):
            expert = expert + jnp.where(col >= e * w, 1, 0)
        target = col - expert * w
        if bases is not None:
            for e in range(N_EXPERTS):
                target = jnp.where(expert == e, target + bases[e], target)
        expand = jnp.where(lax.broadcasted_iota(jnp.int32, (LANES, n), 0) == expert + lane0, 1.0, 0.0)
        per_col = _dot(pos, expand.astype(BF16))
        onehot = jnp.where(per_col == target.astype(F32), 1.0, 0.0).astype(BF16)
        f = _dot(onehot, values)
        o_ref[rows, :] = x1_ref[rows, :] + g2 * _rms(f, g_ref[3:4, :])

    if win == cap:
        n_req = y_ref.shape[1] // cap
        for r in range(n_req):
            values = y_ref[:, r * cap:(r + 1) * cap, :].reshape(N_EXPERTS * cap, D_MODEL)
            finish(cap, None, values, ((b * n_req + r) % SEL_GROUP) * N_EXPERTS, slice(r * tq, (r + 1) * tq))
        return
    grp = b // SEL_GROUP
    lane0 = (b % SEL_GROUP) * N_EXPERTS
    bases = []
    fits = None
    for e in range(N_EXPERTS):
        s0 = starts_ref[grp, i, lane0 + e]
        s1 = starts_ref[grp, i + 1, lane0 + e]
        base = jnp.minimum((s0 // BF16_ROWS) * BF16_ROWS, cap - win)
        bases.append(base)
        ok = s1 - base <= win
        fits = ok if fits is None else jnp.logical_and(fits, ok)

    @pl.when(fits)
    def _():
        rows = [y_ref[e, pl.ds(pl.multiple_of(bases[e], BF16_ROWS), win), :] for e in range(N_EXPERTS)]
        finish(win, bases, jnp.concatenate(rows, axis=0), lane0)

    @pl.when(jnp.logical_not(fits))
    def _():
        finish(cap, None, y_ref[...].reshape(N_EXPERTS * cap, D_MODEL), lane0)


def _combine(starts, pos, y, row_off, x1, mod_l, mod_row_fn, g, B, L, shared_mod=False):
    cap = EC_CAPACITY * L // N_EXPERTS
    tq = TOKEN_TILE
    assert tq == SEL_CHUNK
    nq = L // tq
    blk_off = row_off // cap
    win = min(cap, COMBINE_WINDOW)
    nr = SMALL_STEP_REQUESTS if (shared_mod and win == cap and nq == 1 and blk_off % SMALL_STEP_REQUESTS == 0) else 1
    per_group = SEL_GROUP // nr
    grid_spec = pltpu.PrefetchScalarGridSpec(
        num_scalar_prefetch=1,
        grid=(B // nr, nq),
        in_specs=[pl.BlockSpec((1, tq, LANES), lambda b, i, s: (b // per_group, i, 0)),
                  pl.BlockSpec((N_EXPERTS, nr * cap, D_MODEL), lambda b, i, s: (0, blk_off // nr + b, 0)),
                  pl.BlockSpec((nr * tq, D_MODEL), lambda b, i, s: (b * nq + i, 0)),
                  pl.BlockSpec((1, 1, 6 * D_MODEL), lambda b, i, s: (mod_row_fn(b * nr), 0, 0)),
                  pl.BlockSpec((4, D_MODEL), lambda b, i, s: (0, 0))],
        out_specs=pl.BlockSpec((nr * tq, D_MODEL), lambda b, i, s: (b * nq + i, 0)),
    )
    return pl.pallas_call(
        functools.partial(_combine_body, cap, win),
        grid_spec=grid_spec,
        out_shape=jax.ShapeDtypeStruct((B * L, D_MODEL), F32),
        compiler_params=_cparams(("arbitrary", "arbitrary")),
        name="ec_combine",
    )(starts, pos, y, x1, mod_l, g)


def _rope_tables(L):
    t = jnp.arange(L)
    row = (t // GRID_W).astype(F32)
    col = (t % GRID_W).astype(F32)
    half = HEAD_DIM // 4
    inv = ROPE_BASE ** (-jnp.arange(half, dtype=F32) / half)
    d = np.arange(LANES) % HEAD_DIM
    use_col = (d // (HEAD_DIM // 2)) == 1
    pos = jnp.where(use_col[None, :], col[:, None], row[:, None])
    ang = pos * inv[d % half][None, :]
    sign = np.where((d % (HEAD_DIM // 2)) < half, -1.0, 1.0).astype(np.float32)
    return jnp.cos(ang), jnp.sin(ang) * sign[None, :]


def _permute_wa_heads(a, axis, start):
    cut = lambda lo, hi: lax.slice_in_dim(a, lo, hi, axis=axis)
    heads = [cut(start + h * HEAD_DIM, start + (h + 1) * HEAD_DIM) for h in WA_HEAD_ORDER]
    return jnp.concatenate([cut(0, start)] + heads + [cut(start + WA_WIDTH, a.shape[axis])], axis=axis)


def kernel(x_prompt, x_sample, c, cache_na_k, cache_na_v, cache_wa_k, cache_wa_v, state_ssm, c_ctx, w_ada, b_ada, norm_g, w_in, w_out, na_rpb, wa_sink, ssm_lambda_re, ssm_lambda_im, ssm_log_dt, ssm_b_re, ssm_b_im, ssm_c_re, ssm_c_im, ssm_d, w_glu, w_router, w_exp_gate, w_exp_up, w_exp_down):
    Bc, Lc, _ = x_prompt.shape
    Bs, Ls, _ = x_sample.shape
    ctx_row = Bs
    n_cond = ((Bs + 1 + SUBLANES - 1) // SUBLANES) * SUBLANES
    cond = jnp.zeros((n_cond, D_MODEL), F32).at[0:Bs].set(c).at[ctx_row].set(c_ctx)
    mod = _modulation(cond, w_ada, b_ada)
    mod = mod.reshape(DEPTH, n_cond, 1, 6 * D_MODEL)

    w_in_bf = _permute_wa_heads(w_in, 2, C_WAQ).astype(BF16)
    w_out_bf = _permute_wa_heads(w_out, 1, NA_WIDTH).astype(BF16)
    w_glu_bf = w_glu.astype(BF16)
    wr_hi = w_router.astype(BF16)
    wr_lo = (w_router - wr_hi.astype(F32)).astype(BF16)
    wr_pad = jnp.pad(jnp.concatenate([wr_hi, wr_lo], axis=-1), ((0, 0), (0, 0), (0, LANES - 2 * N_EXPERTS)))
    s5_a, s5_bb = _s5_prep(ssm_lambda_re, ssm_lambda_im, ssm_log_dt, ssm_b_re, ssm_b_im)
    s5_c = _s5_c_matrix(ssm_c_re, ssm_c_im)
    rope_tabs = _rope_tables(Ls)

    ctx_mod_row = lambda b: ctx_row + 0 * b
    lat_mod_row = lambda b: b

    xp = x_prompt.reshape(Bc * Lc, D_MODEL)
    xs = x_sample.reshape(Bs * Ls, D_MODEL)
    new_kv = tuple(jnp.zeros((Bc, DEPTH, Lc, w), F32) for w in (NA_WIDTH, NA_WIDTH, WA_KV_WIDTH, WA_KV_WIDTH))
    new_ssm = []
    for l in range(DEPTH):
        g = norm_g[l]
        dsk = ssm_d[l].reshape(1, SSM_CH)
        sl2 = slice(2 * l, 2 * l + 2)
        qna, kna, vna, qwa, kwa, vwa, u = _inproj(xp, mod, ctx_mod_row, norm_g, w_in_bf, l, Bc, Lc, None, new_kv)
        new_kv = (kna, vna, kwa, vwa)
        ona, owa = _attn_ctx(wa_sink[l], qna, kna, vna, qwa, kwa, vwa, l, Bc, Lc)
        y2, hfin = _s5_scan(u, s5_bb[sl2], s5_a[sl2], s5_c[sl2],
                            jnp.zeros((2, Bc, 2 * SSM_N), F32), Bc, Lc)
        x1, h2, aff = _outproj(ona, owa, y2, u, dsk, w_glu_bf[l], w_out_bf[l], xp, mod[l], ctx_mod_row,
                               g, wr_pad[l], Lc)
        pos_c, post, afft, starts_c = _select(aff, Bc, Lc)
        xg_c, gs_c = _gather(starts_c, h2, post, afft, Bc, Lc)
        x1_c = x1
        new_ssm.append(jnp.transpose(hfin.reshape(2, Bc, 2, SSM_GROUPS, SSM_STATE), (1, 0, 2, 3, 4)))
        qna, kna, vna, qwa, kwa, vwa, u = _inproj(xs, mod, lat_mod_row, norm_g, w_in_bf, l, Bs, Ls, rope_tabs)
        ona = _attn_na(qna, kna, vna, cache_na_k, cache_na_v, _na_bias_pairs(na_rpb[l]), l, Bs, Ls)
        owa = _attn_wa(wa_sink[l], qwa, kwa, vwa, cache_wa_k, cache_wa_v, l, Bs, Ls)
        h0 = jnp.transpose(state_ssm[:, l].reshape(Bs, 2, 2 * SSM_N), (1, 0, 2))
        y2, _ = _s5_scan(u, s5_bb[sl2], s5_a[sl2], s5_c[sl2], h0, Bs, Ls)
        x1, h2, aff = _outproj(ona, owa, y2, u, dsk, w_glu_bf[l], w_out_bf[l], xs, mod[l], lat_mod_row,
                               g, wr_pad[l], Ls)
        pos_s, post, afft, starts_s = _select(aff, Bs, Ls)
        xg_s, gs_s = _gather(starts_s, h2, post, afft, Bs, Ls)
        yy = _ffn(xg_c, gs_c, xg_s, gs_s, w_exp_gate, w_exp_up, w_exp_down, l)
        xp = _combine(starts_c, pos_c, yy, 0, x1_c, mod[l], ctx_mod_row, g, Bc, Lc, shared_mod=True)
        xs = _combine(starts_s, pos_s, yy, xg_c.shape[1], x1, mod[l], lat_mod_row, g, Bs, Ls)
    return (xp.reshape(Bc, Lc, D_MODEL), xs.reshape(Bs, Ls, D_MODEL),
            new_kv[0].reshape(Bc, DEPTH, Lc, NA_HEADS, HEAD_DIM),
            new_kv[1].reshape(Bc, DEPTH, Lc, NA_HEADS, HEAD_DIM),
            new_kv[2].reshape(Bc, DEPTH, Lc, WA_KV_HEADS, HEAD_DIM),
            new_kv[3].reshape(Bc, DEPTH, Lc, WA_KV_HEADS, HEAD_DIM),
            jnp.stack(new_ssm, axis=1))
```

```python
import functools
import math

import numpy as np
import jax
import jax.numpy as jnp
from jax import lax
from jax.experimental import pallas as pl
from jax.experimental.pallas import tpu as pltpu

F32 = jnp.float32
BF16 = jnp.bfloat16

D_MODEL = 1024
DEPTH = 2
GRID_W = 64
HEAD_DIM = 64
NA_HEADS = 6
NA_KH = 8
NA_KW = 16
WA_HEADS = 6
WA_KV_HEADS = 2
WA_WINDOW = 128
WA_QBLOCK = 256
SSM_CH = 256
SSM_GROUP_CH = 16
SSM_GROUPS = 16
SSM_STATE = 64
SSM_N = SSM_GROUPS * SSM_STATE
NA_WIDTH = NA_HEADS * HEAD_DIM
WA_WIDTH = WA_HEADS * HEAD_DIM
WA_KV_WIDTH = WA_KV_HEADS * HEAD_DIM
IN_COLS = 2048
N_EXPERTS = 16
EC_CAPACITY = 2
EXPERT_FF = 1024
ROPE_BASE = 10000.0
RMS_EPS = 1e-6
NEG_INF = -1e30
ATTN_SCALE = HEAD_DIM ** -0.5
LOG2E = math.log2(math.e)
Q_SCALE = ATTN_SCALE * LOG2E

LANES = 128
SUBLANES = 8
VMEM_LIMIT = 48 * 1024 * 1024

C_NAQ, C_NAK, C_NAV, C_WAQ, C_WAK, C_WAV, C_SSU = 0, 384, 768, 1152, 1536, 1664, 1792
WA_HEAD_ORDER = (0, 3, 1, 4, 2, 5)

NA_QROWS = 4
NA_KROWS = NA_QROWS + NA_KH
NA_REL = 2 * NA_KH - 1
NA_PAIRS = 3 * NA_REL
TOKEN_TILE = 256
OUTPROJ_SUBTILE = 128
PROJ_TILE = 512
SCAN_TILE = 128
SMALL_STEP_REQUESTS = 4
CTX_ATTN_REQUESTS = 4
SCAN_BATCH = SUBLANES
FFN_ROWS = 512
SEL_CHUNK = 256
SEL_GROUP = LANES // N_EXPERTS
BF16_ROWS = 2 * SUBLANES
COMBINE_WINDOW = 64


def _cparams(sem):
    return pltpu.CompilerParams(dimension_semantics=sem, vmem_limit_bytes=VMEM_LIMIT)


def _sigmoid(x):
    return 1.0 / (1.0 + jnp.exp(-x))


def _rms(x, g):
    ms = jnp.mean(x * x, axis=-1, keepdims=True)
    return x * lax.rsqrt(ms + RMS_EPS) * g


def _dot(a, b):
    return jnp.dot(a, b, preferred_element_type=F32)


def _dot_nt(a, b):
    return lax.dot_general(a, b, (((1,), (1,)), ((), ())), preferred_element_type=F32)


def _mod_body(c_ref, w_ref, b_ref, o_ref):
    c = c_ref[...]
    s = c * _sigmoid(c)
    o_ref[0] = _dot(s.astype(BF16), w_ref[0].astype(BF16)) + b_ref[0]


def _modulation(cond, w_ada, b_ada):
    n = cond.shape[0]
    tn = 1024
    return pl.pallas_call(
        _mod_body,
        grid=(DEPTH, 6 * D_MODEL // tn),
        in_specs=[
            pl.BlockSpec((n, D_MODEL), lambda l, j: (0, 0)),
            pl.BlockSpec((1, D_MODEL, tn), lambda l, j: (l, 0, j)),
            pl.BlockSpec((1, 1, tn), lambda l, j: (l, 0, j)),
        ],
        out_specs=pl.BlockSpec((1, n, tn), lambda l, j: (l, 0, j)),
        out_shape=jax.ShapeDtypeStruct((DEPTH, n, 6 * D_MODEL), F32),
        compiler_params=_cparams(("arbitrary", "arbitrary")),
        name="modulation",
    )(cond, w_ada, b_ada.reshape(DEPTH, 1, 6 * D_MODEL))


def _rope_chunk(x, cos, sin_signed):
    lane = lax.broadcasted_iota(jnp.int32, x.shape, 1)
    first = (lane % 32) < 16
    rot = jnp.where(first, pltpu.roll(x, LANES - 16, 1), pltpu.roll(x, 16, 1))
    return x * cos + rot * sin_signed


INPROJ_KV_OUTPUTS = (1, 2, 4, 5)


def _inproj_body(rope, n_alias, x_ref, mod_ref, g_ref, w_ref, *rest):
    if rope:
        cos_ref, sin_ref = rest[:2]
        rest = rest[2:]
    qna_ref, kna_ref, vna_ref, qwa_ref, kwa_ref, vwa_ref, u_ref = rest[n_alias:]

    def put(ref, val):
        ref[...] = val.reshape(ref.shape)

    x = x_ref[...]
    m = mod_ref[0, 0]
    sh1 = m[:, 0:D_MODEL]
    sc1 = m[:, D_MODEL:2 * D_MODEL]
    h = _rms(x, g_ref[0, 0:1, :]) * (1.0 + sc1) + sh1
    z = _dot(h.astype(BF16), w_ref[0])
    qna_ref[...] = (z[:, C_NAQ:C_NAK] * Q_SCALE).astype(BF16)
    put(kna_ref, z[:, C_NAK:C_NAV])
    put(vna_ref, z[:, C_NAV:C_WAQ])
    put(vwa_ref, z[:, C_WAV:C_SSU])
    u_ref[...] = z[:, C_SSU:IN_COLS]
    if rope:
        cos = cos_ref[...]
        sin = sin_ref[...]
        for j in range(WA_WIDTH // LANES):
            qc = _rope_chunk(z[:, C_WAQ + j * LANES:C_WAQ + (j + 1) * LANES], cos, sin)
            qwa_ref[:, j * LANES:(j + 1) * LANES] = (qc * Q_SCALE).astype(BF16)
        put(kwa_ref, _rope_chunk(z[:, C_WAK:C_WAV], cos, sin))
    else:
        qwa_ref[...] = (z[:, C_WAQ:C_WAK] * Q_SCALE).astype(BF16)
        put(kwa_ref, z[:, C_WAK:C_WAV])


def _inproj(x, mod, mod_row_fn, norm_g, w_bf, l, B, L, rope_tabs, kv_layers=None):
    T = x.shape[0]
    tm = min(PROJ_TILE, L)
    tiles_per_seq = L // tm
    in_specs = [
        pl.BlockSpec((tm, D_MODEL), lambda i: (i, 0)),
        pl.BlockSpec((1, 1, 1, 6 * D_MODEL), lambda i: (l, mod_row_fn(i // tiles_per_seq), 0, 0)),
        pl.BlockSpec((1, 4, D_MODEL), lambda i: (l, 0, 0)),
        pl.BlockSpec((1, D_MODEL, IN_COLS), lambda i: (l, 0, 0)),
    ]
    args = [x, mod, norm_g, w_bf]
    if rope_tabs is not None:
        in_specs += [pl.BlockSpec((tm, LANES), lambda i: (i % tiles_per_seq, 0))] * 2
        args += list(rope_tabs)
    widths = (NA_WIDTH, NA_WIDTH, NA_WIDTH, WA_WIDTH, WA_KV_WIDTH, WA_KV_WIDTH, SSM_CH)
    dtypes = (BF16, F32, F32, BF16, F32, F32, F32)
    out_specs = [pl.BlockSpec((tm, w), lambda i: (i, 0)) for w in widths]
    out_shape = [jax.ShapeDtypeStruct((T, w), dt) for w, dt in zip(widths, dtypes)]
    aliases = {}
    n_alias = 0
    if kv_layers is not None:
        for o in INPROJ_KV_OUTPUTS:
            out_specs[o] = pl.BlockSpec((1, 1, tm, widths[o]),
                                        lambda i: (i // tiles_per_seq, l, i % tiles_per_seq, 0))
            out_shape[o] = jax.ShapeDtypeStruct((B, DEPTH, L, widths[o]), dtypes[o])
        n_alias = len(kv_layers)
        for k, (o, prev) in enumerate(zip(INPROJ_KV_OUTPUTS, kv_layers)):
            aliases[len(args)] = o
            in_specs.append(pl.BlockSpec(memory_space=pl.ANY))
            args.append(prev)
    return pl.pallas_call(
        functools.partial(_inproj_body, rope_tabs is not None, n_alias),
        grid=(T // tm,),
        in_specs=in_specs,
        out_specs=out_specs,
        out_shape=out_shape,
        input_output_aliases=aliases,
        compiler_params=_cparams(("arbitrary",)),
        name="inproj",
    )(*args)


def _lane_lo(shape):
    return lax.broadcasted_iota(jnp.int32, shape, len(shape) - 1) < HEAD_DIM


def _keep_half(q, lo, half):
    keep = jnp.where(lo, 1.0 - half, 0.0 + half).astype(q.dtype)
    return q * keep


def _values_and_ones(v, lo, half):
    one = jnp.ones_like(v)
    return (jnp.where(lo, v, one) if half == 0 else jnp.where(lo, one, v)).astype(BF16)


def _normalise(o2, extra=None):
    den = pltpu.roll(o2, HEAD_DIM, 1)
    if extra is not None:
        den = den + extra
    return o2 / den


def _attn_ctx_body(sink_ref, qna, kna, vna, qwa, kwa, vwa, ona, owa):
    R, L = qna.shape[0], qna.shape[1]
    npair = WA_WIDTH // LANES
    lo = _lane_lo((L, LANES))
    lo_all = _lane_lo((npair * L, LANES))
    blk = lax.broadcasted_iota(jnp.int32, (npair * L, 1), 0) // L
    for r in range(R):
        for j in range(NA_WIDTH // LANES):
            sl = slice(j * LANES, (j + 1) * LANES)
            q2 = qna[r, :, sl]
            k2 = kna[r, 0, :, sl].astype(BF16)
            v2 = vna[r, 0, :, sl].astype(BF16)
            halves = []
            for half in range(2):
                s = _dot_nt(_keep_half(q2, lo, half), k2)
                p = jnp.exp2(s - jnp.max(s, axis=-1, keepdims=True))
                halves.append(_dot(p.astype(BF16), v2) / jnp.sum(p, axis=-1, keepdims=True))
            ona[r, :, sl] = jnp.where(lo, halves[0], halves[1]).astype(BF16)
        kw = kwa[r, 0].astype(BF16)
        vw = vwa[r, 0].astype(BF16)
        q_all = jnp.concatenate([qwa[r, :, j * LANES:(j + 1) * LANES] for j in range(npair)], axis=0)
        outs = []
        for g in range(WA_KV_HEADS):
            s = _dot_nt(_keep_half(q_all, lo_all, g), kw)
            sk = jnp.zeros((npair * L, 1), F32)
            for j in range(npair):
                sk = jnp.where(blk == j, sink_ref[WA_HEAD_ORDER[2 * j + g]] * LOG2E, sk)
            m = jnp.maximum(jnp.max(s, axis=-1, keepdims=True), sk)
            p = jnp.exp2(s - m)
            l = jnp.sum(p, axis=-1, keepdims=True) + jnp.exp2(sk - m)
            outs.append(_dot(p.astype(BF16), vw) / l)
        o_all = jnp.where(lo_all, outs[0], outs[1]).astype(BF16)
        for j in range(npair):
            owa[r, :, j * LANES:(j + 1) * LANES] = o_all[j * L:(j + 1) * L]


def _attn_ctx(sink, qna, kna, vna, qwa, kwa, vwa, l, B, L):
    R = CTX_ATTN_REQUESTS

    def spec(w):
        return pl.BlockSpec((R, L, w), lambda b: (b, 0, 0))

    def layer_spec(w):
        return pl.BlockSpec((R, 1, L, w), lambda b: (b, l, 0, 0))

    r3 = lambda a: a.reshape(B, L, a.shape[-1])
    ona, owa = pl.pallas_call(
        _attn_ctx_body,
        grid=(B // R,),
        in_specs=[pl.BlockSpec(memory_space=pltpu.SMEM),
                  spec(NA_WIDTH), layer_spec(NA_WIDTH), layer_spec(NA_WIDTH),
                  spec(WA_WIDTH), layer_spec(WA_KV_WIDTH), layer_spec(WA_KV_WIDTH)],
        out_specs=[spec(NA_WIDTH), spec(WA_WIDTH)],
        out_shape=[jax.ShapeDtypeStruct((B, L, NA_WIDTH), BF16),
                   jax.ShapeDtypeStruct((B, L, WA_WIDTH), BF16)],
        compiler_params=_cparams(("arbitrary",)),
        name="attn_ctx",
    )(sink, r3(qna), kna, vna, r3(qwa), kwa, vwa)
    return ona.reshape(B * L, NA_WIDTH), owa.reshape(B * L, WA_WIDTH)


def _na_key_start(rb, rows):
    return jnp.clip(rb * NA_QROWS - NA_KH // 2, 0, rows - NA_KROWS)


def _attn_na_body(rows, idx_ref, q_ref, k_ref, v_ref, kc_ref, vc_ref, pairs_ref, o_ref):
    rb = pl.program_id(1)
    nq = NA_QROWS * GRID_W
    nk = NA_KROWS * GRID_W
    start = pl.multiple_of(_na_key_start(rb, rows) * GRID_W, GRID_W)
    lo = _lane_lo((nq, LANES))
    lo_v = _lane_lo((1, LANES))
    npair = NA_KROWS // 2
    pat = jnp.where(rb == 0, 0, jnp.where(rb == rows // NA_QROWS - 1, 2, 1))
    entry = [[idx_ref[(pat * NA_QROWS + i) * npair + jj] for jj in range(npair)] for i in range(NA_QROWS)]

    def bias(head):
        return jnp.concatenate(
            [jnp.concatenate([pairs_ref[head, entry[i][jj]] for jj in range(npair)], axis=1)
             for i in range(NA_QROWS)], axis=0)

    for j in range(NA_WIDTH // LANES):
        sl = slice(j * LANES, (j + 1) * LANES)
        q2 = q_ref[0, :, sl]
        kl = k_ref[0, pl.ds(start, nk), sl].astype(BF16)
        vl = v_ref[0, pl.ds(start, nk), sl]
        kc = kc_ref[0, 0, :, sl].astype(BF16)
        vc = vc_ref[0, 0, :, sl]
        scores = []
        for half in range(2):
            qm = _keep_half(q2, lo, half)
            scores.append((_dot_nt(qm, kl) + bias(2 * j + half), _dot_nt(qm, kc)))
        probs = []
        for s_loc, s_ctx in scores:
            m = jnp.maximum(jnp.max(s_loc, axis=-1, keepdims=True),
                            jnp.max(s_ctx, axis=-1, keepdims=True))
            probs.append((jnp.exp2(s_loc - m).astype(BF16), jnp.exp2(s_ctx - m).astype(BF16)))
        halves = []
        for half, (p_loc, p_ctx) in enumerate(probs):
            o2 = (_dot(p_loc, _values_and_ones(vl, lo_v, half))
                  + _dot(p_ctx, _values_and_ones(vc, lo_v, half)))
            halves.append(_normalise(o2))
        o_ref[0, :, sl] = jnp.where(lo, halves[0], halves[1]).astype(BF16)


def _na_bias_pairs(rpb_l):
    cq = np.arange(GRID_W)
    col_start = np.clip(cq - NA_KW // 2, 0, GRID_W - NA_KW)
    col_ok = (cq[None, :] >= col_start[:, None]) & (cq[None, :] < col_start[:, None] + NA_KW)
    dcol = np.clip(cq[None, :] - cq[:, None], -(NA_KW - 1), NA_KW - 1) + (NA_KW - 1)
    pick = (dcol[None] == np.arange(2 * NA_KW - 1)[:, None, None]).astype(np.float32)
    tiles = jnp.einsum('hrd,dqc->hrqc', rpb_l.astype(F32) * LOG2E, pick, precision=lax.Precision.HIGHEST)
    tiles = jnp.where(col_ok[None, None], tiles, NEG_INF)
    masked = jnp.full_like(tiles, NEG_INF)
    both = jnp.concatenate([tiles[:, :-1], tiles[:, 1:]], axis=-1)
    right = jnp.concatenate([masked, tiles], axis=-1)
    left = jnp.concatenate([tiles, masked], axis=-1)
    none = jnp.concatenate([masked[:, :1], masked[:, :1]], axis=-1)
    return jnp.concatenate([both, right, left, none], axis=1)


def _na_pair_index(rows):
    nrb = rows // NA_QROWS
    idx = np.zeros((3, NA_QROWS, NA_KROWS // 2), np.int32)
    for pi, rb in enumerate((0, 1, nrb - 1)):
        ks = int(np.clip(rb * NA_QROWS - NA_KH // 2, 0, rows - NA_KROWS))
        for i in range(NA_QROWS):
            qr = rb * NA_QROWS + i
            rs = int(np.clip(qr - NA_KH // 2, 0, rows - NA_KH))
            for jj in range(NA_KROWS // 2):
                kr = ks + 2 * jj
                r = kr - qr + NA_KH - 1
                ok_l = rs <= kr < rs + NA_KH
                ok_r = rs <= kr + 1 < rs + NA_KH
                if ok_l and ok_r:
                    idx[pi, i, jj] = r
                elif ok_r:
                    idx[pi, i, jj] = (NA_REL - 1) + (r + 1)
                elif ok_l:
                    idx[pi, i, jj] = (NA_REL - 1) + NA_REL + r
                else:
                    idx[pi, i, jj] = NA_PAIRS - 1
    return jnp.asarray(idx.reshape(-1))


def _attn_na(q, k, v, cache_k, cache_v, pairs, l, B, L):
    rows = L // GRID_W
    nrb = rows // NA_QROWS
    nq = NA_QROWS * GRID_W
    P = cache_k.shape[2]
    r3 = lambda a: a.reshape(B, L, a.shape[-1])
    ck = cache_k.reshape(B, DEPTH, P, NA_WIDTH)
    cv = cache_v.reshape(B, DEPTH, P, NA_WIDTH)
    grid_spec = pltpu.PrefetchScalarGridSpec(
        num_scalar_prefetch=1,
        grid=(B, nrb),
        in_specs=[
            pl.BlockSpec((1, nq, NA_WIDTH), lambda b, r, s: (b, r, 0)),
            pl.BlockSpec((1, L, NA_WIDTH), lambda b, r, s: (b, 0, 0)),
            pl.BlockSpec((1, L, NA_WIDTH), lambda b, r, s: (b, 0, 0)),
            pl.BlockSpec((1, 1, P, NA_WIDTH), lambda b, r, s: (b, l, 0, 0)),
            pl.BlockSpec((1, 1, P, NA_WIDTH), lambda b, r, s: (b, l, 0, 0)),
            pl.BlockSpec(pairs.shape, lambda b, r, s: (0, 0, 0, 0)),
        ],
        out_specs=pl.BlockSpec((1, nq, NA_WIDTH), lambda b, r, s: (b, r, 0)),
    )
    o = pl.pallas_call(
        functools.partial(_attn_na_body, rows),
        grid_spec=grid_spec,
        out_shape=jax.ShapeDtypeStruct((B, L, NA_WIDTH), BF16),
        compiler_params=_cparams(("arbitrary", "arbitrary")),
        name="attn_na",
    )(_na_pair_index(rows), r3(q), r3(k), r3(v), ck, cv, pairs)
    return o.reshape(B * L, NA_WIDTH)


def _attn_wa_body(L, sink_ref, q_ref, k_ref, v_ref, kc_ref, vc_ref, o_ref):
    n = pl.program_id(1)
    nk = WA_QBLOCK + 2 * WA_WINDOW
    npair = WA_WIDTH // LANES
    start = pl.multiple_of(jnp.clip(n * WA_QBLOCK - WA_WINDOW, 0, L - nk), WA_WINDOW)
    kl = k_ref[0, pl.ds(start, nk), :].astype(BF16)
    vl = v_ref[0, pl.ds(start, nk), :]
    kc = kc_ref[0, 0].astype(BF16)
    vc = vc_ref[0, 0]
    lo_v = _lane_lo((1, LANES))
    rows = npair * WA_QBLOCK
    qpos = n * WA_QBLOCK + lax.broadcasted_iota(jnp.int32, (rows, nk), 0) % WA_QBLOCK
    kpos = start + lax.broadcasted_iota(jnp.int32, (rows, nk), 1)
    in_win = jnp.abs(kpos - qpos) <= WA_WINDOW
    lo = _lane_lo((rows, LANES))
    q_all = jnp.concatenate([q_ref[0, :, j * LANES:(j + 1) * LANES] for j in range(npair)], axis=0)
    blk = lax.broadcasted_iota(jnp.int32, (rows, 1), 0) // WA_QBLOCK
    scores = []
    for g in range(WA_KV_HEADS):
        qm = _keep_half(q_all, lo, g)
        scores.append((jnp.where(in_win, _dot_nt(qm, kl), NEG_INF), _dot_nt(qm, kc)))
    probs = []
    for g in range(WA_KV_HEADS):
        s_loc, s_ctx = scores[g]
        sk = jnp.zeros((rows, 1), F32)
        for j in range(npair):
            sk = jnp.where(blk == j, sink_ref[WA_HEAD_ORDER[2 * j + g]] * LOG2E, sk)
        m = jnp.maximum(jnp.maximum(jnp.max(s_loc, axis=-1, keepdims=True),
                                    jnp.max(s_ctx, axis=-1, keepdims=True)), sk)
        probs.append((jnp.exp2(s_loc - m).astype(BF16), jnp.exp2(s_ctx - m).astype(BF16), jnp.exp2(sk - m)))
    outs = []
    for g in range(WA_KV_HEADS):
        p_loc, p_ctx, p_sink = probs[g]
        o2 = _dot(p_loc, _values_and_ones(vl, lo_v, g)) + _dot(p_ctx, _values_and_ones(vc, lo_v, g))
        outs.append(_normalise(o2, p_sink))
    o_all = jnp.where(lo, outs[0], outs[1]).astype(BF16)
    for j in range(npair):
        o_ref[0, :, j * LANES:(j + 1) * LANES] = o_all[j * WA_QBLOCK:(j + 1) * WA_QBLOCK]


def _attn_wa(sink, q, k, v, cache_k, cache_v, l, B, L):
    nb = L // WA_QBLOCK
    P = cache_k.shape[2]
    r3 = lambda a: a.reshape(B, L, a.shape[-1])
    ck = cache_k.reshape(B, DEPTH, P, WA_KV_WIDTH)
    cv = cache_v.reshape(B, DEPTH, P, WA_KV_WIDTH)
    o = pl.pallas_call(
        functools.partial(_attn_wa_body, L),
        grid=(B, nb),
        in_specs=[
            pl.BlockSpec(memory_space=pltpu.SMEM),
            pl.BlockSpec((1, WA_QBLOCK, WA_WIDTH), lambda b, n: (b, n, 0)),
            pl.BlockSpec((1, L, WA_KV_WIDTH), lambda b, n: (b, 0, 0)),
            pl.BlockSpec((1, L, WA_KV_WIDTH), lambda b, n: (b, 0, 0)),
            pl.BlockSpec((1, 1, P, WA_KV_WIDTH), lambda b, n: (b, l, 0, 0)),
            pl.BlockSpec((1, 1, P, WA_KV_WIDTH), lambda b, n: (b, l, 0, 0)),
        ],
        out_specs=pl.BlockSpec((1, WA_QBLOCK, WA_WIDTH), lambda b, n: (b, n, 0)),
        out_shape=jax.ShapeDtypeStruct((B, L, WA_WIDTH), BF16),
        compiler_params=_cparams(("arbitrary", "arbitrary")),
        name="attn_wa",
    )(sink, r3(q), r3(k), r3(v), ck, cv)
    return o.reshape(B * L, WA_WIDTH)


def _s5_prep_body(lr_ref, li_ref, ldt_ref, br_ref, bi_ref, a_ref, bb_ref):
    lr = lr_ref[0]
    li = li_ref[0]
    dt = jnp.exp(ldt_ref[0])
    mag = jnp.exp(lr * dt)
    ar = mag * jnp.cos(li * dt)
    ai = mag * jnp.sin(li * dt)
    den = lr * lr + li * li
    nr = ar - 1.0
    fr = (nr * lr + ai * li) / den
    fi = (ai * lr - nr * li) / den
    a_ref[0, :, 0:SSM_N] = ar
    a_ref[0, :, SSM_N:2 * SSM_N] = ai
    br = br_ref[0]
    bi = bi_ref[0]
    bb_ref[0, :, 0:SSM_N] = (fr * br - fi * bi).astype(BF16)
    bb_ref[0, :, SSM_N:2 * SSM_N] = (fr * bi + fi * br).astype(BF16)


def _s5_prep(lam_re, lam_im, log_dt, b_re, b_im):
    n = DEPTH * 2
    eye = jnp.eye(SSM_GROUPS, dtype=F32)

    def blockdiag_b(b):
        return jnp.einsum('ngph,gk->nghkp', b.reshape(n, SSM_GROUPS, SSM_STATE, SSM_GROUP_CH), eye
                          ).reshape(n, SSM_CH, SSM_N)

    lr = lam_re.reshape(n, 1, SSM_N)
    li = lam_im.reshape(n, 1, SSM_N)
    ldt = jnp.repeat(log_dt.reshape(n, SSM_GROUPS), SSM_STATE, axis=-1).reshape(n, 1, SSM_N)
    vec = pl.BlockSpec((1, 1, SSM_N), lambda i: (i, 0, 0))
    mat = pl.BlockSpec((1, SSM_CH, SSM_N), lambda i: (i, 0, 0))
    return pl.pallas_call(
        _s5_prep_body,
        grid=(n,),
        in_specs=[vec, vec, vec, mat, mat],
        out_specs=[pl.BlockSpec((1, 1, 2 * SSM_N), lambda i: (i, 0, 0)),
                   pl.BlockSpec((1, SSM_CH, 2 * SSM_N), lambda i: (i, 0, 0))],
        out_shape=[jax.ShapeDtypeStruct((n, 1, 2 * SSM_N), F32),
                   jax.ShapeDtypeStruct((n, SSM_CH, 2 * SSM_N), BF16)],
        compiler_params=_cparams(("arbitrary",)),
        name="s5_prep",
    )(lr, li, ldt, blockdiag_b(b_re), blockdiag_b(b_im))


def _s5_c_matrix(c_re, c_im):
    n = DEPTH * 2
    eye = jnp.eye(SSM_GROUPS, dtype=F32)

    def blk(c):
        return jnp.einsum('nghp,gk->ngpkh', c.reshape(n, SSM_GROUPS, SSM_GROUP_CH, SSM_STATE), eye
                          ).reshape(n, SSM_N, SSM_CH)

    return jnp.concatenate([blk(c_re), -blk(c_im)], axis=1).astype(BF16)


def _s5_scan_body(tl, u_ref, bb_ref, a_ref, c_ref, h0_ref, y_ref, hfin_ref, bu, hb, hst):
    d = pl.program_id(0)
    i = pl.program_id(2)
    sb = SCAN_BATCH
    re = slice(0, SSM_N)
    im = slice(SSM_N, 2 * SSM_N)

    @pl.when(i == 0)
    def _():
        hst[...] = h0_ref[0]

    u_tb = jnp.transpose(u_ref[...], (1, 0, 2)).reshape(tl * sb, SSM_CH)
    u_bf = u_tb.astype(BF16)
    half = tl * sb // 2
    bu[0:half, :] = _dot(u_bf[0:half], bb_ref[0])
    bu[half:, :] = _dot(u_bf[half:], bb_ref[0])
    a = a_ref[0]
    ar = jnp.broadcast_to(a[:, re], (sb, SSM_N))
    ai = jnp.broadcast_to(a[:, im], (sb, SSM_N))

    def scan(reverse):
        def pair(s, carry):
            hr, hi = carry
            t0 = tl - 1 - 2 * s if reverse else 2 * s
            t1 = t0 - 1 if reverse else t0 + 1
            r0 = pl.ds(pl.multiple_of(t0 * sb, sb), sb)
            r1 = pl.ds(pl.multiple_of(t1 * sb, sb), sb)
            h0r = ar * hr - ai * hi + bu[r0, re]
            h0i = ar * hi + ai * hr + bu[r0, im]
            h1r = ar * h0r - ai * h0i + bu[r1, re]
            h1i = ar * h0i + ai * h0r + bu[r1, im]
            rows = pl.ds(pl.multiple_of((t1 if reverse else t0) * sb, 2 * sb), 2 * sb)
            early, late = ((h1r, h1i), (h0r, h0i)) if reverse else ((h0r, h0i), (h1r, h1i))
            hb[rows, re] = jnp.concatenate([early[0], late[0]], axis=0).astype(BF16)
            hb[rows, im] = jnp.concatenate([early[1], late[1]], axis=0).astype(BF16)
            return h1r, h1i

        hr, hi = lax.fori_loop(0, tl // 2, pair, (hst[:, re], hst[:, im]))
        hst[:, re] = hr
        hst[:, im] = hi
        hfin_ref[0, :, re] = hr
        hfin_ref[0, :, im] = hi

    @pl.when(d == 0)
    def _():
        scan(False)

    @pl.when(d == 1)
    def _():
        scan(True)

    y_tb = jnp.concatenate([_dot(hb[0:half, :], c_ref[0]), _dot(hb[half:, :], c_ref[0])], axis=0)
    y_ref[0] = jnp.transpose(y_tb.reshape(tl, sb, SSM_CH), (1, 0, 2))


def _s5_scan(u, bb, a, cm, h0, B, L):
    tl = SCAN_TILE
    nt = L // tl
    tile = lambda d, i: i + d * (nt - 1 - 2 * i)
    return pl.pallas_call(
        functools.partial(_s5_scan_body, tl),
        grid=(2, B // SCAN_BATCH, nt),
        in_specs=[
            pl.BlockSpec((SCAN_BATCH, tl, SSM_CH), lambda d, b, i: (b, tile(d, i), 0)),
            pl.BlockSpec((1, SSM_CH, 2 * SSM_N), lambda d, b, i: (d, 0, 0)),
            pl.BlockSpec((1, 1, 2 * SSM_N), lambda d, b, i: (d, 0, 0)),
            pl.BlockSpec((1, 2 * SSM_N, SSM_CH), lambda d, b, i: (d, 0, 0)),
            pl.BlockSpec((1, SCAN_BATCH, 2 * SSM_N), lambda d, b, i: (d, b, 0)),
        ],
        out_specs=[
            pl.BlockSpec((1, SCAN_BATCH, tl, SSM_CH), lambda d, b, i: (d, b, tile(d, i), 0)),
            pl.BlockSpec((1, SCAN_BATCH, 2 * SSM_N), lambda d, b, i: (d, b, 0)),
        ],
        out_shape=[jax.ShapeDtypeStruct((2, B, L, SSM_CH), F32),
                   jax.ShapeDtypeStruct((2, B, 2 * SSM_N), F32)],
        scratch_shapes=[pltpu.VMEM((SCAN_BATCH * tl, 2 * SSM_N), F32),
                        pltpu.VMEM((SCAN_BATCH * tl, 2 * SSM_N), BF16),
                        pltpu.VMEM((SCAN_BATCH, 2 * SSM_N), F32)],
        compiler_params=_cparams(("arbitrary", "arbitrary", "arbitrary")),
        name="s5_scan",
    )(u.reshape(B, L, SSM_CH), bb, a, cm, h0)


def _outproj_body(ona_ref, owa_ref, y_ref, u_ref, dsk_ref, wglu_ref, wout_ref, x_ref, mod_ref, g_ref,
                  wr_ref, x1_ref, h2_ref, aff_ref):
    m = mod_ref[0]
    g1 = m[:, 2 * D_MODEL:3 * D_MODEL]
    sh2 = m[:, 3 * D_MODEL:4 * D_MODEL]
    sc2 = m[:, 4 * D_MODEL:5 * D_MODEL]
    tm = x_ref.shape[0]
    sub = min(tm, OUTPROJ_SUBTILE)
    parts = [slice(r, r + sub) for r in range(0, tm, sub)]
    mixes = []
    for rs in parts:
        y = dsk_ref[...] * u_ref[rs, :] + y_ref[0, rs, :] + y_ref[1, rs, :]
        zg = _dot(y.astype(BF16), wglu_ref[...])
        oss = zg[:, 0:SSM_CH] * _sigmoid(zg[:, SSM_CH:2 * SSM_CH])
        mixes.append(_dot(ona_ref[rs, :], wout_ref[0:NA_WIDTH, :])
                     + _dot(owa_ref[rs, :], wout_ref[NA_WIDTH:NA_WIDTH + WA_WIDTH, :])
                     + _dot(oss.astype(BF16), wout_ref[NA_WIDTH + WA_WIDTH:D_MODEL, :]))
    split = []
    for rs, mix in zip(parts, mixes):
        x1 = x_ref[rs, :] + g1 * _rms(mix, g_ref[1:2, :])
        x1_ref[rs, :] = x1
        h2 = _rms(x1, g_ref[2:3, :]) * (1.0 + sc2) + sh2
        h2_hi = h2.astype(BF16)
        h2_ref[rs, :] = h2_hi
        split.append((h2_hi, (h2 - h2_hi.astype(F32)).astype(BF16)))
    for rs, (h2_hi, h2_lo) in zip(parts, split):
        r_hi = _dot(h2_hi, wr_ref[...])
        logits = r_hi + pltpu.roll(r_hi, LANES - N_EXPERTS, 1) + _dot(h2_lo, wr_ref[...])
        lane = lax.broadcasted_iota(jnp.int32, logits.shape, 1)
        logits = jnp.where(lane < N_EXPERTS, logits, NEG_INF)
        mx = jnp.max(logits, axis=-1, keepdims=True)
        p = jnp.exp(logits - mx)
        aff_ref[rs, :] = p / jnp.sum(p, axis=-1, keepdims=True)


def _outproj(ona, owa, y2, u, dsk, wglu_bf, wout_bf, x, mod_l, mod_row_fn, g, wr_pad, L):
    T = x.shape[0]
    tm = min(PROJ_TILE, L)
    tiles_per_seq = L // tm
    row = lambda w: pl.BlockSpec((tm, w), lambda i: (i, 0))
    full = lambda a: pl.BlockSpec(a.shape, lambda i: (0,) * a.ndim)
    return pl.pallas_call(
        _outproj_body,
        grid=(T // tm,),
        in_specs=[row(NA_WIDTH), row(WA_WIDTH),
                  pl.BlockSpec((2, tm, SSM_CH), lambda i: (0, i, 0)),
                  row(SSM_CH), full(dsk), full(wglu_bf), full(wout_bf), row(D_MODEL),
                  pl.BlockSpec((1, 1, 6 * D_MODEL), lambda i: (mod_row_fn(i // tiles_per_seq), 0, 0)),
                  full(g), full(wr_pad)],
        out_specs=[row(D_MODEL), row(D_MODEL), row(LANES)],
        out_shape=[jax.ShapeDtypeStruct((T, D_MODEL), F32),
                   jax.ShapeDtypeStruct((T, D_MODEL), BF16),
                   jax.ShapeDtypeStruct((T, LANES), F32)],
        compiler_params=_cparams(("arbitrary",)),
        name="outproj",
    )(ona, owa, y2.reshape(2, T, SSM_CH), u, dsk, wglu_bf, wout_bf, x, mod_l, g, wr_pad)


def _select_body(cap, aff_ref, pos_ref, post_ref, afft_ref, plan_ref):
    L = aff_ref.shape[1]
    aff = aff_ref[0]
    for r in range(1, SEL_GROUP):
        aff = aff + pltpu.roll(aff_ref[r], r * N_EXPERTS, 1)
    capf = jnp.float32(cap)

    def bisect(k, thr):
        cand = thr | jnp.left_shift(jnp.int32(1), 30 - k)
        cnt = jnp.sum(jnp.where(aff >= pltpu.bitcast(cand, F32), 1.0, 0.0), axis=0, keepdims=True)
        return jnp.where(cnt >= capf, cand, thr)

    thr = lax.fori_loop(0, 31, bisect, jnp.zeros((1, LANES), jnp.int32))
    kth = pltpu.bitcast(thr, F32)
    gt = jnp.where(aff > kth, 1.0, 0.0)
    eq = jnp.where(aff == kth, 1.0, 0.0)
    need = capf - jnp.sum(gt, axis=0, keepdims=True)
    ck = SEL_CHUNK
    tri = (lax.broadcasted_iota(jnp.int32, (ck, ck), 0)
           >= lax.broadcasted_iota(jnp.int32, (ck, ck), 1))
    tri = jnp.where(tri, 1.0, 0.0).astype(BF16)

    def prefix(mask):
        outs = []
        carries = [jnp.zeros((1, LANES), F32)]
        for j in range(L // ck):
            c = _dot(tri, mask[j * ck:(j + 1) * ck].astype(BF16)) + carries[-1]
            outs.append(c)
            carries.append(c[ck - 1:ck, :])
        return jnp.concatenate(outs, axis=0), jnp.concatenate(carries, axis=0)

    tie = jnp.where(prefix(eq)[0] <= need, eq, 0.0)
    sel = gt + tie
    rank, starts = prefix(sel)
    pos = jnp.where(sel > 0.0, rank - 1.0, -1.0)
    pos_ref[0] = pos
    post_ref[0] = jnp.transpose(pos)
    afft_ref[0] = jnp.transpose(aff)
    nt = L // ck
    first, last = starts[0:nt], starts[1:nt + 1]
    win = min(cap, COMBINE_WINDOW)
    base = jnp.minimum(jnp.floor(first * (1.0 / BF16_ROWS)) * BF16_ROWS, float(cap - win))
    fit = jnp.where(last - base <= float(win), 1.0, 0.0)
    plan_ref[0] = jnp.concatenate([base, fit], axis=0).astype(jnp.int32)


def _select(aff, B, L):
    cap = EC_CAPACITY * L // N_EXPERTS
    ng = B // SEL_GROUP
    return pl.pallas_call(
        functools.partial(_select_body, cap),
        grid=(ng,),
        in_specs=[pl.BlockSpec((SEL_GROUP, L, LANES), lambda i: (i, 0, 0))],
        out_specs=[pl.BlockSpec((1, L, LANES), lambda i: (i, 0, 0)),
                   pl.BlockSpec((1, LANES, L), lambda i: (i, 0, 0)),
                   pl.BlockSpec((1, LANES, L), lambda i: (i, 0, 0)),
                   pl.BlockSpec((1, 2 * (L // SEL_CHUNK), LANES), lambda i: (i, 0, 0))],
        out_shape=[jax.ShapeDtypeStruct((ng, L, LANES), F32),
                   jax.ShapeDtypeStruct((ng, LANES, L), F32),
                   jax.ShapeDtypeStruct((ng, LANES, L), F32),
                   jax.ShapeDtypeStruct((ng, 2 * (L // SEL_CHUNK), LANES), jnp.int32)],
        compiler_params=_cparams(("arbitrary",)),
        name="ec_select",
    )(aff.reshape(B, L, LANES))


def _window_rows(base, w):
    return pl.ds(base, w) if isinstance(base, int) else pl.ds(pl.multiple_of(base, BF16_ROWS), w)


def _gather_body(cap, win, plan_ref, h_ref, post_ref, afft_ref, xs_ref, gs_ref):
    b = pl.program_id(0)
    j = pl.program_id(1)
    grp = b // SEL_GROUP
    lane0 = (b % SEL_GROUP) * N_EXPERTS

    @pl.when(j == 0)
    def _():
        xs_ref[...] = jnp.zeros_like(xs_ref)
        gs_ref[...] = jnp.zeros_like(gs_ref)

    n_req, ck = h_ref.shape[0], h_ref.shape[1]

    def run(w, bases, r=0):
        h = h_ref[r]
        slot = lax.broadcasted_iota(jnp.int32, (w, ck), 0)
        onehots = []
        for e in range(N_EXPERTS):
            rows = _window_rows(bases[e], w)
            row = r * N_EXPERTS + e
            hit = post_ref[0, row:row + 1, :] == (slot + bases[e]).astype(F32)
            gs_ref[e, r, rows, :] += jnp.sum(jnp.where(hit, afft_ref[0, row:row + 1, :], 0.0), axis=1, keepdims=True)
            onehots.append(jnp.where(hit, 1.0, 0.0).astype(BF16))
        xs = _dot(jnp.concatenate(onehots, axis=0), h).astype(BF16)
        for e in range(N_EXPERTS):
            rows = _window_rows(bases[e], w)
            xs_ref[e, r, rows, :] += xs[e * w:(e + 1) * w]

    if win == cap:
        for r in range(n_req):
            run(cap, [0] * N_EXPERTS, r)
        return
    bases = []
    fits = None
    nt = pl.num_programs(1)
    for e in range(N_EXPERTS):
        bases.append(plan_ref[grp, j, lane0 + e])
        ok = plan_ref[grp, nt + j, lane0 + e]
        fits = ok if fits is None else fits + ok
    fits = fits == N_EXPERTS

    @pl.when(fits)
    def _():
        run(win, bases)

    @pl.when(jnp.logical_not(fits))
    def _():
        run(cap, [0] * N_EXPERTS)


def _gather(starts, h2, post, afft, B, L):
    cap = EC_CAPACITY * L // N_EXPERTS
    ck = SEL_CHUNK
    win = min(cap, COMBINE_WINDOW)
    nr = SMALL_STEP_REQUESTS if win == cap else 1
    per_group = SEL_GROUP // nr
    lane_blk = lambda b, j, s: (b // per_group, b % per_group, j)
    grid_spec = pltpu.PrefetchScalarGridSpec(
        num_scalar_prefetch=1,
        grid=(B // nr, L // ck),
        in_specs=[pl.BlockSpec((nr, ck, D_MODEL), lambda b, j, s: (b, j, 0)),
                  pl.BlockSpec((1, nr * N_EXPERTS, ck), lane_blk),
                  pl.BlockSpec((1, nr * N_EXPERTS, ck), lane_blk)],
        out_specs=[pl.BlockSpec((N_EXPERTS, nr, cap, D_MODEL), lambda b, j, s: (0, b, 0, 0)),
                   pl.BlockSpec((N_EXPERTS, nr, cap, 1), lambda b, j, s: (0, b, 0, 0))],
    )
    xs, gs = pl.pallas_call(
        functools.partial(_gather_body, cap, win),
        grid_spec=grid_spec,
        out_shape=[jax.ShapeDtypeStruct((N_EXPERTS, B, cap, D_MODEL), BF16),
                   jax.ShapeDtypeStruct((N_EXPERTS, B, cap, 1), F32)],
        compiler_params=_cparams(("arbitrary", "arbitrary")),
        name="ec_gather",
    )(starts, h2.reshape(B, L, D_MODEL), post, afft)
    return xs.reshape(N_EXPERTS, B * cap, D_MODEL), gs.reshape(N_EXPERTS, B * cap, 1)


def _ffn_body(n_first, xa_ref, ga_ref, xb_ref, gb_ref, wg_ref, wu_ref, wd_ref, y_ref, wgb, wub, wdb):
    j = pl.program_id(1)

    @pl.when(j == 0)
    def _():
        wgb[...] = wg_ref[0, 0].astype(BF16)
        wub[...] = wu_ref[0, 0].astype(BF16)
        wdb[...] = wd_ref[0, 0].astype(BF16)

    def run(xs_ref, gs_ref):
        x = xs_ref[0]
        a = _dot(x, wgb[...])
        u = _dot(x, wub[...])
        hm = (a * _sigmoid(a) * u).astype(BF16)
        y_ref[0] = (_dot(hm, wdb[...]) * gs_ref[0]).astype(BF16)

    @pl.when(j < n_first)
    def _():
        run(xa_ref, ga_ref)

    @pl.when(j >= n_first)
    def _():
        run(xb_ref, gb_ref)


def _ffn(xa, ga, xb, gb, wg, wu, wd, l):
    rc = FFN_ROWS
    na = xa.shape[1] // rc
    nb = xb.shape[1] // rc
    first = lambda w: pl.BlockSpec((1, rc, w), lambda e, j: (e, jnp.minimum(j, na - 1), 0))
    second = lambda w: pl.BlockSpec((1, rc, w), lambda e, j: (e, jnp.maximum(j - na, 0), 0))
    wspec = lambda a, b: pl.BlockSpec((1, 1, a, b), lambda e, j: (l, e, 0, 0))
    return pl.pallas_call(
        functools.partial(_ffn_body, na),
        grid=(N_EXPERTS, na + nb),
        in_specs=[first(D_MODEL), first(1), second(D_MODEL), second(1),
                  wspec(D_MODEL, EXPERT_FF), wspec(D_MODEL, EXPERT_FF), wspec(EXPERT_FF, D_MODEL)],
        out_specs=pl.BlockSpec((1, rc, D_MODEL), lambda e, j: (e, j, 0)),
        out_shape=jax.ShapeDtypeStruct((N_EXPERTS, (na + nb) * rc, D_MODEL), BF16),
        scratch_shapes=[pltpu.VMEM((D_MODEL, EXPERT_FF), BF16),
                        pltpu.VMEM((D_MODEL, EXPERT_FF), BF16),
                        pltpu.VMEM((EXPERT_FF, D_MODEL), BF16)],
        compiler_params=_cparams(("arbitrary", "arbitrary")),
        name="ec_ffn",
    )(xa, ga, xb, gb, wg, wu, wd)


def _combine_body(cap, win, plan_ref, pos_ref, y_ref, x1_ref, mod_ref, g_ref, o_ref):
    b = pl.program_id(0)
    i = pl.program_id(1)
    pos = pos_ref[0].astype(BF16)
    tq = pos.shape[0]
    g2 = mod_ref[0][:, 5 * D_MODEL:6 * D_MODEL]

    def finish(w, bases, values, lane0, rows=slice(None)):
        n = N_EXPERTS * w
        col = lax.broadcasted_iota(jnp.int32, (1, n), 1)
        expert = jnp.zeros((1, n), jnp.int32)
        for e in range(1, N_EXPERTS):
            expert = expert + jnp.where(col >= e * w, 1, 0)
        target = col - expert * w
        if bases is not None:
            for e in range(N_EXPERTS):
                target = jnp.where(expert == e, target + bases[e], target)
        expand = jnp.where(lax.broadcasted_iota(jnp.int32, (LANES, n), 0) == expert + lane0, 1.0, 0.0)
        per_col = _dot(pos, expand.astype(BF16))
        onehot = jnp.where(per_col == target.astype(F32), 1.0, 0.0).astype(BF16)
        f = _dot(onehot, values)
        o_ref[rows, :] = x1_ref[rows, :] + g2 * _rms(f, g_ref[3:4, :])

    if win == cap:
        n_req = y_ref.shape[1] // cap
        for r in range(n_req):
            values = y_ref[:, r * cap:(r + 1) * cap, :].reshape(N_EXPERTS * cap, D_MODEL)
            finish(cap, None, values, ((b * n_req + r) % SEL_GROUP) * N_EXPERTS, slice(r * tq, (r + 1) * tq))
        return
    grp = b // SEL_GROUP
    lane0 = (b % SEL_GROUP) * N_EXPERTS
    bases = []
    fits = None
    nt = pl.num_programs(1)
    for e in range(N_EXPERTS):
        bases.append(plan_ref[grp, i, lane0 + e])
        ok = plan_ref[grp, nt + i, lane0 + e]
        fits = ok if fits is None else fits + ok
    fits = fits == N_EXPERTS

    @pl.when(fits)
    def _():
        rows = [y_ref[e, pl.ds(pl.multiple_of(bases[e], BF16_ROWS), win), :] for e in range(N_EXPERTS)]
        finish(win, bases, jnp.concatenate(rows, axis=0), lane0)

    @pl.when(jnp.logical_not(fits))
    def _():
        finish(cap, None, y_ref[...].reshape(N_EXPERTS * cap, D_MODEL), lane0)


def _combine(starts, pos, y, row_off, x1, mod_l, mod_row_fn, g, B, L, shared_mod=False):
    cap = EC_CAPACITY * L // N_EXPERTS
    tq = TOKEN_TILE
    assert tq == SEL_CHUNK
    nq = L // tq
    blk_off = row_off // cap
    win = min(cap, COMBINE_WINDOW)
    nr = SMALL_STEP_REQUESTS if (shared_mod and win == cap and nq == 1 and blk_off % SMALL_STEP_REQUESTS == 0) else 1
    per_group = SEL_GROUP // nr
    grid_spec = pltpu.PrefetchScalarGridSpec(
        num_scalar_prefetch=1,
        grid=(B // nr, nq),
        in_specs=[pl.BlockSpec((1, tq, LANES), lambda b, i, s: (b // per_group, i, 0)),
                  pl.BlockSpec((N_EXPERTS, nr * cap, D_MODEL), lambda b, i, s: (0, blk_off // nr + b, 0)),
                  pl.BlockSpec((nr * tq, D_MODEL), lambda b, i, s: (b * nq + i, 0)),
                  pl.BlockSpec((1, 1, 6 * D_MODEL), lambda b, i, s: (mod_row_fn(b * nr), 0, 0)),
                  pl.BlockSpec((4, D_MODEL), lambda b, i, s: (0, 0))],
        out_specs=pl.BlockSpec((nr * tq, D_MODEL), lambda b, i, s: (b * nq + i, 0)),
    )
    return pl.pallas_call(
        functools.partial(_combine_body, cap, win),
        grid_spec=grid_spec,
        out_shape=jax.ShapeDtypeStruct((B * L, D_MODEL), F32),
        compiler_params=_cparams(("arbitrary", "arbitrary")),
        name="ec_combine",
    )(starts, pos, y, x1, mod_l, g)


def _rope_tables(L):
    t = jnp.arange(L)
    row = (t // GRID_W).astype(F32)
    col = (t % GRID_W).astype(F32)
    half = HEAD_DIM // 4
    inv = ROPE_BASE ** (-jnp.arange(half, dtype=F32) / half)
    d = np.arange(LANES) % HEAD_DIM
    use_col = (d // (HEAD_DIM // 2)) == 1
    pos = jnp.where(use_col[None, :], col[:, None], row[:, None])
    ang = pos * inv[d % half][None, :]
    sign = np.where((d % (HEAD_DIM // 2)) < half, -1.0, 1.0).astype(np.float32)
    return jnp.cos(ang), jnp.sin(ang) * sign[None, :]


def _permute_wa_heads(a, axis, start):
    cut = lambda lo, hi: lax.slice_in_dim(a, lo, hi, axis=axis)
    heads = [cut(start + h * HEAD_DIM, start + (h + 1) * HEAD_DIM) for h in WA_HEAD_ORDER]
    return jnp.concatenate([cut(0, start)] + heads + [cut(start + WA_WIDTH, a.shape[axis])], axis=axis)


def kernel(x_prompt, x_sample, c, cache_na_k, cache_na_v, cache_wa_k, cache_wa_v, state_ssm, c_ctx, w_ada, b_ada, norm_g, w_in, w_out, na_rpb, wa_sink, ssm_lambda_re, ssm_lambda_im, ssm_log_dt, ssm_b_re, ssm_b_im, ssm_c_re, ssm_c_im, ssm_d, w_glu, w_router, w_exp_gate, w_exp_up, w_exp_down):
    Bc, Lc, _ = x_prompt.shape
    Bs, Ls, _ = x_sample.shape
    ctx_row = Bs
    n_cond = ((Bs + 1 + SUBLANES - 1) // SUBLANES) * SUBLANES
    cond = jnp.zeros((n_cond, D_MODEL), F32).at[0:Bs].set(c).at[ctx_row].set(c_ctx)
    mod = _modulation(cond, w_ada, b_ada)
    mod = mod.reshape(DEPTH, n_cond, 1, 6 * D_MODEL)

    w_in_bf = _permute_wa_heads(w_in, 2, C_WAQ).astype(BF16)
    w_out_bf = _permute_wa_heads(w_out, 1, NA_WIDTH).astype(BF16)
    w_glu_bf = w_glu.astype(BF16)
    wr_hi = w_router.astype(BF16)
    wr_lo = (w_router - wr_hi.astype(F32)).astype(BF16)
    wr_pad = jnp.pad(jnp.concatenate([wr_hi, wr_lo], axis=-1), ((0, 0), (0, 0), (0, LANES - 2 * N_EXPERTS)))
    s5_a, s5_bb = _s5_prep(ssm_lambda_re, ssm_lambda_im, ssm_log_dt, ssm_b_re, ssm_b_im)
    s5_c = _s5_c_matrix(ssm_c_re, ssm_c_im)
    rope_tabs = _rope_tables(Ls)

    ctx_mod_row = lambda b: ctx_row + 0 * b
    lat_mod_row = lambda b: b

    xp = x_prompt.reshape(Bc * Lc, D_MODEL)
    xs = x_sample.reshape(Bs * Ls, D_MODEL)
    new_kv = tuple(jnp.zeros((Bc, DEPTH, Lc, w), F32) for w in (NA_WIDTH, NA_WIDTH, WA_KV_WIDTH, WA_KV_WIDTH))
    new_ssm = []
    for l in range(DEPTH):
        g = norm_g[l]
        dsk = ssm_d[l].reshape(1, SSM_CH)
        sl2 = slice(2 * l, 2 * l + 2)
        qna, kna, vna, qwa, kwa, vwa, u = _inproj(xp, mod, ctx_mod_row, norm_g, w_in_bf, l, Bc, Lc, None, new_kv)
        new_kv = (kna, vna, kwa, vwa)
        ona, owa = _attn_ctx(wa_sink[l], qna, kna, vna, qwa, kwa, vwa, l, Bc, Lc)
        y2, hfin = _s5_scan(u, s5_bb[sl2], s5_a[sl2], s5_c[sl2],
                            jnp.zeros((2, Bc, 2 * SSM_N), F32), Bc, Lc)
        x1, h2, aff = _outproj(ona, owa, y2, u, dsk, w_glu_bf[l], w_out_bf[l], xp, mod[l], ctx_mod_row,
                               g, wr_pad[l], Lc)
        pos_c, post, afft, starts_c = _select(aff, Bc, Lc)
        xg_c, gs_c = _gather(starts_c, h2, post, afft, Bc, Lc)
        x1_c = x1
        new_ssm.append(jnp.transpose(hfin.reshape(2, Bc, 2, SSM_GROUPS, SSM_STATE), (1, 0, 2, 3, 4)))
        qna, kna, vna, qwa, kwa, vwa, u = _inproj(xs, mod, lat_mod_row, norm_g, w_in_bf, l, Bs, Ls, rope_tabs)
        ona = _attn_na(qna, kna, vna, cache_na_k, cache_na_v, _na_bias_pairs(na_rpb[l]), l, Bs, Ls)
        owa = _attn_wa(wa_sink[l], qwa, kwa, vwa, cache_wa_k, cache_wa_v, l, Bs, Ls)
        h0 = jnp.transpose(state_ssm[:, l].reshape(Bs, 2, 2 * SSM_N), (1, 0, 2))
        y2, _ = _s5_scan(u, s5_bb[sl2], s5_a[sl2], s5_c[sl2], h0, Bs, Ls)
        x1, h2, aff = _outproj(ona, owa, y2, u, dsk, w_glu_bf[l], w_out_bf[l], xs, mod[l], lat_mod_row,
                               g, wr_pad[l], Ls)
        pos_s, post, afft, starts_s = _select(aff, Bs, Ls)
        xg_s, gs_s = _gather(starts_s, h2, post, afft, Bs, Ls)
        yy = _ffn(xg_c, gs_c, xg_s, gs_s, w_exp_gate, w_exp_up, w_exp_down, l)
        xp = _combine(starts_c, pos_c, yy, 0, x1_c, mod[l], ctx_mod_row, g, Bc, Lc, shared_mod=True)
        xs = _combine(starts_s, pos_s, yy, xg_c.shape[1], x1, mod[l], lat_mod_row, g, Bs, Ls)
    return (xp.reshape(Bc, Lc, D_MODEL), xs.reshape(Bs, Ls, D_MODEL),
            new_kv[0].reshape(Bc, DEPTH, Lc, NA_HEADS, HEAD_DIM),
            new_kv[1].reshape(Bc, DEPTH, Lc, NA_HEADS, HEAD_DIM),
            new_kv[2].reshape(Bc, DEPTH, Lc, WA_KV_HEADS, HEAD_DIM),
            new_kv[3].reshape(Bc, DEPTH, Lc, WA_KV_HEADS, HEAD_DIM),
            jnp.stack(new_ssm, axis=1))
```

```python
import functools
import math

import numpy as np
import jax
import jax.numpy as jnp
from jax import lax
from jax.experimental import pallas as pl
from jax.experimental.pallas import tpu as pltpu

F32 = jnp.float32
BF16 = jnp.bfloat16

D_MODEL = 1024
DEPTH = 2
GRID_W = 64
HEAD_DIM = 64
NA_HEADS = 6
NA_KH = 8
NA_KW = 16
WA_HEADS = 6
WA_KV_HEADS = 2
WA_WINDOW = 128
WA_QBLOCK = 256
SSM_CH = 256
SSM_GROUP_CH = 16
SSM_GROUPS = 16
SSM_STATE = 64
SSM_N = SSM_GROUPS * SSM_STATE
NA_WIDTH = NA_HEADS * HEAD_DIM
WA_WIDTH = WA_HEADS * HEAD_DIM
WA_KV_WIDTH = WA_KV_HEADS * HEAD_DIM
IN_COLS = 2048
N_EXPERTS = 16
EC_CAPACITY = 2
EXPERT_FF = 1024
ROPE_BASE = 10000.0
RMS_EPS = 1e-6
NEG_INF = -1e30
ATTN_SCALE = HEAD_DIM ** -0.5
LOG2E = math.log2(math.e)
Q_SCALE = ATTN_SCALE * LOG2E

LANES = 128
SUBLANES = 8
VMEM_LIMIT = 48 * 1024 * 1024

C_NAQ, C_NAK, C_NAV, C_WAQ, C_WAK, C_WAV, C_SSU = 0, 384, 768, 1152, 1536, 1664, 1792
WA_HEAD_ORDER = (0, 3, 1, 4, 2, 5)

NA_QROWS = 4
NA_KROWS = NA_QROWS + NA_KH
NA_REL = 2 * NA_KH - 1
NA_PAIRS = 3 * NA_REL
TOKEN_TILE = 256
OUTPROJ_SUBTILE = 128
PROJ_TILE = 512
SCAN_TILE = 128
SMALL_STEP_REQUESTS = 4
CTX_ATTN_REQUESTS = 4
SCAN_BATCH = SUBLANES
FFN_ROWS = 512
SEL_CHUNK = 256
SEL_GROUP = LANES // N_EXPERTS
BF16_ROWS = 2 * SUBLANES
COMBINE_WINDOW = 64


def _cparams(sem):
    return pltpu.CompilerParams(dimension_semantics=sem, vmem_limit_bytes=VMEM_LIMIT)


def _sigmoid(x):
    return 1.0 / (1.0 + jnp.exp(-x))


def _rms(x, g):
    ms = jnp.mean(x * x, axis=-1, keepdims=True)
    return x * lax.rsqrt(ms + RMS_EPS) * g


def _dot(a, b):
    return jnp.dot(a, b, preferred_element_type=F32)


def _dot_nt(a, b):
    return lax.dot_general(a, b, (((1,), (1,)), ((), ())), preferred_element_type=F32)


def _mod_body(c_ref, w_ref, b_ref, o_ref):
    c = c_ref[...]
    s = c * _sigmoid(c)
    o_ref[0] = _dot(s.astype(BF16), w_ref[0].astype(BF16)) + b_ref[0]


def _modulation(cond, w_ada, b_ada):
    n = cond.shape[0]
    tn = 1024
    return pl.pallas_call(
        _mod_body,
        grid=(DEPTH, 6 * D_MODEL // tn),
        in_specs=[
            pl.BlockSpec((n, D_MODEL), lambda l, j: (0, 0)),
            pl.BlockSpec((1, D_MODEL, tn), lambda l, j: (l, 0, j)),
            pl.BlockSpec((1, 1, tn), lambda l, j: (l, 0, j)),
        ],
        out_specs=pl.BlockSpec((1, n, tn), lambda l, j: (l, 0, j)),
        out_shape=jax.ShapeDtypeStruct((DEPTH, n, 6 * D_MODEL), F32),
        compiler_params=_cparams(("arbitrary", "arbitrary")),
        name="modulation",
    )(cond, w_ada, b_ada.reshape(DEPTH, 1, 6 * D_MODEL))


def _rope_chunk(x, cos, sin_signed):
    lane = lax.broadcasted_iota(jnp.int32, x.shape, 1)
    first = (lane % 32) < 16
    rot = jnp.where(first, pltpu.roll(x, LANES - 16, 1), pltpu.roll(x, 16, 1))
    return x * cos + rot * sin_signed


INPROJ_KV_OUTPUTS = (1, 2, 4, 5)


def _inproj_body(rope, n_alias, x_ref, mod_ref, g_ref, w_ref, *rest):
    if rope:
        cos_ref, sin_ref = rest[:2]
        rest = rest[2:]
    qna_ref, kna_ref, vna_ref, qwa_ref, kwa_ref, vwa_ref, u_ref = rest[n_alias:]

    def put(ref, val):
        ref[...] = val.reshape(ref.shape)

    x = x_ref[...]
    m = mod_ref[0, 0]
    sh1 = m[:, 0:D_MODEL]
    sc1 = m[:, D_MODEL:2 * D_MODEL]
    h = _rms(x, g_ref[0, 0:1, :]) * (1.0 + sc1) + sh1
    z = _dot(h.astype(BF16), w_ref[0])
    qna_ref[...] = (z[:, C_NAQ:C_NAK] * Q_SCALE).astype(BF16)
    put(kna_ref, z[:, C_NAK:C_NAV])
    put(vna_ref, z[:, C_NAV:C_WAQ])
    put(vwa_ref, z[:, C_WAV:C_SSU])
    u_ref[...] = z[:, C_SSU:IN_COLS]
    if rope:
        cos = cos_ref[...]
        sin = sin_ref[...]
        for j in range(WA_WIDTH // LANES):
            qc = _rope_chunk(z[:, C_WAQ + j * LANES:C_WAQ + (j + 1) * LANES], cos, sin)
            qwa_ref[:, j * LANES:(j + 1) * LANES] = (qc * Q_SCALE).astype(BF16)
        put(kwa_ref, _rope_chunk(z[:, C_WAK:C_WAV], cos, sin))
    else:
        qwa_ref[...] = (z[:, C_WAQ:C_WAK] * Q_SCALE).astype(BF16)
        put(kwa_ref, z[:, C_WAK:C_WAV])


def _inproj(x, mod, mod_row_fn, norm_g, w_bf, l, B, L, rope_tabs, kv_layers=None):
    T = x.shape[0]
    tm = min(PROJ_TILE, L)
    tiles_per_seq = L // tm
    in_specs = [
        pl.BlockSpec((tm, D_MODEL), lambda i: (i, 0)),
        pl.BlockSpec((1, 1, 1, 6 * D_MODEL), lambda i: (l, mod_row_fn(i // tiles_per_seq), 0, 0)),
        pl.BlockSpec((1, 4, D_MODEL), lambda i: (l, 0, 0)),
        pl.BlockSpec((1, D_MODEL, IN_COLS), lambda i: (l, 0, 0)),
    ]
    args = [x, mod, norm_g, w_bf]
    if rope_tabs is not None:
        in_specs += [pl.BlockSpec((tm, LANES), lambda i: (i % tiles_per_seq, 0))] * 2
        args += list(rope_tabs)
    widths = (NA_WIDTH, NA_WIDTH, NA_WIDTH, WA_WIDTH, WA_KV_WIDTH, WA_KV_WIDTH, SSM_CH)
    dtypes = (BF16, F32, F32, BF16, F32, F32, F32)
    out_specs = [pl.BlockSpec((tm, w), lambda i: (i, 0)) for w in widths]
    out_shape = [jax.ShapeDtypeStruct((T, w), dt) for w, dt in zip(widths, dtypes)]
    aliases = {}
    n_alias = 0
    if kv_layers is not None:
        for o in INPROJ_KV_OUTPUTS:
            out_specs[o] = pl.BlockSpec((1, 1, tm, widths[o]),
                                        lambda i: (i // tiles_per_seq, l, i % tiles_per_seq, 0))
            out_shape[o] = jax.ShapeDtypeStruct((B, DEPTH, L, widths[o]), dtypes[o])
        n_alias = len(kv_layers)
        for k, (o, prev) in enumerate(zip(INPROJ_KV_OUTPUTS, kv_layers)):
            aliases[len(args)] = o
            in_specs.append(pl.BlockSpec(memory_space=pl.ANY))
            args.append(prev)
    return pl.pallas_call(
        functools.partial(_inproj_body, rope_tabs is not None, n_alias),
        grid=(T // tm,),
        in_specs=in_specs,
        out_specs=out_specs,
        out_shape=out_shape,
        input_output_aliases=aliases,
        compiler_params=_cparams(("arbitrary",)),
        name="inproj",
    )(*args)


def _lane_lo(shape):
    return lax.broadcasted_iota(jnp.int32, shape, len(shape) - 1) < HEAD_DIM


def _keep_half(q, lo, half):
    keep = jnp.where(lo, 1.0 - half, 0.0 + half).astype(q.dtype)
    return q * keep


def _values_and_ones(v, lo, half):
    one = jnp.ones_like(v)
    return (jnp.where(lo, v, one) if half == 0 else jnp.where(lo, one, v)).astype(BF16)


def _normalise(o2, extra=None):
    den = pltpu.roll(o2, HEAD_DIM, 1)
    if extra is not None:
        den = den + extra
    return o2 / den


def _attn_ctx_body(sink_ref, qna, kna, vna, qwa, kwa, vwa, ona, owa):
    R, L = qna.shape[0], qna.shape[1]
    npair = WA_WIDTH // LANES
    lo = _lane_lo((L, LANES))
    lo_all = _lane_lo((npair * L, LANES))
    blk = lax.broadcasted_iota(jnp.int32, (npair * L, 1), 0) // L
    for r in range(R):
        for j in range(NA_WIDTH // LANES):
            sl = slice(j * LANES, (j + 1) * LANES)
            q2 = qna[r, :, sl]
            k2 = kna[r, 0, :, sl].astype(BF16)
            v2 = vna[r, 0, :, sl].astype(BF16)
            halves = []
            for half in range(2):
                s = _dot_nt(_keep_half(q2, lo, half), k2)
                p = jnp.exp2(s - jnp.max(s, axis=-1, keepdims=True))
                halves.append(_dot(p.astype(BF16), v2) / jnp.sum(p, axis=-1, keepdims=True))
            ona[r, :, sl] = jnp.where(lo, halves[0], halves[1]).astype(BF16)
        kw = kwa[r, 0].astype(BF16)
        vw = vwa[r, 0].astype(BF16)
        q_all = jnp.concatenate([qwa[r, :, j * LANES:(j + 1) * LANES] for j in range(npair)], axis=0)
        outs = []
        for g in range(WA_KV_HEADS):
            s = _dot_nt(_keep_half(q_all, lo_all, g), kw)
            sk = jnp.zeros((npair * L, 1), F32)
            for j in range(npair):
                sk = jnp.where(blk == j, sink_ref[WA_HEAD_ORDER[2 * j + g]] * LOG2E, sk)
            m = jnp.maximum(jnp.max(s, axis=-1, keepdims=True), sk)
            p = jnp.exp2(s - m)
            l = jnp.sum(p, axis=-1, keepdims=True) + jnp.exp2(sk - m)
            outs.append(_dot(p.astype(BF16), vw) / l)
        o_all = jnp.where(lo_all, outs[0], outs[1]).astype(BF16)
        for j in range(npair):
            owa[r, :, j * LANES:(j + 1) * LANES] = o_all[j * L:(j + 1) * L]


def _attn_ctx(sink, qna, kna, vna, qwa, kwa, vwa, l, B, L):
    R = CTX_ATTN_REQUESTS

    def spec(w):
        return pl.BlockSpec((R, L, w), lambda b: (b, 0, 0))

    def layer_spec(w):
        return pl.BlockSpec((R, 1, L, w), lambda b: (b, l, 0, 0))

    r3 = lambda a: a.reshape(B, L, a.shape[-1])
    ona, owa = pl.pallas_call(
        _attn_ctx_body,
        grid=(B // R,),
        in_specs=[pl.BlockSpec(memory_space=pltpu.SMEM),
                  spec(NA_WIDTH), layer_spec(NA_WIDTH), layer_spec(NA_WIDTH),
                  spec(WA_WIDTH), layer_spec(WA_KV_WIDTH), layer_spec(WA_KV_WIDTH)],
        out_specs=[spec(NA_WIDTH), spec(WA_WIDTH)],
        out_shape=[jax.ShapeDtypeStruct((B, L, NA_WIDTH), BF16),
                   jax.ShapeDtypeStruct((B, L, WA_WIDTH), BF16)],
        compiler_params=_cparams(("arbitrary",)),
        name="attn_ctx",
    )(sink, r3(qna), kna, vna, r3(qwa), kwa, vwa)
    return ona.reshape(B * L, NA_WIDTH), owa.reshape(B * L, WA_WIDTH)


def _na_key_start(rb, rows):
    return jnp.clip(rb * NA_QROWS - NA_KH // 2, 0, rows - NA_KROWS)


def _attn_na_body(rows, idx_ref, q_ref, k_ref, v_ref, kc_ref, vc_ref, pairs_ref, o_ref):
    rb = pl.program_id(1)
    nq = NA_QROWS * GRID_W
    nk = NA_KROWS * GRID_W
    start = pl.multiple_of(_na_key_start(rb, rows) * GRID_W, GRID_W)
    lo = _lane_lo((nq, LANES))
    lo_v = _lane_lo((1, LANES))
    npair = NA_KROWS // 2
    pat = jnp.where(rb == 0, 0, jnp.where(rb == rows // NA_QROWS - 1, 2, 1))
    entry = [[idx_ref[(pat * NA_QROWS + i) * npair + jj] for jj in range(npair)] for i in range(NA_QROWS)]

    def bias(head):
        return jnp.concatenate(
            [jnp.concatenate([pairs_ref[head, entry[i][jj]] for jj in range(npair)], axis=1)
             for i in range(NA_QROWS)], axis=0)

    for j in range(NA_WIDTH // LANES):
        sl = slice(j * LANES, (j + 1) * LANES)
        q2 = q_ref[0, :, sl]
        kl = k_ref[0, pl.ds(start, nk), sl].astype(BF16)
        vl = v_ref[0, pl.ds(start, nk), sl]
        kc = kc_ref[0, 0, :, sl].astype(BF16)
        vc = vc_ref[0, 0, :, sl]
        scores = []
        for half in range(2):
            qm = _keep_half(q2, lo, half)
            scores.append((_dot_nt(qm, kl) + bias(2 * j + half), _dot_nt(qm, kc)))
        probs = []
        for s_loc, s_ctx in scores:
            m = jnp.maximum(jnp.max(s_loc, axis=-1, keepdims=True),
                            jnp.max(s_ctx, axis=-1, keepdims=True))
            probs.append((jnp.exp2(s_loc - m).astype(BF16), jnp.exp2(s_ctx - m).astype(BF16)))
        halves = []
        for half, (p_loc, p_ctx) in enumerate(probs):
            o2 = (_dot(p_loc, _values_and_ones(vl, lo_v, half))
                  + _dot(p_ctx, _values_and_ones(vc, lo_v, half)))
            halves.append(_normalise(o2))
        o_ref[0, :, sl] = jnp.where(lo, halves[0], halves[1]).astype(BF16)


def _na_bias_pairs(rpb_l):
    cq = np.arange(GRID_W)
    col_start = np.clip(cq - NA_KW // 2, 0, GRID_W - NA_KW)
    col_ok = (cq[None, :] >= col_start[:, None]) & (cq[None, :] < col_start[:, None] + NA_KW)
    dcol = np.clip(cq[None, :] - cq[:, None], -(NA_KW - 1), NA_KW - 1) + (NA_KW - 1)
    pick = (dcol[None] == np.arange(2 * NA_KW - 1)[:, None, None]).astype(np.float32)
    tiles = jnp.einsum('hrd,dqc->hrqc', rpb_l.astype(F32) * LOG2E, pick, precision=lax.Precision.HIGHEST)
    tiles = jnp.where(col_ok[None, None], tiles, NEG_INF)
    masked = jnp.full_like(tiles, NEG_INF)
    both = jnp.concatenate([tiles[:, :-1], tiles[:, 1:]], axis=-1)
    right = jnp.concatenate([masked, tiles], axis=-1)
    left = jnp.concatenate([tiles, masked], axis=-1)
    none = jnp.concatenate([masked[:, :1], masked[:, :1]], axis=-1)
    return jnp.concatenate([both, right, left, none], axis=1)


def _na_pair_index(rows):
    nrb = rows // NA_QROWS
    idx = np.zeros((3, NA_QROWS, NA_KROWS // 2), np.int32)
    for pi, rb in enumerate((0, 1, nrb - 1)):
        ks = int(np.clip(rb * NA_QROWS - NA_KH // 2, 0, rows - NA_KROWS))
        for i in range(NA_QROWS):
            qr = rb * NA_QROWS + i
            rs = int(np.clip(qr - NA_KH // 2, 0, rows - NA_KH))
            for jj in range(NA_KROWS // 2):
                kr = ks + 2 * jj
                r = kr - qr + NA_KH - 1
                ok_l = rs <= kr < rs + NA_KH
                ok_r = rs <= kr + 1 < rs + NA_KH
                if ok_l and ok_r:
                    idx[pi, i, jj] = r
                elif ok_r:
                    idx[pi, i, jj] = (NA_REL - 1) + (r + 1)
                elif ok_l:
                    idx[pi, i, jj] = (NA_REL - 1) + NA_REL + r
                else:
                    idx[pi, i, jj] = NA_PAIRS - 1
    return jnp.asarray(idx.reshape(-1))


def _attn_na(q, k, v, cache_k, cache_v, pairs, l, B, L):
    rows = L // GRID_W
    nrb = rows // NA_QROWS
    nq = NA_QROWS * GRID_W
    P = cache_k.shape[2]
    r3 = lambda a: a.reshape(B, L, a.shape[-1])
    ck = cache_k.reshape(B, DEPTH, P, NA_WIDTH)
    cv = cache_v.reshape(B, DEPTH, P, NA_WIDTH)
    grid_spec = pltpu.PrefetchScalarGridSpec(
        num_scalar_prefetch=1,
        grid=(B, nrb),
        in_specs=[
            pl.BlockSpec((1, nq, NA_WIDTH), lambda b, r, s: (b, r, 0)),
            pl.BlockSpec((1, L, NA_WIDTH), lambda b, r, s: (b, 0, 0)),
            pl.BlockSpec((1, L, NA_WIDTH), lambda b, r, s: (b, 0, 0)),
            pl.BlockSpec((1, 1, P, NA_WIDTH), lambda b, r, s: (b, l, 0, 0)),
            pl.BlockSpec((1, 1, P, NA_WIDTH), lambda b, r, s: (b, l, 0, 0)),
            pl.BlockSpec(pairs.shape, lambda b, r, s: (0, 0, 0, 0)),
        ],
        out_specs=pl.BlockSpec((1, nq, NA_WIDTH), lambda b, r, s: (b, r, 0)),
    )
    o = pl.pallas_call(
        functools.partial(_attn_na_body, rows),
        grid_spec=grid_spec,
        out_shape=jax.ShapeDtypeStruct((B, L, NA_WIDTH), BF16),
        compiler_params=_cparams(("arbitrary", "arbitrary")),
        name="attn_na",
    )(_na_pair_index(rows), r3(q), r3(k), r3(v), ck, cv, pairs)
    return o.reshape(B * L, NA_WIDTH)


def _attn_wa_body(L, sink_ref, q_ref, k_ref, v_ref, kc_ref, vc_ref, o_ref):
    n = pl.program_id(1)
    nk = WA_QBLOCK + 2 * WA_WINDOW
    npair = WA_WIDTH // LANES
    start = pl.multiple_of(jnp.clip(n * WA_QBLOCK - WA_WINDOW, 0, L - nk), WA_WINDOW)
    kl = k_ref[0, pl.ds(start, nk), :].astype(BF16)
    vl = v_ref[0, pl.ds(start, nk), :]
    kc = kc_ref[0, 0].astype(BF16)
    vc = vc_ref[0, 0]
    lo_v = _lane_lo((1, LANES))
    rows = npair * WA_QBLOCK
    qpos = n * WA_QBLOCK + lax.broadcasted_iota(jnp.int32, (rows, nk), 0) % WA_QBLOCK
    kpos = start + lax.broadcasted_iota(jnp.int32, (rows, nk), 1)
    in_win = jnp.abs(kpos - qpos) <= WA_WINDOW
    lo = _lane_lo((rows, LANES))
    q_all = jnp.concatenate([q_ref[0, :, j * LANES:(j + 1) * LANES] for j in range(npair)], axis=0)
    blk = lax.broadcasted_iota(jnp.int32, (rows, 1), 0) // WA_QBLOCK
    scores = []
    for g in range(WA_KV_HEADS):
        qm = _keep_half(q_all, lo, g)
        scores.append((jnp.where(in_win, _dot_nt(qm, kl), NEG_INF), _dot_nt(qm, kc)))
    probs = []
    for g in range(WA_KV_HEADS):
        s_loc, s_ctx = scores[g]
        sk = jnp.zeros((rows, 1), F32)
        for j in range(npair):
            sk = jnp.where(blk == j, sink_ref[WA_HEAD_ORDER[2 * j + g]] * LOG2E, sk)
        m = jnp.maximum(jnp.maximum(jnp.max(s_loc, axis=-1, keepdims=True),
                                    jnp.max(s_ctx, axis=-1, keepdims=True)), sk)
        probs.append((jnp.exp2(s_loc - m).astype(BF16), jnp.exp2(s_ctx - m).astype(BF16), jnp.exp2(sk - m)))
    outs = []
    for g in range(WA_KV_HEADS):
        p_loc, p_ctx, p_sink = probs[g]
        o2 = _dot(p_loc, _values_and_ones(vl, lo_v, g)) + _dot(p_ctx, _values_and_ones(vc, lo_v, g))
        outs.append(_normalise(o2, p_sink))
    o_all = jnp.where(lo, outs[0], outs[1]).astype(BF16)
    for j in range(npair):
        o_ref[0, :, j * LANES:(j + 1) * LANES] = o_all[j * WA_QBLOCK:(j + 1) * WA_QBLOCK]


def _attn_wa(sink, q, k, v, cache_k, cache_v, l, B, L):
    nb = L // WA_QBLOCK
    P = cache_k.shape[2]
    r3 = lambda a: a.reshape(B, L, a.shape[-1])
    ck = cache_k.reshape(B, DEPTH, P, WA_KV_WIDTH)
    cv = cache_v.reshape(B, DEPTH, P, WA_KV_WIDTH)
    o = pl.pallas_call(
        functools.partial(_attn_wa_body, L),
        grid=(B, nb),
        in_specs=[
            pl.BlockSpec(memory_space=pltpu.SMEM),
            pl.BlockSpec((1, WA_QBLOCK, WA_WIDTH), lambda b, n: (b, n, 0)),
            pl.BlockSpec((1, L, WA_KV_WIDTH), lambda b, n: (b, 0, 0)),
            pl.BlockSpec((1, L, WA_KV_WIDTH), lambda b, n: (b, 0, 0)),
            pl.BlockSpec((1, 1, P, WA_KV_WIDTH), lambda b, n: (b, l, 0, 0)),
            pl.BlockSpec((1, 1, P, WA_KV_WIDTH), lambda b, n: (b, l, 0, 0)),
        ],
        out_specs=pl.BlockSpec((1, WA_QBLOCK, WA_WIDTH), lambda b, n: (b, n, 0)),
        out_shape=jax.ShapeDtypeStruct((B, L, WA_WIDTH), BF16),
        compiler_params=_cparams(("arbitrary", "arbitrary")),
        name="attn_wa",
    )(sink, r3(q), r3(k), r3(v), ck, cv)
    return o.reshape(B * L, WA_WIDTH)


def _s5_prep_body(lr_ref, li_ref, ldt_ref, br_ref, bi_ref, a_ref, bb_ref):
    lr = lr_ref[0]
    li = li_ref[0]
    dt = jnp.exp(ldt_ref[0])
    mag = jnp.exp(lr * dt)
    ar = mag * jnp.cos(li * dt)
    ai = mag * jnp.sin(li * dt)
    den = lr * lr + li * li
    nr = ar - 1.0
    fr = (nr * lr + ai * li) / den
    fi = (ai * lr - nr * li) / den
    a_ref[0, :, 0:SSM_N] = ar
    a_ref[0, :, SSM_N:2 * SSM_N] = ai
    br = br_ref[0]
    bi = bi_ref[0]
    bb_ref[0, :, 0:SSM_N] = (fr * br - fi * bi).astype(BF16)
    bb_ref[0, :, SSM_N:2 * SSM_N] = (fr * bi + fi * br).astype(BF16)


def _s5_prep(lam_re, lam_im, log_dt, b_re, b_im):
    n = DEPTH * 2
    eye = jnp.eye(SSM_GROUPS, dtype=F32)

    def blockdiag_b(b):
        return jnp.einsum('ngph,gk->nghkp', b.reshape(n, SSM_GROUPS, SSM_STATE, SSM_GROUP_CH), eye
                          ).reshape(n, SSM_CH, SSM_N)

    lr = lam_re.reshape(n, 1, SSM_N)
    li = lam_im.reshape(n, 1, SSM_N)
    ldt = jnp.repeat(log_dt.reshape(n, SSM_GROUPS), SSM_STATE, axis=-1).reshape(n, 1, SSM_N)
    vec = pl.BlockSpec((1, 1, SSM_N), lambda i: (i, 0, 0))
    mat = pl.BlockSpec((1, SSM_CH, SSM_N), lambda i: (i, 0, 0))
    return pl.pallas_call(
        _s5_prep_body,
        grid=(n,),
        in_specs=[vec, vec, vec, mat, mat],
        out_specs=[pl.BlockSpec((1, 1, 2 * SSM_N), lambda i: (i, 0, 0)),
                   pl.BlockSpec((1, SSM_CH, 2 * SSM_N), lambda i: (i, 0, 0))],
        out_shape=[jax.ShapeDtypeStruct((n, 1, 2 * SSM_N), F32),
                   jax.ShapeDtypeStruct((n, SSM_CH, 2 * SSM_N), BF16)],
        compiler_params=_cparams(("arbitrary",)),
        name="s5_prep",
    )(lr, li, ldt, blockdiag_b(b_re), blockdiag_b(b_im))


def _s5_c_matrix(c_re, c_im):
    n = DEPTH * 2
    eye = jnp.eye(SSM_GROUPS, dtype=F32)

    def blk(c):
        return jnp.einsum('nghp,gk->ngpkh', c.reshape(n, SSM_GROUPS, SSM_GROUP_CH, SSM_STATE), eye
                          ).reshape(n, SSM_N, SSM_CH)

    return jnp.concatenate([blk(c_re), -blk(c_im)], axis=1).astype(BF16)


def _s5_scan_body(tl, u_ref, bb_ref, a_ref, c_ref, h0_ref, y_ref, hfin_ref, bu, hb, hst):
    d = pl.program_id(0)
    i = pl.program_id(2)
    sb = SCAN_BATCH
    re = slice(0, SSM_N)
    im = slice(SSM_N, 2 * SSM_N)

    @pl.when(i == 0)
    def _():
        hst[...] = h0_ref[0]

    u_tb = jnp.transpose(u_ref[...], (1, 0, 2)).reshape(tl * sb, SSM_CH)
    u_bf = u_tb.astype(BF16)
    half = tl * sb // 2
    bu[0:half, :] = _dot(u_bf[0:half], bb_ref[0])
    bu[half:, :] = _dot(u_bf[half:], bb_ref[0])
    a = a_ref[0]
    ar = jnp.broadcast_to(a[:, re], (sb, SSM_N))
    ai = jnp.broadcast_to(a[:, im], (sb, SSM_N))

    def scan(reverse):
        def pair(s, carry):
            hr, hi = carry
            t0 = tl - 1 - 2 * s if reverse else 2 * s
            t1 = t0 - 1 if reverse else t0 + 1
            r0 = pl.ds(pl.multiple_of(t0 * sb, sb), sb)
            r1 = pl.ds(pl.multiple_of(t1 * sb, sb), sb)
            h0r = ar * hr - ai * hi + bu[r0, re]
            h0i = ar * hi + ai * hr + bu[r0, im]
            h1r = ar * h0r - ai * h0i + bu[r1, re]
            h1i = ar * h0i + ai * h0r + bu[r1, im]
            rows = pl.ds(pl.multiple_of((t1 if reverse else t0) * sb, 2 * sb), 2 * sb)
            early, late = ((h1r, h1i), (h0r, h0i)) if reverse else ((h0r, h0i), (h1r, h1i))
            hb[rows, re] = jnp.concatenate([early[0], late[0]], axis=0).astype(BF16)
            hb[rows, im] = jnp.concatenate([early[1], late[1]], axis=0).astype(BF16)
            return h1r, h1i

        hr, hi = lax.fori_loop(0, tl // 2, pair, (hst[:, re], hst[:, im]))
        hst[:, re] = hr
        hst[:, im] = hi
        hfin_ref[0, :, re] = hr
        hfin_ref[0, :, im] = hi

    @pl.when(d == 0)
    def _():
        scan(False)

    @pl.when(d == 1)
    def _():
        scan(True)

    y_tb = jnp.concatenate([_dot(hb[0:half, :], c_ref[0]), _dot(hb[half:, :], c_ref[0])], axis=0)
    y_ref[0] = jnp.transpose(y_tb.reshape(tl, sb, SSM_CH), (1, 0, 2))


def _s5_scan(u, bb, a, cm, h0, B, L):
    tl = SCAN_TILE
    nt = L // tl
    tile = lambda d, i: i + d * (nt - 1 - 2 * i)
    return pl.pallas_call(
        functools.partial(_s5_scan_body, tl),
        grid=(2, B // SCAN_BATCH, nt),
        in_specs=[
            pl.BlockSpec((SCAN_BATCH, tl, SSM_CH), lambda d, b, i: (b, tile(d, i), 0)),
            pl.BlockSpec((1, SSM_CH, 2 * SSM_N), lambda d, b, i: (d, 0, 0)),
            pl.BlockSpec((1, 1, 2 * SSM_N), lambda d, b, i: (d, 0, 0)),
            pl.BlockSpec((1, 2 * SSM_N, SSM_CH), lambda d, b, i: (d, 0, 0)),
            pl.BlockSpec((1, SCAN_BATCH, 2 * SSM_N), lambda d, b, i: (d, b, 0)),
        ],
        out_specs=[
            pl.BlockSpec((1, SCAN_BATCH, tl, SSM_CH), lambda d, b, i: (d, b, tile(d, i), 0)),
            pl.BlockSpec((1, SCAN_BATCH, 2 * SSM_N), lambda d, b, i: (d, b, 0)),
        ],
        out_shape=[jax.ShapeDtypeStruct((2, B, L, SSM_CH), F32),
                   jax.ShapeDtypeStruct((2, B, 2 * SSM_N), F32)],
        scratch_shapes=[pltpu.VMEM((SCAN_BATCH * tl, 2 * SSM_N), F32),
                        pltpu.VMEM((SCAN_BATCH * tl, 2 * SSM_N), BF16),
                        pltpu.VMEM((SCAN_BATCH, 2 * SSM_N), F32)],
        compiler_params=_cparams(("arbitrary", "arbitrary", "arbitrary")),
        name="s5_scan",
    )(u.reshape(B, L, SSM_CH), bb, a, cm, h0)


def _outproj_body(ona_ref, owa_ref, y_ref, u_ref, dsk_ref, wglu_ref, wout_ref, x_ref, mod_ref, g_ref,
                  wr_ref, x1_ref, h2_ref, aff_ref):
    m = mod_ref[0]
    g1 = m[:, 2 * D_MODEL:3 * D_MODEL]
    sh2 = m[:, 3 * D_MODEL:4 * D_MODEL]
    sc2 = m[:, 4 * D_MODEL:5 * D_MODEL]
    tm = x_ref.shape[0]
    sub = min(tm, OUTPROJ_SUBTILE)
    parts = [slice(r, r + sub) for r in range(0, tm, sub)]
    mixes = []
    for rs in parts:
        y = dsk_ref[...] * u_ref[rs, :] + y_ref[0, rs, :] + y_ref[1, rs, :]
        zg = _dot(y.astype(BF16), wglu_ref[...])
        oss = zg[:, 0:SSM_CH] * _sigmoid(zg[:, SSM_CH:2 * SSM_CH])
        mixes.append(_dot(ona_ref[rs, :], wout_ref[0:NA_WIDTH, :])
                     + _dot(owa_ref[rs, :], wout_ref[NA_WIDTH:NA_WIDTH + WA_WIDTH, :])
                     + _dot(oss.astype(BF16), wout_ref[NA_WIDTH + WA_WIDTH:D_MODEL, :]))
    split = []
    for rs, mix in zip(parts, mixes):
        x1 = x_ref[rs, :] + g1 * _rms(mix, g_ref[1:2, :])
        x1_ref[rs, :] = x1
        h2 = _rms(x1, g_ref[2:3, :]) * (1.0 + sc2) + sh2
        h2_hi = h2.astype(BF16)
        h2_ref[rs, :] = h2_hi
        split.append((h2_hi, (h2 - h2_hi.astype(F32)).astype(BF16)))
    for rs, (h2_hi, h2_lo) in zip(parts, split):
        r_hi = _dot(h2_hi, wr_ref[...])
        logits = r_hi + pltpu.roll(r_hi, LANES - N_EXPERTS, 1) + _dot(h2_lo, wr_ref[...])
        lane = lax.broadcasted_iota(jnp.int32, logits.shape, 1)
        logits = jnp.where(lane < N_EXPERTS, logits, NEG_INF)
        mx = jnp.max(logits, axis=-1, keepdims=True)
        p = jnp.exp(logits - mx)
        aff_ref[rs, :] = p / jnp.sum(p, axis=-1, keepdims=True)


def _outproj(ona, owa, y2, u, dsk, wglu_bf, wout_bf, x, mod_l, mod_row_fn, g, wr_pad, L):
    T = x.shape[0]
    tm = min(PROJ_TILE, L)
    tiles_per_seq = L // tm
    row = lambda w: pl.BlockSpec((tm, w), lambda i: (i, 0))
    full = lambda a: pl.BlockSpec(a.shape, lambda i: (0,) * a.ndim)
    return pl.pallas_call(
        _outproj_body,
        grid=(T // tm,),
        in_specs=[row(NA_WIDTH), row(WA_WIDTH),
                  pl.BlockSpec((2, tm, SSM_CH), lambda i: (0, i, 0)),
                  row(SSM_CH), full(dsk), full(wglu_bf), full(wout_bf), row(D_MODEL),
                  pl.BlockSpec((1, 1, 6 * D_MODEL), lambda i: (mod_row_fn(i // tiles_per_seq), 0, 0)),
                  full(g), full(wr_pad)],
        out_specs=[row(D_MODEL), row(D_MODEL), row(LANES)],
        out_shape=[jax.ShapeDtypeStruct((T, D_MODEL), F32),
                   jax.ShapeDtypeStruct((T, D_MODEL), BF16),
                   jax.ShapeDtypeStruct((T, LANES), F32)],
        compiler_params=_cparams(("arbitrary",)),
        name="outproj",
    )(ona, owa, y2.reshape(2, T, SSM_CH), u, dsk, wglu_bf, wout_bf, x, mod_l, g, wr_pad)


def _select_body(cap, aff_ref, pos_ref, post_ref, afft_ref, plan_ref):
    L = aff_ref.shape[1]
    aff = aff_ref[0]
    for r in range(1, SEL_GROUP):
        aff = aff + pltpu.roll(aff_ref[r], r * N_EXPERTS, 1)
    capf = jnp.float32(cap)

    def bisect(k, thr):
        cand = thr | jnp.left_shift(jnp.int32(1), 30 - k)
        cnt = jnp.sum(jnp.where(aff >= pltpu.bitcast(cand, F32), 1.0, 0.0), axis=0, keepdims=True)
        return jnp.where(cnt >= capf, cand, thr)

    thr = lax.fori_loop(0, 31, bisect, jnp.zeros((1, LANES), jnp.int32))
    kth = pltpu.bitcast(thr, F32)
    gt = jnp.where(aff > kth, 1.0, 0.0)
    eq = jnp.where(aff == kth, 1.0, 0.0)
    need = capf - jnp.sum(gt, axis=0, keepdims=True)
    ck = SEL_CHUNK
    tri = (lax.broadcasted_iota(jnp.int32, (ck, ck), 0)
           >= lax.broadcasted_iota(jnp.int32, (ck, ck), 1))
    tri = jnp.where(tri, 1.0, 0.0).astype(BF16)

    def prefix(mask):
        outs = []
        carries = [jnp.zeros((1, LANES), F32)]
        for j in range(L // ck):
            c = _dot(tri, mask[j * ck:(j + 1) * ck].astype(BF16)) + carries[-1]
            outs.append(c)
            carries.append(c[ck - 1:ck, :])
        return jnp.concatenate(outs, axis=0), jnp.concatenate(carries, axis=0)

    tie = jnp.where(prefix(eq)[0] <= need, eq, 0.0)
    sel = gt + tie
    rank, starts = prefix(sel)
    pos = jnp.where(sel > 0.0, rank - 1.0, -1.0)
    pos_ref[0] = pos
    post_ref[0] = jnp.transpose(pos)
    afft_ref[0] = jnp.transpose(aff)
    nt = L // ck
    first, last = starts[0:nt], starts[1:nt + 1]
    win = min(cap, COMBINE_WINDOW)
    base = jnp.minimum(jnp.floor(first * (1.0 / BF16_ROWS)) * BF16_ROWS, float(cap - win))
    fit = jnp.where(last - base <= float(win), 1.0, 0.0)
    plan_ref[0] = jnp.concatenate([base, fit], axis=0).astype(jnp.int32)


def _select(aff, B, L):
    cap = EC_CAPACITY * L // N_EXPERTS
    ng = B // SEL_GROUP
    return pl.pallas_call(
        functools.partial(_select_body, cap),
        grid=(ng,),
        in_specs=[pl.BlockSpec((SEL_GROUP, L, LANES), lambda i: (i, 0, 0))],
        out_specs=[pl.BlockSpec((1, L, LANES), lambda i: (i, 0, 0)),
                   pl.BlockSpec((1, LANES, L), lambda i: (i, 0, 0)),
                   pl.BlockSpec((1, LANES, L), lambda i: (i, 0, 0)),
                   pl.BlockSpec((1, 2 * (L // SEL_CHUNK), LANES), lambda i: (i, 0, 0))],
        out_shape=[jax.ShapeDtypeStruct((ng, L, LANES), F32),
                   jax.ShapeDtypeStruct((ng, LANES, L), F32),
                   jax.ShapeDtypeStruct((ng, LANES, L), F32),
                   jax.ShapeDtypeStruct((ng, 2 * (L // SEL_CHUNK), LANES), jnp.int32)],
        compiler_params=_cparams(("arbitrary",)),
        name="ec_select",
    )(aff.reshape(B, L, LANES))


def _window_rows(base, w):
    return pl.ds(base, w) if isinstance(base, int) else pl.ds(pl.multiple_of(base, BF16_ROWS), w)


def _gather_body(cap, win, plan_ref, h_ref, post_ref, afft_ref, xs_ref, gs_ref):
    b = pl.program_id(0)
    j = pl.program_id(1)
    grp = b // SEL_GROUP
    lane0 = (b % SEL_GROUP) * N_EXPERTS

    @pl.when(j == 0)
    def _():
        xs_ref[...] = jnp.zeros_like(xs_ref)
        gs_ref[...] = jnp.zeros_like(gs_ref)

    n_req, ck = h_ref.shape[0], h_ref.shape[1]

    def run(w, bases, r=0):
        h = h_ref[r]
        slot = lax.broadcasted_iota(jnp.int32, (w, ck), 0)
        onehots = []
        for e in range(N_EXPERTS):
            rows = _window_rows(bases[e], w)
            row = r * N_EXPERTS + e
            hit = post_ref[0, row:row + 1, :] == (slot + bases[e]).astype(F32)
            gs_ref[e, r, rows, :] += jnp.sum(jnp.where(hit, afft_ref[0, row:row + 1, :], 0.0), axis=1, keepdims=True)
            onehots.append(jnp.where(hit, 1.0, 0.0).astype(BF16))
        xs = _dot(jnp.concatenate(onehots, axis=0), h).astype(BF16)
        for e in range(N_EXPERTS):
            rows = _window_rows(bases[e], w)
            xs_ref[e, r, rows, :] += xs[e * w:(e + 1) * w]

    if win == cap:
        for r in range(n_req):
            run(cap, [0] * N_EXPERTS, r)
        return
    bases = []
    fits = None
    nt = pl.num_programs(1)
    for e in range(N_EXPERTS):
        bases.append(plan_ref[grp, j, lane0 + e])
        ok = plan_ref[grp, nt + j, lane0 + e]
        fits = ok if fits is None else fits + ok
    fits = fits == N_EXPERTS

    @pl.when(fits)
    def _():
        run(win, bases)

    @pl.when(jnp.logical_not(fits))
    def _():
        run(cap, [0] * N_EXPERTS)


def _gather(starts, h2, post, afft, B, L):
    cap = EC_CAPACITY * L // N_EXPERTS
    ck = SEL_CHUNK
    win = min(cap, COMBINE_WINDOW)
    nr = SMALL_STEP_REQUESTS if win == cap else 1
    per_group = SEL_GROUP // nr
    lane_blk = lambda b, j, s: (b // per_group, b % per_group, j)
    grid_spec = pltpu.PrefetchScalarGridSpec(
        num_scalar_prefetch=1,
        grid=(B // nr, L // ck),
        in_specs=[pl.BlockSpec((nr, ck, D_MODEL), lambda b, j, s: (b, j, 0)),
                  pl.BlockSpec((1, nr * N_EXPERTS, ck), lane_blk),
                  pl.BlockSpec((1, nr * N_EXPERTS, ck), lane_blk)],
        out_specs=[pl.BlockSpec((N_EXPERTS, nr, cap, D_MODEL), lambda b, j, s: (0, b, 0, 0)),
                   pl.BlockSpec((N_EXPERTS, nr, cap, 1), lambda b, j, s: (0, b, 0, 0))],
    )
    xs, gs = pl.pallas_call(
        functools.partial(_gather_body, cap, win),
        grid_spec=grid_spec,
        out_shape=[jax.ShapeDtypeStruct((N_EXPERTS, B, cap, D_MODEL), BF16),
                   jax.ShapeDtypeStruct((N_EXPERTS, B, cap, 1), F32)],
        compiler_params=_cparams(("arbitrary", "arbitrary")),
        name="ec_gather",
    )(starts, h2.reshape(B, L, D_MODEL), post, afft)
    return xs.reshape(N_EXPERTS, B * cap, D_MODEL), gs.reshape(N_EXPERTS, B * cap, 1)


def _ffn_body(n_first, xa_ref, ga_ref, xb_ref, gb_ref, wg_ref, wu_ref, wd_ref, y_ref, wgb, wub, wdb):
    j = pl.program_id(1)

    @pl.when(j == 0)
    def _():
        wgb[...] = wg_ref[0, 0].astype(BF16)
        wub[...] = wu_ref[0, 0].astype(BF16)
        wdb[...] = wd_ref[0, 0].astype(BF16)

    def run(xs_ref, gs_ref):
        x = xs_ref[0]
        a = _dot(x, wgb[...])
        u = _dot(x, wub[...])
        hm = (a * _sigmoid(a) * u).astype(BF16)
        y_ref[0] = (_dot(hm, wdb[...]) * gs_ref[0]).astype(BF16)

    @pl.when(j < n_first)
    def _():
        run(xa_ref, ga_ref)

    @pl.when(j >= n_first)
    def _():
        run(xb_ref, gb_ref)


def _ffn(xa, ga, xb, gb, wg, wu, wd, l):
    rc = FFN_ROWS
    na = xa.shape[1] // rc
    nb = xb.shape[1] // rc
    first = lambda w: pl.BlockSpec((1, rc, w), lambda e, j: (e, jnp.minimum(j, na - 1), 0))
    second = lambda w: pl.BlockSpec((1, rc, w), lambda e, j: (e, jnp.maximum(j - na, 0), 0))
    ahead = lambda e, j: jnp.minimum(e + jnp.minimum(j, 1), N_EXPERTS - 1)
    wspec = lambda a, b: pl.BlockSpec((1, 1, a, b), lambda e, j: (l, ahead(e, j), 0, 0))
    return pl.pallas_call(
        functools.partial(_ffn_body, na),
        grid=(N_EXPERTS, na + nb),
        in_specs=[first(D_MODEL), first(1), second(D_MODEL), second(1),
                  wspec(D_MODEL, EXPERT_FF), wspec(D_MODEL, EXPERT_FF), wspec(EXPERT_FF, D_MODEL)],
        out_specs=pl.BlockSpec((1, rc, D_MODEL), lambda e, j: (e, j, 0)),
        out_shape=jax.ShapeDtypeStruct((N_EXPERTS, (na + nb) * rc, D_MODEL), BF16),
        scratch_shapes=[pltpu.VMEM((D_MODEL, EXPERT_FF), BF16),
                        pltpu.VMEM((D_MODEL, EXPERT_FF), BF16),
                        pltpu.VMEM((EXPERT_FF, D_MODEL), BF16)],
        compiler_params=_cparams(("arbitrary", "arbitrary")),
        name="ec_ffn",
    )(xa, ga, xb, gb, wg, wu, wd)


def _combine_body(cap, win, plan_ref, pos_ref, y_ref, x1_ref, mod_ref, g_ref, o_ref):
    b = pl.program_id(0)
    i = pl.program_id(1)
    pos = pos_ref[0].astype(BF16)
    tq = pos.shape[0]
    g2 = mod_ref[0][:, 5 * D_MODEL:6 * D_MODEL]

    def finish(w, bases, values, lane0, rows=slice(None)):
        n = N_EXPERTS * w
        col = lax.broadcasted_iota(jnp.int32, (1, n), 1)
        expert = jnp.zeros((1, n), jnp.int32)
        for e in range(1, N_EXPERTS):
            expert = expert + jnp.where(col >= e * w, 1, 0)
        target = col - expert * w
        if bases is not None:
            for e in range(N_EXPERTS):
                target = jnp.where(expert == e, target + bases[e], target)
        expand = jnp.where(lax.broadcasted_iota(jnp.int32, (LANES, n), 0) == expert + lane0, 1.0, 0.0)
        per_col = _dot(pos, expand.astype(BF16))
        onehot = jnp.where(per_col == target.astype(F32), 1.0, 0.0).astype(BF16)
        f = _dot(onehot, values)
        o_ref[rows, :] = x1_ref[rows, :] + g2 * _rms(f, g_ref[3:4, :])

    if win == cap:
        n_req = y_ref.shape[1] // cap
        for r in range(n_req):
            values = y_ref[:, r * cap:(r + 1) * cap, :].reshape(N_EXPERTS * cap, D_MODEL)
            finish(cap, None, values, ((b * n_req + r) % SEL_GROUP) * N_EXPERTS, slice(r * tq, (r + 1) * tq))
        return
    grp = b // SEL_GROUP
    lane0 = (b % SEL_GROUP) * N_EXPERTS
    bases = []
    fits = None
    nt = pl.num_programs(1)
    for e in range(N_EXPERTS):
        bases.append(plan_ref[grp, i, lane0 + e])
        ok = plan_ref[grp, nt + i, lane0 + e]
        fits = ok if fits is None else fits + ok
    fits = fits == N_EXPERTS

    @pl.when(fits)
    def _():
        rows = [y_ref[e, pl.ds(pl.multiple_of(bases[e], BF16_ROWS), win), :] for e in range(N_EXPERTS)]
        finish(win, bases, jnp.concatenate(rows, axis=0), lane0)

    @pl.when(jnp.logical_not(fits))
    def _():
        finish(cap, None, y_ref[...].reshape(N_EXPERTS * cap, D_MODEL), lane0)


def _combine(starts, pos, y, row_off, x1, mod_l, mod_row_fn, g, B, L, shared_mod=False):
    cap = EC_CAPACITY * L // N_EXPERTS
    tq = TOKEN_TILE
    assert tq == SEL_CHUNK
    nq = L // tq
    blk_off = row_off // cap
    win = min(cap, COMBINE_WINDOW)
    nr = SMALL_STEP_REQUESTS if (shared_mod and win == cap and nq == 1 and blk_off % SMALL_STEP_REQUESTS == 0) else 1
    per_group = SEL_GROUP // nr
    grid_spec = pltpu.PrefetchScalarGridSpec(
        num_scalar_prefetch=1,
        grid=(B // nr, nq),
        in_specs=[pl.BlockSpec((1, tq, LANES), lambda b, i, s: (b // per_group, i, 0)),
                  pl.BlockSpec((N_EXPERTS, nr * cap, D_MODEL), lambda b, i, s: (0, blk_off // nr + b, 0)),
                  pl.BlockSpec((nr * tq, D_MODEL), lambda b, i, s: (b * nq + i, 0)),
                  pl.BlockSpec((1, 1, 6 * D_MODEL), lambda b, i, s: (mod_row_fn(b * nr), 0, 0)),
                  pl.BlockSpec((4, D_MODEL), lambda b, i, s: (0, 0))],
        out_specs=pl.BlockSpec((nr * tq, D_MODEL), lambda b, i, s: (b * nq + i, 0)),
    )
    return pl.pallas_call(
        functools.partial(_combine_body, cap, win),
        grid_spec=grid_spec,
        out_shape=jax.ShapeDtypeStruct((B * L, D_MODEL), F32),
        compiler_params=_cparams(("arbitrary", "arbitrary")),
        name="ec_combine",
    )(starts, pos, y, x1, mod_l, g)


def _rope_tables(L):
    t = jnp.arange(L)
    row = (t // GRID_W).astype(F32)
    col = (t % GRID_W).astype(F32)
    half = HEAD_DIM // 4
    inv = ROPE_BASE ** (-jnp.arange(half, dtype=F32) / half)
    d = np.arange(LANES) % HEAD_DIM
    use_col = (d // (HEAD_DIM // 2)) == 1
    pos = jnp.where(use_col[None, :], col[:, None], row[:, None])
    ang = pos * inv[d % half][None, :]
    sign = np.where((d % (HEAD_DIM // 2)) < half, -1.0, 1.0).astype(np.float32)
    return jnp.cos(ang), jnp.sin(ang) * sign[None, :]


def _permute_wa_heads(a, axis, start):
    cut = lambda lo, hi: lax.slice_in_dim(a, lo, hi, axis=axis)
    heads = [cut(start + h * HEAD_DIM, start + (h + 1) * HEAD_DIM) for h in WA_HEAD_ORDER]
    return jnp.concatenate([cut(0, start)] + heads + [cut(start + WA_WIDTH, a.shape[axis])], axis=axis)


def kernel(x_prompt, x_sample, c, cache_na_k, cache_na_v, cache_wa_k, cache_wa_v, state_ssm, c_ctx, w_ada, b_ada, norm_g, w_in, w_out, na_rpb, wa_sink, ssm_lambda_re, ssm_lambda_im, ssm_log_dt, ssm_b_re, ssm_b_im, ssm_c_re, ssm_c_im, ssm_d, w_glu, w_router, w_exp_gate, w_exp_up, w_exp_down):
    Bc, Lc, _ = x_prompt.shape
    Bs, Ls, _ = x_sample.shape
    ctx_row = Bs
    n_cond = ((Bs + 1 + SUBLANES - 1) // SUBLANES) * SUBLANES
    cond = jnp.zeros((n_cond, D_MODEL), F32).at[0:Bs].set(c).at[ctx_row].set(c_ctx)
    mod = _modulation(cond, w_ada, b_ada)
    mod = mod.reshape(DEPTH, n_cond, 1, 6 * D_MODEL)

    w_in_bf = _permute_wa_heads(w_in, 2, C_WAQ).astype(BF16)
    w_out_bf = _permute_wa_heads(w_out, 1, NA_WIDTH).astype(BF16)
    w_glu_bf = w_glu.astype(BF16)
    wr_hi = w_router.astype(BF16)
    wr_lo = (w_router - wr_hi.astype(F32)).astype(BF16)
    wr_pad = jnp.pad(jnp.concatenate([wr_hi, wr_lo], axis=-1), ((0, 0), (0, 0), (0, LANES - 2 * N_EXPERTS)))
    s5_a, s5_bb = _s5_prep(ssm_lambda_re, ssm_lambda_im, ssm_log_dt, ssm_b_re, ssm_b_im)
    s5_c = _s5_c_matrix(ssm_c_re, ssm_c_im)
    rope_tabs = _rope_tables(Ls)

    ctx_mod_row = lambda b: ctx_row + 0 * b
    lat_mod_row = lambda b: b

    xp = x_prompt.reshape(Bc * Lc, D_MODEL)
    xs = x_sample.reshape(Bs * Ls, D_MODEL)
    new_kv = tuple(jnp.zeros((Bc, DEPTH, Lc, w), F32) for w in (NA_WIDTH, NA_WIDTH, WA_KV_WIDTH, WA_KV_WIDTH))
    new_ssm = []
    for l in range(DEPTH):
        g = norm_g[l]
        dsk = ssm_d[l].reshape(1, SSM_CH)
        sl2 = slice(2 * l, 2 * l + 2)
        qna, kna, vna, qwa, kwa, vwa, u = _inproj(xp, mod, ctx_mod_row, norm_g, w_in_bf, l, Bc, Lc, None, new_kv)
        new_kv = (kna, vna, kwa, vwa)
        ona, owa = _attn_ctx(wa_sink[l], qna, kna, vna, qwa, kwa, vwa, l, Bc, Lc)
        y2, hfin = _s5_scan(u, s5_bb[sl2], s5_a[sl2], s5_c[sl2],
                            jnp.zeros((2, Bc, 2 * SSM_N), F32), Bc, Lc)
        x1, h2, aff = _outproj(ona, owa, y2, u, dsk, w_glu_bf[l], w_out_bf[l], xp, mod[l], ctx_mod_row,
                               g, wr_pad[l], Lc)
        pos_c, post, afft, starts_c = _select(aff, Bc, Lc)
        xg_c, gs_c = _gather(starts_c, h2, post, afft, Bc, Lc)
        x1_c = x1
        new_ssm.append(jnp.transpose(hfin.reshape(2, Bc, 2, SSM_GROUPS, SSM_STATE), (1, 0, 2, 3, 4)))
        qna, kna, vna, qwa, kwa, vwa, u = _inproj(xs, mod, lat_mod_row, norm_g, w_in_bf, l, Bs, Ls, rope_tabs)
        ona = _attn_na(qna, kna, vna, cache_na_k, cache_na_v, _na_bias_pairs(na_rpb[l]), l, Bs, Ls)
        owa = _attn_wa(wa_sink[l], qwa, kwa, vwa, cache_wa_k, cache_wa_v, l, Bs, Ls)
        h0 = jnp.transpose(state_ssm[:, l].reshape(Bs, 2, 2 * SSM_N), (1, 0, 2))
        y2, _ = _s5_scan(u, s5_bb[sl2], s5_a[sl2], s5_c[sl2], h0, Bs, Ls)
        x1, h2, aff = _outproj(ona, owa, y2, u, dsk, w_glu_bf[l], w_out_bf[l], xs, mod[l], lat_mod_row,
                               g, wr_pad[l], Ls)
        pos_s, post, afft, starts_s = _select(aff, Bs, Ls)
        xg_s, gs_s = _gather(starts_s, h2, post, afft, Bs, Ls)
        yy = _ffn(xg_c, gs_c, xg_s, gs_s, w_exp_gate, w_exp_up, w_exp_down, l)
        xp = _combine(starts_c, pos_c, yy, 0, x1_c, mod[l], ctx_mod_row, g, Bc, Lc, shared_mod=True)
        xs = _combine(starts_s, pos_s, yy, xg_c.shape[1], x1, mod[l], lat_mod_row, g, Bs, Ls)
    return (xp.reshape(Bc, Lc, D_MODEL), xs.reshape(Bs, Ls, D_MODEL),
            new_kv[0].reshape(Bc, DEPTH, Lc, NA_HEADS, HEAD_DIM),
            new_kv[1].reshape(Bc, DEPTH, Lc, NA_HEADS, HEAD_DIM),
            new_kv[2].reshape(Bc, DEPTH, Lc, WA_KV_HEADS, HEAD_DIM),
            new_kv[3].reshape(Bc, DEPTH, Lc, WA_KV_HEADS, HEAD_DIM),
            jnp.stack(new_ssm, axis=1))
```

```python
import functools
import math

import numpy as np
import jax
import jax.numpy as jnp
from jax import lax
from jax.experimental import pallas as pl
from jax.experimental.pallas import tpu as pltpu

F32 = jnp.float32
BF16 = jnp.bfloat16

D_MODEL = 1024
DEPTH = 2
GRID_W = 64
HEAD_DIM = 64
NA_HEADS = 6
NA_KH = 8
NA_KW = 16
WA_HEADS = 6
WA_KV_HEADS = 2
WA_WINDOW = 128
WA_QBLOCK = 256
SSM_CH = 256
SSM_GROUP_CH = 16
SSM_GROUPS = 16
SSM_STATE = 64
SSM_N = SSM_GROUPS * SSM_STATE
NA_WIDTH = NA_HEADS * HEAD_DIM
WA_WIDTH = WA_HEADS * HEAD_DIM
WA_KV_WIDTH = WA_KV_HEADS * HEAD_DIM
IN_COLS = 2048
N_EXPERTS = 16
EC_CAPACITY = 2
EXPERT_FF = 1024
ROPE_BASE = 10000.0
RMS_EPS = 1e-6
NEG_INF = -1e30
ATTN_SCALE = HEAD_DIM ** -0.5
LOG2E = math.log2(math.e)
Q_SCALE = ATTN_SCALE * LOG2E

LANES = 128
SUBLANES = 8
VMEM_LIMIT = 48 * 1024 * 1024

C_NAQ, C_NAK, C_NAV, C_WAQ, C_WAK, C_WAV, C_SSU = 0, 384, 768, 1152, 1536, 1664, 1792
WA_HEAD_ORDER = (0, 3, 1, 4, 2, 5)

NA_QROWS = 4
NA_KROWS = NA_QROWS + NA_KH
NA_REL = 2 * NA_KH - 1
NA_PAIRS = 3 * NA_REL
TOKEN_TILE = 256
OUTPROJ_SUBTILE = 128
PROJ_TILE = 512
SCAN_TILE = 128
SMALL_STEP_REQUESTS = 4
CTX_ATTN_REQUESTS = 4
SCAN_BATCH = SUBLANES
FFN_ROWS = 512
SEL_CHUNK = 256
SEL_GROUP = LANES // N_EXPERTS
BF16_ROWS = 2 * SUBLANES
COMBINE_WINDOW = 64


def _cparams(sem):
    return pltpu.CompilerParams(dimension_semantics=sem, vmem_limit_bytes=VMEM_LIMIT)


def _sigmoid(x):
    return 1.0 / (1.0 + jnp.exp(-x))


def _rms(x, g):
    ms = jnp.mean(x * x, axis=-1, keepdims=True)
    return x * lax.rsqrt(ms + RMS_EPS) * g


def _dot(a, b):
    return jnp.dot(a, b, preferred_element_type=F32)


def _dot_nt(a, b):
    return lax.dot_general(a, b, (((1,), (1,)), ((), ())), preferred_element_type=F32)


def _mod_body(c_ref, w_ref, b_ref, o_ref):
    c = c_ref[...]
    s = c * _sigmoid(c)
    o_ref[0] = _dot(s.astype(BF16), w_ref[0].astype(BF16)) + b_ref[0]


def _modulation(cond, w_ada, b_ada):
    n = cond.shape[0]
    tn = 1024
    return pl.pallas_call(
        _mod_body,
        grid=(DEPTH, 6 * D_MODEL // tn),
        in_specs=[
            pl.BlockSpec((n, D_MODEL), lambda l, j: (0, 0)),
            pl.BlockSpec((1, D_MODEL, tn), lambda l, j: (l, 0, j)),
            pl.BlockSpec((1, 1, tn), lambda l, j: (l, 0, j)),
        ],
        out_specs=pl.BlockSpec((1, n, tn), lambda l, j: (l, 0, j)),
        out_shape=jax.ShapeDtypeStruct((DEPTH, n, 6 * D_MODEL), F32),
        compiler_params=_cparams(("arbitrary", "arbitrary")),
        name="modulation",
    )(cond, w_ada, b_ada.reshape(DEPTH, 1, 6 * D_MODEL))


def _rope_chunk(x, cos, sin_signed):
    lane = lax.broadcasted_iota(jnp.int32, x.shape, 1)
    axis_dims = HEAD_DIM // 2
    half = axis_dims // 2
    first = (lane % axis_dims) < half
    rot = jnp.where(first, pltpu.roll(x, LANES - half, 1), pltpu.roll(x, half, 1))
    return x * cos + rot * sin_signed


INPROJ_KV_OUTPUTS = (1, 2, 4, 5)


def _inproj_body(rope, n_alias, layer, x_ref, mod_ref, g_ref, w_ref, *rest):
    if rope:
        cos_ref, sin_ref = rest[:2]
        rest = rest[2:]
    qna_ref, kna_ref, vna_ref, qwa_ref, kwa_ref, vwa_ref, u_ref = rest[n_alias:]

    def put(ref, val):
        if len(ref.shape) == 4 and ref.shape[1] > 1:
            for d in range(ref.shape[1]):
                ref[0, d] = val if d == layer else jnp.zeros_like(val)
        else:
            ref[...] = val.reshape(ref.shape)

    x = x_ref[...]
    m = mod_ref[0, 0]
    sh1 = m[:, 0:D_MODEL]
    sc1 = m[:, D_MODEL:2 * D_MODEL]
    h = _rms(x, g_ref[0, 0:1, :]) * (1.0 + sc1) + sh1
    z = _dot(h.astype(BF16), w_ref[0])
    qna_ref[...] = (z[:, C_NAQ:C_NAK] * Q_SCALE).astype(BF16)
    put(kna_ref, z[:, C_NAK:C_NAV])
    put(vna_ref, z[:, C_NAV:C_WAQ])
    put(vwa_ref, z[:, C_WAV:C_SSU])
    u_ref[...] = z[:, C_SSU:IN_COLS]
    if rope:
        cos = cos_ref[...]
        sin = sin_ref[...]
        for j in range(WA_WIDTH // LANES):
            qc = _rope_chunk(z[:, C_WAQ + j * LANES:C_WAQ + (j + 1) * LANES], cos, sin)
            qwa_ref[:, j * LANES:(j + 1) * LANES] = (qc * Q_SCALE).astype(BF16)
        put(kwa_ref, _rope_chunk(z[:, C_WAK:C_WAV], cos, sin))
    else:
        qwa_ref[...] = (z[:, C_WAQ:C_WAK] * Q_SCALE).astype(BF16)
        put(kwa_ref, z[:, C_WAK:C_WAV])


def _inproj(x, mod, mod_row_fn, norm_g, w_bf, l, B, L, rope_tabs, kv_layers=None):
    T = x.shape[0]
    tm = min(PROJ_TILE, L)
    tiles_per_seq = L // tm
    in_specs = [
        pl.BlockSpec((tm, D_MODEL), lambda i: (i, 0)),
        pl.BlockSpec((1, 1, 1, 6 * D_MODEL), lambda i: (l, mod_row_fn(i // tiles_per_seq), 0, 0)),
        pl.BlockSpec((1, 4, D_MODEL), lambda i: (l, 0, 0)),
        pl.BlockSpec((1, D_MODEL, IN_COLS), lambda i: (l, 0, 0)),
    ]
    args = [x, mod, norm_g, w_bf]
    if rope_tabs is not None:
        in_specs += [pl.BlockSpec((tm, LANES), lambda i: (i % tiles_per_seq, 0))] * 2
        args += list(rope_tabs)
    widths = (NA_WIDTH, NA_WIDTH, NA_WIDTH, WA_WIDTH, WA_KV_WIDTH, WA_KV_WIDTH, SSM_CH)
    dtypes = (BF16, F32, F32, BF16, F32, F32, F32)
    out_specs = [pl.BlockSpec((tm, w), lambda i: (i, 0)) for w in widths]
    out_shape = [jax.ShapeDtypeStruct((T, w), dt) for w, dt in zip(widths, dtypes)]
    aliases = {}
    n_alias = 0
    if kv_layers is not None:
        create = len(kv_layers) == 0
        for o in INPROJ_KV_OUTPUTS:
            out_specs[o] = pl.BlockSpec((1, DEPTH if create else 1, tm, widths[o]),
                                        lambda i: (i // tiles_per_seq, 0 if create else l, i % tiles_per_seq, 0))
            out_shape[o] = jax.ShapeDtypeStruct((B, DEPTH, L, widths[o]), dtypes[o])
        n_alias = len(kv_layers)
        for k, (o, prev) in enumerate(zip(INPROJ_KV_OUTPUTS, kv_layers)):
            aliases[len(args)] = o
            in_specs.append(pl.BlockSpec(memory_space=pl.ANY))
            args.append(prev)
    return pl.pallas_call(
        functools.partial(_inproj_body, rope_tabs is not None, n_alias, l),
        grid=(T // tm,),
        in_specs=in_specs,
        out_specs=out_specs,
        out_shape=out_shape,
        input_output_aliases=aliases,
        compiler_params=_cparams(("arbitrary",)),
        name="inproj",
    )(*args)


def _lane_lo(shape):
    return lax.broadcasted_iota(jnp.int32, shape, len(shape) - 1) < HEAD_DIM


def _keep_half(q, lo, half):
    keep = jnp.where(lo, 1.0 - half, 0.0 + half).astype(q.dtype)
    return q * keep


def _values_and_ones(v, lo, half):
    one = jnp.ones_like(v)
    return (jnp.where(lo, v, one) if half == 0 else jnp.where(lo, one, v)).astype(BF16)


def _normalise(o2, extra=None):
    den = pltpu.roll(o2, HEAD_DIM, 1)
    if extra is not None:
        den = den + extra
    return o2 / den


def _attn_ctx_body(sink_ref, qna, kna, vna, qwa, kwa, vwa, ona, owa):
    R, L = qna.shape[0], qna.shape[1]
    npair = WA_WIDTH // LANES
    lo = _lane_lo((L, LANES))
    lo_all = _lane_lo((npair * L, LANES))
    blk = lax.broadcasted_iota(jnp.int32, (npair * L, 1), 0) // L
    for r in range(R):
        for j in range(NA_WIDTH // LANES):
            sl = slice(j * LANES, (j + 1) * LANES)
            q2 = qna[r, :, sl]
            k2 = kna[r, 0, :, sl].astype(BF16)
            v2 = vna[r, 0, :, sl].astype(BF16)
            halves = []
            for half in range(2):
                s = _dot_nt(_keep_half(q2, lo, half), k2)
                p = jnp.exp2(s - jnp.max(s, axis=-1, keepdims=True))
                halves.append(_dot(p.astype(BF16), v2) / jnp.sum(p, axis=-1, keepdims=True))
            ona[r, :, sl] = jnp.where(lo, halves[0], halves[1]).astype(BF16)
        kw = kwa[r, 0].astype(BF16)
        vw = vwa[r, 0].astype(BF16)
        q_all = jnp.concatenate([qwa[r, :, j * LANES:(j + 1) * LANES] for j in range(npair)], axis=0)
        outs = []
        for g in range(WA_KV_HEADS):
            s = _dot_nt(_keep_half(q_all, lo_all, g), kw)
            sk = jnp.zeros((npair * L, 1), F32)
            for j in range(npair):
                sk = jnp.where(blk == j, sink_ref[WA_HEAD_ORDER[2 * j + g]] * LOG2E, sk)
            m = jnp.maximum(jnp.max(s, axis=-1, keepdims=True), sk)
            p = jnp.exp2(s - m)
            l = jnp.sum(p, axis=-1, keepdims=True) + jnp.exp2(sk - m)
            outs.append(_dot(p.astype(BF16), vw) / l)
        o_all = jnp.where(lo_all, outs[0], outs[1]).astype(BF16)
        for j in range(npair):
            owa[r, :, j * LANES:(j + 1) * LANES] = o_all[j * L:(j + 1) * L]


def _attn_ctx(sink, qna, kna, vna, qwa, kwa, vwa, l, B, L):
    R = CTX_ATTN_REQUESTS

    def spec(w):
        return pl.BlockSpec((R, L, w), lambda b: (b, 0, 0))

    def layer_spec(w):
        return pl.BlockSpec((R, 1, L, w), lambda b: (b, l, 0, 0))

    r3 = lambda a: a.reshape(B, L, a.shape[-1])
    ona, owa = pl.pallas_call(
        _attn_ctx_body,
        grid=(B // R,),
        in_specs=[pl.BlockSpec(memory_space=pltpu.SMEM),
                  spec(NA_WIDTH), layer_spec(NA_WIDTH), layer_spec(NA_WIDTH),
                  spec(WA_WIDTH), layer_spec(WA_KV_WIDTH), layer_spec(WA_KV_WIDTH)],
        out_specs=[spec(NA_WIDTH), spec(WA_WIDTH)],
        out_shape=[jax.ShapeDtypeStruct((B, L, NA_WIDTH), BF16),
                   jax.ShapeDtypeStruct((B, L, WA_WIDTH), BF16)],
        compiler_params=_cparams(("arbitrary",)),
        name="attn_ctx",
    )(sink, r3(qna), kna, vna, r3(qwa), kwa, vwa)
    return ona.reshape(B * L, NA_WIDTH), owa.reshape(B * L, WA_WIDTH)


def _na_key_start(rb, rows):
    return jnp.clip(rb * NA_QROWS - NA_KH // 2, 0, rows - NA_KROWS)


def _attn_na_body(rows, idx_ref, q_ref, k_ref, v_ref, kc_ref, vc_ref, pairs_ref, o_ref):
    rb = pl.program_id(1)
    nq = NA_QROWS * GRID_W
    nk = NA_KROWS * GRID_W
    start = pl.multiple_of(_na_key_start(rb, rows) * GRID_W, GRID_W)
    lo = _lane_lo((nq, LANES))
    lo_v = _lane_lo((1, LANES))
    npair = NA_KROWS // 2
    pat = jnp.where(rb == 0, 0, jnp.where(rb == rows // NA_QROWS - 1, 2, 1))
    entry = [[idx_ref[(pat * NA_QROWS + i) * npair + jj] for jj in range(npair)] for i in range(NA_QROWS)]

    def bias(head):
        return jnp.concatenate(
            [jnp.concatenate([pairs_ref[head, entry[i][jj]] for jj in range(npair)], axis=1)
             for i in range(NA_QROWS)], axis=0)

    for j in range(NA_WIDTH // LANES):
        sl = slice(j * LANES, (j + 1) * LANES)
        q2 = q_ref[0, :, sl]
        kl = k_ref[0, pl.ds(start, nk), sl].astype(BF16)
        vl = v_ref[0, pl.ds(start, nk), sl]
        kc = kc_ref[0, 0, :, sl].astype(BF16)
        vc = vc_ref[0, 0, :, sl]
        scores = []
        for half in range(2):
            qm = _keep_half(q2, lo, half)
            scores.append((_dot_nt(qm, kl) + bias(2 * j + half), _dot_nt(qm, kc)))
        probs = []
        for s_loc, s_ctx in scores:
            m = jnp.maximum(jnp.max(s_loc, axis=-1, keepdims=True),
                            jnp.max(s_ctx, axis=-1, keepdims=True))
            probs.append((jnp.exp2(s_loc - m).astype(BF16), jnp.exp2(s_ctx - m).astype(BF16)))
        halves = []
        for half, (p_loc, p_ctx) in enumerate(probs):
            o2 = (_dot(p_loc, _values_and_ones(vl, lo_v, half))
                  + _dot(p_ctx, _values_and_ones(vc, lo_v, half)))
            halves.append(_normalise(o2))
        o_ref[0, :, sl] = jnp.where(lo, halves[0], halves[1]).astype(BF16)


def _na_bias_pairs(rpb_l):
    cq = np.arange(GRID_W)
    col_start = np.clip(cq - NA_KW // 2, 0, GRID_W - NA_KW)
    col_ok = (cq[None, :] >= col_start[:, None]) & (cq[None, :] < col_start[:, None] + NA_KW)
    dcol = np.clip(cq[None, :] - cq[:, None], -(NA_KW - 1), NA_KW - 1) + (NA_KW - 1)
    pick = (dcol[None] == np.arange(2 * NA_KW - 1)[:, None, None]).astype(np.float32)
    tiles = jnp.einsum('hrd,dqc->hrqc', rpb_l.astype(F32) * LOG2E, pick, precision=lax.Precision.HIGHEST)
    tiles = jnp.where(col_ok[None, None], tiles, NEG_INF)
    masked = jnp.full_like(tiles, NEG_INF)
    both = jnp.concatenate([tiles[:, :-1], tiles[:, 1:]], axis=-1)
    right = jnp.concatenate([masked, tiles], axis=-1)
    left = jnp.concatenate([tiles, masked], axis=-1)
    none = jnp.concatenate([masked[:, :1], masked[:, :1]], axis=-1)
    return jnp.concatenate([both, right, left, none], axis=1)


def _na_pair_index(rows):
    nrb = rows // NA_QROWS
    idx = np.zeros((3, NA_QROWS, NA_KROWS // 2), np.int32)
    for pi, rb in enumerate((0, 1, nrb - 1)):
        ks = int(np.clip(rb * NA_QROWS - NA_KH // 2, 0, rows - NA_KROWS))
        for i in range(NA_QROWS):
            qr = rb * NA_QROWS + i
            rs = int(np.clip(qr - NA_KH // 2, 0, rows - NA_KH))
            for jj in range(NA_KROWS // 2):
                kr = ks + 2 * jj
                r = kr - qr + NA_KH - 1
                ok_l = rs <= kr < rs + NA_KH
                ok_r = rs <= kr + 1 < rs + NA_KH
                if ok_l and ok_r:
                    idx[pi, i, jj] = r
                elif ok_r:
                    idx[pi, i, jj] = (NA_REL - 1) + (r + 1)
                elif ok_l:
                    idx[pi, i, jj] = (NA_REL - 1) + NA_REL + r
                else:
                    idx[pi, i, jj] = NA_PAIRS - 1
    return jnp.asarray(idx.reshape(-1))


def _attn_na(q, k, v, cache_k, cache_v, pairs, l, B, L):
    rows = L // GRID_W
    nrb = rows // NA_QROWS
    nq = NA_QROWS * GRID_W
    P = cache_k.shape[2]
    r3 = lambda a: a.reshape(B, L, a.shape[-1])
    ck = cache_k.reshape(B, DEPTH, P, NA_WIDTH)
    cv = cache_v.reshape(B, DEPTH, P, NA_WIDTH)
    grid_spec = pltpu.PrefetchScalarGridSpec(
        num_scalar_prefetch=1,
        grid=(B, nrb),
        in_specs=[
            pl.BlockSpec((1, nq, NA_WIDTH), lambda b, r, s: (b, r, 0)),
            pl.BlockSpec((1, L, NA_WIDTH), lambda b, r, s: (b, 0, 0)),
            pl.BlockSpec((1, L, NA_WIDTH), lambda b, r, s: (b, 0, 0)),
            pl.BlockSpec((1, 1, P, NA_WIDTH), lambda b, r, s: (b, l, 0, 0)),
            pl.BlockSpec((1, 1, P, NA_WIDTH), lambda b, r, s: (b, l, 0, 0)),
            pl.BlockSpec(pairs.shape, lambda b, r, s: (0, 0, 0, 0)),
        ],
        out_specs=pl.BlockSpec((1, nq, NA_WIDTH), lambda b, r, s: (b, r, 0)),
    )
    o = pl.pallas_call(
        functools.partial(_attn_na_body, rows),
        grid_spec=grid_spec,
        out_shape=jax.ShapeDtypeStruct((B, L, NA_WIDTH), BF16),
        compiler_params=_cparams(("arbitrary", "arbitrary")),
        name="attn_na",
    )(_na_pair_index(rows), r3(q), r3(k), r3(v), ck, cv, pairs)
    return o.reshape(B * L, NA_WIDTH)


def _attn_wa_body(L, sink_ref, q_ref, k_ref, v_ref, kc_ref, vc_ref, o_ref):
    n = pl.program_id(1)
    nk = WA_QBLOCK + 2 * WA_WINDOW
    npair = WA_WIDTH // LANES
    start = pl.multiple_of(jnp.clip(n * WA_QBLOCK - WA_WINDOW, 0, L - nk), WA_WINDOW)
    kl = k_ref[0, pl.ds(start, nk), :].astype(BF16)
    vl = v_ref[0, pl.ds(start, nk), :]
    kc = kc_ref[0, 0].astype(BF16)
    vc = vc_ref[0, 0]
    lo_v = _lane_lo((1, LANES))
    rows = npair * WA_QBLOCK
    qpos = n * WA_QBLOCK + lax.broadcasted_iota(jnp.int32, (rows, nk), 0) % WA_QBLOCK
    kpos = start + lax.broadcasted_iota(jnp.int32, (rows, nk), 1)
    in_win = jnp.abs(kpos - qpos) <= WA_WINDOW
    lo = _lane_lo((rows, LANES))
    q_all = jnp.concatenate([q_ref[0, :, j * LANES:(j + 1) * LANES] for j in range(npair)], axis=0)
    blk = lax.broadcasted_iota(jnp.int32, (rows, 1), 0) // WA_QBLOCK
    scores = []
    for g in range(WA_KV_HEADS):
        qm = _keep_half(q_all, lo, g)
        scores.append((jnp.where(in_win, _dot_nt(qm, kl), NEG_INF), _dot_nt(qm, kc)))
    probs = []
    for g in range(WA_KV_HEADS):
        s_loc, s_ctx = scores[g]
        sk = jnp.zeros((rows, 1), F32)
        for j in range(npair):
            sk = jnp.where(blk == j, sink_ref[WA_HEAD_ORDER[2 * j + g]] * LOG2E, sk)
        m = jnp.maximum(jnp.maximum(jnp.max(s_loc, axis=-1, keepdims=True),
                                    jnp.max(s_ctx, axis=-1, keepdims=True)), sk)
        probs.append((jnp.exp2(s_loc - m).astype(BF16), jnp.exp2(s_ctx - m).astype(BF16), jnp.exp2(sk - m)))
    outs = []
    for g in range(WA_KV_HEADS):
        p_loc, p_ctx, p_sink = probs[g]
        o2 = _dot(p_loc, _values_and_ones(vl, lo_v, g)) + _dot(p_ctx, _values_and_ones(vc, lo_v, g))
        outs.append(_normalise(o2, p_sink))
    o_all = jnp.where(lo, outs[0], outs[1]).astype(BF16)
    for j in range(npair):
        o_ref[0, :, j * LANES:(j + 1) * LANES] = o_all[j * WA_QBLOCK:(j + 1) * WA_QBLOCK]


def _attn_wa(sink, q, k, v, cache_k, cache_v, l, B, L):
    nb = L // WA_QBLOCK
    P = cache_k.shape[2]
    r3 = lambda a: a.reshape(B, L, a.shape[-1])
    ck = cache_k.reshape(B, DEPTH, P, WA_KV_WIDTH)
    cv = cache_v.reshape(B, DEPTH, P, WA_KV_WIDTH)
    o = pl.pallas_call(
        functools.partial(_attn_wa_body, L),
        grid=(B, nb),
        in_specs=[
            pl.BlockSpec(memory_space=pltpu.SMEM),
            pl.BlockSpec((1, WA_QBLOCK, WA_WIDTH), lambda b, n: (b, n, 0)),
            pl.BlockSpec((1, L, WA_KV_WIDTH), lambda b, n: (b, 0, 0)),
            pl.BlockSpec((1, L, WA_KV_WIDTH), lambda b, n: (b, 0, 0)),
            pl.BlockSpec((1, 1, P, WA_KV_WIDTH), lambda b, n: (b, l, 0, 0)),
            pl.BlockSpec((1, 1, P, WA_KV_WIDTH), lambda b, n: (b, l, 0, 0)),
        ],
        out_specs=pl.BlockSpec((1, WA_QBLOCK, WA_WIDTH), lambda b, n: (b, n, 0)),
        out_shape=jax.ShapeDtypeStruct((B, L, WA_WIDTH), BF16),
        compiler_params=_cparams(("arbitrary", "arbitrary")),
        name="attn_wa",
    )(sink, r3(q), r3(k), r3(v), ck, cv)
    return o.reshape(B * L, WA_WIDTH)


def _s5_prep_body(lr_ref, li_ref, ldt_ref, br_ref, bi_ref, a_ref, bb_ref):
    lr = lr_ref[0]
    li = li_ref[0]
    dt = jnp.exp(ldt_ref[0])
    mag = jnp.exp(lr * dt)
    ar = mag * jnp.cos(li * dt)
    ai = mag * jnp.sin(li * dt)
    den = lr * lr + li * li
    nr = ar - 1.0
    fr = (nr * lr + ai * li) / den
    fi = (ai * lr - nr * li) / den
    a_ref[0, :, 0:SSM_N] = ar
    a_ref[0, :, SSM_N:2 * SSM_N] = ai
    br = br_ref[0]
    bi = bi_ref[0]
    bb_ref[0, :, 0:SSM_N] = (fr * br - fi * bi).astype(BF16)
    bb_ref[0, :, SSM_N:2 * SSM_N] = (fr * bi + fi * br).astype(BF16)


def _s5_prep(lam_re, lam_im, log_dt, b_re, b_im):
    n = DEPTH * 2
    eye = jnp.eye(SSM_GROUPS, dtype=F32)

    def blockdiag_b(b):
        return jnp.einsum('ngph,gk->nghkp', b.reshape(n, SSM_GROUPS, SSM_STATE, SSM_GROUP_CH), eye
                          ).reshape(n, SSM_CH, SSM_N)

    lr = lam_re.reshape(n, 1, SSM_N)
    li = lam_im.reshape(n, 1, SSM_N)
    ldt = jnp.repeat(log_dt.reshape(n, SSM_GROUPS), SSM_STATE, axis=-1).reshape(n, 1, SSM_N)
    vec = pl.BlockSpec((1, 1, SSM_N), lambda i: (i, 0, 0))
    mat = pl.BlockSpec((1, SSM_CH, SSM_N), lambda i: (i, 0, 0))
    return pl.pallas_call(
        _s5_prep_body,
        grid=(n,),
        in_specs=[vec, vec, vec, mat, mat],
        out_specs=[pl.BlockSpec((1, 1, 2 * SSM_N), lambda i: (i, 0, 0)),
                   pl.BlockSpec((1, SSM_CH, 2 * SSM_N), lambda i: (i, 0, 0))],
        out_shape=[jax.ShapeDtypeStruct((n, 1, 2 * SSM_N), F32),
                   jax.ShapeDtypeStruct((n, SSM_CH, 2 * SSM_N), BF16)],
        compiler_params=_cparams(("arbitrary",)),
        name="s5_prep",
    )(lr, li, ldt, blockdiag_b(b_re), blockdiag_b(b_im))


def _s5_c_matrix(c_re, c_im):
    n = DEPTH * 2
    eye = jnp.eye(SSM_GROUPS, dtype=F32)

    def blk(c):
        return jnp.einsum('nghp,gk->ngpkh', c.reshape(n, SSM_GROUPS, SSM_GROUP_CH, SSM_STATE), eye
                          ).reshape(n, SSM_N, SSM_CH)

    return jnp.concatenate([blk(c_re), -blk(c_im)], axis=1).astype(BF16)


def _s5_scan_body(tl, u_ref, bb_ref, a_ref, c_ref, h0_ref, y_ref, hfin_ref, bu, hb, hst):
    d = pl.program_id(0)
    i = pl.program_id(2)
    sb = SCAN_BATCH
    re = slice(0, SSM_N)
    im = slice(SSM_N, 2 * SSM_N)

    @pl.when(i == 0)
    def _():
        hst[...] = h0_ref[0]

    u_tb = jnp.transpose(u_ref[...], (1, 0, 2)).reshape(tl * sb, SSM_CH)
    u_bf = u_tb.astype(BF16)
    half = tl * sb // 2
    bu[0:half, :] = _dot(u_bf[0:half], bb_ref[0])
    bu[half:, :] = _dot(u_bf[half:], bb_ref[0])
    a = a_ref[0]
    ar = jnp.broadcast_to(a[:, re], (sb, SSM_N))
    ai = jnp.broadcast_to(a[:, im], (sb, SSM_N))

    def scan(reverse):
        def pair(s, carry):
            hr, hi = carry
            t0 = tl - 1 - 2 * s if reverse else 2 * s
            t1 = t0 - 1 if reverse else t0 + 1
            r0 = pl.ds(pl.multiple_of(t0 * sb, sb), sb)
            r1 = pl.ds(pl.multiple_of(t1 * sb, sb), sb)
            h0r = ar * hr - ai * hi + bu[r0, re]
            h0i = ar * hi + ai * hr + bu[r0, im]
            h1r = ar * h0r - ai * h0i + bu[r1, re]
            h1i = ar * h0i + ai * h0r + bu[r1, im]
            rows = pl.ds(pl.multiple_of((t1 if reverse else t0) * sb, 2 * sb), 2 * sb)
            early, late = ((h1r, h1i), (h0r, h0i)) if reverse else ((h0r, h0i), (h1r, h1i))
            hb[rows, re] = jnp.concatenate([early[0], late[0]], axis=0).astype(BF16)
            hb[rows, im] = jnp.concatenate([early[1], late[1]], axis=0).astype(BF16)
            return h1r, h1i

        hr, hi = lax.fori_loop(0, tl // 2, pair, (hst[:, re], hst[:, im]))
        hst[:, re] = hr
        hst[:, im] = hi
        hfin_ref[0, :, re] = hr
        hfin_ref[0, :, im] = hi

    @pl.when(d == 0)
    def _():
        scan(False)

    @pl.when(d == 1)
    def _():
        scan(True)

    y_tb = jnp.concatenate([_dot(hb[0:half, :], c_ref[0]), _dot(hb[half:, :], c_ref[0])], axis=0)
    y_ref[0] = jnp.transpose(y_tb.reshape(tl, sb, SSM_CH), (1, 0, 2))


def _s5_scan(u, bb, a, cm, h0, B, L):
    tl = SCAN_TILE
    nt = L // tl
    tile = lambda d, i: i + d * (nt - 1 - 2 * i)
    return pl.pallas_call(
        functools.partial(_s5_scan_body, tl),
        grid=(2, B // SCAN_BATCH, nt),
        in_specs=[
            pl.BlockSpec((SCAN_BATCH, tl, SSM_CH), lambda d, b, i: (b, tile(d, i), 0)),
            pl.BlockSpec((1, SSM_CH, 2 * SSM_N), lambda d, b, i: (d, 0, 0)),
            pl.BlockSpec((1, 1, 2 * SSM_N), lambda d, b, i: (d, 0, 0)),
            pl.BlockSpec((1, 2 * SSM_N, SSM_CH), lambda d, b, i: (d, 0, 0)),
            pl.BlockSpec((1, SCAN_BATCH, 2 * SSM_N), lambda d, b, i: (d, b, 0)),
        ],
        out_specs=[
            pl.BlockSpec((1, SCAN_BATCH, tl, SSM_CH), lambda d, b, i: (d, b, tile(d, i), 0)),
            pl.BlockSpec((1, SCAN_BATCH, 2 * SSM_N), lambda d, b, i: (d, b, 0)),
        ],
        out_shape=[jax.ShapeDtypeStruct((2, B, L, SSM_CH), F32),
                   jax.ShapeDtypeStruct((2, B, 2 * SSM_N), F32)],
        scratch_shapes=[pltpu.VMEM((SCAN_BATCH * tl, 2 * SSM_N), F32),
                        pltpu.VMEM((SCAN_BATCH * tl, 2 * SSM_N), BF16),
                        pltpu.VMEM((SCAN_BATCH, 2 * SSM_N), F32)],
        compiler_params=_cparams(("arbitrary", "arbitrary", "arbitrary")),
        name="s5_scan",
    )(u.reshape(B, L, SSM_CH), bb, a, cm, h0)


def _outproj_body(ona_ref, owa_ref, y_ref, u_ref, dsk_ref, wglu_ref, wout_ref, x_ref, mod_ref, g_ref,
                  wr_ref, x1_ref, h2_ref, aff_ref):
    m = mod_ref[0]
    g1 = m[:, 2 * D_MODEL:3 * D_MODEL]
    sh2 = m[:, 3 * D_MODEL:4 * D_MODEL]
    sc2 = m[:, 4 * D_MODEL:5 * D_MODEL]
    tm = x_ref.shape[0]
    sub = min(tm, OUTPROJ_SUBTILE)
    parts = [slice(r, r + sub) for r in range(0, tm, sub)]
    mixes = []
    for rs in parts:
        y = dsk_ref[...] * u_ref[rs, :] + y_ref[0, rs, :] + y_ref[1, rs, :]
        zg = _dot(y.astype(BF16), wglu_ref[...])
        oss = zg[:, 0:SSM_CH] * _sigmoid(zg[:, SSM_CH:2 * SSM_CH])
        mixes.append(_dot(ona_ref[rs, :], wout_ref[0:NA_WIDTH, :])
                     + _dot(owa_ref[rs, :], wout_ref[NA_WIDTH:NA_WIDTH + WA_WIDTH, :])
                     + _dot(oss.astype(BF16), wout_ref[NA_WIDTH + WA_WIDTH:D_MODEL, :]))
    split = []
    for rs, mix in zip(parts, mixes):
        x1 = x_ref[rs, :] + g1 * _rms(mix, g_ref[1:2, :])
        x1_ref[rs, :] = x1
        h2 = _rms(x1, g_ref[2:3, :]) * (1.0 + sc2) + sh2
        h2_hi = h2.astype(BF16)
        h2_ref[rs, :] = h2_hi
        split.append((h2_hi, (h2 - h2_hi.astype(F32)).astype(BF16)))
    for rs, (h2_hi, h2_lo) in zip(parts, split):
        r_hi = _dot(h2_hi, wr_ref[...])
        logits = r_hi + pltpu.roll(r_hi, LANES - N_EXPERTS, 1) + _dot(h2_lo, wr_ref[...])
        lane = lax.broadcasted_iota(jnp.int32, logits.shape, 1)
        logits = jnp.where(lane < N_EXPERTS, logits, NEG_INF)
        mx = jnp.max(logits, axis=-1, keepdims=True)
        p = jnp.exp(logits - mx)
        aff_ref[rs, :] = p / jnp.sum(p, axis=-1, keepdims=True)


def _outproj(ona, owa, y2, u, dsk, wglu_bf, wout_bf, x, mod_l, mod_row_fn, g, wr_pad, L):
    T = x.shape[0]
    tm = min(PROJ_TILE, L)
    tiles_per_seq = L // tm
    row = lambda w: pl.BlockSpec((tm, w), lambda i: (i, 0))
    full = lambda a: pl.BlockSpec(a.shape, lambda i: (0,) * a.ndim)
    return pl.pallas_call(
        _outproj_body,
        grid=(T // tm,),
        in_specs=[row(NA_WIDTH), row(WA_WIDTH),
                  pl.BlockSpec((2, tm, SSM_CH), lambda i: (0, i, 0)),
                  row(SSM_CH), full(dsk), full(wglu_bf), full(wout_bf), row(D_MODEL),
                  pl.BlockSpec((1, 1, 6 * D_MODEL), lambda i: (mod_row_fn(i // tiles_per_seq), 0, 0)),
                  full(g), full(wr_pad)],
        out_specs=[row(D_MODEL), row(D_MODEL), row(LANES)],
        out_shape=[jax.ShapeDtypeStruct((T, D_MODEL), F32),
                   jax.ShapeDtypeStruct((T, D_MODEL), BF16),
                   jax.ShapeDtypeStruct((T, LANES), F32)],
        compiler_params=_cparams(("arbitrary",)),
        name="outproj",
    )(ona, owa, y2.reshape(2, T, SSM_CH), u, dsk, wglu_bf, wout_bf, x, mod_l, g, wr_pad)


def _select_body(cap, aff_ref, pos_ref, post_ref, afft_ref, plan_ref):
    L = aff_ref.shape[1]
    aff = aff_ref[0]
    for r in range(1, SEL_GROUP):
        aff = aff + pltpu.roll(aff_ref[r], r * N_EXPERTS, 1)
    capf = jnp.float32(cap)

    def bisect(k, thr):
        cand = thr | jnp.left_shift(jnp.int32(1), 30 - k)
        cnt = jnp.sum(jnp.where(aff >= pltpu.bitcast(cand, F32), 1.0, 0.0), axis=0, keepdims=True)
        return jnp.where(cnt >= capf, cand, thr)

    thr = lax.fori_loop(0, 31, bisect, jnp.zeros((1, LANES), jnp.int32))
    kth = pltpu.bitcast(thr, F32)
    gt = jnp.where(aff > kth, 1.0, 0.0)
    eq = jnp.where(aff == kth, 1.0, 0.0)
    need = capf - jnp.sum(gt, axis=0, keepdims=True)
    ck = SEL_CHUNK
    tri = (lax.broadcasted_iota(jnp.int32, (ck, ck), 0)
           >= lax.broadcasted_iota(jnp.int32, (ck, ck), 1))
    tri = jnp.where(tri, 1.0, 0.0).astype(BF16)

    def prefix(mask):
        outs = []
        carries = [jnp.zeros((1, LANES), F32)]
        for j in range(L // ck):
            c = _dot(tri, mask[j * ck:(j + 1) * ck].astype(BF16)) + carries[-1]
            outs.append(c)
            carries.append(c[ck - 1:ck, :])
        return jnp.concatenate(outs, axis=0), jnp.concatenate(carries, axis=0)

    tie = jnp.where(prefix(eq)[0] <= need, eq, 0.0)
    sel = gt + tie
    rank, starts = prefix(sel)
    pos = jnp.where(sel > 0.0, rank - 1.0, -1.0)
    pos_ref[0] = pos
    post_ref[0] = jnp.transpose(pos)
    afft_ref[0] = jnp.transpose(aff)
    nt = L // ck
    first, last = starts[0:nt], starts[1:nt + 1]
    win = min(cap, COMBINE_WINDOW)
    base = jnp.minimum(jnp.floor(first * (1.0 / BF16_ROWS)) * BF16_ROWS, float(cap - win))
    fit = jnp.where(last - base <= float(win), 1.0, 0.0)
    plan_ref[0] = jnp.concatenate([base, fit], axis=0).astype(jnp.int32)


def _select(aff, B, L):
    cap = EC_CAPACITY * L // N_EXPERTS
    ng = B // SEL_GROUP
    return pl.pallas_call(
        functools.partial(_select_body, cap),
        grid=(ng,),
        in_specs=[pl.BlockSpec((SEL_GROUP, L, LANES), lambda i: (i, 0, 0))],
        out_specs=[pl.BlockSpec((1, L, LANES), lambda i: (i, 0, 0)),
                   pl.BlockSpec((1, LANES, L), lambda i: (i, 0, 0)),
                   pl.BlockSpec((1, LANES, L), lambda i: (i, 0, 0)),
                   pl.BlockSpec((1, 2 * (L // SEL_CHUNK), LANES), lambda i: (i, 0, 0))],
        out_shape=[jax.ShapeDtypeStruct((ng, L, LANES), F32),
                   jax.ShapeDtypeStruct((ng, LANES, L), F32),
                   jax.ShapeDtypeStruct((ng, LANES, L), F32),
                   jax.ShapeDtypeStruct((ng, 2 * (L // SEL_CHUNK), LANES), jnp.int32)],
        compiler_params=_cparams(("arbitrary",)),
        name="ec_select",
    )(aff.reshape(B, L, LANES))


def _window_rows(base, w):
    return pl.ds(base, w) if isinstance(base, int) else pl.ds(pl.multiple_of(base, BF16_ROWS), w)


def _gather_body(cap, win, plan_ref, h_ref, post_ref, afft_ref, xs_ref, gs_ref):
    b = pl.program_id(0)
    j = pl.program_id(1)
    grp = b // SEL_GROUP
    lane0 = (b % SEL_GROUP) * N_EXPERTS

    @pl.when(j == 0)
    def _():
        xs_ref[...] = jnp.zeros_like(xs_ref)
        gs_ref[...] = jnp.zeros_like(gs_ref)

    n_req, ck = h_ref.shape[0], h_ref.shape[1]

    def run(w, bases, r=0):
        h = h_ref[r]
        slot = lax.broadcasted_iota(jnp.int32, (w, ck), 0)
        onehots = []
        for e in range(N_EXPERTS):
            rows = _window_rows(bases[e], w)
            row = r * N_EXPERTS + e
            hit = post_ref[0, row:row + 1, :] == (slot + bases[e]).astype(F32)
            gs_ref[e, r, rows, :] += jnp.sum(jnp.where(hit, afft_ref[0, row:row + 1, :], 0.0), axis=1, keepdims=True)
            onehots.append(jnp.where(hit, 1.0, 0.0).astype(BF16))
        xs = _dot(jnp.concatenate(onehots, axis=0), h).astype(BF16)
        for e in range(N_EXPERTS):
            rows = _window_rows(bases[e], w)
            xs_ref[e, r, rows, :] += xs[e * w:(e + 1) * w]

    if win == cap:
        for r in range(n_req):
            run(cap, [0] * N_EXPERTS, r)
        return
    bases = []
    fits = None
    nt = pl.num_programs(1)
    for e in range(N_EXPERTS):
        bases.append(plan_ref[grp, j, lane0 + e])
        ok = plan_ref[grp, nt + j, lane0 + e]
        fits = ok if fits is None else fits + ok
    fits = fits == N_EXPERTS

    @pl.when(fits)
    def _():
        run(win, bases)

    @pl.when(jnp.logical_not(fits))
    def _():
        run(cap, [0] * N_EXPERTS)


def _gather(starts, h2, post, afft, B, L):
    cap = EC_CAPACITY * L // N_EXPERTS
    ck = SEL_CHUNK
    win = min(cap, COMBINE_WINDOW)
    nr = SMALL_STEP_REQUESTS if win == cap else 1
    per_group = SEL_GROUP // nr
    lane_blk = lambda b, j, s: (b // per_group, b % per_group, j)
    grid_spec = pltpu.PrefetchScalarGridSpec(
        num_scalar_prefetch=1,
        grid=(B // nr, L // ck),
        in_specs=[pl.BlockSpec((nr, ck, D_MODEL), lambda b, j, s: (b, j, 0)),
                  pl.BlockSpec((1, nr * N_EXPERTS, ck), lane_blk),
                  pl.BlockSpec((1, nr * N_EXPERTS, ck), lane_blk)],
        out_specs=[pl.BlockSpec((N_EXPERTS, nr, cap, D_MODEL), lambda b, j, s: (0, b, 0, 0)),
                   pl.BlockSpec((N_EXPERTS, nr, cap, 1), lambda b, j, s: (0, b, 0, 0))],
    )
    xs, gs = pl.pallas_call(
        functools.partial(_gather_body, cap, win),
        grid_spec=grid_spec,
        out_shape=[jax.ShapeDtypeStruct((N_EXPERTS, B, cap, D_MODEL), BF16),
                   jax.ShapeDtypeStruct((N_EXPERTS, B, cap, 1), F32)],
        compiler_params=_cparams(("arbitrary", "arbitrary")),
        name="ec_gather",
    )(starts, h2.reshape(B, L, D_MODEL), post, afft)
    return xs.reshape(N_EXPERTS, B * cap, D_MODEL), gs.reshape(N_EXPERTS, B * cap, 1)


def _ffn_body(n_first, xa_ref, ga_ref, xb_ref, gb_ref, wg_ref, wu_ref, wd_ref, y_ref, wgb, wub, wdb):
    j = pl.program_id(1)

    @pl.when(j == 0)
    def _():
        wgb[...] = wg_ref[0, 0].astype(BF16)
        wub[...] = wu_ref[0, 0].astype(BF16)
        wdb[...] = wd_ref[0, 0].astype(BF16)

    def run(xs_ref, gs_ref):
        x = xs_ref[0]
        a = _dot(x, wgb[...])
        u = _dot(x, wub[...])
        hm = (a * _sigmoid(a) * u).astype(BF16)
        y_ref[0] = (_dot(hm, wdb[...]) * gs_ref[0]).astype(BF16)

    @pl.when(j < n_first)
    def _():
        run(xa_ref, ga_ref)

    @pl.when(j >= n_first)
    def _():
        run(xb_ref, gb_ref)


def _ffn(xa, ga, xb, gb, wg, wu, wd, l):
    rc = FFN_ROWS
    na = xa.shape[1] // rc
    nb = xb.shape[1] // rc
    first = lambda w: pl.BlockSpec((1, rc, w), lambda e, j: (e, jnp.minimum(j, na - 1), 0))
    second = lambda w: pl.BlockSpec((1, rc, w), lambda e, j: (e, jnp.maximum(j - na, 0), 0))
    ahead = lambda e, j: jnp.minimum(e + jnp.minimum(j, 1), N_EXPERTS - 1)
    wspec = lambda a, b: pl.BlockSpec((1, 1, a, b), lambda e, j: (l, ahead(e, j), 0, 0))
    return pl.pallas_call(
        functools.partial(_ffn_body, na),
        grid=(N_EXPERTS, na + nb),
        in_specs=[first(D_MODEL), first(1), second(D_MODEL), second(1),
                  wspec(D_MODEL, EXPERT_FF), wspec(D_MODEL, EXPERT_FF), wspec(EXPERT_FF, D_MODEL)],
        out_specs=pl.BlockSpec((1, rc, D_MODEL), lambda e, j: (e, j, 0)),
        out_shape=jax.ShapeDtypeStruct((N_EXPERTS, (na + nb) * rc, D_MODEL), BF16),
        scratch_shapes=[pltpu.VMEM((D_MODEL, EXPERT_FF), BF16),
                        pltpu.VMEM((D_MODEL, EXPERT_FF), BF16),
                        pltpu.VMEM((EXPERT_FF, D_MODEL), BF16)],
        compiler_params=_cparams(("arbitrary", "arbitrary")),
        name="ec_ffn",
    )(xa, ga, xb, gb, wg, wu, wd)


def _combine_body(cap, win, plan_ref, pos_ref, y_ref, x1_ref, mod_ref, g_ref, o_ref):
    b = pl.program_id(0)
    i = pl.program_id(1)
    pos = pos_ref[0].astype(BF16)
    tq = pos.shape[0]
    g2 = mod_ref[0][:, 5 * D_MODEL:6 * D_MODEL]

    def finish(w, bases, values, lane0, rows=slice(None)):
        n = N_EXPERTS * w
        col = lax.broadcasted_iota(jnp.int32, (1, n), 1)
        expert = jnp.zeros((1, n), jnp.int32)
        for e in range(1, N_EXPERTS):
            expert = expert + jnp.where(col >= e * w, 1, 0)
        target = col - expert * w
        if bases is not None:
            for e in range(N_EXPERTS):
                target = jnp.where(expert == e, target + bases[e], target)
        expand = jnp.where(lax.broadcasted_iota(jnp.int32, (LANES, n), 0) == expert + lane0, 1.0, 0.0)
        per_col = _dot(pos, expand.astype(BF16))
        onehot = jnp.where(per_col == target.astype(F32), 1.0, 0.0).astype(BF16)
        f = _dot(onehot, values)
        o_ref[rows, :] = x1_ref[rows, :] + g2 * _rms(f, g_ref[3:4, :])

    if win == cap:
        n_req = y_ref.shape[1] // cap
        for r in range(n_req):
            values = y_ref[:, r * cap:(r + 1) * cap, :].reshape(N_EXPERTS * cap, D_MODEL)
            finish(cap, None, values, ((b * n_req + r) % SEL_GROUP) * N_EXPERTS, slice(r * tq, (r + 1) * tq))
        return
    grp = b // SEL_GROUP
    lane0 = (b % SEL_GROUP) * N_EXPERTS
    bases = []
    fits = None
    nt = pl.num_programs(1)
    for e in range(N_EXPERTS):
        bases.append(plan_ref[grp, i, lane0 + e])
        ok = plan_ref[grp, nt + i, lane0 + e]
        fits = ok if fits is None else fits + ok
    fits = fits == N_EXPERTS

    @pl.when(fits)
    def _():
        rows = [y_ref[e, pl.ds(pl.multiple_of(bases[e], BF16_ROWS), win), :] for e in range(N_EXPERTS)]
        finish(win, bases, jnp.concatenate(rows, axis=0), lane0)

    @pl.when(jnp.logical_not(fits))
    def _():
        finish(cap, None, y_ref[...].reshape(N_EXPERTS * cap, D_MODEL), lane0)


def _combine(starts, pos, y, row_off, x1, mod_l, mod_row_fn, g, B, L, shared_mod=False):
    cap = EC_CAPACITY * L // N_EXPERTS
    tq = TOKEN_TILE
    assert tq == SEL_CHUNK
    nq = L // tq
    blk_off = row_off // cap
    win = min(cap, COMBINE_WINDOW)
    nr = SMALL_STEP_REQUESTS if (shared_mod and win == cap and nq == 1 and blk_off % SMALL_STEP_REQUESTS == 0) else 1
    per_group = SEL_GROUP // nr
    grid_spec = pltpu.PrefetchScalarGridSpec(
        num_scalar_prefetch=1,
        grid=(B // nr, nq),
        in_specs=[pl.BlockSpec((1, tq, LANES), lambda b, i, s: (b // per_group, i, 0)),
                  pl.BlockSpec((N_EXPERTS, nr * cap, D_MODEL), lambda b, i, s: (0, blk_off // nr + b, 0)),
                  pl.BlockSpec((nr * tq, D_MODEL), lambda b, i, s: (b * nq + i, 0)),
                  pl.BlockSpec((1, 1, 6 * D_MODEL), lambda b, i, s: (mod_row_fn(b * nr), 0, 0)),
                  pl.BlockSpec((4, D_MODEL), lambda b, i, s: (0, 0))],
        out_specs=pl.BlockSpec((nr * tq, D_MODEL), lambda b, i, s: (b * nq + i, 0)),
    )
    return pl.pallas_call(
        functools.partial(_combine_body, cap, win),
        grid_spec=grid_spec,
        out_shape=jax.ShapeDtypeStruct((B * L, D_MODEL), F32),
        compiler_params=_cparams(("arbitrary", "arbitrary")),
        name="ec_combine",
    )(starts, pos, y, x1, mod_l, g)


def _rope_tables(L):
    t = jnp.arange(L)
    row = (t // GRID_W).astype(F32)
    col = (t % GRID_W).astype(F32)
    half = HEAD_DIM // 4
    inv = ROPE_BASE ** (-jnp.arange(half, dtype=F32) / half)
    d = np.arange(LANES) % HEAD_DIM
    use_col = (d // (HEAD_DIM // 2)) == 1
    pos = jnp.where(use_col[None, :], col[:, None], row[:, None])
    ang = pos * inv[d % half][None, :]
    sign = np.where((d % (HEAD_DIM // 2)) < half, -1.0, 1.0).astype(np.float32)
    return jnp.cos(ang), jnp.sin(ang) * sign[None, :]


def _permute_wa_heads(a, axis, start):
    cut = lambda lo, hi: lax.slice_in_dim(a, lo, hi, axis=axis)
    heads = [cut(start + h * HEAD_DIM, start + (h + 1) * HEAD_DIM) for h in WA_HEAD_ORDER]
    return jnp.concatenate([cut(0, start)] + heads + [cut(start + WA_WIDTH, a.shape[axis])], axis=axis)


def kernel(x_prompt, x_sample, c, cache_na_k, cache_na_v, cache_wa_k, cache_wa_v, state_ssm, c_ctx, w_ada, b_ada, norm_g, w_in, w_out, na_rpb, wa_sink, ssm_lambda_re, ssm_lambda_im, ssm_log_dt, ssm_b_re, ssm_b_im, ssm_c_re, ssm_c_im, ssm_d, w_glu, w_router, w_exp_gate, w_exp_up, w_exp_down):
    Bc, Lc, _ = x_prompt.shape
    Bs, Ls, _ = x_sample.shape
    ctx_row = Bs
    n_cond = ((Bs + 1 + SUBLANES - 1) // SUBLANES) * SUBLANES
    cond = jnp.zeros((n_cond, D_MODEL), F32).at[0:Bs].set(c).at[ctx_row].set(c_ctx)
    mod = _modulation(cond, w_ada, b_ada)
    mod = mod.reshape(DEPTH, n_cond, 1, 6 * D_MODEL)

    w_in_bf = _permute_wa_heads(w_in, 2, C_WAQ).astype(BF16)
    w_out_bf = _permute_wa_heads(w_out, 1, NA_WIDTH).astype(BF16)
    w_glu_bf = w_glu.astype(BF16)
    wr_hi = w_router.astype(BF16)
    wr_lo = (w_router - wr_hi.astype(F32)).astype(BF16)
    wr_pad = jnp.pad(jnp.concatenate([wr_hi, wr_lo], axis=-1), ((0, 0), (0, 0), (0, LANES - 2 * N_EXPERTS)))
    s5_a, s5_bb = _s5_prep(ssm_lambda_re, ssm_lambda_im, ssm_log_dt, ssm_b_re, ssm_b_im)
    s5_c = _s5_c_matrix(ssm_c_re, ssm_c_im)
    rope_tabs = _rope_tables(Ls)

    ctx_mod_row = lambda b: ctx_row + 0 * b
    lat_mod_row = lambda b: b

    xp = x_prompt.reshape(Bc * Lc, D_MODEL)
    xs = x_sample.reshape(Bs * Ls, D_MODEL)
    new_kv = ()
    new_ssm = []
    for l in range(DEPTH):
        g = norm_g[l]
        dsk = ssm_d[l].reshape(1, SSM_CH)
        sl2 = slice(2 * l, 2 * l + 2)
        qna, kna, vna, qwa, kwa, vwa, u = _inproj(xp, mod, ctx_mod_row, norm_g, w_in_bf, l, Bc, Lc, None, new_kv)
        new_kv = (kna, vna, kwa, vwa)
        ona, owa = _attn_ctx(wa_sink[l], qna, kna, vna, qwa, kwa, vwa, l, Bc, Lc)
        y2, hfin = _s5_scan(u, s5_bb[sl2], s5_a[sl2], s5_c[sl2],
                            jnp.zeros((2, Bc, 2 * SSM_N), F32), Bc, Lc)
        x1, h2, aff = _outproj(ona, owa, y2, u, dsk, w_glu_bf[l], w_out_bf[l], xp, mod[l], ctx_mod_row,
                               g, wr_pad[l], Lc)
        pos_c, post, afft, starts_c = _select(aff, Bc, Lc)
        xg_c, gs_c = _gather(starts_c, h2, post, afft, Bc, Lc)
        x1_c = x1
        new_ssm.append(jnp.transpose(hfin.reshape(2, Bc, 2, SSM_GROUPS, SSM_STATE), (1, 0, 2, 3, 4)))
        qna, kna, vna, qwa, kwa, vwa, u = _inproj(xs, mod, lat_mod_row, norm_g, w_in_bf, l, Bs, Ls, rope_tabs)
        ona = _attn_na(qna, kna, vna, cache_na_k, cache_na_v, _na_bias_pairs(na_rpb[l]), l, Bs, Ls)
        owa = _attn_wa(wa_sink[l], qwa, kwa, vwa, cache_wa_k, cache_wa_v, l, Bs, Ls)
        h0 = jnp.transpose(state_ssm[:, l].reshape(Bs, 2, 2 * SSM_N), (1, 0, 2))
        y2, _ = _s5_scan(u, s5_bb[sl2], s5_a[sl2], s5_c[sl2], h0, Bs, Ls)
        x1, h2, aff = _outproj(ona, owa, y2, u, dsk, w_glu_bf[l], w_out_bf[l], xs, mod[l], lat_mod_row,
                               g, wr_pad[l], Ls)
        pos_s, post, afft, starts_s = _select(aff, Bs, Ls)
        xg_s, gs_s = _gather(starts_s, h2, post, afft, Bs, Ls)
        yy = _ffn(xg_c, gs_c, xg_s, gs_s, w_exp_gate, w_exp_up, w_exp_down, l)
        xp = _combine(starts_c, pos_c, yy, 0, x1_c, mod[l], ctx_mod_row, g, Bc, Lc, shared_mod=True)
        xs = _combine(starts_s, pos_s, yy, xg_c.shape[1], x1, mod[l], lat_mod_row, g, Bs, Ls)
    return (xp.reshape(Bc, Lc, D_MODEL), xs.reshape(Bs, Ls, D_MODEL),
            new_kv[0].reshape(Bc, DEPTH, Lc, NA_HEADS, HEAD_DIM),
            new_kv[1].reshape(Bc, DEPTH, Lc, NA_HEADS, HEAD_DIM),
            new_kv[2].reshape(Bc, DEPTH, Lc, WA_KV_HEADS, HEAD_DIM),
            new_kv[3].reshape(Bc, DEPTH, Lc, WA_KV_HEADS, HEAD_DIM),
            jnp.stack(new_ssm, axis=1))
```

```python
import functools
import math

import numpy as np
import jax
import jax.numpy as jnp
from jax import lax
from jax.experimental import pallas as pl
from jax.experimental.pallas import tpu as pltpu

F32 = jnp.float32
BF16 = jnp.bfloat16

D_MODEL = 1024
DEPTH = 2
GRID_W = 64
HEAD_DIM = 64
NA_HEADS = 6
NA_KH = 8
NA_KW = 16
WA_HEADS = 6
WA_KV_HEADS = 2
WA_WINDOW = 128
WA_QBLOCK = 256
SSM_CH = 256
SSM_GROUP_CH = 16
SSM_GROUPS = 16
SSM_STATE = 64
SSM_N = SSM_GROUPS * SSM_STATE
NA_WIDTH = NA_HEADS * HEAD_DIM
WA_WIDTH = WA_HEADS * HEAD_DIM
WA_KV_WIDTH = WA_KV_HEADS * HEAD_DIM
IN_COLS = 2048
N_EXPERTS = 16
EC_CAPACITY = 2
EXPERT_FF = 1024
ROPE_BASE = 10000.0
RMS_EPS = 1e-6
NEG_INF = -1e30
ATTN_SCALE = HEAD_DIM ** -0.5
LOG2E = math.log2(math.e)
Q_SCALE = ATTN_SCALE * LOG2E

LANES = 128
SUBLANES = 8
VMEM_LIMIT = 48 * 1024 * 1024

C_NAQ, C_NAK, C_NAV, C_WAQ, C_WAK, C_WAV, C_SSU = 0, 384, 768, 1152, 1536, 1664, 1792
WA_HEAD_ORDER = (0, 3, 1, 4, 2, 5)

NA_QROWS = 4
NA_KROWS = NA_QROWS + NA_KH
NA_REL = 2 * NA_KH - 1
NA_PAIRS = 3 * NA_REL
TOKEN_TILE = 256
OUTPROJ_SUBTILE = 128
PROJ_TILE = 1024
SCAN_TILE = 256
SMALL_STEP_REQUESTS = 4
CTX_ATTN_REQUESTS = 4
SCAN_BATCH = SUBLANES
FFN_ROWS = 512
SEL_CHUNK = 256
SEL_GROUP = LANES // N_EXPERTS
BF16_ROWS = 2 * SUBLANES
COMBINE_WINDOW = 64


def _cparams(sem):
    return pltpu.CompilerParams(dimension_semantics=sem, vmem_limit_bytes=VMEM_LIMIT)


def _sigmoid(x):
    return 1.0 / (1.0 + jnp.exp(-x))


def _rms(x, g):
    ms = jnp.mean(x * x, axis=-1, keepdims=True)
    return x * lax.rsqrt(ms + RMS_EPS) * g


def _dot(a, b):
    return jnp.dot(a, b, preferred_element_type=F32)


def _dot_nt(a, b):
    return lax.dot_general(a, b, (((1,), (1,)), ((), ())), preferred_element_type=F32)


def _mod_body(c_ref, w_ref, b_ref, o_ref):
    c = c_ref[...]
    s = c * _sigmoid(c)
    o_ref[0] = _dot(s.astype(BF16), w_ref[0].astype(BF16)) + b_ref[0]


def _modulation(cond, w_ada, b_ada):
    n = cond.shape[0]
    tn = 1024
    return pl.pallas_call(
        _mod_body,
        grid=(DEPTH, 6 * D_MODEL // tn),
        in_specs=[
            pl.BlockSpec((n, D_MODEL), lambda l, j: (0, 0)),
            pl.BlockSpec((1, D_MODEL, tn), lambda l, j: (l, 0, j)),
            pl.BlockSpec((1, 1, tn), lambda l, j: (l, 0, j)),
        ],
        out_specs=pl.BlockSpec((1, n, tn), lambda l, j: (l, 0, j)),
        out_shape=jax.ShapeDtypeStruct((DEPTH, n, 6 * D_MODEL), F32),
        compiler_params=_cparams(("arbitrary", "arbitrary")),
        name="modulation",
    )(cond, w_ada, b_ada.reshape(DEPTH, 1, 6 * D_MODEL))


def _rope_chunk(x, cos, sin_signed):
    lane = lax.broadcasted_iota(jnp.int32, x.shape, 1)
    axis_dims = HEAD_DIM // 2
    half = axis_dims // 2
    first = (lane % axis_dims) < half
    rot = jnp.where(first, pltpu.roll(x, LANES - half, 1), pltpu.roll(x, half, 1))
    return x * cos + rot * sin_signed


INPROJ_KV_OUTPUTS = (1, 2, 4, 5)


def _inproj_body(rope, n_alias, layer, x_ref, mod_ref, g_ref, w_ref, *rest):
    if rope:
        cos_ref, sin_ref = rest[:2]
        rest = rest[2:]
    qna_ref, kna_ref, vna_ref, qwa_ref, kwa_ref, vwa_ref, u_ref = rest[n_alias:]

    def put(ref, val):
        if len(ref.shape) == 4 and ref.shape[1] > 1:
            for d in range(ref.shape[1]):
                ref[0, d] = val if d == layer else jnp.zeros_like(val)
        else:
            ref[...] = val.reshape(ref.shape)

    x = x_ref[...]
    m = mod_ref[0, 0]
    sh1 = m[:, 0:D_MODEL]
    sc1 = m[:, D_MODEL:2 * D_MODEL]
    h = _rms(x, g_ref[0, 0:1, :]) * (1.0 + sc1) + sh1
    z = _dot(h.astype(BF16), w_ref[0])
    qna_ref[...] = (z[:, C_NAQ:C_NAK] * Q_SCALE).astype(BF16)
    put(kna_ref, z[:, C_NAK:C_NAV])
    put(vna_ref, z[:, C_NAV:C_WAQ])
    put(vwa_ref, z[:, C_WAV:C_SSU])
    u_ref[...] = z[:, C_SSU:IN_COLS]
    if rope:
        cos = cos_ref[...]
        sin = sin_ref[...]
        for j in range(WA_WIDTH // LANES):
            qc = _rope_chunk(z[:, C_WAQ + j * LANES:C_WAQ + (j + 1) * LANES], cos, sin)
            qwa_ref[:, j * LANES:(j + 1) * LANES] = (qc * Q_SCALE).astype(BF16)
        put(kwa_ref, _rope_chunk(z[:, C_WAK:C_WAV], cos, sin))
    else:
        qwa_ref[...] = (z[:, C_WAQ:C_WAK] * Q_SCALE).astype(BF16)
        put(kwa_ref, z[:, C_WAK:C_WAV])


def _inproj(x, mod, mod_row_fn, norm_g, w_bf, l, B, L, rope_tabs, kv_layers=None):
    T = x.shape[0]
    tm = min(PROJ_TILE, L)
    tiles_per_seq = L // tm
    in_specs = [
        pl.BlockSpec((tm, D_MODEL), lambda i: (i, 0)),
        pl.BlockSpec((1, 1, 1, 6 * D_MODEL), lambda i: (l, mod_row_fn(i // tiles_per_seq), 0, 0)),
        pl.BlockSpec((1, 4, D_MODEL), lambda i: (l, 0, 0)),
        pl.BlockSpec((1, D_MODEL, IN_COLS), lambda i: (l, 0, 0)),
    ]
    args = [x, mod, norm_g, w_bf]
    if rope_tabs is not None:
        in_specs += [pl.BlockSpec((tm, LANES), lambda i: (i % tiles_per_seq, 0))] * 2
        args += list(rope_tabs)
    widths = (NA_WIDTH, NA_WIDTH, NA_WIDTH, WA_WIDTH, WA_KV_WIDTH, WA_KV_WIDTH, SSM_CH)
    dtypes = (BF16, F32, F32, BF16, F32, F32, F32)
    out_specs = [pl.BlockSpec((tm, w), lambda i: (i, 0)) for w in widths]
    out_shape = [jax.ShapeDtypeStruct((T, w), dt) for w, dt in zip(widths, dtypes)]
    aliases = {}
    n_alias = 0
    if kv_layers is not None:
        create = len(kv_layers) == 0
        for o in INPROJ_KV_OUTPUTS:
            out_specs[o] = pl.BlockSpec((1, DEPTH if create else 1, tm, widths[o]),
                                        lambda i: (i // tiles_per_seq, 0 if create else l, i % tiles_per_seq, 0))
            out_shape[o] = jax.ShapeDtypeStruct((B, DEPTH, L, widths[o]), dtypes[o])
        n_alias = len(kv_layers)
        for k, (o, prev) in enumerate(zip(INPROJ_KV_OUTPUTS, kv_layers)):
            aliases[len(args)] = o
            in_specs.append(pl.BlockSpec(memory_space=pl.ANY))
            args.append(prev)
    return pl.pallas_call(
        functools.partial(_inproj_body, rope_tabs is not None, n_alias, l),
        grid=(T // tm,),
        in_specs=in_specs,
        out_specs=out_specs,
        out_shape=out_shape,
        input_output_aliases=aliases,
        compiler_params=_cparams(("arbitrary",)),
        name="inproj",
    )(*args)


def _lane_lo(shape):
    return lax.broadcasted_iota(jnp.int32, shape, len(shape) - 1) < HEAD_DIM


def _keep_half(q, lo, half):
    keep = jnp.where(lo, 1.0 - half, 0.0 + half).astype(q.dtype)
    return q * keep


def _values_and_ones(v, lo, half):
    one = jnp.ones_like(v)
    return (jnp.where(lo, v, one) if half == 0 else jnp.where(lo, one, v)).astype(BF16)


def _normalise(o2, extra=None):
    den = pltpu.roll(o2, HEAD_DIM, 1)
    if extra is not None:
        den = den + extra
    return o2 / den


def _attn_ctx_body(sink_ref, qna, kna, vna, qwa, kwa, vwa, ona, owa):
    R, L = qna.shape[0], qna.shape[1]
    npair = WA_WIDTH // LANES
    lo = _lane_lo((L, LANES))
    lo_all = _lane_lo((npair * L, LANES))
    blk = lax.broadcasted_iota(jnp.int32, (npair * L, 1), 0) // L
    for r in range(R):
        for j in range(NA_WIDTH // LANES):
            sl = slice(j * LANES, (j + 1) * LANES)
            q2 = qna[r, :, sl]
            k2 = kna[r, 0, :, sl].astype(BF16)
            v2 = vna[r, 0, :, sl].astype(BF16)
            halves = []
            for half in range(2):
                s = _dot_nt(_keep_half(q2, lo, half), k2)
                p = jnp.exp2(s - jnp.max(s, axis=-1, keepdims=True))
                halves.append(_dot(p.astype(BF16), v2) / jnp.sum(p, axis=-1, keepdims=True))
            ona[r, :, sl] = jnp.where(lo, halves[0], halves[1]).astype(BF16)
        kw = kwa[r, 0].astype(BF16)
        vw = vwa[r, 0].astype(BF16)
        q_all = jnp.concatenate([qwa[r, :, j * LANES:(j + 1) * LANES] for j in range(npair)], axis=0)
        outs = []
        for g in range(WA_KV_HEADS):
            s = _dot_nt(_keep_half(q_all, lo_all, g), kw)
            sk = jnp.zeros((npair * L, 1), F32)
            for j in range(npair):
                sk = jnp.where(blk == j, sink_ref[WA_HEAD_ORDER[2 * j + g]] * LOG2E, sk)
            m = jnp.maximum(jnp.max(s, axis=-1, keepdims=True), sk)
            p = jnp.exp2(s - m)
            l = jnp.sum(p, axis=-1, keepdims=True) + jnp.exp2(sk - m)
            outs.append(_dot(p.astype(BF16), vw) / l)
        o_all = jnp.where(lo_all, outs[0], outs[1]).astype(BF16)
        for j in range(npair):
            owa[r, :, j * LANES:(j + 1) * LANES] = o_all[j * L:(j + 1) * L]


def _attn_ctx(sink, qna, kna, vna, qwa, kwa, vwa, l, B, L):
    R = CTX_ATTN_REQUESTS

    def spec(w):
        return pl.BlockSpec((R, L, w), lambda b: (b, 0, 0))

    def layer_spec(w):
        return pl.BlockSpec((R, 1, L, w), lambda b: (b, l, 0, 0))

    r3 = lambda a: a.reshape(B, L, a.shape[-1])
    ona, owa = pl.pallas_call(
        _attn_ctx_body,
        grid=(B // R,),
        in_specs=[pl.BlockSpec(memory_space=pltpu.SMEM),
                  spec(NA_WIDTH), layer_spec(NA_WIDTH), layer_spec(NA_WIDTH),
                  spec(WA_WIDTH), layer_spec(WA_KV_WIDTH), layer_spec(WA_KV_WIDTH)],
        out_specs=[spec(NA_WIDTH), spec(WA_WIDTH)],
        out_shape=[jax.ShapeDtypeStruct((B, L, NA_WIDTH), BF16),
                   jax.ShapeDtypeStruct((B, L, WA_WIDTH), BF16)],
        compiler_params=_cparams(("arbitrary",)),
        name="attn_ctx",
    )(sink, r3(qna), kna, vna, r3(qwa), kwa, vwa)
    return ona.reshape(B * L, NA_WIDTH), owa.reshape(B * L, WA_WIDTH)


def _na_key_start(rb, rows):
    return jnp.clip(rb * NA_QROWS - NA_KH // 2, 0, rows - NA_KROWS)


def _attn_na_body(rows, idx_ref, q_ref, k_ref, v_ref, kc_ref, vc_ref, pairs_ref, o_ref):
    rb = pl.program_id(1)
    nq = NA_QROWS * GRID_W
    nk = NA_KROWS * GRID_W
    start = pl.multiple_of(_na_key_start(rb, rows) * GRID_W, GRID_W)
    lo = _lane_lo((nq, LANES))
    lo_v = _lane_lo((1, LANES))
    npair = NA_KROWS // 2
    pat = jnp.where(rb == 0, 0, jnp.where(rb == rows // NA_QROWS - 1, 2, 1))
    entry = [[idx_ref[(pat * NA_QROWS + i) * npair + jj] for jj in range(npair)] for i in range(NA_QROWS)]

    def bias(head):
        return jnp.concatenate(
            [jnp.concatenate([pairs_ref[head, entry[i][jj]] for jj in range(npair)], axis=1)
             for i in range(NA_QROWS)], axis=0)

    for j in range(NA_WIDTH // LANES):
        sl = slice(j * LANES, (j + 1) * LANES)
        q2 = q_ref[0, :, sl]
        kl = k_ref[0, pl.ds(start, nk), sl].astype(BF16)
        vl = v_ref[0, pl.ds(start, nk), sl]
        kc = kc_ref[0, 0, :, sl].astype(BF16)
        vc = vc_ref[0, 0, :, sl]
        scores = []
        for half in range(2):
            qm = _keep_half(q2, lo, half)
            scores.append((_dot_nt(qm, kl) + bias(2 * j + half), _dot_nt(qm, kc)))
        probs = []
        for s_loc, s_ctx in scores:
            m = jnp.maximum(jnp.max(s_loc, axis=-1, keepdims=True),
                            jnp.max(s_ctx, axis=-1, keepdims=True))
            probs.append((jnp.exp2(s_loc - m).astype(BF16), jnp.exp2(s_ctx - m).astype(BF16)))
        halves = []
        for half, (p_loc, p_ctx) in enumerate(probs):
            o2 = (_dot(p_loc, _values_and_ones(vl, lo_v, half))
                  + _dot(p_ctx, _values_and_ones(vc, lo_v, half)))
            halves.append(_normalise(o2))
        o_ref[0, :, sl] = jnp.where(lo, halves[0], halves[1]).astype(BF16)


def _na_bias_pairs(rpb_l):
    cq = np.arange(GRID_W)
    col_start = np.clip(cq - NA_KW // 2, 0, GRID_W - NA_KW)
    col_ok = (cq[None, :] >= col_start[:, None]) & (cq[None, :] < col_start[:, None] + NA_KW)
    dcol = np.clip(cq[None, :] - cq[:, None], -(NA_KW - 1), NA_KW - 1) + (NA_KW - 1)
    pick = (dcol[None] == np.arange(2 * NA_KW - 1)[:, None, None]).astype(np.float32)
    tiles = jnp.einsum('hrd,dqc->hrqc', rpb_l.astype(F32) * LOG2E, pick, precision=lax.Precision.HIGHEST)
    tiles = jnp.where(col_ok[None, None], tiles, NEG_INF)
    masked = jnp.full_like(tiles, NEG_INF)
    both = jnp.concatenate([tiles[:, :-1], tiles[:, 1:]], axis=-1)
    right = jnp.concatenate([masked, tiles], axis=-1)
    left = jnp.concatenate([tiles, masked], axis=-1)
    none = jnp.concatenate([masked[:, :1], masked[:, :1]], axis=-1)
    return jnp.concatenate([both, right, left, none], axis=1)


def _na_pair_index(rows):
    nrb = rows // NA_QROWS
    idx = np.zeros((3, NA_QROWS, NA_KROWS // 2), np.int32)
    for pi, rb in enumerate((0, 1, nrb - 1)):
        ks = int(np.clip(rb * NA_QROWS - NA_KH // 2, 0, rows - NA_KROWS))
        for i in range(NA_QROWS):
            qr = rb * NA_QROWS + i
            rs = int(np.clip(qr - NA_KH // 2, 0, rows - NA_KH))
            for jj in range(NA_KROWS // 2):
                kr = ks + 2 * jj
                r = kr - qr + NA_KH - 1
                ok_l = rs <= kr < rs + NA_KH
                ok_r = rs <= kr + 1 < rs + NA_KH
                if ok_l and ok_r:
                    idx[pi, i, jj] = r
                elif ok_r:
                    idx[pi, i, jj] = (NA_REL - 1) + (r + 1)
                elif ok_l:
                    idx[pi, i, jj] = (NA_REL - 1) + NA_REL + r
                else:
                    idx[pi, i, jj] = NA_PAIRS - 1
    return jnp.asarray(idx.reshape(-1))


def _attn_na(q, k, v, cache_k, cache_v, pairs, l, B, L):
    rows = L // GRID_W
    nrb = rows // NA_QROWS
    nq = NA_QROWS * GRID_W
    P = cache_k.shape[2]
    r3 = lambda a: a.reshape(B, L, a.shape[-1])
    ck = cache_k.reshape(B, DEPTH, P, NA_WIDTH)
    cv = cache_v.reshape(B, DEPTH, P, NA_WIDTH)
    grid_spec = pltpu.PrefetchScalarGridSpec(
        num_scalar_prefetch=1,
        grid=(B, nrb),
        in_specs=[
            pl.BlockSpec((1, nq, NA_WIDTH), lambda b, r, s: (b, r, 0)),
            pl.BlockSpec((1, L, NA_WIDTH), lambda b, r, s: (b, 0, 0)),
            pl.BlockSpec((1, L, NA_WIDTH), lambda b, r, s: (b, 0, 0)),
            pl.BlockSpec((1, 1, P, NA_WIDTH), lambda b, r, s: (b, l, 0, 0)),
            pl.BlockSpec((1, 1, P, NA_WIDTH), lambda b, r, s: (b, l, 0, 0)),
            pl.BlockSpec(pairs.shape, lambda b, r, s: (0, 0, 0, 0)),
        ],
        out_specs=pl.BlockSpec((1, nq, NA_WIDTH), lambda b, r, s: (b, r, 0)),
    )
    o = pl.pallas_call(
        functools.partial(_attn_na_body, rows),
        grid_spec=grid_spec,
        out_shape=jax.ShapeDtypeStruct((B, L, NA_WIDTH), BF16),
        compiler_params=_cparams(("arbitrary", "arbitrary")),
        name="attn_na",
    )(_na_pair_index(rows), r3(q), r3(k), r3(v), ck, cv, pairs)
    return o.reshape(B * L, NA_WIDTH)


def _attn_wa_body(L, sink_ref, q_ref, k_ref, v_ref, kc_ref, vc_ref, o_ref):
    n = pl.program_id(1)
    nk = WA_QBLOCK + 2 * WA_WINDOW
    npair = WA_WIDTH // LANES
    start = pl.multiple_of(jnp.clip(n * WA_QBLOCK - WA_WINDOW, 0, L - nk), WA_WINDOW)
    kl = k_ref[0, pl.ds(start, nk), :].astype(BF16)
    vl = v_ref[0, pl.ds(start, nk), :]
    kc = kc_ref[0, 0].astype(BF16)
    vc = vc_ref[0, 0]
    lo_v = _lane_lo((1, LANES))
    rows = npair * WA_QBLOCK
    qpos = n * WA_QBLOCK + lax.broadcasted_iota(jnp.int32, (rows, nk), 0) % WA_QBLOCK
    kpos = start + lax.broadcasted_iota(jnp.int32, (rows, nk), 1)
    in_win = jnp.abs(kpos - qpos) <= WA_WINDOW
    lo = _lane_lo((rows, LANES))
    q_all = jnp.concatenate([q_ref[0, :, j * LANES:(j + 1) * LANES] for j in range(npair)], axis=0)
    blk = lax.broadcasted_iota(jnp.int32, (rows, 1), 0) // WA_QBLOCK
    scores = []
    for g in range(WA_KV_HEADS):
        qm = _keep_half(q_all, lo, g)
        scores.append((jnp.where(in_win, _dot_nt(qm, kl), NEG_INF), _dot_nt(qm, kc)))
    probs = []
    for g in range(WA_KV_HEADS):
        s_loc, s_ctx = scores[g]
        sk = jnp.zeros((rows, 1), F32)
        for j in range(npair):
            sk = jnp.where(blk == j, sink_ref[WA_HEAD_ORDER[2 * j + g]] * LOG2E, sk)
        m = jnp.maximum(jnp.maximum(jnp.max(s_loc, axis=-1, keepdims=True),
                                    jnp.max(s_ctx, axis=-1, keepdims=True)), sk)
        probs.append((jnp.exp2(s_loc - m).astype(BF16), jnp.exp2(s_ctx - m).astype(BF16), jnp.exp2(sk - m)))
    outs = []
    for g in range(WA_KV_HEADS):
        p_loc, p_ctx, p_sink = probs[g]
        o2 = _dot(p_loc, _values_and_ones(vl, lo_v, g)) + _dot(p_ctx, _values_and_ones(vc, lo_v, g))
        outs.append(_normalise(o2, p_sink))
    o_all = jnp.where(lo, outs[0], outs[1]).astype(BF16)
    for j in range(npair):
        o_ref[0, :, j * LANES:(j + 1) * LANES] = o_all[j * WA_QBLOCK:(j + 1) * WA_QBLOCK]


def _attn_wa(sink, q, k, v, cache_k, cache_v, l, B, L):
    nb = L // WA_QBLOCK
    P = cache_k.shape[2]
    r3 = lambda a: a.reshape(B, L, a.shape[-1])
    ck = cache_k.reshape(B, DEPTH, P, WA_KV_WIDTH)
    cv = cache_v.reshape(B, DEPTH, P, WA_KV_WIDTH)
    o = pl.pallas_call(
        functools.partial(_attn_wa_body, L),
        grid=(B, nb),
        in_specs=[
            pl.BlockSpec(memory_space=pltpu.SMEM),
            pl.BlockSpec((1, WA_QBLOCK, WA_WIDTH), lambda b, n: (b, n, 0)),
            pl.BlockSpec((1, L, WA_KV_WIDTH), lambda b, n: (b, 0, 0)),
            pl.BlockSpec((1, L, WA_KV_WIDTH), lambda b, n: (b, 0, 0)),
            pl.BlockSpec((1, 1, P, WA_KV_WIDTH), lambda b, n: (b, l, 0, 0)),
            pl.BlockSpec((1, 1, P, WA_KV_WIDTH), lambda b, n: (b, l, 0, 0)),
        ],
        out_specs=pl.BlockSpec((1, WA_QBLOCK, WA_WIDTH), lambda b, n: (b, n, 0)),
        out_shape=jax.ShapeDtypeStruct((B, L, WA_WIDTH), BF16),
        compiler_params=_cparams(("arbitrary", "arbitrary")),
        name="attn_wa",
    )(sink, r3(q), r3(k), r3(v), ck, cv)
    return o.reshape(B * L, WA_WIDTH)


def _s5_prep_body(lr_ref, li_ref, ldt_ref, br_ref, bi_ref, a_ref, bb_ref):
    lr = lr_ref[0]
    li = li_ref[0]
    dt = jnp.exp(ldt_ref[0])
    mag = jnp.exp(lr * dt)
    ar = mag * jnp.cos(li * dt)
    ai = mag * jnp.sin(li * dt)
    den = lr * lr + li * li
    nr = ar - 1.0
    fr = (nr * lr + ai * li) / den
    fi = (ai * lr - nr * li) / den
    a_ref[0, :, 0:SSM_N] = ar
    a_ref[0, :, SSM_N:2 * SSM_N] = ai
    br = br_ref[0]
    bi = bi_ref[0]
    bb_ref[0, :, 0:SSM_N] = (fr * br - fi * bi).astype(BF16)
    bb_ref[0, :, SSM_N:2 * SSM_N] = (fr * bi + fi * br).astype(BF16)


def _s5_prep(lam_re, lam_im, log_dt, b_re, b_im):
    n = DEPTH * 2
    eye = jnp.eye(SSM_GROUPS, dtype=F32)

    def blockdiag_b(b):
        return jnp.einsum('ngph,gk->nghkp', b.reshape(n, SSM_GROUPS, SSM_STATE, SSM_GROUP_CH), eye
                          ).reshape(n, SSM_CH, SSM_N)

    lr = lam_re.reshape(n, 1, SSM_N)
    li = lam_im.reshape(n, 1, SSM_N)
    ldt = jnp.repeat(log_dt.reshape(n, SSM_GROUPS), SSM_STATE, axis=-1).reshape(n, 1, SSM_N)
    vec = pl.BlockSpec((1, 1, SSM_N), lambda i: (i, 0, 0))
    mat = pl.BlockSpec((1, SSM_CH, SSM_N), lambda i: (i, 0, 0))
    return pl.pallas_call(
        _s5_prep_body,
        grid=(n,),
        in_specs=[vec, vec, vec, mat, mat],
        out_specs=[pl.BlockSpec((1, 1, 2 * SSM_N), lambda i: (i, 0, 0)),
                   pl.BlockSpec((1, SSM_CH, 2 * SSM_N), lambda i: (i, 0, 0))],
        out_shape=[jax.ShapeDtypeStruct((n, 1, 2 * SSM_N), F32),
                   jax.ShapeDtypeStruct((n, SSM_CH, 2 * SSM_N), BF16)],
        compiler_params=_cparams(("arbitrary",)),
        name="s5_prep",
    )(lr, li, ldt, blockdiag_b(b_re), blockdiag_b(b_im))


def _s5_c_matrix(c_re, c_im):
    n = DEPTH * 2
    eye = jnp.eye(SSM_GROUPS, dtype=F32)

    def blk(c):
        return jnp.einsum('nghp,gk->ngpkh', c.reshape(n, SSM_GROUPS, SSM_GROUP_CH, SSM_STATE), eye
                          ).reshape(n, SSM_N, SSM_CH)

    return jnp.concatenate([blk(c_re), -blk(c_im)], axis=1).astype(BF16)


def _s5_scan_body(tl, u_ref, bb_ref, a_ref, c_ref, h0_ref, y_ref, hfin_ref, bu, hb, hst):
    d = pl.program_id(0)
    i = pl.program_id(2)
    sb = SCAN_BATCH
    re = slice(0, SSM_N)
    im = slice(SSM_N, 2 * SSM_N)

    @pl.when(i == 0)
    def _():
        hst[...] = h0_ref[0]

    u_tb = jnp.transpose(u_ref[...], (1, 0, 2)).reshape(tl * sb, SSM_CH)
    u_bf = u_tb.astype(BF16)
    half = tl * sb // 2
    bu[0:half, :] = _dot(u_bf[0:half], bb_ref[0])
    bu[half:, :] = _dot(u_bf[half:], bb_ref[0])
    a = a_ref[0]
    ar = jnp.broadcast_to(a[:, re], (sb, SSM_N))
    ai = jnp.broadcast_to(a[:, im], (sb, SSM_N))

    def scan(reverse):
        def pair(s, carry):
            hr, hi = carry
            t0 = tl - 1 - 2 * s if reverse else 2 * s
            t1 = t0 - 1 if reverse else t0 + 1
            r0 = pl.ds(pl.multiple_of(t0 * sb, sb), sb)
            r1 = pl.ds(pl.multiple_of(t1 * sb, sb), sb)
            h0r = ar * hr - ai * hi + bu[r0, re]
            h0i = ar * hi + ai * hr + bu[r0, im]
            h1r = ar * h0r - ai * h0i + bu[r1, re]
            h1i = ar * h0i + ai * h0r + bu[r1, im]
            rows = pl.ds(pl.multiple_of((t1 if reverse else t0) * sb, 2 * sb), 2 * sb)
            early, late = ((h1r, h1i), (h0r, h0i)) if reverse else ((h0r, h0i), (h1r, h1i))
            hb[rows, re] = jnp.concatenate([early[0], late[0]], axis=0).astype(BF16)
            hb[rows, im] = jnp.concatenate([early[1], late[1]], axis=0).astype(BF16)
            return h1r, h1i

        hr, hi = lax.fori_loop(0, tl // 2, pair, (hst[:, re], hst[:, im]))
        hst[:, re] = hr
        hst[:, im] = hi
        hfin_ref[0, :, re] = hr
        hfin_ref[0, :, im] = hi

    @pl.when(d == 0)
    def _():
        scan(False)

    @pl.when(d == 1)
    def _():
        scan(True)

    y_tb = jnp.concatenate([_dot(hb[0:half, :], c_ref[0]), _dot(hb[half:, :], c_ref[0])], axis=0)
    y_ref[0] = jnp.transpose(y_tb.reshape(tl, sb, SSM_CH), (1, 0, 2))


def _s5_scan(u, bb, a, cm, h0, B, L):
    tl = SCAN_TILE
    nt = L // tl
    tile = lambda d, i: i + d * (nt - 1 - 2 * i)
    return pl.pallas_call(
        functools.partial(_s5_scan_body, tl),
        grid=(2, B // SCAN_BATCH, nt),
        in_specs=[
            pl.BlockSpec((SCAN_BATCH, tl, SSM_CH), lambda d, b, i: (b, tile(d, i), 0)),
            pl.BlockSpec((1, SSM_CH, 2 * SSM_N), lambda d, b, i: (d, 0, 0)),
            pl.BlockSpec((1, 1, 2 * SSM_N), lambda d, b, i: (d, 0, 0)),
            pl.BlockSpec((1, 2 * SSM_N, SSM_CH), lambda d, b, i: (d, 0, 0)),
            pl.BlockSpec((1, SCAN_BATCH, 2 * SSM_N), lambda d, b, i: (d, b, 0)),
        ],
        out_specs=[
            pl.BlockSpec((1, SCAN_BATCH, tl, SSM_CH), lambda d, b, i: (d, b, tile(d, i), 0)),
            pl.BlockSpec((1, SCAN_BATCH, 2 * SSM_N), lambda d, b, i: (d, b, 0)),
        ],
        out_shape=[jax.ShapeDtypeStruct((2, B, L, SSM_CH), F32),
                   jax.ShapeDtypeStruct((2, B, 2 * SSM_N), F32)],
        scratch_shapes=[pltpu.VMEM((SCAN_BATCH * tl, 2 * SSM_N), F32),
                        pltpu.VMEM((SCAN_BATCH * tl, 2 * SSM_N), BF16),
                        pltpu.VMEM((SCAN_BATCH, 2 * SSM_N), F32)],
        compiler_params=_cparams(("arbitrary", "arbitrary", "arbitrary")),
        name="s5_scan",
    )(u.reshape(B, L, SSM_CH), bb, a, cm, h0)


def _outproj_body(ona_ref, owa_ref, y_ref, u_ref, dsk_ref, wglu_ref, wout_ref, x_ref, mod_ref, g_ref,
                  wr_ref, x1_ref, h2_ref, aff_ref):
    m = mod_ref[0]
    g1 = m[:, 2 * D_MODEL:3 * D_MODEL]
    sh2 = m[:, 3 * D_MODEL:4 * D_MODEL]
    sc2 = m[:, 4 * D_MODEL:5 * D_MODEL]
    tm = x_ref.shape[0]
    sub = min(tm, OUTPROJ_SUBTILE)
    parts = [slice(r, r + sub) for r in range(0, tm, sub)]
    mixes = []
    for rs in parts:
        y = dsk_ref[...] * u_ref[rs, :] + y_ref[0, rs, :] + y_ref[1, rs, :]
        zg = _dot(y.astype(BF16), wglu_ref[...])
        oss = zg[:, 0:SSM_CH] * _sigmoid(zg[:, SSM_CH:2 * SSM_CH])
        mixes.append(_dot(ona_ref[rs, :], wout_ref[0:NA_WIDTH, :])
                     + _dot(owa_ref[rs, :], wout_ref[NA_WIDTH:NA_WIDTH + WA_WIDTH, :])
                     + _dot(oss.astype(BF16), wout_ref[NA_WIDTH + WA_WIDTH:D_MODEL, :]))
    split = []
    for rs, mix in zip(parts, mixes):
        x1 = x_ref[rs, :] + g1 * _rms(mix, g_ref[1:2, :])
        x1_ref[rs, :] = x1
        h2 = _rms(x1, g_ref[2:3, :]) * (1.0 + sc2) + sh2
        h2_hi = h2.astype(BF16)
        h2_ref[rs, :] = h2_hi
        split.append((h2_hi, (h2 - h2_hi.astype(F32)).astype(BF16)))
    for rs, (h2_hi, h2_lo) in zip(parts, split):
        r_hi = _dot(h2_hi, wr_ref[...])
        logits = r_hi + pltpu.roll(r_hi, LANES - N_EXPERTS, 1) + _dot(h2_lo, wr_ref[...])
        lane = lax.broadcasted_iota(jnp.int32, logits.shape, 1)
        logits = jnp.where(lane < N_EXPERTS, logits, NEG_INF)
        mx = jnp.max(logits, axis=-1, keepdims=True)
        p = jnp.exp(logits - mx)
        aff_ref[rs, :] = p / jnp.sum(p, axis=-1, keepdims=True)


def _outproj(ona, owa, y2, u, dsk, wglu_bf, wout_bf, x, mod_l, mod_row_fn, g, wr_pad, L):
    T = x.shape[0]
    tm = min(PROJ_TILE, L)
    tiles_per_seq = L // tm
    row = lambda w: pl.BlockSpec((tm, w), lambda i: (i, 0))
    full = lambda a: pl.BlockSpec(a.shape, lambda i: (0,) * a.ndim)
    return pl.pallas_call(
        _outproj_body,
        grid=(T // tm,),
        in_specs=[row(NA_WIDTH), row(WA_WIDTH),
                  pl.BlockSpec((2, tm, SSM_CH), lambda i: (0, i, 0)),
                  row(SSM_CH), full(dsk), full(wglu_bf), full(wout_bf), row(D_MODEL),
                  pl.BlockSpec((1, 1, 6 * D_MODEL), lambda i: (mod_row_fn(i // tiles_per_seq), 0, 0)),
                  full(g), full(wr_pad)],
        out_specs=[row(D_MODEL), row(D_MODEL), row(LANES)],
        out_shape=[jax.ShapeDtypeStruct((T, D_MODEL), F32),
                   jax.ShapeDtypeStruct((T, D_MODEL), BF16),
                   jax.ShapeDtypeStruct((T, LANES), F32)],
        compiler_params=_cparams(("arbitrary",)),
        name="outproj",
    )(ona, owa, y2.reshape(2, T, SSM_CH), u, dsk, wglu_bf, wout_bf, x, mod_l, g, wr_pad)


def _select_body(cap, aff_ref, pos_ref, post_ref, afft_ref, plan_ref):
    L = aff_ref.shape[1]
    aff = aff_ref[0]
    for r in range(1, SEL_GROUP):
        aff = aff + pltpu.roll(aff_ref[r], r * N_EXPERTS, 1)
    capf = jnp.float32(cap)

    def bisect(k, thr):
        cand = thr | jnp.left_shift(jnp.int32(1), 30 - k)
        cnt = jnp.sum(jnp.where(aff >= pltpu.bitcast(cand, F32), 1.0, 0.0), axis=0, keepdims=True)
        return jnp.where(cnt >= capf, cand, thr)

    thr = lax.fori_loop(0, 31, bisect, jnp.zeros((1, LANES), jnp.int32))
    kth = pltpu.bitcast(thr, F32)
    gt = jnp.where(aff > kth, 1.0, 0.0)
    eq = jnp.where(aff == kth, 1.0, 0.0)
    need = capf - jnp.sum(gt, axis=0, keepdims=True)
    ck = SEL_CHUNK
    tri = (lax.broadcasted_iota(jnp.int32, (ck, ck), 0)
           >= lax.broadcasted_iota(jnp.int32, (ck, ck), 1))
    tri = jnp.where(tri, 1.0, 0.0).astype(BF16)

    def prefix(mask):
        outs = []
        carries = [jnp.zeros((1, LANES), F32)]
        for j in range(L // ck):
            c = _dot(tri, mask[j * ck:(j + 1) * ck].astype(BF16)) + carries[-1]
            outs.append(c)
            carries.append(c[ck - 1:ck, :])
        return jnp.concatenate(outs, axis=0), jnp.concatenate(carries, axis=0)

    tie = jnp.where(prefix(eq)[0] <= need, eq, 0.0)
    sel = gt + tie
    rank, starts = prefix(sel)
    pos = jnp.where(sel > 0.0, rank - 1.0, -1.0)
    pos_ref[0] = pos
    post_ref[0] = jnp.transpose(pos)
    afft_ref[0] = jnp.transpose(aff)
    nt = L // ck
    first, last = starts[0:nt], starts[1:nt + 1]
    win = min(cap, COMBINE_WINDOW)
    base = jnp.minimum(jnp.floor(first * (1.0 / BF16_ROWS)) * BF16_ROWS, float(cap - win))
    fit = jnp.where(last - base <= float(win), 1.0, 0.0)
    plan_ref[0] = jnp.concatenate([base, fit], axis=0).astype(jnp.int32)


def _select(aff, B, L):
    cap = EC_CAPACITY * L // N_EXPERTS
    ng = B // SEL_GROUP
    return pl.pallas_call(
        functools.partial(_select_body, cap),
        grid=(ng,),
        in_specs=[pl.BlockSpec((SEL_GROUP, L, LANES), lambda i: (i, 0, 0))],
        out_specs=[pl.BlockSpec((1, L, LANES), lambda i: (i, 0, 0)),
                   pl.BlockSpec((1, LANES, L), lambda i: (i, 0, 0)),
                   pl.BlockSpec((1, LANES, L), lambda i: (i, 0, 0)),
                   pl.BlockSpec((1, 2 * (L // SEL_CHUNK), LANES), lambda i: (i, 0, 0))],
        out_shape=[jax.ShapeDtypeStruct((ng, L, LANES), F32),
                   jax.ShapeDtypeStruct((ng, LANES, L), F32),
                   jax.ShapeDtypeStruct((ng, LANES, L), F32),
                   jax.ShapeDtypeStruct((ng, 2 * (L // SEL_CHUNK), LANES), jnp.int32)],
        compiler_params=_cparams(("arbitrary",)),
        name="ec_select",
    )(aff.reshape(B, L, LANES))


def _window_rows(base, w):
    return pl.ds(base, w) if isinstance(base, int) else pl.ds(pl.multiple_of(base, BF16_ROWS), w)


def _gather_body(cap, win, plan_ref, h_ref, post_ref, afft_ref, xs_ref, gs_ref):
    b = pl.program_id(0)
    j = pl.program_id(1)
    grp = b // SEL_GROUP
    lane0 = (b % SEL_GROUP) * N_EXPERTS

    @pl.when(j == 0)
    def _():
        xs_ref[...] = jnp.zeros_like(xs_ref)
        gs_ref[...] = jnp.zeros_like(gs_ref)

    n_req, ck = h_ref.shape[0], h_ref.shape[1]

    def run(w, bases, r=0):
        h = h_ref[r]
        slot = lax.broadcasted_iota(jnp.int32, (w, ck), 0)
        onehots = []
        for e in range(N_EXPERTS):
            rows = _window_rows(bases[e], w)
            row = r * N_EXPERTS + e
            hit = post_ref[0, row:row + 1, :] == (slot + bases[e]).astype(F32)
            gs_ref[e, r, rows, :] += jnp.sum(jnp.where(hit, afft_ref[0, row:row + 1, :], 0.0), axis=1, keepdims=True)
            onehots.append(jnp.where(hit, 1.0, 0.0).astype(BF16))
        xs = _dot(jnp.concatenate(onehots, axis=0), h).astype(BF16)
        for e in range(N_EXPERTS):
            rows = _window_rows(bases[e], w)
            xs_ref[e, r, rows, :] += xs[e * w:(e + 1) * w]

    if win == cap:
        for r in range(n_req):
            run(cap, [0] * N_EXPERTS, r)
        return
    bases = []
    fits = None
    nt = pl.num_programs(1)
    for e in range(N_EXPERTS):
        bases.append(plan_ref[grp, j, lane0 + e])
        ok = plan_ref[grp, nt + j, lane0 + e]
        fits = ok if fits is None else fits + ok
    fits = fits == N_EXPERTS

    @pl.when(fits)
    def _():
        run(win, bases)

    @pl.when(jnp.logical_not(fits))
    def _():
        run(cap, [0] * N_EXPERTS)


def _gather(starts, h2, post, afft, B, L):
    cap = EC_CAPACITY * L // N_EXPERTS
    ck = SEL_CHUNK
    win = min(cap, COMBINE_WINDOW)
    nr = SMALL_STEP_REQUESTS if win == cap else 1
    per_group = SEL_GROUP // nr
    lane_blk = lambda b, j, s: (b // per_group, b % per_group, j)
    grid_spec = pltpu.PrefetchScalarGridSpec(
        num_scalar_prefetch=1,
        grid=(B // nr, L // ck),
        in_specs=[pl.BlockSpec((nr, ck, D_MODEL), lambda b, j, s: (b, j, 0)),
                  pl.BlockSpec((1, nr * N_EXPERTS, ck), lane_blk),
                  pl.BlockSpec((1, nr * N_EXPERTS, ck), lane_blk)],
        out_specs=[pl.BlockSpec((N_EXPERTS, nr, cap, D_MODEL), lambda b, j, s: (0, b, 0, 0)),
                   pl.BlockSpec((N_EXPERTS, nr, cap, 1), lambda b, j, s: (0, b, 0, 0))],
    )
    xs, gs = pl.pallas_call(
        functools.partial(_gather_body, cap, win),
        grid_spec=grid_spec,
        out_shape=[jax.ShapeDtypeStruct((N_EXPERTS, B, cap, D_MODEL), BF16),
                   jax.ShapeDtypeStruct((N_EXPERTS, B, cap, 1), F32)],
        compiler_params=_cparams(("arbitrary", "arbitrary")),
        name="ec_gather",
    )(starts, h2.reshape(B, L, D_MODEL), post, afft)
    return xs.reshape(N_EXPERTS, B * cap, D_MODEL), gs.reshape(N_EXPERTS, B * cap, 1)


def _ffn_body(n_first, xa_ref, ga_ref, xb_ref, gb_ref, wg_ref, wu_ref, wd_ref, y_ref, wgb, wub, wdb):
    j = pl.program_id(1)

    @pl.when(j == 0)
    def _():
        wgb[...] = wg_ref[0, 0].astype(BF16)
        wub[...] = wu_ref[0, 0].astype(BF16)
        wdb[...] = wd_ref[0, 0].astype(BF16)

    def run(xs_ref, gs_ref):
        x = xs_ref[0]
        a = _dot(x, wgb[...])
        u = _dot(x, wub[...])
        hm = (a * _sigmoid(a) * u).astype(BF16)
        y_ref[0] = (_dot(hm, wdb[...]) * gs_ref[0]).astype(BF16)

    @pl.when(j < n_first)
    def _():
        run(xa_ref, ga_ref)

    @pl.when(j >= n_first)
    def _():
        run(xb_ref, gb_ref)


def _ffn(xa, ga, xb, gb, wg, wu, wd, l):
    rc = FFN_ROWS
    na = xa.shape[1] // rc
    nb = xb.shape[1] // rc
    first = lambda w: pl.BlockSpec((1, rc, w), lambda e, j: (e, jnp.minimum(j, na - 1), 0))
    second = lambda w: pl.BlockSpec((1, rc, w), lambda e, j: (e, jnp.maximum(j - na, 0), 0))
    ahead = lambda e, j: jnp.minimum(e + jnp.minimum(j, 1), N_EXPERTS - 1)
    wspec = lambda a, b: pl.BlockSpec((1, 1, a, b), lambda e, j: (l, ahead(e, j), 0, 0))
    return pl.pallas_call(
        functools.partial(_ffn_body, na),
        grid=(N_EXPERTS, na + nb),
        in_specs=[first(D_MODEL), first(1), second(D_MODEL), second(1),
                  wspec(D_MODEL, EXPERT_FF), wspec(D_MODEL, EXPERT_FF), wspec(EXPERT_FF, D_MODEL)],
        out_specs=pl.BlockSpec((1, rc, D_MODEL), lambda e, j: (e, j, 0)),
        out_shape=jax.ShapeDtypeStruct((N_EXPERTS, (na + nb) * rc, D_MODEL), BF16),
        scratch_shapes=[pltpu.VMEM((D_MODEL, EXPERT_FF), BF16),
                        pltpu.VMEM((D_MODEL, EXPERT_FF), BF16),
                        pltpu.VMEM((EXPERT_FF, D_MODEL), BF16)],
        compiler_params=_cparams(("arbitrary", "arbitrary")),
        name="ec_ffn",
    )(xa, ga, xb, gb, wg, wu, wd)


def _combine_body(cap, win, plan_ref, pos_ref, y_ref, x1_ref, mod_ref, g_ref, o_ref):
    b = pl.program_id(0)
    i = pl.program_id(1)
    pos = pos_ref[0].astype(BF16)
    tq = pos.shape[0]
    g2 = mod_ref[0][:, 5 * D_MODEL:6 * D_MODEL]

    def finish(w, bases, values, lane0, rows=slice(None)):
        n = N_EXPERTS * w
        col = lax.broadcasted_iota(jnp.int32, (1, n), 1)
        expert = jnp.zeros((1, n), jnp.int32)
        for e in range(1, N_EXPERTS):
            expert = expert + jnp.where(col >= e * w, 1, 0)
        target = col - expert * w
        if bases is not None:
            for e in range(N_EXPERTS):
                target = jnp.where(expert == e, target + bases[e], target)
        expand = jnp.where(lax.broadcasted_iota(jnp.int32, (LANES, n), 0) == expert + lane0, 1.0, 0.0)
        per_col = _dot(pos, expand.astype(BF16))
        onehot = jnp.where(per_col == target.astype(F32), 1.0, 0.0).astype(BF16)
        f = _dot(onehot, values)
        o_ref[rows, :] = x1_ref[rows, :] + g2 * _rms(f, g_ref[3:4, :])

    if win == cap:
        n_req = y_ref.shape[1] // cap
        for r in range(n_req):
            values = y_ref[:, r * cap:(r + 1) * cap, :].reshape(N_EXPERTS * cap, D_MODEL)
            finish(cap, None, values, ((b * n_req + r) % SEL_GROUP) * N_EXPERTS, slice(r * tq, (r + 1) * tq))
        return
    grp = b // SEL_GROUP
    lane0 = (b % SEL_GROUP) * N_EXPERTS
    bases = []
    fits = None
    nt = pl.num_programs(1)
    for e in range(N_EXPERTS):
        bases.append(plan_ref[grp, i, lane0 + e])
        ok = plan_ref[grp, nt + i, lane0 + e]
        fits = ok if fits is None else fits + ok
    fits = fits == N_EXPERTS

    @pl.when(fits)
    def _():
        rows = [y_ref[e, pl.ds(pl.multiple_of(bases[e], BF16_ROWS), win), :] for e in range(N_EXPERTS)]
        finish(win, bases, jnp.concatenate(rows, axis=0), lane0)

    @pl.when(jnp.logical_not(fits))
    def _():
        finish(cap, None, y_ref[...].reshape(N_EXPERTS * cap, D_MODEL), lane0)


def _combine(starts, pos, y, row_off, x1, mod_l, mod_row_fn, g, B, L, shared_mod=False):
    cap = EC_CAPACITY * L // N_EXPERTS
    tq = TOKEN_TILE
    assert tq == SEL_CHUNK
    nq = L // tq
    blk_off = row_off // cap
    win = min(cap, COMBINE_WINDOW)
    nr = SMALL_STEP_REQUESTS if (shared_mod and win == cap and nq == 1 and blk_off % SMALL_STEP_REQUESTS == 0) else 1
    per_group = SEL_GROUP // nr
    grid_spec = pltpu.PrefetchScalarGridSpec(
        num_scalar_prefetch=1,
        grid=(B // nr, nq),
        in_specs=[pl.BlockSpec((1, tq, LANES), lambda b, i, s: (b // per_group, i, 0)),
                  pl.BlockSpec((N_EXPERTS, nr * cap, D_MODEL), lambda b, i, s: (0, blk_off // nr + b, 0)),
                  pl.BlockSpec((nr * tq, D_MODEL), lambda b, i, s: (b * nq + i, 0)),
                  pl.BlockSpec((1, 1, 6 * D_MODEL), lambda b, i, s: (mod_row_fn(b * nr), 0, 0)),
                  pl.BlockSpec((4, D_MODEL), lambda b, i, s: (0, 0))],
        out_specs=pl.BlockSpec((nr * tq, D_MODEL), lambda b, i, s: (b * nq + i, 0)),
    )
    return pl.pallas_call(
        functools.partial(_combine_body, cap, win),
        grid_spec=grid_spec,
        out_shape=jax.ShapeDtypeStruct((B * L, D_MODEL), F32),
        compiler_params=_cparams(("arbitrary", "arbitrary")),
        name="ec_combine",
    )(starts, pos, y, x1, mod_l, g)


def _rope_tables(L):
    t = jnp.arange(L)
    row = (t // GRID_W).astype(F32)
    col = (t % GRID_W).astype(F32)
    half = HEAD_DIM // 4
    inv = ROPE_BASE ** (-jnp.arange(half, dtype=F32) / half)
    d = np.arange(LANES) % HEAD_DIM
    use_col = (d // (HEAD_DIM // 2)) == 1
    pos = jnp.where(use_col[None, :], col[:, None], row[:, None])
    ang = pos * inv[d % half][None, :]
    sign = np.where((d % (HEAD_DIM // 2)) < half, -1.0, 1.0).astype(np.float32)
    return jnp.cos(ang), jnp.sin(ang) * sign[None, :]


def _permute_wa_heads(a, axis, start):
    cut = lambda lo, hi: lax.slice_in_dim(a, lo, hi, axis=axis)
    heads = [cut(start + h * HEAD_DIM, start + (h + 1) * HEAD_DIM) for h in WA_HEAD_ORDER]
    return jnp.concatenate([cut(0, start)] + heads + [cut(start + WA_WIDTH, a.shape[axis])], axis=axis)


def kernel(x_prompt, x_sample, c, cache_na_k, cache_na_v, cache_wa_k, cache_wa_v, state_ssm, c_ctx, w_ada, b_ada, norm_g, w_in, w_out, na_rpb, wa_sink, ssm_lambda_re, ssm_lambda_im, ssm_log_dt, ssm_b_re, ssm_b_im, ssm_c_re, ssm_c_im, ssm_d, w_glu, w_router, w_exp_gate, w_exp_up, w_exp_down):
    Bc, Lc, _ = x_prompt.shape
    Bs, Ls, _ = x_sample.shape
    ctx_row = Bs
    n_cond = ((Bs + 1 + SUBLANES - 1) // SUBLANES) * SUBLANES
    cond = jnp.zeros((n_cond, D_MODEL), F32).at[0:Bs].set(c).at[ctx_row].set(c_ctx)
    mod = _modulation(cond, w_ada, b_ada)
    mod = mod.reshape(DEPTH, n_cond, 1, 6 * D_MODEL)

    w_in_bf = _permute_wa_heads(w_in, 2, C_WAQ).astype(BF16)
    w_out_bf = _permute_wa_heads(w_out, 1, NA_WIDTH).astype(BF16)
    w_glu_bf = w_glu.astype(BF16)
    wr_hi = w_router.astype(BF16)
    wr_lo = (w_router - wr_hi.astype(F32)).astype(BF16)
    wr_pad = jnp.pad(jnp.concatenate([wr_hi, wr_lo], axis=-1), ((0, 0), (0, 0), (0, LANES - 2 * N_EXPERTS)))
    s5_a, s5_bb = _s5_prep(ssm_lambda_re, ssm_lambda_im, ssm_log_dt, ssm_b_re, ssm_b_im)
    s5_c = _s5_c_matrix(ssm_c_re, ssm_c_im)
    rope_tabs = _rope_tables(Ls)

    ctx_mod_row = lambda b: ctx_row + 0 * b
    lat_mod_row = lambda b: b

    xp = x_prompt.reshape(Bc * Lc, D_MODEL)
    xs = x_sample.reshape(Bs * Ls, D_MODEL)
    new_kv = ()
    new_ssm = []
    for l in range(DEPTH):
        g = norm_g[l]
        dsk = ssm_d[l].reshape(1, SSM_CH)
        sl2 = slice(2 * l, 2 * l + 2)
        qna, kna, vna, qwa, kwa, vwa, u = _inproj(xp, mod, ctx_mod_row, norm_g, w_in_bf, l, Bc, Lc, None, new_kv)
        new_kv = (kna, vna, kwa, vwa)
        ona, owa = _attn_ctx(wa_sink[l], qna, kna, vna, qwa, kwa, vwa, l, Bc, Lc)
        y2, hfin = _s5_scan(u, s5_bb[sl2], s5_a[sl2], s5_c[sl2],
                            jnp.zeros((2, Bc, 2 * SSM_N), F32), Bc, Lc)
        x1, h2, aff = _outproj(ona, owa, y2, u, dsk, w_glu_bf[l], w_out_bf[l], xp, mod[l], ctx_mod_row,
                               g, wr_pad[l], Lc)
        pos_c, post, afft, starts_c = _select(aff, Bc, Lc)
        xg_c, gs_c = _gather(starts_c, h2, post, afft, Bc, Lc)
        x1_c = x1
        new_ssm.append(jnp.transpose(hfin.reshape(2, Bc, 2, SSM_GROUPS, SSM_STATE), (1, 0, 2, 3, 4)))
        qna, kna, vna, qwa, kwa, vwa, u = _inproj(xs, mod, lat_mod_row, norm_g, w_in_bf, l, Bs, Ls, rope_tabs)
        ona = _attn_na(qna, kna, vna, cache_na_k, cache_na_v, _na_bias_pairs(na_rpb[l]), l, Bs, Ls)
        owa = _attn_wa(wa_sink[l], qwa, kwa, vwa, cache_wa_k, cache_wa_v, l, Bs, Ls)
        h0 = jnp.transpose(state_ssm[:, l].reshape(Bs, 2, 2 * SSM_N), (1, 0, 2))
        y2, _ = _s5_scan(u, s5_bb[sl2], s5_a[sl2], s5_c[sl2], h0, Bs, Ls)
        x1, h2, aff = _outproj(ona, owa, y2, u, dsk, w_glu_bf[l], w_out_bf[l], xs, mod[l], lat_mod_row,
                               g, wr_pad[l], Ls)
        pos_s, post, afft, starts_s = _select(aff, Bs, Ls)
        xg_s, gs_s = _gather(starts_s, h2, post, afft, Bs, Ls)
        yy = _ffn(xg_c, gs_c, xg_s, gs_s, w_exp_gate, w_exp_up, w_exp_down, l)
        xp = _combine(starts_c, pos_c, yy, 0, x1_c, mod[l], ctx_mod_row, g, Bc, Lc, shared_mod=True)
        xs = _combine(starts_s, pos_s, yy, xg_c.shape[1], x1, mod[l], lat_mod_row, g, Bs, Ls)
    return (xp.reshape(Bc, Lc, D_MODEL), xs.reshape(Bs, Ls, D_MODEL),
            new_kv[0].reshape(Bc, DEPTH, Lc, NA_HEADS, HEAD_DIM),
            new_kv[1].reshape(Bc, DEPTH, Lc, NA_HEADS, HEAD_DIM),
            new_kv[2].reshape(Bc, DEPTH, Lc, WA_KV_HEADS, HEAD_DIM),
            new_kv[3].reshape(Bc, DEPTH, Lc, WA_KV_HEADS, HEAD_DIM),
            jnp.stack(new_ssm, axis=1))
```
